```python
import jax, jax.numpy as jnp
from jax import lax
import numpy as np

D_MODEL = 2048
BATCH = 4
SEQ = 4096
DEPTH = 1

HEAD_DIM = D_MODEL // 16
FOX_HEADS = 6
SWA_HEADS = 6
SWA_KV_HEADS = 2
MEM_HEADS = 4
MEM_LEN = 256
WINDOW = 128
Q_BLOCK = 128
D_FF = 5632
EPS = 1e-6
NEG_INF = -1e30

FOX_W = FOX_HEADS * HEAD_DIM
SWA_Q_W = SWA_HEADS * HEAD_DIM
SWA_KV_W = SWA_KV_HEADS * HEAD_DIM
MEM_W = MEM_HEADS * HEAD_DIM
MIX_WIDTH = FOX_W + SWA_Q_W + MEM_W
IN_SPLITS = [FOX_W, FOX_W, FOX_W, FOX_HEADS, SWA_Q_W, SWA_KV_W, SWA_KV_W, MEM_W]
IN_WIDTH = int(sum(IN_SPLITS))
IN_CUTS = [int(c) for c in np.cumsum(IN_SPLITS)[:-1]]

kernel_name = "hybrid_fox_swa_memory_macaron"


def rms_norm(x, g):
    xf = x.astype(jnp.float32)
    y = xf * lax.rsqrt(jnp.mean(xf * xf, axis=-1, keepdims=True) + EPS)
    return (y * g.astype(jnp.float32)).astype(x.dtype)


def swiglu(x, w_gate, w_up, w_down):
    return (jax.nn.silu(x @ w_gate) * (x @ w_up)) @ w_down


def alibi_slopes(n):
    return jnp.asarray(2.0 ** (-8.0 * np.arange(1, n + 1) / n), dtype=jnp.float32)


def forgetting_attention(q, k, v, log_f):
    B, S, H, D = q.shape
    nb = S // Q_BLOCK
    scale = D ** -0.5
    c = jnp.cumsum(log_f, axis=1)
    c_k = c.transpose(0, 2, 1)
    kf = k.astype(jnp.float32)
    vf = v.astype(jnp.float32)
    qb = q.astype(jnp.float32).reshape(B, nb, Q_BLOCK, H, D).transpose(1, 0, 2, 3, 4)
    cb = c.reshape(B, nb, Q_BLOCK, H).transpose(1, 0, 3, 2)
    kpos = jnp.arange(S)

    def block(args):
        qi, ci, i = args
        qpos = i * Q_BLOCK + jnp.arange(Q_BLOCK)
        s = jnp.einsum('bqhd,bkhd->bhqk', qi, kf) * scale
        s = s + ci[..., None] - c_k[:, :, None, :]
        causal = kpos[None, :] <= qpos[:, None]
        s = jnp.where(causal, s, NEG_INF)
        p = jax.nn.softmax(s, axis=-1)
        return jnp.einsum('bhqk,bkhd->bqhd', p, vf)

    out = lax.map(block, (qb, cb, jnp.arange(nb)))
    return out.transpose(1, 0, 2, 3, 4).reshape(B, S, H, D).astype(q.dtype)


def sliding_window_sink_attention(q, k, v, sinks, slopes):
    B, S, Hq, D = q.shape
    Hkv = k.shape[2]
    G = Hq // Hkv
    nb = S // WINDOW
    scale = D ** -0.5
    qb = q.astype(jnp.float32).reshape(B, nb, WINDOW, Hkv, G, D)

    def band(t):
        tb = t.astype(jnp.float32).reshape(B, nb, WINDOW, Hkv, D)
        prev = jnp.pad(tb[:, :-1], ((0, 0), (1, 0), (0, 0), (0, 0), (0, 0)))
        return jnp.concatenate([prev, tb], axis=2)

    kb, vb = band(k), band(v)
    s = jnp.einsum('bnqhgd,bnkhd->bnhgqk', qb, kb) * scale
    r = jnp.arange(WINDOW)[:, None]
    j = jnp.arange(2 * WINDOW)[None, :]
    dist = WINDOW + r - j
    in_window = (dist >= 0) & (dist < WINDOW)
    valid = in_window[None] & ((jnp.arange(nb)[:, None, None] > 0) | (j[None] >= WINDOW))
    alibi = -slopes.astype(jnp.float32).reshape(Hkv, G)[:, :, None, None] * dist.astype(jnp.float32)
    s = s + alibi[None, None]
    s = jnp.where(valid[None, :, None, None], s, NEG_INF)
    sink = jnp.broadcast_to(
        sinks.astype(jnp.float32).reshape(Hkv, G)[None, None, :, :, None, None],
        s.shape[:-1] + (1,))
    p = jax.nn.softmax(jnp.concatenate([s, sink], axis=-1), axis=-1)[..., :-1]
    out = jnp.einsum('bnhgqk,bnkhd->bnqhgd', p, vb)
    return out.reshape(B, S, Hq, D).astype(q.dtype)


def memory_attention(q, mk, mv):
    scale = q.shape[-1] ** -0.5
    s = jnp.einsum('bqhd,bmhd->bhqm', q.astype(jnp.float32), mk.astype(jnp.float32)) * scale
    p = jax.nn.softmax(s, axis=-1)
    return jnp.einsum('bhqm,bmhd->bqhd', p, mv.astype(jnp.float32)).astype(q.dtype)


def setup_inputs(seed: int = 0) -> dict:
    key = jax.random.key(seed)
    ks = jax.random.split(key, 32)
    f32 = jnp.float32

    def w(k, shape, fan_in):
        return jax.random.normal(k, shape, f32) * (fan_in ** -0.5)

    def gain(k, shape):
        return 1.0 + 0.02 * jax.random.normal(k, shape, f32)

    L = DEPTH
    return {
        "x": jax.random.normal(ks[0], (BATCH, SEQ, D_MODEL), f32),
        "mem": jax.random.normal(ks[1], (BATCH, MEM_LEN, D_MODEL), f32),
        "ffn1_norm": gain(ks[2], (L, D_MODEL)),
        "ffn1_gate": w(ks[3], (L, D_MODEL, D_FF), D_MODEL),
        "ffn1_up": w(ks[4], (L, D_MODEL, D_FF), D_MODEL),
        "ffn1_down": w(ks[5], (L, D_FF, D_MODEL), D_FF),
        "mix_norm": gain(ks[6], (L, D_MODEL)),
        "mem_norm": gain(ks[7], (L, D_MODEL)),
        "w_in": w(ks[8], (L, D_MODEL, IN_WIDTH), D_MODEL),
        "forget_bias": jax.random.uniform(ks[9], (L, FOX_HEADS), f32, 1.0, 4.0),
        "w_mem_k": w(ks[10], (L, D_MODEL, MEM_W), D_MODEL),
        "w_mem_v": w(ks[11], (L, D_MODEL, MEM_W), D_MODEL),
        "fox_q_gain": gain(ks[12], (L, HEAD_DIM)),
        "fox_k_gain": gain(ks[13], (L, HEAD_DIM)),
        "swa_q_gain": gain(ks[14], (L, HEAD_DIM)),
        "swa_k_gain": gain(ks[15], (L, HEAD_DIM)),
        "swa_sinks": jax.random.normal(ks[16], (L, SWA_HEADS), f32),
        "mem_q_gain": gain(ks[17], (L, HEAD_DIM)),
        "mem_k_gain": gain(ks[18], (L, HEAD_DIM)),
        "w_out": w(ks[19], (L, MIX_WIDTH, D_MODEL), MIX_WIDTH),
        "ffn2_norm": gain(ks[20], (L, D_MODEL)),
        "ffn2_gate": w(ks[21], (L, D_MODEL, D_FF), D_MODEL),
        "ffn2_up": w(ks[22], (L, D_MODEL, D_FF), D_MODEL),
        "ffn2_down": w(ks[23], (L, D_FF, D_MODEL), D_FF),
    }


def reference(x, mem, ffn1_norm, ffn1_gate, ffn1_up, ffn1_down, mix_norm, mem_norm, w_in,
              forget_bias, w_mem_k, w_mem_v, fox_q_gain, fox_k_gain, swa_q_gain, swa_k_gain,
              swa_sinks, mem_q_gain, mem_k_gain, w_out, ffn2_norm, ffn2_gate, ffn2_up, ffn2_down):
    B, S, _ = x.shape
    M = mem.shape[1]
    slopes = alibi_slopes(SWA_HEADS).astype(x.dtype)
    for l in range(DEPTH):
        x = x + 0.5 * swiglu(rms_norm(x, ffn1_norm[l]), ffn1_gate[l], ffn1_up[l], ffn1_down[l])

        h = rms_norm(x, mix_norm[l])
        proj = h @ w_in[l]
        fq, fk, fv, f_logit, sq, sk, sv, mq = jnp.split(proj, IN_CUTS, axis=-1)

        fq = rms_norm(fq.reshape(B, S, FOX_HEADS, HEAD_DIM), fox_q_gain[l])
        fk = rms_norm(fk.reshape(B, S, FOX_HEADS, HEAD_DIM), fox_k_gain[l])
        fv = fv.reshape(B, S, FOX_HEADS, HEAD_DIM)
        log_f = jax.nn.log_sigmoid(f_logit.astype(jnp.float32) + forget_bias[l].astype(jnp.float32))
        out_a = forgetting_attention(fq, fk, fv, log_f)

        sq = rms_norm(sq.reshape(B, S, SWA_HEADS, HEAD_DIM), swa_q_gain[l])
        sk = rms_norm(sk.reshape(B, S, SWA_KV_HEADS, HEAD_DIM), swa_k_gain[l])
        sv = sv.reshape(B, S, SWA_KV_HEADS, HEAD_DIM)
        out_b = sliding_window_sink_attention(sq, sk, sv, swa_sinks[l], slopes)

        mn = rms_norm(mem, mem_norm[l])
        mk = rms_norm((mn @ w_mem_k[l]).reshape(B, M, MEM_HEADS, HEAD_DIM), mem_k_gain[l])
        mv = (mn @ w_mem_v[l]).reshape(B, M, MEM_HEADS, HEAD_DIM)
        mq = rms_norm(mq.reshape(B, S, MEM_HEADS, HEAD_DIM), mem_q_gain[l])
        out_c = memory_attention(mq, mk, mv)

        mixed = jnp.concatenate([out_a.reshape(B, S, FOX_W), out_b.reshape(B, S, SWA_Q_W),
                                 out_c.reshape(B, S, MEM_W)], axis=-1)
        x = x + mixed @ w_out[l]

        x = x + 0.5 * swiglu(rms_norm(x, ffn2_norm[l]), ffn2_gate[l], ffn2_up[l], ffn2_down[l])
    return x
```

```python
import functools

import jax
import jax.numpy as jnp
from jax import lax
from jax.experimental import pallas as pl
from jax.experimental.pallas import tpu as pltpu

F32 = jnp.float32
BF16 = jnp.bfloat16

HEAD_DIM = 128
FOX_HEADS = 6
SWA_HEADS = 6
SWA_KV_HEADS = 2
SWA_GROUP = SWA_HEADS // SWA_KV_HEADS
MEM_HEADS = 4
WINDOW = 128
EPS = 1e-6
NEG_INF = -1e30

FOX_W = FOX_HEADS * HEAD_DIM
SWA_Q_W = SWA_HEADS * HEAD_DIM
SWA_KV_W = SWA_KV_HEADS * HEAD_DIM
MEM_W = MEM_HEADS * HEAD_DIM

NORMED_W = 2 * FOX_W + SWA_Q_W + SWA_KV_W + MEM_W
PROJ_W = NORMED_W + FOX_W + SWA_KV_W
LANES = 128
SUBLANES = 8

VMEM_LIMIT = 58 * 1024 * 1024


def _cparams(sem):
    return pltpu.CompilerParams(dimension_semantics=sem, vmem_limit_bytes=VMEM_LIMIT)


def _ffn_kernel(x_ref, g_ref, wg_ref, wu_ref, wd_ref, o_ref, xn_ref, *, n_chunk):
    j = pl.program_id(1)
    nj = pl.num_programs(1)

    @pl.when(j == 0)
    def _():
        x = x_ref[...]
        ms = jnp.mean(x * x, axis=-1, keepdims=True)
        xn_ref[...] = (x * lax.rsqrt(ms + EPS) * g_ref[...]).astype(BF16)

    xn = xn_ref[...]
    gate = jnp.dot(xn, wg_ref[...], preferred_element_type=F32)
    up = jnp.dot(xn, wu_ref[...], preferred_element_type=F32)
    h = (gate * jax.nn.sigmoid(gate) * up).astype(BF16)
    d_model = o_ref.shape[1]
    cw = d_model // n_chunk
    for c in range(n_chunk):
        cs = slice(c * cw, (c + 1) * cw)
        d = jnp.dot(h, wd_ref[:, cs], preferred_element_type=F32)

        @pl.when(j == 0)
        def _():
            o_ref[:, cs] = d

        @pl.when(j > 0)
        def _():
            o_ref[:, cs] += d

    @pl.when(j == nj - 1)
    def _():
        o_ref[...] = x_ref[...] + 0.5 * o_ref[...]


def _ffn(x, gain, wg, wu, wd, *, tm=1024, tf=512):
    t, d = x.shape
    dff = wg.shape[1]
    return pl.pallas_call(
        functools.partial(_ffn_kernel, n_chunk=4),
        out_shape=jax.ShapeDtypeStruct((t, d), F32),
        grid=(t // tm, dff // tf),
        in_specs=[
            pl.BlockSpec((tm, d), lambda i, j: (i, 0)),
            pl.BlockSpec((1, d), lambda i, j: (0, 0)),
            pl.BlockSpec((d, tf), lambda i, j: (0, j)),
            pl.BlockSpec((d, tf), lambda i, j: (0, j)),
            pl.BlockSpec((tf, d), lambda i, j: (j, 0)),
        ],
        out_specs=pl.BlockSpec((tm, d), lambda i, j: (i, 0)),
        scratch_shapes=[pltpu.VMEM((tm, d), BF16)],
        compiler_params=_cparams(("parallel", "arbitrary")),
        name="ffn",
    )(x, gain, wg, wu, wd)


def _in_proj_kernel(x_ref, g_ref, w_ref, wf_ref, hg_ref, fb_ref,
                    o_ref, csl_ref, crow_ref, hn_ref, carry_ref, *, blocks_per_seq, tk):
    i = pl.program_id(0)
    j = pl.program_id(1)
    tm, tn = o_ref.shape

    @pl.when(j == 0)
    def _():
        x = x_ref[...]
        ms = jnp.mean(x * x, axis=-1, keepdims=True)
        hn = (x * lax.rsqrt(ms + EPS) * g_ref[...]).astype(BF16)
        hn_ref[...] = hn

        logit = jnp.dot(hn, wf_ref[...], preferred_element_type=F32)
        c = jax.nn.log_sigmoid(logit + fb_ref[...])
        row = lax.broadcasted_iota(jnp.int32, c.shape, 0)
        sh = 1
        while sh < tm:
            c = c + jnp.where(row >= sh, pltpu.roll(c, sh, 0), 0.0)
            sh *= 2

        @pl.when(i % blocks_per_seq == 0)
        def _():
            carry_ref[...] = jnp.zeros_like(carry_ref)

        c = c + carry_ref[0:1, :]
        carry_ref[0:1, :] = c[tm - 1:tm, :]
        csl_ref[...] = c
        ct = c.T
        for r in range(tm // tk):
            crow_ref[r] = ct[0:SUBLANES, r * tk:(r + 1) * tk]

    p = jnp.dot(hn_ref[...], w_ref[...], preferred_element_type=F32)

    @pl.when(j * tn < NORMED_W)
    def _():
        for k in range(tn // HEAD_DIM):
            ph = p[:, k * HEAD_DIM:(k + 1) * HEAD_DIM]
            ms = jnp.mean(ph * ph, axis=-1, keepdims=True)
            o_ref[:, k * HEAD_DIM:(k + 1) * HEAD_DIM] = (
                ph * lax.rsqrt(ms + EPS) * hg_ref[0, k:k + 1, :]).astype(BF16)

    @pl.when(j * tn >= NORMED_W)
    def _():
        o_ref[...] = p.astype(BF16)


def _in_proj(x, gain, w, wf, head_gain, fbias, *, seq, tm=1024, tn=1024, tk=256):
    t, d = x.shape
    assert NORMED_W % tn == 0 and seq % tm == 0 and tm % tk == 0
    hpb = tn // HEAD_DIM
    head_gain = head_gain.reshape(PROJ_W // tn, hpb, HEAD_DIM)
    return pl.pallas_call(
        functools.partial(_in_proj_kernel, blocks_per_seq=seq // tm, tk=tk),
        out_shape=(
            jax.ShapeDtypeStruct((t, PROJ_W), BF16),
            jax.ShapeDtypeStruct((t, LANES), F32),
            jax.ShapeDtypeStruct((t // tk, SUBLANES, tk), F32),
        ),
        grid=(t // tm, PROJ_W // tn),
        in_specs=[
            pl.BlockSpec((tm, d), lambda i, j: (i, 0)),
            pl.BlockSpec((1, d), lambda i, j: (0, 0)),
            pl.BlockSpec((d, tn), lambda i, j: (0, j)),
            pl.BlockSpec((d, LANES), lambda i, j: (0, 0)),
            pl.BlockSpec((1, hpb, HEAD_DIM), lambda i, j: (j, 0, 0)),
            pl.BlockSpec((1, LANES), lambda i, j: (0, 0)),
        ],
        out_specs=(
            pl.BlockSpec((tm, tn), lambda i, j: (i, j)),
            pl.BlockSpec((tm, LANES), lambda i, j: (i, 0)),
            pl.BlockSpec((tm // tk, SUBLANES, tk), lambda i, j: (i, 0, 0)),
        ),
        scratch_shapes=[pltpu.VMEM((tm, d), BF16), pltpu.VMEM((SUBLANES, LANES), F32)],
        compiler_params=_cparams(("arbitrary", "arbitrary")),
        name="in_proj",
    )(x, gain, w, wf, head_gain, fbias)


def _fox_kernel(q_ref, k_ref, v_ref, csl_ref, crow_ref, o_ref, *, tk):
    qi = pl.program_id(1)
    tq = q_ref.shape[0]
    n_diag = tq // tk
    row = lax.broadcasted_iota(jnp.int32, (tq, tk), 0)
    col = lax.broadcasted_iota(jnp.int32, (tq, tk), 1)

    for h in range(FOX_HEADS):
        hs = slice(h * HEAD_DIM, (h + 1) * HEAD_DIM)
        q = q_ref[:, hs]
        cq = csl_ref[:, h:h + 1]

        def step(j, carry, mask_off=None):
            m, l, acc = carry
            ks = pl.ds(pl.multiple_of(j * tk, tk), tk)
            k = k_ref[ks, hs]
            v = v_ref[ks, hs]
            s = lax.dot_general(q, k, (((1,), (1,)), ((), ())), preferred_element_type=F32)
            s = s + cq - crow_ref[j, h:h + 1, :]
            if mask_off is not None:
                s = jnp.where(col + mask_off <= row, s, NEG_INF)
            m_new = jnp.maximum(m, jnp.max(s, axis=-1, keepdims=True))
            alpha = jnp.exp(m - m_new)
            p = jnp.exp(s - m_new)
            l = alpha * l + jnp.sum(p, axis=-1, keepdims=True)
            acc = alpha * acc + jnp.dot(p.astype(BF16), v, preferred_element_type=F32)
            return m_new, l, acc

        carry = (jnp.full((tq, 1), NEG_INF, F32), jnp.zeros((tq, 1), F32),
                 jnp.zeros((tq, HEAD_DIM), F32))
        carry = lax.fori_loop(0, qi * n_diag, step, carry)
        for d in range(n_diag):
            carry = step(qi * n_diag + d, carry, mask_off=d * tk)
        _, l, acc = carry
        o_ref[:, hs] = (acc / l).astype(o_ref.dtype)


def _fox(proj, csl, crow, *, batch, seq, tq=256, tk=256):
    t = proj.shape[0]
    nq = seq // tq
    fw = FOX_W
    k_blk = FOX_W // fw
    v_blk = NORMED_W // fw
    return pl.pallas_call(
        functools.partial(_fox_kernel, tk=tk),
        out_shape=jax.ShapeDtypeStruct((t, fw), BF16),
        grid=(batch, nq),
        in_specs=[
            pl.BlockSpec((tq, fw), lambda b, i: (b * nq + i, 0)),
            pl.BlockSpec((seq, fw), lambda b, i: (b, k_blk)),
            pl.BlockSpec((seq, fw), lambda b, i: (b, v_blk)),
            pl.BlockSpec((tq, LANES), lambda b, i: (b * nq + i, 0)),
            pl.BlockSpec((seq // tk, SUBLANES, tk), lambda b, i: (b, 0, 0)),
        ],
        out_specs=pl.BlockSpec((tq, fw), lambda b, i: (b * nq + i, 0)),
        compiler_params=_cparams(("parallel", "arbitrary")),
        name="fox",
    )(proj, proj, proj, csl, crow)


def _swa_kernel(q_ref, k_ref, v_ref, sink_ref, slope_ref, o_ref):
    qi = pl.program_id(1)
    tq = q_ref.shape[0]
    w = WINDOW
    row = lax.broadcasted_iota(jnp.int32, (w, 2 * w), 0)
    col = lax.broadcasted_iota(jnp.int32, (w, 2 * w), 1)

    for r in range(tq // w):
        n = qi * (tq // w) + r
        kstart = pl.multiple_of(jnp.maximum(n - 1, 0) * w, w)
        dist = (n * w - kstart) + row - col
        valid = (dist >= 0) & (dist < w)
        dist_f = dist.astype(F32)
        for h in range(SWA_HEADS):
            g = h // SWA_GROUP
            q = q_ref[r * w:(r + 1) * w, h * HEAD_DIM:(h + 1) * HEAD_DIM]
            k = k_ref[pl.ds(kstart, 2 * w), g * HEAD_DIM:(g + 1) * HEAD_DIM]
            v = v_ref[pl.ds(kstart, 2 * w), g * HEAD_DIM:(g + 1) * HEAD_DIM]
            s = lax.dot_general(q, k, (((1,), (1,)), ((), ())), preferred_element_type=F32)
            s = s - slope_ref[h] * dist_f
            s = jnp.where(valid, s, NEG_INF)
            sink = sink_ref[h]
            m = jnp.maximum(jnp.max(s, axis=-1, keepdims=True), sink)
            p = jnp.exp(s - m)
            denom = jnp.sum(p, axis=-1, keepdims=True) + jnp.exp(sink - m)
            o = jnp.dot(p.astype(BF16), v, preferred_element_type=F32)
            o_ref[r * w:(r + 1) * w, h * HEAD_DIM:(h + 1) * HEAD_DIM] = (o / denom).astype(o_ref.dtype)


def _swa(proj, sinks, slopes, *, batch, seq, tq=512):
    t = proj.shape[0]
    nq = seq // tq
    q_blk = (2 * FOX_W) // SWA_Q_W
    k_blk = (2 * FOX_W + SWA_Q_W) // SWA_KV_W
    v_blk = (NORMED_W + FOX_W) // SWA_KV_W
    smem = pl.BlockSpec(memory_space=pltpu.SMEM)
    return pl.pallas_call(
        _swa_kernel,
        out_shape=jax.ShapeDtypeStruct((t, SWA_Q_W), BF16),
        grid=(batch, nq),
        in_specs=[
            pl.BlockSpec((tq, SWA_Q_W), lambda b, i: (b * nq + i, q_blk)),
            pl.BlockSpec((seq, SWA_KV_W), lambda b, i: (b, k_blk)),
            pl.BlockSpec((seq, SWA_KV_W), lambda b, i: (b, v_blk)),
            smem, smem,
        ],
        out_specs=pl.BlockSpec((tq, SWA_Q_W), lambda b, i: (b * nq + i, 0)),
        compiler_params=_cparams(("parallel", "arbitrary")),
        name="swa",
    )(proj, proj, proj, sinks, slopes)


def _mem_kv_kernel(mem_ref, g_ref, wk_ref, wv_ref, kg_ref, mk_ref, mv_ref):
    x = mem_ref[...]
    ms = jnp.mean(x * x, axis=-1, keepdims=True)
    mn = (x * lax.rsqrt(ms + EPS) * g_ref[...]).astype(BF16)
    k = jnp.dot(mn, wk_ref[...], preferred_element_type=F32)
    v = jnp.dot(mn, wv_ref[...], preferred_element_type=F32)
    for h in range(MEM_HEADS):
        hs = slice(h * HEAD_DIM, (h + 1) * HEAD_DIM)
        kh = k[:, hs]
        ms = jnp.mean(kh * kh, axis=-1, keepdims=True)
        mk_ref[:, hs] = (kh * lax.rsqrt(ms + EPS) * kg_ref[...]).astype(BF16)
    mv_ref[...] = v.astype(BF16)


def _mem_kv(mem2d, gain, wk, wv, kgain, *, batch, mem_len):
    d = mem2d.shape[1]
    full = lambda shape: pl.BlockSpec(shape, lambda b: (0, 0))
    out = jax.ShapeDtypeStruct((batch * mem_len, MEM_W), BF16)
    return pl.pallas_call(
        _mem_kv_kernel,
        out_shape=(out, out),
        grid=(batch,),
        in_specs=[
            pl.BlockSpec((mem_len, d), lambda b: (b, 0)),
            full((1, d)), full((d, MEM_W)), full((d, MEM_W)), full((1, HEAD_DIM)),
        ],
        out_specs=(pl.BlockSpec((mem_len, MEM_W), lambda b: (b, 0)),
                   pl.BlockSpec((mem_len, MEM_W), lambda b: (b, 0))),
        compiler_params=_cparams(("parallel",)),
        name="mem_kv",
    )(mem2d, gain, wk, wv, kgain)


def _mem_attn_kernel(q_ref, mk_ref, mv_ref, o_ref):
    for h in range(MEM_HEADS):
        hs = slice(h * HEAD_DIM, (h + 1) * HEAD_DIM)
        s = lax.dot_general(q_ref[:, hs], mk_ref[:, hs], (((1,), (1,)), ((), ())),
                            preferred_element_type=F32)
        m = jnp.max(s, axis=-1, keepdims=True)
        p = jnp.exp(s - m)
        denom = jnp.sum(p, axis=-1, keepdims=True)
        o = jnp.dot(p.astype(BF16), mv_ref[:, hs], preferred_element_type=F32)
        o_ref[:, hs] = (o / denom).astype(o_ref.dtype)


def _mem_attn(proj, mk, mv, *, batch, seq, mem_len, tq=512):
    t = proj.shape[0]
    nq = seq // tq
    q_blk = (2 * FOX_W + SWA_Q_W + SWA_KV_W) // MEM_W
    return pl.pallas_call(
        _mem_attn_kernel,
        out_shape=jax.ShapeDtypeStruct((t, MEM_W), BF16),
        grid=(batch, nq),
        in_specs=[
            pl.BlockSpec((tq, MEM_W), lambda b, i: (b * nq + i, q_blk)),
            pl.BlockSpec((mem_len, MEM_W), lambda b, i: (b, 0)),
            pl.BlockSpec((mem_len, MEM_W), lambda b, i: (b, 0)),
        ],
        out_specs=pl.BlockSpec((tq, MEM_W), lambda b, i: (b * nq + i, 0)),
        compiler_params=_cparams(("parallel", "arbitrary")),
        name="mem_attn",
    )(proj, mk, mv)


def _out_proj_kernel(x_ref, a_ref, b_ref, c_ref, w_ref, o_ref):
    acc = jnp.dot(a_ref[...], w_ref[0:FOX_W, :], preferred_element_type=F32)
    acc += jnp.dot(b_ref[...], w_ref[FOX_W:FOX_W + SWA_Q_W, :], preferred_element_type=F32)
    acc += jnp.dot(c_ref[...], w_ref[FOX_W + SWA_Q_W:, :], preferred_element_type=F32)
    o_ref[...] = x_ref[...] + acc


def _out_proj(x, a, b, c, w, *, tm=512):
    t, d = x.shape
    return pl.pallas_call(
        _out_proj_kernel,
        out_shape=jax.ShapeDtypeStruct((t, d), F32),
        grid=(t // tm,),
        in_specs=[
            pl.BlockSpec((tm, d), lambda i: (i, 0)),
            pl.BlockSpec((tm, FOX_W), lambda i: (i, 0)),
            pl.BlockSpec((tm, SWA_Q_W), lambda i: (i, 0)),
            pl.BlockSpec((tm, MEM_W), lambda i: (i, 0)),
            pl.BlockSpec(w.shape, lambda i: (0, 0)),
        ],
        out_specs=pl.BlockSpec((tm, d), lambda i: (i, 0)),
        compiler_params=_cparams(("parallel",)),
        name="out_proj",
    )(x, a, b, c, w)


def _alibi_slopes(n):
    return jnp.asarray([2.0 ** (-8.0 * i / n) for i in range(1, n + 1)], dtype=F32)


def _pack_w_in(w_in):
    cuts = [0, FOX_W, 2 * FOX_W, 3 * FOX_W, 3 * FOX_W + FOX_HEADS]
    fq, fk, fv, fl = (w_in[:, cuts[i]:cuts[i + 1]] for i in range(4))
    o = cuts[4]
    sq = w_in[:, o:o + SWA_Q_W]
    sk = w_in[:, o + SWA_Q_W:o + SWA_Q_W + SWA_KV_W]
    sv = w_in[:, o + SWA_Q_W + SWA_KV_W:o + SWA_Q_W + 2 * SWA_KV_W]
    mq = w_in[:, o + SWA_Q_W + 2 * SWA_KV_W:]
    w_main = jnp.concatenate([fq, fk, sq, sk, mq, fv, sv], axis=1).astype(BF16)
    w_f = jnp.pad(fl, ((0, 0), (0, LANES - FOX_HEADS))).astype(BF16)
    return w_main, w_f


def kernel(x, mem, ffn1_norm, ffn1_gate, ffn1_up, ffn1_down, mix_norm, mem_norm, w_in, forget_bias, w_mem_k, w_mem_v, fox_q_gain, fox_k_gain, swa_q_gain, swa_k_gain, swa_sinks, mem_q_gain, mem_k_gain, w_out, ffn2_norm, ffn2_gate, ffn2_up, ffn2_down):
    batch, seq, d = x.shape
    mem_len = mem.shape[1]
    depth = w_in.shape[0]
    scale = HEAD_DIM ** -0.5
    slopes = _alibi_slopes(SWA_HEADS)
    x2 = x.reshape(batch * seq, d)
    mem2 = mem.reshape(batch * mem_len, d)
    ones = jnp.ones((HEAD_DIM,), F32)

    for l in range(depth):
        x2 = _ffn(x2, ffn1_norm[l][None], ffn1_gate[l].astype(BF16), ffn1_up[l].astype(BF16),
                  ffn1_down[l].astype(BF16))

        w_main, w_f = _pack_w_in(w_in[l])
        head_gain = jnp.stack(
            [fox_q_gain[l] * scale] * FOX_HEADS + [fox_k_gain[l]] * FOX_HEADS
            + [swa_q_gain[l] * scale] * SWA_HEADS + [swa_k_gain[l]] * SWA_KV_HEADS
            + [mem_q_gain[l] * scale] * MEM_HEADS + [ones] * (FOX_HEADS + SWA_KV_HEADS))
        fbias = jnp.pad(forget_bias[l], (0, LANES - FOX_HEADS))[None]
        proj, csl, crow = _in_proj(x2, mix_norm[l][None], w_main, w_f, head_gain, fbias, seq=seq)

        out_a = _fox(proj, csl, crow, batch=batch, seq=seq)
        out_b = _swa(proj, swa_sinks[l], slopes, batch=batch, seq=seq)
        mk, mv = _mem_kv(mem2, mem_norm[l][None], w_mem_k[l].astype(BF16), w_mem_v[l].astype(BF16),
                         mem_k_gain[l][None], batch=batch, mem_len=mem_len)
        out_c = _mem_attn(proj, mk, mv, batch=batch, seq=seq, mem_len=mem_len)

        x2 = _out_proj(x2, out_a, out_b, out_c, w_out[l].astype(BF16))
        x2 = _ffn(x2, ffn2_norm[l][None], ffn2_gate[l].astype(BF16), ffn2_up[l].astype(BF16),
                  ffn2_down[l].astype(BF16))
    return x2.reshape(batch, seq, d)
```

```python
import functools

import jax
import jax.numpy as jnp
from jax import lax
from jax.experimental import pallas as pl
from jax.experimental.pallas import tpu as pltpu

F32 = jnp.float32
BF16 = jnp.bfloat16

HEAD_DIM = 128
FOX_HEADS = 6
SWA_HEADS = 6
SWA_KV_HEADS = 2
SWA_GROUP = SWA_HEADS // SWA_KV_HEADS
MEM_HEADS = 4
WINDOW = 128
EPS = 1e-6
NEG_INF = -1e30
LOG2E = 1.4426950408889634
FOX_BLK = 256

FOX_W = FOX_HEADS * HEAD_DIM
SWA_Q_W = SWA_HEADS * HEAD_DIM
SWA_KV_W = SWA_KV_HEADS * HEAD_DIM
MEM_W = MEM_HEADS * HEAD_DIM

NORMED_W = 2 * FOX_W + SWA_Q_W + SWA_KV_W + MEM_W
PROJ_W = NORMED_W + FOX_W + SWA_KV_W
LANES = 128
SUBLANES = 8

VMEM_LIMIT = 58 * 1024 * 1024


def _cparams(sem):
    return pltpu.CompilerParams(dimension_semantics=sem, vmem_limit_bytes=VMEM_LIMIT)


def _ffn_kernel(x_ref, g_ref, wg_ref, wu_ref, wd_ref, o_ref, xn_ref, *, n_chunk):
    j = pl.program_id(1)
    nj = pl.num_programs(1)

    @pl.when(j == 0)
    def _():
        x = x_ref[...]
        ms = jnp.mean(x * x, axis=-1, keepdims=True)
        xn_ref[...] = (x * lax.rsqrt(ms + EPS) * g_ref[...]).astype(BF16)

    xn = xn_ref[...]
    gate = jnp.dot(xn, wg_ref[...], preferred_element_type=F32)
    up = jnp.dot(xn, wu_ref[...], preferred_element_type=F32)
    h = (gate * jax.nn.sigmoid(gate) * up).astype(BF16)
    d_model = o_ref.shape[1]
    cw = d_model // n_chunk
    for c in range(n_chunk):
        cs = slice(c * cw, (c + 1) * cw)
        d = jnp.dot(h, wd_ref[:, cs], preferred_element_type=F32)

        @pl.when(j == 0)
        def _():
            o_ref[:, cs] = d

        @pl.when(j > 0)
        def _():
            o_ref[:, cs] += d

    @pl.when(j == nj - 1)
    def _():
        o_ref[...] = x_ref[...] + 0.5 * o_ref[...]


def _ffn(x, gain, wg, wu, wd, *, tm=1024, tf=512):
    t, d = x.shape
    dff = wg.shape[1]
    return pl.pallas_call(
        functools.partial(_ffn_kernel, n_chunk=4),
        out_shape=jax.ShapeDtypeStruct((t, d), F32),
        grid=(t // tm, dff // tf),
        in_specs=[
            pl.BlockSpec((tm, d), lambda i, j: (i, 0)),
            pl.BlockSpec((1, d), lambda i, j: (0, 0)),
            pl.BlockSpec((d, tf), lambda i, j: (0, j)),
            pl.BlockSpec((d, tf), lambda i, j: (0, j)),
            pl.BlockSpec((tf, d), lambda i, j: (j, 0)),
        ],
        out_specs=pl.BlockSpec((tm, d), lambda i, j: (i, 0)),
        scratch_shapes=[pltpu.VMEM((tm, d), BF16)],
        compiler_params=_cparams(("parallel", "arbitrary")),
        name="ffn",
    )(x, gain, wg, wu, wd)


def _in_proj_kernel(x_ref, g_ref, w_ref, wf_ref, hg_ref, fb_ref,
                    o_ref, crel_ref, anc_ref, hn_ref, carry_ref, *, blocks_per_seq):
    i = pl.program_id(0)
    j = pl.program_id(1)
    tm, tn = o_ref.shape

    @pl.when(j == 0)
    def _():
        x = x_ref[...]
        ms = jnp.mean(x * x, axis=-1, keepdims=True)
        hn = (x * lax.rsqrt(ms + EPS) * g_ref[...]).astype(BF16)
        hn_ref[...] = hn

        logit = jnp.dot(hn, wf_ref[...], preferred_element_type=F32)
        c = jax.nn.log_sigmoid(logit + fb_ref[...]) * LOG2E
        row = lax.broadcasted_iota(jnp.int32, c.shape, 0) % FOX_BLK
        sh = 1
        while sh < FOX_BLK:
            c = c + jnp.where(row >= sh, pltpu.roll(c, sh, 0), 0.0)
            sh *= 2
        crel_ref[...] = c

        @pl.when(i % blocks_per_seq == 0)
        def _():
            carry_ref[...] = jnp.zeros_like(carry_ref)

        anchor = carry_ref[0:1, :]
        anc_ref[...] = jnp.zeros_like(anc_ref)
        for r in range(tm // FOX_BLK):
            anc_ref[0, r:r + 1, :] = anchor
            anchor = anchor + c[(r + 1) * FOX_BLK - 1:(r + 1) * FOX_BLK, :]
        carry_ref[0:1, :] = anchor

    p = jnp.dot(hn_ref[...], w_ref[...], preferred_element_type=F32)

    @pl.when(j * tn < NORMED_W)
    def _():
        for k in range(tn // HEAD_DIM):
            ph = p[:, k * HEAD_DIM:(k + 1) * HEAD_DIM]
            ms = jnp.mean(ph * ph, axis=-1, keepdims=True)
            o_ref[:, k * HEAD_DIM:(k + 1) * HEAD_DIM] = (
                ph * lax.rsqrt(ms + EPS) * hg_ref[0, k:k + 1, :]).astype(BF16)

    @pl.when(j * tn >= NORMED_W)
    def _():
        o_ref[...] = p.astype(BF16)


def _in_proj(x, gain, w, wf, head_gain, fbias, *, seq, tm=1024, tn=1024):
    t, d = x.shape
    assert NORMED_W % tn == 0 and seq % tm == 0 and tm % FOX_BLK == 0 and tm // FOX_BLK <= SUBLANES
    hpb = tn // HEAD_DIM
    head_gain = head_gain.reshape(PROJ_W // tn, hpb, HEAD_DIM)
    return pl.pallas_call(
        functools.partial(_in_proj_kernel, blocks_per_seq=seq // tm),
        out_shape=(
            jax.ShapeDtypeStruct((t, PROJ_W), BF16),
            jax.ShapeDtypeStruct((t, LANES), F32),
            jax.ShapeDtypeStruct((t // tm, SUBLANES, LANES), F32),
        ),
        grid=(t // tm, PROJ_W // tn),
        in_specs=[
            pl.BlockSpec((tm, d), lambda i, j: (i, 0)),
            pl.BlockSpec((1, d), lambda i, j: (0, 0)),
            pl.BlockSpec((d, tn), lambda i, j: (0, j)),
            pl.BlockSpec((d, LANES), lambda i, j: (0, 0)),
            pl.BlockSpec((1, hpb, HEAD_DIM), lambda i, j: (j, 0, 0)),
            pl.BlockSpec((1, LANES), lambda i, j: (0, 0)),
        ],
        out_specs=(
            pl.BlockSpec((tm, tn), lambda i, j: (i, j)),
            pl.BlockSpec((tm, LANES), lambda i, j: (i, 0)),
            pl.BlockSpec((1, SUBLANES, LANES), lambda i, j: (i, 0, 0)),
        ),
        scratch_shapes=[pltpu.VMEM((tm, d), BF16), pltpu.VMEM((SUBLANES, LANES), F32)],
        compiler_params=_cparams(("arbitrary", "arbitrary")),
        name="in_proj",
    )(x, gain, w, wf, head_gain, fbias)


def _fox_prep_kernel(q_ref, k_ref, v_ref, crel_ref, qa_ref, ka_ref, vt_ref):
    crel = crel_ref[...]
    lane = lax.broadcasted_iota(jnp.int32, crel.shape, 1)
    for h in range(FOX_HEADS):
        hs = slice(h * HEAD_DIM, (h + 1) * HEAD_DIM)
        ch = crel[:, h:h + 1]
        hi = ch.astype(BF16).astype(F32)
        lo = ch - hi
        aq = jnp.where(lane == 0, hi, jnp.where(lane == 1, lo, jnp.where(lane < 4, 1.0, 0.0)))
        ak = jnp.where(lane < 2, 1.0, jnp.where(lane == 2, -hi, jnp.where(lane == 3, -lo, 0.0)))
        qa_ref[:, 2 * h * HEAD_DIM:(2 * h + 1) * HEAD_DIM] = q_ref[:, hs]
        qa_ref[:, (2 * h + 1) * HEAD_DIM:(2 * h + 2) * HEAD_DIM] = aq.astype(BF16)
        ka_ref[:, 2 * h * HEAD_DIM:(2 * h + 1) * HEAD_DIM] = k_ref[:, hs]
        ka_ref[:, (2 * h + 1) * HEAD_DIM:(2 * h + 2) * HEAD_DIM] = ak.astype(BF16)
        vt_ref[0, hs, :] = v_ref[:, hs].astype(F32).T.astype(BF16)


def _fox_prep(proj, crel):
    t = proj.shape[0]
    blk = FOX_BLK
    k_blk = FOX_W // FOX_W
    v_blk = NORMED_W // FOX_W
    return pl.pallas_call(
        _fox_prep_kernel,
        out_shape=(
            jax.ShapeDtypeStruct((t, 2 * FOX_W), BF16),
            jax.ShapeDtypeStruct((t, 2 * FOX_W), BF16),
            jax.ShapeDtypeStruct((t // blk, FOX_W, blk), BF16),
        ),
        grid=(t // blk,),
        in_specs=[
            pl.BlockSpec((blk, FOX_W), lambda i: (i, 0)),
            pl.BlockSpec((blk, FOX_W), lambda i: (i, k_blk)),
            pl.BlockSpec((blk, FOX_W), lambda i: (i, v_blk)),
            pl.BlockSpec((blk, LANES), lambda i: (i, 0)),
        ],
        out_specs=(
            pl.BlockSpec((blk, 2 * FOX_W), lambda i: (i, 0)),
            pl.BlockSpec((blk, 2 * FOX_W), lambda i: (i, 0)),
            pl.BlockSpec((1, FOX_W, blk), lambda i: (i, 0, 0)),
        ),
        compiler_params=_cparams(("parallel",)),
        name="fox_prep",
    )(proj, proj, proj, crel)


def _fox_kernel(anc_ref, q_ref, k_ref, vt_ref, o_ref, m_ref, l_ref, acc_ref, *, nblk):
    b = pl.program_id(0)
    i = pl.program_id(1)
    blk = q_ref.shape[0]
    aw = 2 * HEAD_DIM
    m_ref[...] = jnp.full(m_ref.shape, NEG_INF, F32)
    l_ref[...] = jnp.zeros(l_ref.shape, F32)
    acc_ref[...] = jnp.zeros(acc_ref.shape, F32)
    krow = lax.broadcasted_iota(jnp.int32, (blk, blk), 0)
    qcol = lax.broadcasted_iota(jnp.int32, (blk, blk), 1)

    def tile(j, masked):
        ks = pl.ds(pl.multiple_of(j * blk, blk), blk)
        sts = [lax.dot_general(k_ref[ks, h * aw:(h + 1) * aw], q_ref[:, h * aw:(h + 1) * aw],
                               (((1,), (1,)), ((), ())), preferred_element_type=F32)
               for h in range(FOX_HEADS)]
        for h in range(FOX_HEADS):
            st = sts[h]
            d = anc_ref[(b * nblk + i) * SUBLANES + h] - anc_ref[(b * nblk + j) * SUBLANES + h]
            if masked:
                st = jnp.where(krow <= qcol, st, NEG_INF)
            m_prev = m_ref[h]
            m_new = jnp.maximum(m_prev, jnp.max(st, axis=0, keepdims=True) + d)
            alpha = jnp.exp2(m_prev - m_new)
            p = jnp.exp2(st - (m_new - d))
            l_ref[h] = alpha * l_ref[h] + jnp.sum(p, axis=0, keepdims=True)
            pv = jnp.dot(vt_ref[j, h * HEAD_DIM:(h + 1) * HEAD_DIM, :], p.astype(BF16),
                         preferred_element_type=F32)
            acc_ref[h] = alpha * acc_ref[h] + pv
            m_ref[h] = m_new

    def body(j, carry):
        tile(j, False)
        return carry

    lax.fori_loop(0, i, body, 0)
    tile(i, True)
    for h in range(FOX_HEADS):
        o_ref[:, h * HEAD_DIM:(h + 1) * HEAD_DIM] = (acc_ref[h] / l_ref[h]).T.astype(o_ref.dtype)


def _fox(anchors, qa, ka, vt, *, batch, seq):
    t = qa.shape[0]
    blk = FOX_BLK
    nblk = seq // blk
    return pl.pallas_call(
        functools.partial(_fox_kernel, nblk=nblk),
        out_shape=jax.ShapeDtypeStruct((t, FOX_W), BF16),
        grid=(batch, nblk),
        in_specs=[
            pl.BlockSpec(memory_space=pltpu.SMEM),
            pl.BlockSpec((blk, 2 * FOX_W), lambda b, i: (b * nblk + i, 0)),
            pl.BlockSpec((seq, 2 * FOX_W), lambda b, i: (b, 0)),
            pl.BlockSpec((nblk, FOX_W, blk), lambda b, i: (b, 0, 0)),
        ],
        out_specs=pl.BlockSpec((blk, FOX_W), lambda b, i: (b * nblk + i, 0)),
        scratch_shapes=[
            pltpu.VMEM((FOX_HEADS, 1, blk), F32),
            pltpu.VMEM((FOX_HEADS, 1, blk), F32),
            pltpu.VMEM((FOX_HEADS, HEAD_DIM, blk), F32),
        ],
        compiler_params=_cparams(("parallel", "arbitrary")),
        name="fox",
    )(anchors, qa, ka, vt)


def _swa_kernel(q_ref, k_ref, v_ref, sink_ref, slope_ref, o_ref):
    qi = pl.program_id(1)
    tq = q_ref.shape[0]
    w = WINDOW
    row = lax.broadcasted_iota(jnp.int32, (w, 2 * w), 0)
    col = lax.broadcasted_iota(jnp.int32, (w, 2 * w), 1)

    for r in range(tq // w):
        n = qi * (tq // w) + r
        kstart = pl.multiple_of(jnp.maximum(n - 1, 0) * w, w)
        dist = (n * w - kstart) + row - col
        valid = (dist >= 0) & (dist < w)
        dist_f = dist.astype(F32)
        for h in range(SWA_HEADS):
            g = h // SWA_GROUP
            q = q_ref[r * w:(r + 1) * w, h * HEAD_DIM:(h + 1) * HEAD_DIM]
            k = k_ref[pl.ds(kstart, 2 * w), g * HEAD_DIM:(g + 1) * HEAD_DIM]
            v = v_ref[pl.ds(kstart, 2 * w), g * HEAD_DIM:(g + 1) * HEAD_DIM]
            s = lax.dot_general(q, k, (((1,), (1,)), ((), ())), preferred_element_type=F32)
            s = s - slope_ref[h] * dist_f
            s = jnp.where(valid, s, NEG_INF)
            sink = sink_ref[h]
            m = jnp.maximum(jnp.max(s, axis=-1, keepdims=True), sink)
            p = jnp.exp(s - m)
            denom = jnp.sum(p, axis=-1, keepdims=True) + jnp.exp(sink - m)
            o = jnp.dot(p.astype(BF16), v, preferred_element_type=F32)
            o_ref[r * w:(r + 1) * w, h * HEAD_DIM:(h + 1) * HEAD_DIM] = (o / denom).astype(o_ref.dtype)


def _swa(proj, sinks, slopes, *, batch, seq, tq=512):
    t = proj.shape[0]
    nq = seq // tq
    q_blk = (2 * FOX_W) // SWA_Q_W
    k_blk = (2 * FOX_W + SWA_Q_W) // SWA_KV_W
    v_blk = (NORMED_W + FOX_W) // SWA_KV_W
    smem = pl.BlockSpec(memory_space=pltpu.SMEM)
    return pl.pallas_call(
        _swa_kernel,
        out_shape=jax.ShapeDtypeStruct((t, SWA_Q_W), BF16),
        grid=(batch, nq),
        in_specs=[
            pl.BlockSpec((tq, SWA_Q_W), lambda b, i: (b * nq + i, q_blk)),
            pl.BlockSpec((seq, SWA_KV_W), lambda b, i: (b, k_blk)),
            pl.BlockSpec((seq, SWA_KV_W), lambda b, i: (b, v_blk)),
            smem, smem,
        ],
        out_specs=pl.BlockSpec((tq, SWA_Q_W), lambda b, i: (b * nq + i, 0)),
        compiler_params=_cparams(("parallel", "arbitrary")),
        name="swa",
    )(proj, proj, proj, sinks, slopes)


def _mem_kv_kernel(mem_ref, g_ref, wk_ref, wv_ref, kg_ref, mk_ref, mv_ref):
    x = mem_ref[...]
    ms = jnp.mean(x * x, axis=-1, keepdims=True)
    mn = (x * lax.rsqrt(ms + EPS) * g_ref[...]).astype(BF16)
    k = jnp.dot(mn, wk_ref[...], preferred_element_type=F32)
    v = jnp.dot(mn, wv_ref[...], preferred_element_type=F32)
    for h in range(MEM_HEADS):
        hs = slice(h * HEAD_DIM, (h + 1) * HEAD_DIM)
        kh = k[:, hs]
        ms = jnp.mean(kh * kh, axis=-1, keepdims=True)
        mk_ref[:, hs] = (kh * lax.rsqrt(ms + EPS) * kg_ref[...]).astype(BF16)
    mv_ref[...] = v.astype(BF16)


def _mem_kv(mem2d, gain, wk, wv, kgain, *, batch, mem_len):
    d = mem2d.shape[1]
    full = lambda shape: pl.BlockSpec(shape, lambda b: (0, 0))
    out = jax.ShapeDtypeStruct((batch * mem_len, MEM_W), BF16)
    return pl.pallas_call(
        _mem_kv_kernel,
        out_shape=(out, out),
        grid=(batch,),
        in_specs=[
            pl.BlockSpec((mem_len, d), lambda b: (b, 0)),
            full((1, d)), full((d, MEM_W)), full((d, MEM_W)), full((1, HEAD_DIM)),
        ],
        out_specs=(pl.BlockSpec((mem_len, MEM_W), lambda b: (b, 0)),
                   pl.BlockSpec((mem_len, MEM_W), lambda b: (b, 0))),
        compiler_params=_cparams(("parallel",)),
        name="mem_kv",
    )(mem2d, gain, wk, wv, kgain)


def _mem_attn_kernel(q_ref, mk_ref, mv_ref, o_ref):
    for h in range(MEM_HEADS):
        hs = slice(h * HEAD_DIM, (h + 1) * HEAD_DIM)
        s = lax.dot_general(q_ref[:, hs], mk_ref[:, hs], (((1,), (1,)), ((), ())),
                            preferred_element_type=F32)
        m = jnp.max(s, axis=-1, keepdims=True)
        p = jnp.exp(s - m)
        denom = jnp.sum(p, axis=-1, keepdims=True)
        o = jnp.dot(p.astype(BF16), mv_ref[:, hs], preferred_element_type=F32)
        o_ref[:, hs] = (o / denom).astype(o_ref.dtype)


def _mem_attn(proj, mk, mv, *, batch, seq, mem_len, tq=512):
    t = proj.shape[0]
    nq = seq // tq
    q_blk = (2 * FOX_W + SWA_Q_W + SWA_KV_W) // MEM_W
    return pl.pallas_call(
        _mem_attn_kernel,
        out_shape=jax.ShapeDtypeStruct((t, MEM_W), BF16),
        grid=(batch, nq),
        in_specs=[
            pl.BlockSpec((tq, MEM_W), lambda b, i: (b * nq + i, q_blk)),
            pl.BlockSpec((mem_len, MEM_W), lambda b, i: (b, 0)),
            pl.BlockSpec((mem_len, MEM_W), lambda b, i: (b, 0)),
        ],
        out_specs=pl.BlockSpec((tq, MEM_W), lambda b, i: (b * nq + i, 0)),
        compiler_params=_cparams(("parallel", "arbitrary")),
        name="mem_attn",
    )(proj, mk, mv)


def _out_proj_kernel(x_ref, a_ref, b_ref, c_ref, w_ref, o_ref):
    acc = jnp.dot(a_ref[...], w_ref[0:FOX_W, :], preferred_element_type=F32)
    acc += jnp.dot(b_ref[...], w_ref[FOX_W:FOX_W + SWA_Q_W, :], preferred_element_type=F32)
    acc += jnp.dot(c_ref[...], w_ref[FOX_W + SWA_Q_W:, :], preferred_element_type=F32)
    o_ref[...] = x_ref[...] + acc


def _out_proj(x, a, b, c, w, *, tm=512):
    t, d = x.shape
    return pl.pallas_call(
        _out_proj_kernel,
        out_shape=jax.ShapeDtypeStruct((t, d), F32),
        grid=(t // tm,),
        in_specs=[
            pl.BlockSpec((tm, d), lambda i: (i, 0)),
            pl.BlockSpec((tm, FOX_W), lambda i: (i, 0)),
            pl.BlockSpec((tm, SWA_Q_W), lambda i: (i, 0)),
            pl.BlockSpec((tm, MEM_W), lambda i: (i, 0)),
            pl.BlockSpec(w.shape, lambda i: (0, 0)),
        ],
        out_specs=pl.BlockSpec((tm, d), lambda i: (i, 0)),
        compiler_params=_cparams(("parallel",)),
        name="out_proj",
    )(x, a, b, c, w)


def _alibi_slopes(n):
    return jnp.asarray([2.0 ** (-8.0 * i / n) for i in range(1, n + 1)], dtype=F32)


def _pack_w_in(w_in):
    cuts = [0, FOX_W, 2 * FOX_W, 3 * FOX_W, 3 * FOX_W + FOX_HEADS]
    fq, fk, fv, fl = (w_in[:, cuts[i]:cuts[i + 1]] for i in range(4))
    o = cuts[4]
    sq = w_in[:, o:o + SWA_Q_W]
    sk = w_in[:, o + SWA_Q_W:o + SWA_Q_W + SWA_KV_W]
    sv = w_in[:, o + SWA_Q_W + SWA_KV_W:o + SWA_Q_W + 2 * SWA_KV_W]
    mq = w_in[:, o + SWA_Q_W + 2 * SWA_KV_W:]
    w_main = jnp.concatenate([fq, fk, sq, sk, mq, fv, sv], axis=1).astype(BF16)
    w_f = jnp.pad(fl, ((0, 0), (0, LANES - FOX_HEADS))).astype(BF16)
    return w_main, w_f


def kernel(x, mem, ffn1_norm, ffn1_gate, ffn1_up, ffn1_down, mix_norm, mem_norm, w_in, forget_bias, w_mem_k, w_mem_v, fox_q_gain, fox_k_gain, swa_q_gain, swa_k_gain, swa_sinks, mem_q_gain, mem_k_gain, w_out, ffn2_norm, ffn2_gate, ffn2_up, ffn2_down):
    batch, seq, d = x.shape
    mem_len = mem.shape[1]
    depth = w_in.shape[0]
    scale = HEAD_DIM ** -0.5
    slopes = _alibi_slopes(SWA_HEADS)
    x2 = x.reshape(batch * seq, d)
    mem2 = mem.reshape(batch * mem_len, d)
    ones = jnp.ones((HEAD_DIM,), F32)

    for l in range(depth):
        x2 = _ffn(x2, ffn1_norm[l][None], ffn1_gate[l].astype(BF16), ffn1_up[l].astype(BF16),
                  ffn1_down[l].astype(BF16))

        w_main, w_f = _pack_w_in(w_in[l])
        head_gain = jnp.stack(
            [fox_q_gain[l] * (scale * LOG2E)] * FOX_HEADS + [fox_k_gain[l]] * FOX_HEADS
            + [swa_q_gain[l] * scale] * SWA_HEADS + [swa_k_gain[l]] * SWA_KV_HEADS
            + [mem_q_gain[l] * scale] * MEM_HEADS + [ones] * (FOX_HEADS + SWA_KV_HEADS))
        fbias = jnp.pad(forget_bias[l], (0, LANES - FOX_HEADS))[None]
        proj, crel, anc = _in_proj(x2, mix_norm[l][None], w_main, w_f, head_gain, fbias, seq=seq)

        qa, ka, vt = _fox_prep(proj, crel)
        n_sub = (x2.shape[0] // FOX_BLK) // anc.shape[0]
        anchors = anc[:, :n_sub, :SUBLANES].reshape(-1)
        out_a = _fox(anchors, qa, ka, vt, batch=batch, seq=seq)
        out_b = _swa(proj, swa_sinks[l], slopes, batch=batch, seq=seq)
        mk, mv = _mem_kv(mem2, mem_norm[l][None], w_mem_k[l].astype(BF16), w_mem_v[l].astype(BF16),
                         mem_k_gain[l][None], batch=batch, mem_len=mem_len)
        out_c = _mem_attn(proj, mk, mv, batch=batch, seq=seq, mem_len=mem_len)

        x2 = _out_proj(x2, out_a, out_b, out_c, w_out[l].astype(BF16))
        x2 = _ffn(x2, ffn2_norm[l][None], ffn2_gate[l].astype(BF16), ffn2_up[l].astype(BF16),
                  ffn2_down[l].astype(BF16))
    return x2.reshape(batch, seq, d)
```

```python
import functools

import jax
import jax.numpy as jnp
from jax import lax
from jax.experimental import pallas as pl
from jax.experimental.pallas import tpu as pltpu

F32 = jnp.float32
BF16 = jnp.bfloat16

HEAD_DIM = 128
FOX_HEADS = 6
SWA_HEADS = 6
SWA_KV_HEADS = 2
SWA_GROUP = SWA_HEADS // SWA_KV_HEADS
MEM_HEADS = 4
WINDOW = 128
EPS = 1e-6
NEG_INF = -1e30
LOG2E = 1.4426950408889634
FOX_BLK = 256

FOX_W = FOX_HEADS * HEAD_DIM
SWA_Q_W = SWA_HEADS * HEAD_DIM
SWA_KV_W = SWA_KV_HEADS * HEAD_DIM
MEM_W = MEM_HEADS * HEAD_DIM

PROJ_W = 3 * FOX_W + SWA_Q_W + 2 * SWA_KV_W + MEM_W
REST_W = SWA_Q_W + 2 * SWA_KV_W + MEM_W
REST_SQ, REST_SK, REST_MQ, REST_SV = 0, SWA_Q_W, SWA_Q_W + SWA_KV_W, SWA_Q_W + SWA_KV_W + MEM_W
GAIN_FQ, GAIN_FK, GAIN_SQ, GAIN_SK, GAIN_MQ = range(5)

LANES = 128
SUBLANES = 8
MXU_N = 256

VMEM_LIMIT = 58 * 1024 * 1024


def _cparams(sem):
    return pltpu.CompilerParams(dimension_semantics=sem, vmem_limit_bytes=VMEM_LIMIT)


def _rms(x, gain):
    ms = jnp.mean(x * x, axis=-1, keepdims=True)
    return x * lax.rsqrt(ms + EPS) * gain


def _ffn_kernel(x_ref, g_ref, wg_ref, wu_ref, wd_ref, o_ref, xn_ref, *, n_chunk):
    j = pl.program_id(1)

    @pl.when(j == 0)
    def _():
        x = x_ref[...]
        xn_ref[...] = _rms(x, g_ref[...]).astype(BF16)
        o_ref[...] = x

    xn = xn_ref[...]
    gate = jnp.dot(xn, wg_ref[...], preferred_element_type=F32)
    up = jnp.dot(xn, wu_ref[...], preferred_element_type=F32)
    h = (gate * jax.nn.sigmoid(gate) * (0.5 * up)).astype(BF16)
    d_model = o_ref.shape[1]
    cw = d_model // n_chunk
    for c in range(n_chunk):
        cs = slice(c * cw, (c + 1) * cw)
        o_ref[:, cs] += jnp.dot(h, wd_ref[:, cs], preferred_element_type=F32)


def _ffn(x, gain, wg, wu, wd, *, tm=1024, tf=512):
    t, d = x.shape
    dff = wg.shape[1]
    return pl.pallas_call(
        functools.partial(_ffn_kernel, n_chunk=4),
        out_shape=jax.ShapeDtypeStruct((t, d), F32),
        grid=(t // tm, dff // tf),
        in_specs=[
            pl.BlockSpec((tm, d), lambda i, j: (i, 0)),
            pl.BlockSpec((1, d), lambda i, j: (0, 0)),
            pl.BlockSpec((d, tf), lambda i, j: (0, j)),
            pl.BlockSpec((d, tf), lambda i, j: (0, j)),
            pl.BlockSpec((tf, d), lambda i, j: (j, 0)),
        ],
        out_specs=pl.BlockSpec((tm, d), lambda i, j: (i, 0)),
        scratch_shapes=[pltpu.VMEM((tm, d), BF16)],
        compiler_params=_cparams(("parallel", "arbitrary")),
        name="ffn",
    )(x, gain, wg, wu, wd)


def _in_proj_kernel(x_ref, g_ref, w_ref, wf_ref, hg_ref, fb_ref,
                    qa_ref, ka_ref, vt_ref, rest_ref, anc_ref, carry_ref, *, blocks_per_seq):
    i = pl.program_id(0)
    tm = x_ref.shape[0]
    hn = _rms(x_ref[...], g_ref[...]).astype(BF16)

    logit = jnp.dot(hn, wf_ref[...], preferred_element_type=F32)
    c = jax.nn.log_sigmoid(logit + fb_ref[...]) * LOG2E
    row = lax.broadcasted_iota(jnp.int32, c.shape, 0) % FOX_BLK
    sh = 1
    while sh < FOX_BLK:
        c = c + jnp.where(row >= sh, pltpu.roll(c, sh, 0), 0.0)
        sh *= 2

    @pl.when(i % blocks_per_seq == 0)
    def _():
        carry_ref[...] = jnp.zeros_like(carry_ref)

    anchor = carry_ref[0:1, :]
    anc_ref[...] = jnp.zeros_like(anc_ref)
    for r in range(tm // FOX_BLK):
        anc_ref[0, r:r + 1, :] = anchor
        anchor = anchor + c[(r + 1) * FOX_BLK - 1:(r + 1) * FOX_BLK, :]
    carry_ref[0:1, :] = anchor

    lane = lax.broadcasted_iota(jnp.int32, c.shape, 1)
    for h in range(FOX_HEADS):
        ch = c[:, h:h + 1]
        hi = ch.astype(BF16).astype(F32)
        lo = ch - hi
        aq = jnp.where(lane == 0, hi, jnp.where(lane == 1, lo, jnp.where(lane < 4, 1.0, 0.0)))
        ak = jnp.where(lane < 2, 1.0, jnp.where(lane == 2, -hi, jnp.where(lane == 3, -lo, 0.0)))
        qa_ref[:, (2 * h + 1) * HEAD_DIM:(2 * h + 2) * HEAD_DIM] = aq.astype(BF16)
        ka_ref[:, (2 * h + 1) * HEAD_DIM:(2 * h + 2) * HEAD_DIM] = ak.astype(BF16)

    def normed(ph, row_id):
        return _rms(ph, hg_ref[row_id:row_id + 1, :]).astype(BF16)

    heads_per_chunk = MXU_N // HEAD_DIM
    for ck in range(PROJ_W // MXU_N):
        p = jnp.dot(hn, w_ref[:, ck * MXU_N:(ck + 1) * MXU_N], preferred_element_type=F32)
        for e in range(heads_per_chunk):
            col = ck * MXU_N + e * HEAD_DIM
            ph = p[:, e * HEAD_DIM:(e + 1) * HEAD_DIM]
            if col < FOX_W:
                h = col // HEAD_DIM
                qa_ref[:, 2 * h * HEAD_DIM:(2 * h + 1) * HEAD_DIM] = normed(ph, GAIN_FQ)
            elif col < 2 * FOX_W:
                h = (col - FOX_W) // HEAD_DIM
                ka_ref[:, 2 * h * HEAD_DIM:(2 * h + 1) * HEAD_DIM] = normed(ph, GAIN_FK)
            elif col < 3 * FOX_W:
                h = (col - 2 * FOX_W) // HEAD_DIM
                for r in range(tm // FOX_BLK):
                    vt_ref[r, h * HEAD_DIM:(h + 1) * HEAD_DIM, :] = (
                        ph[r * FOX_BLK:(r + 1) * FOX_BLK, :].T.astype(BF16))
            else:
                rc = col - 3 * FOX_W
                if rc < REST_SK:
                    val = normed(ph, GAIN_SQ)
                elif rc < REST_MQ:
                    val = normed(ph, GAIN_SK)
                elif rc < REST_SV:
                    val = normed(ph, GAIN_MQ)
                else:
                    val = ph.astype(BF16)
                rest_ref[:, rc:rc + HEAD_DIM] = val


def _in_proj(x, gain, w, wf, head_gain, fbias, *, seq, tm=512):
    t, d = x.shape
    assert seq % tm == 0 and tm % FOX_BLK == 0 and tm // FOX_BLK <= SUBLANES
    const = lambda shape: pl.BlockSpec(shape, lambda i: (0, 0), pipeline_mode=pl.Buffered(1))
    return pl.pallas_call(
        functools.partial(_in_proj_kernel, blocks_per_seq=seq // tm),
        out_shape=(
            jax.ShapeDtypeStruct((t, 2 * FOX_W), BF16),
            jax.ShapeDtypeStruct((t, 2 * FOX_W), BF16),
            jax.ShapeDtypeStruct((t // FOX_BLK, FOX_W, FOX_BLK), BF16),
            jax.ShapeDtypeStruct((t, REST_W), BF16),
            jax.ShapeDtypeStruct((t // tm, SUBLANES, LANES), F32),
        ),
        grid=(t // tm,),
        in_specs=[
            pl.BlockSpec((tm, d), lambda i: (i, 0)),
            const((1, d)),
            const((d, PROJ_W)),
            const((d, LANES)),
            const((SUBLANES, HEAD_DIM)),
            const((1, LANES)),
        ],
        out_specs=(
            pl.BlockSpec((tm, 2 * FOX_W), lambda i: (i, 0)),
            pl.BlockSpec((tm, 2 * FOX_W), lambda i: (i, 0)),
            pl.BlockSpec((tm // FOX_BLK, FOX_W, FOX_BLK), lambda i: (i, 0, 0)),
            pl.BlockSpec((tm, REST_W), lambda i: (i, 0)),
            pl.BlockSpec((1, SUBLANES, LANES), lambda i: (i, 0, 0)),
        ),
        scratch_shapes=[pltpu.VMEM((SUBLANES, LANES), F32)],
        compiler_params=_cparams(("arbitrary",)),
        name="in_proj",
    )(x, gain, w, wf, head_gain, fbias)


def _fox_kernel(anc_ref, q_ref, k_ref, vt_ref, o_ref, m_ref, l_ref, acc_ref, *, nblk):
    b = pl.program_id(0)
    i = pl.program_id(1)
    blk = q_ref.shape[0]
    aw = 2 * HEAD_DIM
    m_ref[...] = jnp.full(m_ref.shape, NEG_INF, F32)
    l_ref[...] = jnp.zeros(l_ref.shape, F32)
    acc_ref[...] = jnp.zeros(acc_ref.shape, F32)
    krow = lax.broadcasted_iota(jnp.int32, (blk, blk), 0)
    qcol = lax.broadcasted_iota(jnp.int32, (blk, blk), 1)

    def tile(j, masked):
        ks = pl.ds(pl.multiple_of(j * blk, blk), blk)
        sts = [lax.dot_general(k_ref[ks, h * aw:(h + 1) * aw], q_ref[:, h * aw:(h + 1) * aw],
                               (((1,), (1,)), ((), ())), preferred_element_type=F32)
               for h in range(FOX_HEADS)]
        for h in range(FOX_HEADS):
            st = sts[h]
            d = anc_ref[(b * nblk + i) * SUBLANES + h] - anc_ref[(b * nblk + j) * SUBLANES + h]
            if masked:
                st = jnp.where(krow <= qcol, st, NEG_INF)
            m_prev = m_ref[h]
            m_new = jnp.maximum(m_prev, jnp.max(st, axis=0, keepdims=True) + d)
            alpha = jnp.exp2(m_prev - m_new)
            p = jnp.exp2(st - (m_new - d))
            l_ref[h] = alpha * l_ref[h] + jnp.sum(p, axis=0, keepdims=True)
            pv = jnp.dot(vt_ref[j, h * HEAD_DIM:(h + 1) * HEAD_DIM, :], p.astype(BF16),
                         preferred_element_type=F32)
            acc_ref[h] = alpha * acc_ref[h] + pv
            m_ref[h] = m_new

    def body(j, carry):
        tile(j, False)
        return carry

    lax.fori_loop(0, i, body, 0)
    tile(i, True)
    for h in range(FOX_HEADS):
        o_ref[:, h * HEAD_DIM:(h + 1) * HEAD_DIM] = (acc_ref[h] / l_ref[h]).T.astype(o_ref.dtype)


def _fox(anchors, qa, ka, vt, *, batch, seq):
    t = qa.shape[0]
    blk = FOX_BLK
    nblk = seq // blk
    return pl.pallas_call(
        functools.partial(_fox_kernel, nblk=nblk),
        out_shape=jax.ShapeDtypeStruct((t, FOX_W), BF16),
        grid=(batch, nblk),
        in_specs=[
            pl.BlockSpec(memory_space=pltpu.SMEM),
            pl.BlockSpec((blk, 2 * FOX_W), lambda b, i: (b * nblk + i, 0)),
            pl.BlockSpec((seq, 2 * FOX_W), lambda b, i: (b, 0)),
            pl.BlockSpec((nblk, FOX_W, blk), lambda b, i: (b, 0, 0)),
        ],
        out_specs=pl.BlockSpec((blk, FOX_W), lambda b, i: (b * nblk + i, 0)),
        scratch_shapes=[
            pltpu.VMEM((FOX_HEADS, 1, blk), F32),
            pltpu.VMEM((FOX_HEADS, 1, blk), F32),
            pltpu.VMEM((FOX_HEADS, HEAD_DIM, blk), F32),
        ],
        compiler_params=_cparams(("parallel", "arbitrary")),
        name="fox",
    )(anchors, qa, ka, vt)


def _swa_kernel(q_ref, k_ref, v_ref, sink_ref, slope_ref, o_ref):
    qi = pl.program_id(1)
    tq = q_ref.shape[0]
    w = WINDOW
    row = lax.broadcasted_iota(jnp.int32, (w, 2 * w), 0)
    col = lax.broadcasted_iota(jnp.int32, (w, 2 * w), 1)

    for r in range(tq // w):
        n = qi * (tq // w) + r
        kstart = pl.multiple_of(jnp.maximum(n - 1, 0) * w, w)
        dist = (n * w - kstart) + row - col
        valid = (dist >= 0) & (dist < w)
        dist_f = dist.astype(F32)
        for h in range(SWA_HEADS):
            g = h // SWA_GROUP
            q = q_ref[r * w:(r + 1) * w, h * HEAD_DIM:(h + 1) * HEAD_DIM]
            k = k_ref[pl.ds(kstart, 2 * w), g * HEAD_DIM:(g + 1) * HEAD_DIM]
            v = v_ref[pl.ds(kstart, 2 * w), g * HEAD_DIM:(g + 1) * HEAD_DIM]
            s = lax.dot_general(q, k, (((1,), (1,)), ((), ())), preferred_element_type=F32)
            s = s - slope_ref[h] * dist_f
            s = jnp.where(valid, s, NEG_INF)
            sink = sink_ref[h]
            m = jnp.maximum(jnp.max(s, axis=-1, keepdims=True), sink)
            p = jnp.exp(s - m)
            denom = jnp.sum(p, axis=-1, keepdims=True) + jnp.exp(sink - m)
            o = jnp.dot(p.astype(BF16), v, preferred_element_type=F32)
            o_ref[r * w:(r + 1) * w, h * HEAD_DIM:(h + 1) * HEAD_DIM] = (o / denom).astype(o_ref.dtype)


def _swa(rest, sinks, slopes, *, batch, seq, tq=512):
    t = rest.shape[0]
    nq = seq // tq
    smem = pl.BlockSpec(memory_space=pltpu.SMEM)
    return pl.pallas_call(
        _swa_kernel,
        out_shape=jax.ShapeDtypeStruct((t, SWA_Q_W), BF16),
        grid=(batch, nq),
        in_specs=[
            pl.BlockSpec((tq, SWA_Q_W), lambda b, i: (b * nq + i, REST_SQ // SWA_Q_W)),
            pl.BlockSpec((seq, SWA_KV_W), lambda b, i: (b, REST_SK // SWA_KV_W)),
            pl.BlockSpec((seq, SWA_KV_W), lambda b, i: (b, REST_SV // SWA_KV_W)),
            smem, smem,
        ],
        out_specs=pl.BlockSpec((tq, SWA_Q_W), lambda b, i: (b * nq + i, 0)),
        compiler_params=_cparams(("parallel", "arbitrary")),
        name="swa",
    )(rest, rest, rest, sinks, slopes)


def _mem_kv_kernel(mem_ref, g_ref, wk_ref, wv_ref, kg_ref, mk_ref, mv_ref):
    mn = _rms(mem_ref[...], g_ref[...]).astype(BF16)
    k = jnp.dot(mn, wk_ref[...], preferred_element_type=F32)
    v = jnp.dot(mn, wv_ref[...], preferred_element_type=F32)
    for h in range(MEM_HEADS):
        hs = slice(h * HEAD_DIM, (h + 1) * HEAD_DIM)
        mk_ref[:, hs] = _rms(k[:, hs], kg_ref[...]).astype(BF16)
    mv_ref[...] = v.astype(BF16)


def _mem_kv(mem2d, gain, wk, wv, kgain, *, batch, mem_len):
    d = mem2d.shape[1]
    full = lambda shape: pl.BlockSpec(shape, lambda b: (0, 0))
    out = jax.ShapeDtypeStruct((batch * mem_len, MEM_W), BF16)
    return pl.pallas_call(
        _mem_kv_kernel,
        out_shape=(out, out),
        grid=(batch,),
        in_specs=[
            pl.BlockSpec((mem_len, d), lambda b: (b, 0)),
            full((1, d)), full((d, MEM_W)), full((d, MEM_W)), full((1, HEAD_DIM)),
        ],
        out_specs=(pl.BlockSpec((mem_len, MEM_W), lambda b: (b, 0)),
                   pl.BlockSpec((mem_len, MEM_W), lambda b: (b, 0))),
        compiler_params=_cparams(("parallel",)),
        name="mem_kv",
    )(mem2d, gain, wk, wv, kgain)


def _mem_attn_kernel(q_ref, mk_ref, mv_ref, o_ref):
    for h in range(MEM_HEADS):
        hs = slice(h * HEAD_DIM, (h + 1) * HEAD_DIM)
        s = lax.dot_general(q_ref[:, hs], mk_ref[:, hs], (((1,), (1,)), ((), ())),
                            preferred_element_type=F32)
        m = jnp.max(s, axis=-1, keepdims=True)
        p = jnp.exp(s - m)
        denom = jnp.sum(p, axis=-1, keepdims=True)
        o = jnp.dot(p.astype(BF16), mv_ref[:, hs], preferred_element_type=F32)
        o_ref[:, hs] = (o / denom).astype(o_ref.dtype)


def _mem_attn(rest, mk, mv, *, batch, seq, mem_len, tq=512):
    t = rest.shape[0]
    nq = seq // tq
    return pl.pallas_call(
        _mem_attn_kernel,
        out_shape=jax.ShapeDtypeStruct((t, MEM_W), BF16),
        grid=(batch, nq),
        in_specs=[
            pl.BlockSpec((tq, MEM_W), lambda b, i: (b * nq + i, REST_MQ // MEM_W)),
            pl.BlockSpec((mem_len, MEM_W), lambda b, i: (b, 0)),
            pl.BlockSpec((mem_len, MEM_W), lambda b, i: (b, 0)),
        ],
        out_specs=pl.BlockSpec((tq, MEM_W), lambda b, i: (b * nq + i, 0)),
        compiler_params=_cparams(("parallel", "arbitrary")),
        name="mem_attn",
    )(rest, mk, mv)


def _out_proj_kernel(x_ref, a_ref, b_ref, c_ref, w_ref, o_ref):
    acc = jnp.dot(a_ref[...], w_ref[0:FOX_W, :], preferred_element_type=F32)
    acc += jnp.dot(b_ref[...], w_ref[FOX_W:FOX_W + SWA_Q_W, :], preferred_element_type=F32)
    acc += jnp.dot(c_ref[...], w_ref[FOX_W + SWA_Q_W:, :], preferred_element_type=F32)
    o_ref[...] = x_ref[...] + acc


def _out_proj(x, a, b, c, w, *, tm=512):
    t, d = x.shape
    return pl.pallas_call(
        _out_proj_kernel,
        out_shape=jax.ShapeDtypeStruct((t, d), F32),
        grid=(t // tm,),
        in_specs=[
            pl.BlockSpec((tm, d), lambda i: (i, 0)),
            pl.BlockSpec((tm, FOX_W), lambda i: (i, 0)),
            pl.BlockSpec((tm, SWA_Q_W), lambda i: (i, 0)),
            pl.BlockSpec((tm, MEM_W), lambda i: (i, 0)),
            pl.BlockSpec(w.shape, lambda i: (0, 0)),
        ],
        out_specs=pl.BlockSpec((tm, d), lambda i: (i, 0)),
        compiler_params=_cparams(("parallel",)),
        name="out_proj",
    )(x, a, b, c, w)


def _alibi_slopes(n):
    return jnp.asarray([2.0 ** (-8.0 * i / n) for i in range(1, n + 1)], dtype=F32)


def _pack_w_in(w_in):
    cuts = [0, FOX_W, 2 * FOX_W, 3 * FOX_W, 3 * FOX_W + FOX_HEADS]
    fq, fk, fv, fl = (w_in[:, cuts[i]:cuts[i + 1]] for i in range(4))
    o = cuts[4]
    sq = w_in[:, o:o + SWA_Q_W]
    sk = w_in[:, o + SWA_Q_W:o + SWA_Q_W + SWA_KV_W]
    sv = w_in[:, o + SWA_Q_W + SWA_KV_W:o + SWA_Q_W + 2 * SWA_KV_W]
    mq = w_in[:, o + SWA_Q_W + 2 * SWA_KV_W:]
    w_main = jnp.concatenate([fq, fk, fv, sq, sk, mq, sv], axis=1).astype(BF16)
    w_f = jnp.pad(fl, ((0, 0), (0, LANES - FOX_HEADS))).astype(BF16)
    return w_main, w_f


def kernel(x, mem, ffn1_norm, ffn1_gate, ffn1_up, ffn1_down, mix_norm, mem_norm, w_in, forget_bias, w_mem_k, w_mem_v, fox_q_gain, fox_k_gain, swa_q_gain, swa_k_gain, swa_sinks, mem_q_gain, mem_k_gain, w_out, ffn2_norm, ffn2_gate, ffn2_up, ffn2_down):
    batch, seq, d = x.shape
    mem_len = mem.shape[1]
    depth = w_in.shape[0]
    scale = HEAD_DIM ** -0.5
    slopes = _alibi_slopes(SWA_HEADS)
    x2 = x.reshape(batch * seq, d)
    mem2 = mem.reshape(batch * mem_len, d)
    zeros = jnp.zeros((HEAD_DIM,), F32)

    for l in range(depth):
        x2 = _ffn(x2, ffn1_norm[l][None], ffn1_gate[l].astype(BF16), ffn1_up[l].astype(BF16),
                  ffn1_down[l].astype(BF16))

        w_main, w_f = _pack_w_in(w_in[l])
        head_gain = jnp.stack([fox_q_gain[l] * (scale * LOG2E), fox_k_gain[l], swa_q_gain[l] * scale,
                               swa_k_gain[l], mem_q_gain[l] * scale, zeros, zeros, zeros])
        fbias = jnp.pad(forget_bias[l], (0, LANES - FOX_HEADS))[None]
        qa, ka, vt, rest, anc = _in_proj(x2, mix_norm[l][None], w_main, w_f, head_gain, fbias, seq=seq)

        n_sub = (x2.shape[0] // FOX_BLK) // anc.shape[0]
        anchors = anc[:, :n_sub, :SUBLANES].reshape(-1)
        out_a = _fox(anchors, qa, ka, vt, batch=batch, seq=seq)
        out_b = _swa(rest, swa_sinks[l], slopes, batch=batch, seq=seq)
        mk, mv = _mem_kv(mem2, mem_norm[l][None], w_mem_k[l].astype(BF16), w_mem_v[l].astype(BF16),
                         mem_k_gain[l][None], batch=batch, mem_len=mem_len)
        out_c = _mem_attn(rest, mk, mv, batch=batch, seq=seq, mem_len=mem_len)

        x2 = _out_proj(x2, out_a, out_b, out_c, w_out[l].astype(BF16))
        x2 = _ffn(x2, ffn2_norm[l][None], ffn2_gate[l].astype(BF16), ffn2_up[l].astype(BF16),
                  ffn2_down[l].astype(BF16))
    return x2.reshape(batch, seq, d)
```

```python
import functools

import jax
import jax.numpy as jnp
from jax import lax
from jax.experimental import pallas as pl
from jax.experimental.pallas import tpu as pltpu

F32 = jnp.float32
BF16 = jnp.bfloat16

HEAD_DIM = 128
FOX_HEADS = 6
SWA_HEADS = 6
SWA_KV_HEADS = 2
SWA_GROUP = SWA_HEADS // SWA_KV_HEADS
MEM_HEADS = 4
WINDOW = 128
EPS = 1e-6
NEG_INF = -1e30
LOG2E = 1.4426950408889634
FOX_BLK = 256

FOX_W = FOX_HEADS * HEAD_DIM
SWA_Q_W = SWA_HEADS * HEAD_DIM
SWA_KV_W = SWA_KV_HEADS * HEAD_DIM
MEM_W = MEM_HEADS * HEAD_DIM

PROJ_W = 3 * FOX_W + SWA_Q_W + 2 * SWA_KV_W + MEM_W
REST_W = SWA_Q_W + 2 * SWA_KV_W + MEM_W
REST_SQ, REST_SK, REST_MQ, REST_SV = 0, SWA_Q_W, SWA_Q_W + SWA_KV_W, SWA_Q_W + SWA_KV_W + MEM_W
GAIN_FQ, GAIN_FK, GAIN_SQ, GAIN_SK, GAIN_MQ = range(5)

LANES = 128
SUBLANES = 8
MXU_N = 256

VMEM_LIMIT = 58 * 1024 * 1024


def _cparams(sem):
    return pltpu.CompilerParams(dimension_semantics=sem, vmem_limit_bytes=VMEM_LIMIT)


def _rms(x, gain):
    ms = jnp.mean(x * x, axis=-1, keepdims=True)
    return x * lax.rsqrt(ms + EPS) * gain


def _ffn_kernel(x_ref, g_ref, wg_ref, wu_ref, wd_ref, o_ref, xn_ref, *, n_chunk):
    j = pl.program_id(1)

    @pl.when(j == 0)
    def _():
        x = x_ref[...]
        xn_ref[...] = _rms(x, g_ref[...]).astype(BF16)
        o_ref[...] = x

    xn = xn_ref[...]
    gate = jnp.dot(xn, wg_ref[...], preferred_element_type=F32)
    up = jnp.dot(xn, wu_ref[...], preferred_element_type=F32)
    h = (gate * jax.nn.sigmoid(gate) * (0.5 * up)).astype(BF16)
    d_model = o_ref.shape[1]
    cw = d_model // n_chunk
    for c in range(n_chunk):
        cs = slice(c * cw, (c + 1) * cw)
        o_ref[:, cs] += jnp.dot(h, wd_ref[:, cs], preferred_element_type=F32)


def _ffn(x, gain, wg, wu, wd, *, tm=1024, tf=512):
    t, d = x.shape
    dff = wg.shape[1]
    return pl.pallas_call(
        functools.partial(_ffn_kernel, n_chunk=4),
        out_shape=jax.ShapeDtypeStruct((t, d), F32),
        grid=(t // tm, dff // tf),
        in_specs=[
            pl.BlockSpec((tm, d), lambda i, j: (i, 0)),
            pl.BlockSpec((1, d), lambda i, j: (0, 0)),
            pl.BlockSpec((d, tf), lambda i, j: (0, j)),
            pl.BlockSpec((d, tf), lambda i, j: (0, j)),
            pl.BlockSpec((tf, d), lambda i, j: (j, 0)),
        ],
        out_specs=pl.BlockSpec((tm, d), lambda i, j: (i, 0)),
        scratch_shapes=[pltpu.VMEM((tm, d), BF16)],
        compiler_params=_cparams(("parallel", "arbitrary")),
        name="ffn",
    )(x, gain, wg, wu, wd)


def _in_proj_kernel(x_ref, g_ref, w_ref, wf_ref, hg_ref, fb_ref,
                    qa_ref, ka_ref, vt_ref, rest_ref, anc_ref, carry_ref, *, blocks_per_seq):
    i = pl.program_id(0)
    tm = x_ref.shape[0]
    hn = _rms(x_ref[...], g_ref[...]).astype(BF16)

    logit = jnp.dot(hn, wf_ref[...], preferred_element_type=F32)
    c = jax.nn.log_sigmoid(logit + fb_ref[...]) * LOG2E
    row = lax.broadcasted_iota(jnp.int32, c.shape, 0) % FOX_BLK
    sh = 1
    while sh < FOX_BLK:
        c = c + jnp.where(row >= sh, pltpu.roll(c, sh, 0), 0.0)
        sh *= 2

    @pl.when(i % blocks_per_seq == 0)
    def _():
        carry_ref[...] = jnp.zeros_like(carry_ref)

    anchor = carry_ref[0:1, :]
    anc_ref[...] = jnp.zeros_like(anc_ref)
    for r in range(tm // FOX_BLK):
        anc_ref[0, r:r + 1, :] = anchor
        anchor = anchor + c[(r + 1) * FOX_BLK - 1:(r + 1) * FOX_BLK, :]
    carry_ref[0:1, :] = anchor

    lane = lax.broadcasted_iota(jnp.int32, c.shape, 1)
    for h in range(FOX_HEADS):
        ch = c[:, h:h + 1]
        hi = ch.astype(BF16).astype(F32)
        lo = ch - hi
        aq = jnp.where(lane == 0, hi, jnp.where(lane == 1, lo, jnp.where(lane < 4, 1.0, 0.0)))
        ak = jnp.where(lane < 2, 1.0, jnp.where(lane == 2, -hi, jnp.where(lane == 3, -lo, 0.0)))
        qa_ref[:, (2 * h + 1) * HEAD_DIM:(2 * h + 2) * HEAD_DIM] = aq.astype(BF16)
        ka_ref[:, (2 * h + 1) * HEAD_DIM:(2 * h + 2) * HEAD_DIM] = ak.astype(BF16)

    def normed(ph, row_id):
        return _rms(ph, hg_ref[row_id:row_id + 1, :]).astype(BF16)

    heads_per_chunk = MXU_N // HEAD_DIM
    for ck in range(PROJ_W // MXU_N):
        p = jnp.dot(hn, w_ref[:, ck * MXU_N:(ck + 1) * MXU_N], preferred_element_type=F32)
        for e in range(heads_per_chunk):
            col = ck * MXU_N + e * HEAD_DIM
            ph = p[:, e * HEAD_DIM:(e + 1) * HEAD_DIM]
            if col < FOX_W:
                h = col // HEAD_DIM
                qa_ref[:, 2 * h * HEAD_DIM:(2 * h + 1) * HEAD_DIM] = normed(ph, GAIN_FQ)
            elif col < 2 * FOX_W:
                h = (col - FOX_W) // HEAD_DIM
                ka_ref[:, 2 * h * HEAD_DIM:(2 * h + 1) * HEAD_DIM] = normed(ph, GAIN_FK)
            elif col < 3 * FOX_W:
                h = (col - 2 * FOX_W) // HEAD_DIM
                for r in range(tm // FOX_BLK):
                    vt_ref[r, h * HEAD_DIM:(h + 1) * HEAD_DIM, :] = (
                        ph[r * FOX_BLK:(r + 1) * FOX_BLK, :].T.astype(BF16))
            else:
                rc = col - 3 * FOX_W
                if rc < REST_SK:
                    val = normed(ph, GAIN_SQ)
                elif rc < REST_MQ:
                    val = normed(ph, GAIN_SK)
                elif rc < REST_SV:
                    val = normed(ph, GAIN_MQ)
                else:
                    val = ph.astype(BF16)
                rest_ref[:, rc:rc + HEAD_DIM] = val


def _in_proj(x, gain, w, wf, head_gain, fbias, *, seq, tm=512):
    t, d = x.shape
    assert seq % tm == 0 and tm % FOX_BLK == 0 and tm // FOX_BLK <= SUBLANES
    const = lambda shape: pl.BlockSpec(shape, lambda i: (0, 0), pipeline_mode=pl.Buffered(1))
    return pl.pallas_call(
        functools.partial(_in_proj_kernel, blocks_per_seq=seq // tm),
        out_shape=(
            jax.ShapeDtypeStruct((t, 2 * FOX_W), BF16),
            jax.ShapeDtypeStruct((t, 2 * FOX_W), BF16),
            jax.ShapeDtypeStruct((t // FOX_BLK, FOX_W, FOX_BLK), BF16),
            jax.ShapeDtypeStruct((t, REST_W), BF16),
            jax.ShapeDtypeStruct((t // tm, SUBLANES, LANES), F32),
        ),
        grid=(t // tm,),
        in_specs=[
            pl.BlockSpec((tm, d), lambda i: (i, 0)),
            const((1, d)),
            const((d, PROJ_W)),
            const((d, LANES)),
            const((SUBLANES, HEAD_DIM)),
            const((1, LANES)),
        ],
        out_specs=(
            pl.BlockSpec((tm, 2 * FOX_W), lambda i: (i, 0)),
            pl.BlockSpec((tm, 2 * FOX_W), lambda i: (i, 0)),
            pl.BlockSpec((tm // FOX_BLK, FOX_W, FOX_BLK), lambda i: (i, 0, 0)),
            pl.BlockSpec((tm, REST_W), lambda i: (i, 0)),
            pl.BlockSpec((1, SUBLANES, LANES), lambda i: (i, 0, 0)),
        ),
        scratch_shapes=[pltpu.VMEM((SUBLANES, LANES), F32)],
        compiler_params=_cparams(("arbitrary",)),
        name="in_proj",
    )(x, gain, w, wf, head_gain, fbias)


def _fox_kernel(anc_ref, q_ref, k_ref, vt_ref, o_ref, m_ref, l_ref, acc_ref, st_ref, *, nblk):
    b = pl.program_id(0)
    i = pl.program_id(1)
    blk = q_ref.shape[0]
    aw = 2 * HEAD_DIM
    m_ref[...] = jnp.full(m_ref.shape, NEG_INF, F32)
    l_ref[...] = jnp.zeros(l_ref.shape, F32)
    acc_ref[...] = jnp.zeros(acc_ref.shape, F32)
    krow = lax.broadcasted_iota(jnp.int32, (blk, blk), 0)
    qcol = lax.broadcasted_iota(jnp.int32, (blk, blk), 1)

    def scores(j, slot):
        ks = pl.ds(pl.multiple_of(j * blk, blk), blk)
        for h in range(FOX_HEADS):
            st_ref[slot, h] = lax.dot_general(
                k_ref[ks, h * aw:(h + 1) * aw], q_ref[:, h * aw:(h + 1) * aw],
                (((1,), (1,)), ((), ())), preferred_element_type=F32)

    def softmax_pv(j, slot, masked):
        for h in range(FOX_HEADS):
            st = st_ref[slot, h]
            d = anc_ref[(b * nblk + i) * SUBLANES + h] - anc_ref[(b * nblk + j) * SUBLANES + h]
            if masked:
                st = jnp.where(krow <= qcol, st, NEG_INF)
            m_prev = m_ref[h]
            m_new = jnp.maximum(m_prev, jnp.max(st, axis=0, keepdims=True) + d)
            alpha = jnp.exp2(m_prev - m_new)
            p = jnp.exp2(st - (m_new - d))
            l_ref[h] = alpha * l_ref[h] + jnp.sum(p, axis=0, keepdims=True)
            pv = jnp.dot(vt_ref[j, h * HEAD_DIM:(h + 1) * HEAD_DIM, :], p.astype(BF16),
                         preferred_element_type=F32)
            acc_ref[h] = alpha * acc_ref[h] + pv
            m_ref[h] = m_new

    def body(t, carry):
        j = 2 * t
        scores(j + 1, 1)
        softmax_pv(j, 0, False)
        scores(j + 2, 0)
        softmax_pv(j + 1, 1, False)
        return carry

    scores(0, 0)
    lax.fori_loop(0, i // 2, body, 0)

    @pl.when(i % 2 == 0)
    def _():
        softmax_pv(i, 0, True)

    @pl.when(i % 2 == 1)
    def _():
        scores(i, 1)
        softmax_pv(i - 1, 0, False)
        softmax_pv(i, 1, True)

    for h in range(FOX_HEADS):
        o_ref[:, h * HEAD_DIM:(h + 1) * HEAD_DIM] = (acc_ref[h] / l_ref[h]).T.astype(o_ref.dtype)


def _fox(anchors, qa, ka, vt, *, batch, seq):
    t = qa.shape[0]
    blk = FOX_BLK
    nblk = seq // blk
    return pl.pallas_call(
        functools.partial(_fox_kernel, nblk=nblk),
        out_shape=jax.ShapeDtypeStruct((t, FOX_W), BF16),
        grid=(batch, nblk),
        in_specs=[
            pl.BlockSpec(memory_space=pltpu.SMEM),
            pl.BlockSpec((blk, 2 * FOX_W), lambda b, i: (b * nblk + i, 0)),
            pl.BlockSpec((seq, 2 * FOX_W), lambda b, i: (b, 0)),
            pl.BlockSpec((nblk, FOX_W, blk), lambda b, i: (b, 0, 0)),
        ],
        out_specs=pl.BlockSpec((blk, FOX_W), lambda b, i: (b * nblk + i, 0)),
        scratch_shapes=[
            pltpu.VMEM((FOX_HEADS, 1, blk), F32),
            pltpu.VMEM((FOX_HEADS, 1, blk), F32),
            pltpu.VMEM((FOX_HEADS, HEAD_DIM, blk), F32),
            pltpu.VMEM((2, FOX_HEADS, blk, blk), F32),
        ],
        compiler_params=_cparams(("parallel", "arbitrary")),
        name="fox",
    )(anchors, qa, ka, vt)


def _swa_kernel(q_ref, k_ref, v_ref, sink_ref, slope_ref, o_ref):
    qi = pl.program_id(1)
    tq = q_ref.shape[0]
    w = WINDOW
    row = lax.broadcasted_iota(jnp.int32, (w, 2 * w), 0)
    col = lax.broadcasted_iota(jnp.int32, (w, 2 * w), 1)

    for r in range(tq // w):
        n = qi * (tq // w) + r
        kstart = pl.multiple_of(jnp.maximum(n - 1, 0) * w, w)
        dist = (n * w - kstart) + row - col
        valid = (dist >= 0) & (dist < w)
        dist_f = dist.astype(F32)
        for h in range(SWA_HEADS):
            g = h // SWA_GROUP
            q = q_ref[r * w:(r + 1) * w, h * HEAD_DIM:(h + 1) * HEAD_DIM]
            k = k_ref[pl.ds(kstart, 2 * w), g * HEAD_DIM:(g + 1) * HEAD_DIM]
            v = v_ref[pl.ds(kstart, 2 * w), g * HEAD_DIM:(g + 1) * HEAD_DIM]
            s = lax.dot_general(q, k, (((1,), (1,)), ((), ())), preferred_element_type=F32)
            s = s - slope_ref[h] * dist_f
            s = jnp.where(valid, s, NEG_INF)
            sink = sink_ref[h]
            m = jnp.maximum(jnp.max(s, axis=-1, keepdims=True), sink)
            p = jnp.exp(s - m)
            denom = jnp.sum(p, axis=-1, keepdims=True) + jnp.exp(sink - m)
            o = jnp.dot(p.astype(BF16), v, preferred_element_type=F32)
            o_ref[r * w:(r + 1) * w, h * HEAD_DIM:(h + 1) * HEAD_DIM] = (o / denom).astype(o_ref.dtype)


def _swa(rest, sinks, slopes, *, batch, seq, tq=512):
    t = rest.shape[0]
    nq = seq // tq
    smem = pl.BlockSpec(memory_space=pltpu.SMEM)
    return pl.pallas_call(
        _swa_kernel,
        out_shape=jax.ShapeDtypeStruct((t, SWA_Q_W), BF16),
        grid=(batch, nq),
        in_specs=[
            pl.BlockSpec((tq, SWA_Q_W), lambda b, i: (b * nq + i, REST_SQ // SWA_Q_W)),
            pl.BlockSpec((seq, SWA_KV_W), lambda b, i: (b, REST_SK // SWA_KV_W)),
            pl.BlockSpec((seq, SWA_KV_W), lambda b, i: (b, REST_SV // SWA_KV_W)),
            smem, smem,
        ],
        out_specs=pl.BlockSpec((tq, SWA_Q_W), lambda b, i: (b * nq + i, 0)),
        compiler_params=_cparams(("parallel", "arbitrary")),
        name="swa",
    )(rest, rest, rest, sinks, slopes)


def _mem_kv_kernel(mem_ref, g_ref, wk_ref, wv_ref, kg_ref, mk_ref, mv_ref):
    mn = _rms(mem_ref[...], g_ref[...]).astype(BF16)
    k = jnp.dot(mn, wk_ref[...], preferred_element_type=F32)
    v = jnp.dot(mn, wv_ref[...], preferred_element_type=F32)
    for h in range(MEM_HEADS):
        hs = slice(h * HEAD_DIM, (h + 1) * HEAD_DIM)
        mk_ref[:, hs] = _rms(k[:, hs], kg_ref[...]).astype(BF16)
    mv_ref[...] = v.astype(BF16)


def _mem_kv(mem2d, gain, wk, wv, kgain, *, batch, mem_len):
    d = mem2d.shape[1]
    full = lambda shape: pl.BlockSpec(shape, lambda b: (0, 0))
    out = jax.ShapeDtypeStruct((batch * mem_len, MEM_W), BF16)
    return pl.pallas_call(
        _mem_kv_kernel,
        out_shape=(out, out),
        grid=(batch,),
        in_specs=[
            pl.BlockSpec((mem_len, d), lambda b: (b, 0)),
            full((1, d)), full((d, MEM_W)), full((d, MEM_W)), full((1, HEAD_DIM)),
        ],
        out_specs=(pl.BlockSpec((mem_len, MEM_W), lambda b: (b, 0)),
                   pl.BlockSpec((mem_len, MEM_W), lambda b: (b, 0))),
        compiler_params=_cparams(("parallel",)),
        name="mem_kv",
    )(mem2d, gain, wk, wv, kgain)


def _mem_attn_kernel(q_ref, mk_ref, mv_ref, o_ref):
    for h in range(MEM_HEADS):
        hs = slice(h * HEAD_DIM, (h + 1) * HEAD_DIM)
        s = lax.dot_general(q_ref[:, hs], mk_ref[:, hs], (((1,), (1,)), ((), ())),
                            preferred_element_type=F32)
        m = jnp.max(s, axis=-1, keepdims=True)
        p = jnp.exp(s - m)
        denom = jnp.sum(p, axis=-1, keepdims=True)
        o = jnp.dot(p.astype(BF16), mv_ref[:, hs], preferred_element_type=F32)
        o_ref[:, hs] = (o / denom).astype(o_ref.dtype)


def _mem_attn(rest, mk, mv, *, batch, seq, mem_len, tq=512):
    t = rest.shape[0]
    nq = seq // tq
    return pl.pallas_call(
        _mem_attn_kernel,
        out_shape=jax.ShapeDtypeStruct((t, MEM_W), BF16),
        grid=(batch, nq),
        in_specs=[
            pl.BlockSpec((tq, MEM_W), lambda b, i: (b * nq + i, REST_MQ // MEM_W)),
            pl.BlockSpec((mem_len, MEM_W), lambda b, i: (b, 0)),
            pl.BlockSpec((mem_len, MEM_W), lambda b, i: (b, 0)),
        ],
        out_specs=pl.BlockSpec((tq, MEM_W), lambda b, i: (b * nq + i, 0)),
        compiler_params=_cparams(("parallel", "arbitrary")),
        name="mem_attn",
    )(rest, mk, mv)


def _out_proj_kernel(x_ref, a_ref, b_ref, c_ref, w_ref, o_ref):
    acc = jnp.dot(a_ref[...], w_ref[0:FOX_W, :], preferred_element_type=F32)
    acc += jnp.dot(b_ref[...], w_ref[FOX_W:FOX_W + SWA_Q_W, :], preferred_element_type=F32)
    acc += jnp.dot(c_ref[...], w_ref[FOX_W + SWA_Q_W:, :], preferred_element_type=F32)
    o_ref[...] = x_ref[...] + acc


def _out_proj(x, a, b, c, w, *, tm=512):
    t, d = x.shape
    return pl.pallas_call(
        _out_proj_kernel,
        out_shape=jax.ShapeDtypeStruct((t, d), F32),
        grid=(t // tm,),
        in_specs=[
            pl.BlockSpec((tm, d), lambda i: (i, 0)),
            pl.BlockSpec((tm, FOX_W), lambda i: (i, 0)),
            pl.BlockSpec((tm, SWA_Q_W), lambda i: (i, 0)),
            pl.BlockSpec((tm, MEM_W), lambda i: (i, 0)),
            pl.BlockSpec(w.shape, lambda i: (0, 0)),
        ],
        out_specs=pl.BlockSpec((tm, d), lambda i: (i, 0)),
        compiler_params=_cparams(("parallel",)),
        name="out_proj",
    )(x, a, b, c, w)


def _alibi_slopes(n):
    return jnp.asarray([2.0 ** (-8.0 * i / n) for i in range(1, n + 1)], dtype=F32)


def _pack_w_in(w_in):
    cuts = [0, FOX_W, 2 * FOX_W, 3 * FOX_W, 3 * FOX_W + FOX_HEADS]
    fq, fk, fv, fl = (w_in[:, cuts[i]:cuts[i + 1]] for i in range(4))
    o = cuts[4]
    sq = w_in[:, o:o + SWA_Q_W]
    sk = w_in[:, o + SWA_Q_W:o + SWA_Q_W + SWA_KV_W]
    sv = w_in[:, o + SWA_Q_W + SWA_KV_W:o + SWA_Q_W + 2 * SWA_KV_W]
    mq = w_in[:, o + SWA_Q_W + 2 * SWA_KV_W:]
    w_main = jnp.concatenate([fq, fk, fv, sq, sk, mq, sv], axis=1).astype(BF16)
    w_f = jnp.pad(fl, ((0, 0), (0, LANES - FOX_HEADS))).astype(BF16)
    return w_main, w_f


def kernel(x, mem, ffn1_norm, ffn1_gate, ffn1_up, ffn1_down, mix_norm, mem_norm, w_in, forget_bias, w_mem_k, w_mem_v, fox_q_gain, fox_k_gain, swa_q_gain, swa_k_gain, swa_sinks, mem_q_gain, mem_k_gain, w_out, ffn2_norm, ffn2_gate, ffn2_up, ffn2_down):
    batch, seq, d = x.shape
    mem_len = mem.shape[1]
    depth = w_in.shape[0]
    scale = HEAD_DIM ** -0.5
    slopes = _alibi_slopes(SWA_HEADS)
    x2 = x.reshape(batch * seq, d)
    mem2 = mem.reshape(batch * mem_len, d)
    zeros = jnp.zeros((HEAD_DIM,), F32)

    for l in range(depth):
        x2 = _ffn(x2, ffn1_norm[l][None], ffn1_gate[l].astype(BF16), ffn1_up[l].astype(BF16),
                  ffn1_down[l].astype(BF16))

        w_main, w_f = _pack_w_in(w_in[l])
        head_gain = jnp.stack([fox_q_gain[l] * (scale * LOG2E), fox_k_gain[l], swa_q_gain[l] * scale,
                               swa_k_gain[l], mem_q_gain[l] * scale, zeros, zeros, zeros])
        fbias = jnp.pad(forget_bias[l], (0, LANES - FOX_HEADS))[None]
        qa, ka, vt, rest, anc = _in_proj(x2, mix_norm[l][None], w_main, w_f, head_gain, fbias, seq=seq)

        n_sub = (x2.shape[0] // FOX_BLK) // anc.shape[0]
        anchors = anc[:, :n_sub, :SUBLANES].reshape(-1)
        out_a = _fox(anchors, qa, ka, vt, batch=batch, seq=seq)
        out_b = _swa(rest, swa_sinks[l], slopes, batch=batch, seq=seq)
        mk, mv = _mem_kv(mem2, mem_norm[l][None], w_mem_k[l].astype(BF16), w_mem_v[l].astype(BF16),
                         mem_k_gain[l][None], batch=batch, mem_len=mem_len)
        out_c = _mem_attn(rest, mk, mv, batch=batch, seq=seq, mem_len=mem_len)

        x2 = _out_proj(x2, out_a, out_b, out_c, w_out[l].astype(BF16))
        x2 = _ffn(x2, ffn2_norm[l][None], ffn2_gate[l].astype(BF16), ffn2_up[l].astype(BF16),
                  ffn2_down[l].astype(BF16))
    return x2.reshape(batch, seq, d)
```

```python
import functools

import jax
import jax.numpy as jnp
from jax import lax
from jax.experimental import pallas as pl
from jax.experimental.pallas import tpu as pltpu

F32 = jnp.float32
BF16 = jnp.bfloat16

HEAD_DIM = 128
FOX_HEADS = 6
SWA_HEADS = 6
SWA_KV_HEADS = 2
SWA_GROUP = SWA_HEADS // SWA_KV_HEADS
MEM_HEADS = 4
WINDOW = 128
EPS = 1e-6
NEG_INF = -1e30
LOG2E = 1.4426950408889634
FOX_BLK = 256

FOX_W = FOX_HEADS * HEAD_DIM
SWA_Q_W = SWA_HEADS * HEAD_DIM
SWA_KV_W = SWA_KV_HEADS * HEAD_DIM
MEM_W = MEM_HEADS * HEAD_DIM

PROJ_W = 3 * FOX_W + SWA_Q_W + 2 * SWA_KV_W + MEM_W
REST_W = SWA_Q_W + 2 * SWA_KV_W + MEM_W
REST_SQ, REST_SK, REST_MQ, REST_SV = 0, SWA_Q_W, SWA_Q_W + SWA_KV_W, SWA_Q_W + SWA_KV_W + MEM_W
GAIN_FQ, GAIN_FK, GAIN_SQ, GAIN_SK, GAIN_MQ = range(5)

LANES = 128
SUBLANES = 8
MXU_N = 256

VMEM_LIMIT = 58 * 1024 * 1024


def _cparams(sem):
    return pltpu.CompilerParams(dimension_semantics=sem, vmem_limit_bytes=VMEM_LIMIT)


def _rms(x, gain):
    ms = jnp.mean(x * x, axis=-1, keepdims=True)
    return x * lax.rsqrt(ms + EPS) * gain


def _ffn_kernel(x_ref, g_ref, wg_ref, wu_ref, wd_ref, o_ref, xn_ref, *, n_chunk):
    j = pl.program_id(1)

    @pl.when(j == 0)
    def _():
        x = x_ref[...]
        xn_ref[...] = _rms(x, g_ref[...]).astype(BF16)
        o_ref[...] = x

    xn = xn_ref[...]
    gate = jnp.dot(xn, wg_ref[...], preferred_element_type=F32)
    up = jnp.dot(xn, wu_ref[...], preferred_element_type=F32)
    h = (gate * jax.nn.sigmoid(gate) * (0.5 * up)).astype(BF16)
    d_model = o_ref.shape[1]
    cw = d_model // n_chunk
    for c in range(n_chunk):
        cs = slice(c * cw, (c + 1) * cw)
        o_ref[:, cs] += jnp.dot(h, wd_ref[:, cs], preferred_element_type=F32)


def _ffn(x, gain, wg, wu, wd, *, tm=1024, tf=512):
    t, d = x.shape
    dff = wg.shape[1]
    return pl.pallas_call(
        functools.partial(_ffn_kernel, n_chunk=4),
        out_shape=jax.ShapeDtypeStruct((t, d), F32),
        grid=(t // tm, dff // tf),
        in_specs=[
            pl.BlockSpec((tm, d), lambda i, j: (i, 0)),
            pl.BlockSpec((1, d), lambda i, j: (0, 0)),
            pl.BlockSpec((d, tf), lambda i, j: (0, j)),
            pl.BlockSpec((d, tf), lambda i, j: (0, j)),
            pl.BlockSpec((tf, d), lambda i, j: (j, 0)),
        ],
        out_specs=pl.BlockSpec((tm, d), lambda i, j: (i, 0)),
        scratch_shapes=[pltpu.VMEM((tm, d), BF16)],
        compiler_params=_cparams(("parallel", "arbitrary")),
        name="ffn",
    )(x, gain, wg, wu, wd)


def _in_proj_kernel(x_ref, g_ref, w_ref, wf_ref, hg_ref, fb_ref,
                    qt_ref, ka_ref, vt_ref, rest_ref, anc_ref, carry_ref, *, blocks_per_seq):
    i = pl.program_id(0)
    tm = x_ref.shape[0]
    hn = _rms(x_ref[...], g_ref[...]).astype(BF16)

    logit = jnp.dot(hn, wf_ref[...], preferred_element_type=F32)
    c = jax.nn.log_sigmoid(logit + fb_ref[...]) * LOG2E
    row = lax.broadcasted_iota(jnp.int32, c.shape, 0) % FOX_BLK
    sh = 1
    while sh < FOX_BLK:
        c = c + jnp.where(row >= sh, pltpu.roll(c, sh, 0), 0.0)
        sh *= 2

    @pl.when(i % blocks_per_seq == 0)
    def _():
        carry_ref[...] = jnp.zeros_like(carry_ref)

    anchor = carry_ref[0:1, :]
    anc_ref[...] = jnp.zeros_like(anc_ref)
    for r in range(tm // FOX_BLK):
        anc_ref[0, r:r + 1, :] = anchor
        anchor = anchor + c[(r + 1) * FOX_BLK - 1:(r + 1) * FOX_BLK, :]
    carry_ref[0:1, :] = anchor

    n_sub = tm // FOX_BLK
    lane = lax.broadcasted_iota(jnp.int32, c.shape, 1)
    sub = lax.broadcasted_iota(jnp.int32, (HEAD_DIM, FOX_BLK), 0)
    ct = c.T
    for h in range(FOX_HEADS):
        ch = c[:, h:h + 1]
        hi = ch.astype(BF16).astype(F32)
        lo = ch - hi
        ak = jnp.where(lane < 2, 1.0, jnp.where(lane == 2, -hi, jnp.where(lane == 3, -lo, 0.0)))
        ka_ref[:, (2 * h + 1) * HEAD_DIM:(2 * h + 2) * HEAD_DIM] = ak.astype(BF16)
        for r in range(n_sub):
            cr = ct[h:h + 1, r * FOX_BLK:(r + 1) * FOX_BLK]
            hi_r = cr.astype(BF16).astype(F32)
            lo_r = cr - hi_r
            aq = jnp.where(sub == 0, hi_r, jnp.where(sub == 1, lo_r, jnp.where(sub < 4, 1.0, 0.0)))
            qt_ref[r, (2 * h + 1) * HEAD_DIM:(2 * h + 2) * HEAD_DIM, :] = aq.astype(BF16)

    def normed(ph, row_id):
        return _rms(ph, hg_ref[row_id:row_id + 1, :]).astype(BF16)

    heads_per_chunk = MXU_N // HEAD_DIM
    for ck in range(PROJ_W // MXU_N):
        p = jnp.dot(hn, w_ref[:, ck * MXU_N:(ck + 1) * MXU_N], preferred_element_type=F32)
        for e in range(heads_per_chunk):
            col = ck * MXU_N + e * HEAD_DIM
            ph = p[:, e * HEAD_DIM:(e + 1) * HEAD_DIM]
            if col < FOX_W:
                h = col // HEAD_DIM
                qn = _rms(ph, hg_ref[GAIN_FQ:GAIN_FQ + 1, :])
                for r in range(n_sub):
                    qt_ref[r, 2 * h * HEAD_DIM:(2 * h + 1) * HEAD_DIM, :] = (
                        qn[r * FOX_BLK:(r + 1) * FOX_BLK, :].T.astype(BF16))
            elif col < 2 * FOX_W:
                h = (col - FOX_W) // HEAD_DIM
                ka_ref[:, 2 * h * HEAD_DIM:(2 * h + 1) * HEAD_DIM] = normed(ph, GAIN_FK)
            elif col < 3 * FOX_W:
                h = (col - 2 * FOX_W) // HEAD_DIM
                for r in range(n_sub):
                    vt_ref[r, h * HEAD_DIM:(h + 1) * HEAD_DIM, :] = (
                        ph[r * FOX_BLK:(r + 1) * FOX_BLK, :].T.astype(BF16))
            else:
                wc = col - 3 * FOX_W
                if wc < SWA_Q_W:
                    val, rc = normed(ph, GAIN_SQ), REST_SQ + wc
                elif wc < SWA_Q_W + SWA_KV_W:
                    val, rc = normed(ph, GAIN_SK), REST_SK + wc - SWA_Q_W
                elif wc < SWA_Q_W + 2 * SWA_KV_W:
                    val, rc = ph.astype(BF16), REST_SV + wc - SWA_Q_W - SWA_KV_W
                else:
                    val, rc = normed(ph, GAIN_MQ), REST_MQ + wc - SWA_Q_W - 2 * SWA_KV_W
                rest_ref[:, rc:rc + HEAD_DIM] = val


def _in_proj(x, gain, w, wf, head_gain, fbias, *, seq, tm=512):
    t, d = x.shape
    assert seq % tm == 0 and tm % FOX_BLK == 0 and tm // FOX_BLK <= SUBLANES
    const = lambda shape: pl.BlockSpec(shape, lambda i: (0, 0), pipeline_mode=pl.Buffered(1))
    return pl.pallas_call(
        functools.partial(_in_proj_kernel, blocks_per_seq=seq // tm),
        out_shape=(
            jax.ShapeDtypeStruct((t // FOX_BLK, 2 * FOX_W, FOX_BLK), BF16),
            jax.ShapeDtypeStruct((t, 2 * FOX_W), BF16),
            jax.ShapeDtypeStruct((t // FOX_BLK, FOX_W, FOX_BLK), BF16),
            jax.ShapeDtypeStruct((t, REST_W), BF16),
            jax.ShapeDtypeStruct((t // tm, SUBLANES, LANES), F32),
        ),
        grid=(t // tm,),
        in_specs=[
            pl.BlockSpec((tm, d), lambda i: (i, 0)),
            const((1, d)),
            const((d, PROJ_W)),
            const((d, LANES)),
            const((SUBLANES, HEAD_DIM)),
            const((1, LANES)),
        ],
        out_specs=(
            pl.BlockSpec((tm // FOX_BLK, 2 * FOX_W, FOX_BLK), lambda i: (i, 0, 0)),
            pl.BlockSpec((tm, 2 * FOX_W), lambda i: (i, 0)),
            pl.BlockSpec((tm // FOX_BLK, FOX_W, FOX_BLK), lambda i: (i, 0, 0)),
            pl.BlockSpec((tm, REST_W), lambda i: (i, 0)),
            pl.BlockSpec((1, SUBLANES, LANES), lambda i: (i, 0, 0)),
        ),
        scratch_shapes=[pltpu.VMEM((SUBLANES, LANES), F32)],
        compiler_params=_cparams(("arbitrary",)),
        name="in_proj",
    )(x, gain, w, wf, head_gain, fbias)


def _fox_kernel(anc_ref, q_ref, k_ref, vt_ref, o_ref, m_ref, l_ref, acc_ref, st_ref, *, nblk):
    b = pl.program_id(0)
    i = pl.program_id(1)
    blk = q_ref.shape[2]
    aw = 2 * HEAD_DIM
    m_ref[...] = jnp.full(m_ref.shape, NEG_INF, F32)
    l_ref[...] = jnp.zeros(l_ref.shape, F32)
    acc_ref[...] = jnp.zeros(acc_ref.shape, F32)
    krow = lax.broadcasted_iota(jnp.int32, (blk, blk), 0)
    qcol = lax.broadcasted_iota(jnp.int32, (blk, blk), 1)

    def scores(j, slot):
        ks = pl.ds(pl.multiple_of(j * blk, blk), blk)
        for h in range(FOX_HEADS):
            st_ref[slot, h] = jnp.dot(k_ref[ks, h * aw:(h + 1) * aw], q_ref[0, h * aw:(h + 1) * aw, :],
                                      preferred_element_type=F32)

    def softmax_pv(j, slot, masked):
        for h in range(FOX_HEADS):
            st = st_ref[slot, h]
            d = anc_ref[(b * nblk + i) * SUBLANES + h] - anc_ref[(b * nblk + j) * SUBLANES + h]
            if masked:
                st = jnp.where(krow <= qcol, st, NEG_INF)
            m_prev = m_ref[h]
            m_new = jnp.maximum(m_prev, jnp.max(st, axis=0, keepdims=True) + d)
            alpha = jnp.exp2(m_prev - m_new)
            p = jnp.exp2(st - (m_new - d))
            l_ref[h] = alpha * l_ref[h] + jnp.sum(p, axis=0, keepdims=True)
            pv = jnp.dot(vt_ref[j, h * HEAD_DIM:(h + 1) * HEAD_DIM, :], p.astype(BF16),
                         preferred_element_type=F32)
            acc_ref[h] = alpha * acc_ref[h] + pv
            m_ref[h] = m_new

    def body(t, carry):
        j = 2 * t
        scores(j + 1, 1)
        softmax_pv(j, 0, False)
        scores(j + 2, 0)
        softmax_pv(j + 1, 1, False)
        return carry

    scores(0, 0)
    lax.fori_loop(0, i // 2, body, 0)

    @pl.when(i % 2 == 0)
    def _():
        softmax_pv(i, 0, True)

    @pl.when(i % 2 == 1)
    def _():
        scores(i, 1)
        softmax_pv(i - 1, 0, False)
        softmax_pv(i, 1, True)

    for h in range(FOX_HEADS):
        o_ref[:, h * HEAD_DIM:(h + 1) * HEAD_DIM] = (acc_ref[h] / l_ref[h]).T.astype(o_ref.dtype)


def _fox(anchors, qt, ka, vt, *, batch, seq):
    t = ka.shape[0]
    blk = FOX_BLK
    nblk = seq // blk
    return pl.pallas_call(
        functools.partial(_fox_kernel, nblk=nblk),
        out_shape=jax.ShapeDtypeStruct((t, FOX_W), BF16),
        grid=(batch, nblk),
        in_specs=[
            pl.BlockSpec(memory_space=pltpu.SMEM),
            pl.BlockSpec((1, 2 * FOX_W, blk), lambda b, i: (b * nblk + i, 0, 0)),
            pl.BlockSpec((seq, 2 * FOX_W), lambda b, i: (b, 0)),
            pl.BlockSpec((nblk, FOX_W, blk), lambda b, i: (b, 0, 0)),
        ],
        out_specs=pl.BlockSpec((blk, FOX_W), lambda b, i: (b * nblk + i, 0)),
        scratch_shapes=[
            pltpu.VMEM((FOX_HEADS, 1, blk), F32),
            pltpu.VMEM((FOX_HEADS, 1, blk), F32),
            pltpu.VMEM((FOX_HEADS, HEAD_DIM, blk), F32),
            pltpu.VMEM((2, FOX_HEADS, blk, blk), F32),
        ],
        compiler_params=_cparams(("parallel", "arbitrary")),
        name="fox",
    )(anchors, qt, ka, vt)


def _swa_kernel(q_ref, k_ref, v_ref, sink_ref, slope_ref, o_ref):
    qi = pl.program_id(1)
    tq = q_ref.shape[0]
    w = WINDOW
    row = lax.broadcasted_iota(jnp.int32, (w, 2 * w), 0)
    col = lax.broadcasted_iota(jnp.int32, (w, 2 * w), 1)

    for r in range(tq // w):
        n = qi * (tq // w) + r
        kstart = pl.multiple_of(jnp.maximum(n - 1, 0) * w, w)
        dist = (n * w - kstart) + row - col
        valid = (dist >= 0) & (dist < w)
        dist_f = dist.astype(F32)
        for h in range(SWA_HEADS):
            g = h // SWA_GROUP
            q = q_ref[r * w:(r + 1) * w, h * HEAD_DIM:(h + 1) * HEAD_DIM]
            k = k_ref[pl.ds(kstart, 2 * w), g * HEAD_DIM:(g + 1) * HEAD_DIM]
            v = v_ref[pl.ds(kstart, 2 * w), g * HEAD_DIM:(g + 1) * HEAD_DIM]
            s = lax.dot_general(q, k, (((1,), (1,)), ((), ())), preferred_element_type=F32)
            s = s - slope_ref[h] * dist_f
            s = jnp.where(valid, s, NEG_INF)
            sink = sink_ref[h]
            m = jnp.maximum(jnp.max(s, axis=-1, keepdims=True), sink)
            p = jnp.exp(s - m)
            denom = jnp.sum(p, axis=-1, keepdims=True) + jnp.exp(sink - m)
            o = jnp.dot(p.astype(BF16), v, preferred_element_type=F32)
            o_ref[r * w:(r + 1) * w, h * HEAD_DIM:(h + 1) * HEAD_DIM] = (o / denom).astype(o_ref.dtype)


def _swa(rest, sinks, slopes, *, batch, seq, tq=512):
    t = rest.shape[0]
    nq = seq // tq
    smem = pl.BlockSpec(memory_space=pltpu.SMEM)
    return pl.pallas_call(
        _swa_kernel,
        out_shape=jax.ShapeDtypeStruct((t, SWA_Q_W), BF16),
        grid=(batch, nq),
        in_specs=[
            pl.BlockSpec((tq, SWA_Q_W), lambda b, i: (b * nq + i, REST_SQ // SWA_Q_W)),
            pl.BlockSpec((seq, SWA_KV_W), lambda b, i: (b, REST_SK // SWA_KV_W)),
            pl.BlockSpec((seq, SWA_KV_W), lambda b, i: (b, REST_SV // SWA_KV_W)),
            smem, smem,
        ],
        out_specs=pl.BlockSpec((tq, SWA_Q_W), lambda b, i: (b * nq + i, 0)),
        compiler_params=_cparams(("parallel", "arbitrary")),
        name="swa",
    )(rest, rest, rest, sinks, slopes)


def _mem_kv_kernel(mem_ref, g_ref, wk_ref, wv_ref, kg_ref, mk_ref, mv_ref):
    mn = _rms(mem_ref[...], g_ref[...]).astype(BF16)
    k = jnp.dot(mn, wk_ref[...], preferred_element_type=F32)
    v = jnp.dot(mn, wv_ref[...], preferred_element_type=F32)
    for h in range(MEM_HEADS):
        hs = slice(h * HEAD_DIM, (h + 1) * HEAD_DIM)
        mk_ref[:, hs] = _rms(k[:, hs], kg_ref[...]).astype(BF16)
    mv_ref[...] = v.astype(BF16)


def _mem_kv(mem2d, gain, wk, wv, kgain, *, batch, mem_len):
    d = mem2d.shape[1]
    full = lambda shape: pl.BlockSpec(shape, lambda b: (0, 0))
    out = jax.ShapeDtypeStruct((batch * mem_len, MEM_W), BF16)
    return pl.pallas_call(
        _mem_kv_kernel,
        out_shape=(out, out),
        grid=(batch,),
        in_specs=[
            pl.BlockSpec((mem_len, d), lambda b: (b, 0)),
            full((1, d)), full((d, MEM_W)), full((d, MEM_W)), full((1, HEAD_DIM)),
        ],
        out_specs=(pl.BlockSpec((mem_len, MEM_W), lambda b: (b, 0)),
                   pl.BlockSpec((mem_len, MEM_W), lambda b: (b, 0))),
        compiler_params=_cparams(("parallel",)),
        name="mem_kv",
    )(mem2d, gain, wk, wv, kgain)


def _mem_attn_kernel(q_ref, mk_ref, mv_ref, o_ref):
    for h in range(MEM_HEADS):
        hs = slice(h * HEAD_DIM, (h + 1) * HEAD_DIM)
        s = lax.dot_general(q_ref[:, hs], mk_ref[:, hs], (((1,), (1,)), ((), ())),
                            preferred_element_type=F32)
        m = jnp.max(s, axis=-1, keepdims=True)
        p = jnp.exp(s - m)
        denom = jnp.sum(p, axis=-1, keepdims=True)
        o = jnp.dot(p.astype(BF16), mv_ref[:, hs], preferred_element_type=F32)
        o_ref[:, hs] = (o / denom).astype(o_ref.dtype)


def _mem_attn(rest, mk, mv, *, batch, seq, mem_len, tq=512):
    t = rest.shape[0]
    nq = seq // tq
    return pl.pallas_call(
        _mem_attn_kernel,
        out_shape=jax.ShapeDtypeStruct((t, MEM_W), BF16),
        grid=(batch, nq),
        in_specs=[
            pl.BlockSpec((tq, MEM_W), lambda b, i: (b * nq + i, REST_MQ // MEM_W)),
            pl.BlockSpec((mem_len, MEM_W), lambda b, i: (b, 0)),
            pl.BlockSpec((mem_len, MEM_W), lambda b, i: (b, 0)),
        ],
        out_specs=pl.BlockSpec((tq, MEM_W), lambda b, i: (b * nq + i, 0)),
        compiler_params=_cparams(("parallel", "arbitrary")),
        name="mem_attn",
    )(rest, mk, mv)


def _out_proj_kernel(x_ref, a_ref, b_ref, c_ref, w_ref, o_ref):
    acc = jnp.dot(a_ref[...], w_ref[0:FOX_W, :], preferred_element_type=F32)
    acc += jnp.dot(b_ref[...], w_ref[FOX_W:FOX_W + SWA_Q_W, :], preferred_element_type=F32)
    acc += jnp.dot(c_ref[...], w_ref[FOX_W + SWA_Q_W:, :], preferred_element_type=F32)
    o_ref[...] = x_ref[...] + acc


def _out_proj(x, a, b, c, w, *, tm=512):
    t, d = x.shape
    return pl.pallas_call(
        _out_proj_kernel,
        out_shape=jax.ShapeDtypeStruct((t, d), F32),
        grid=(t // tm,),
        in_specs=[
            pl.BlockSpec((tm, d), lambda i: (i, 0)),
            pl.BlockSpec((tm, FOX_W), lambda i: (i, 0)),
            pl.BlockSpec((tm, SWA_Q_W), lambda i: (i, 0)),
            pl.BlockSpec((tm, MEM_W), lambda i: (i, 0)),
            pl.BlockSpec(w.shape, lambda i: (0, 0)),
        ],
        out_specs=pl.BlockSpec((tm, d), lambda i: (i, 0)),
        compiler_params=_cparams(("parallel",)),
        name="out_proj",
    )(x, a, b, c, w)


def _alibi_slopes(n):
    return jnp.asarray([2.0 ** (-8.0 * i / n) for i in range(1, n + 1)], dtype=F32)


def _pack_w_in(w_in):
    fox = w_in[:, :3 * FOX_W].astype(BF16)
    fl = w_in[:, 3 * FOX_W:3 * FOX_W + FOX_HEADS]
    tail = w_in[:, 3 * FOX_W + FOX_HEADS:].astype(BF16)
    w_main = jnp.concatenate([fox, tail], axis=1)
    w_f = jnp.pad(fl, ((0, 0), (0, LANES - FOX_HEADS))).astype(BF16)
    return w_main, w_f


def kernel(x, mem, ffn1_norm, ffn1_gate, ffn1_up, ffn1_down, mix_norm, mem_norm, w_in, forget_bias, w_mem_k, w_mem_v, fox_q_gain, fox_k_gain, swa_q_gain, swa_k_gain, swa_sinks, mem_q_gain, mem_k_gain, w_out, ffn2_norm, ffn2_gate, ffn2_up, ffn2_down):
    batch, seq, d = x.shape
    mem_len = mem.shape[1]
    depth = w_in.shape[0]
    scale = HEAD_DIM ** -0.5
    slopes = _alibi_slopes(SWA_HEADS)
    x2 = x.reshape(batch * seq, d)
    mem2 = mem.reshape(batch * mem_len, d)
    zeros = jnp.zeros((HEAD_DIM,), F32)

    for l in range(depth):
        x2 = _ffn(x2, ffn1_norm[l][None], ffn1_gate[l].astype(BF16), ffn1_up[l].astype(BF16),
                  ffn1_down[l].astype(BF16))

        w_main, w_f = _pack_w_in(w_in[l])
        head_gain = jnp.stack([fox_q_gain[l] * (scale * LOG2E), fox_k_gain[l], swa_q_gain[l] * scale,
                               swa_k_gain[l], mem_q_gain[l] * scale, zeros, zeros, zeros])
        fbias = jnp.pad(forget_bias[l], (0, LANES - FOX_HEADS))[None]
        qt, ka, vt, rest, anc = _in_proj(x2, mix_norm[l][None], w_main, w_f, head_gain, fbias, seq=seq)

        n_sub = (x2.shape[0] // FOX_BLK) // anc.shape[0]
        anchors = anc[:, :n_sub, :SUBLANES].reshape(-1)
        out_a = _fox(anchors, qt, ka, vt, batch=batch, seq=seq)
        out_b = _swa(rest, swa_sinks[l], slopes, batch=batch, seq=seq)
        mk, mv = _mem_kv(mem2, mem_norm[l][None], w_mem_k[l].astype(BF16), w_mem_v[l].astype(BF16),
                         mem_k_gain[l][None], batch=batch, mem_len=mem_len)
        out_c = _mem_attn(rest, mk, mv, batch=batch, seq=seq, mem_len=mem_len)

        x2 = _out_proj(x2, out_a, out_b, out_c, w_out[l].astype(BF16))
        x2 = _ffn(x2, ffn2_norm[l][None], ffn2_gate[l].astype(BF16), ffn2_up[l].astype(BF16),
                  ffn2_down[l].astype(BF16))
    return x2.reshape(batch, seq, d)
```

```python
import functools

import jax
import jax.numpy as jnp
from jax import lax
from jax.experimental import pallas as pl
from jax.experimental.pallas import tpu as pltpu

F32 = jnp.float32
BF16 = jnp.bfloat16

HEAD_DIM = 128
FOX_HEADS = 6
SWA_HEADS = 6
SWA_KV_HEADS = 2
SWA_GROUP = SWA_HEADS // SWA_KV_HEADS
MEM_HEADS = 4
WINDOW = 128
EPS = 1e-6
NEG_INF = -1e30
LOG2E = 1.4426950408889634
FOX_BLK = 256

FOX_W = FOX_HEADS * HEAD_DIM
SWA_Q_W = SWA_HEADS * HEAD_DIM
SWA_KV_W = SWA_KV_HEADS * HEAD_DIM
MEM_W = MEM_HEADS * HEAD_DIM

PROJ_W = 3 * FOX_W + SWA_Q_W + 2 * SWA_KV_W + MEM_W
REST_W = SWA_Q_W + 2 * SWA_KV_W + MEM_W
REST_SQ, REST_SK, REST_MQ, REST_SV = 0, SWA_Q_W, SWA_Q_W + SWA_KV_W, SWA_Q_W + SWA_KV_W + MEM_W
GAIN_FQ, GAIN_FK, GAIN_SQ, GAIN_SK, GAIN_MQ = range(5)

LANES = 128
SUBLANES = 8
MXU_N = 256

VMEM_LIMIT = 58 * 1024 * 1024


def _cparams(sem):
    return pltpu.CompilerParams(dimension_semantics=sem, vmem_limit_bytes=VMEM_LIMIT)


def _rms(x, gain):
    ms = jnp.mean(x * x, axis=-1, keepdims=True)
    return x * lax.rsqrt(ms + EPS) * gain


def _ffn_kernel(x_ref, g_ref, wg_ref, wu_ref, wd_ref, *rest, n_chunk, n_cast):
    cast_in, o_ref, cast_out, xn_ref = rest[:n_cast], rest[n_cast], rest[n_cast + 1:-1], rest[-1]
    j = pl.program_id(1)

    for src, dst in zip(cast_in, cast_out):
        dst[...] = src[...].astype(BF16)

    @pl.when(j == 0)
    def _():
        x = x_ref[...]
        xn_ref[...] = _rms(x, g_ref[...]).astype(BF16)
        o_ref[...] = x

    xn = xn_ref[...]
    gate = jnp.dot(xn, wg_ref[...], preferred_element_type=F32)
    up = jnp.dot(xn, wu_ref[...], preferred_element_type=F32)
    h = (gate * jax.nn.sigmoid(gate) * (0.5 * up)).astype(BF16)
    d_model = o_ref.shape[1]
    cw = d_model // n_chunk
    for c in range(n_chunk):
        cs = slice(c * cw, (c + 1) * cw)
        o_ref[:, cs] += jnp.dot(h, wd_ref[:, cs], preferred_element_type=F32)


def _ffn(x, gain, wg, wu, wd, *, next_weights=(), tm=1024, tf=512):
    t, d = x.shape
    dff = wg.shape[1]
    gi, gj = t // tm, dff // tf
    cast_specs, cast_shapes = [], []
    for w in next_weights:
        ri, cj = (gi, gj) if w.shape[1] == dff else (gj, gi)
        blk = (w.shape[0] // ri, w.shape[1] // cj)
        assert blk[0] % SUBLANES == 0 and blk[1] % LANES == 0
        index = (lambda i, j: (i, j)) if w.shape[1] == dff else (lambda i, j: (j, i))
        cast_specs.append(pl.BlockSpec(blk, index))
        cast_shapes.append(jax.ShapeDtypeStruct(w.shape, BF16))
    out = pl.pallas_call(
        functools.partial(_ffn_kernel, n_chunk=4, n_cast=len(next_weights)),
        out_shape=(jax.ShapeDtypeStruct((t, d), F32), *cast_shapes),
        grid=(gi, gj),
        in_specs=[
            pl.BlockSpec((tm, d), lambda i, j: (i, 0)),
            pl.BlockSpec((1, d), lambda i, j: (0, 0)),
            pl.BlockSpec((d, tf), lambda i, j: (0, j)),
            pl.BlockSpec((d, tf), lambda i, j: (0, j)),
            pl.BlockSpec((tf, d), lambda i, j: (j, 0)),
            *cast_specs,
        ],
        out_specs=(pl.BlockSpec((tm, d), lambda i, j: (i, 0)), *cast_specs),
        scratch_shapes=[pltpu.VMEM((tm, d), BF16)],
        compiler_params=_cparams(("parallel", "arbitrary")),
        name="ffn",
    )(x, gain, wg, wu, wd, *next_weights)
    return out[0], out[1:]


def _in_proj_kernel(x_ref, g_ref, w_ref, wf_ref, hg_ref, fb_ref,
                    qt_ref, ka_ref, vt_ref, rest_ref, anc_ref, carry_ref, *, blocks_per_seq):
    i = pl.program_id(0)
    tm = x_ref.shape[0]
    hn = _rms(x_ref[...], g_ref[...]).astype(BF16)

    logit = jnp.dot(hn, wf_ref[...], preferred_element_type=F32)
    c = jax.nn.log_sigmoid(logit + fb_ref[...]) * LOG2E
    row = lax.broadcasted_iota(jnp.int32, c.shape, 0) % FOX_BLK
    sh = 1
    while sh < FOX_BLK:
        c = c + jnp.where(row >= sh, pltpu.roll(c, sh, 0), 0.0)
        sh *= 2

    @pl.when(i % blocks_per_seq == 0)
    def _():
        carry_ref[...] = jnp.zeros_like(carry_ref)

    anchor = carry_ref[0:1, :]
    anc_ref[...] = jnp.zeros_like(anc_ref)
    for r in range(tm // FOX_BLK):
        anc_ref[0, r:r + 1, :] = anchor
        anchor = anchor + c[(r + 1) * FOX_BLK - 1:(r + 1) * FOX_BLK, :]
    carry_ref[0:1, :] = anchor

    n_sub = tm // FOX_BLK
    lane = lax.broadcasted_iota(jnp.int32, c.shape, 1)
    sub = lax.broadcasted_iota(jnp.int32, (HEAD_DIM, FOX_BLK), 0)
    ct = c.T
    for h in range(FOX_HEADS):
        ch = c[:, h:h + 1]
        hi = ch.astype(BF16).astype(F32)
        lo = ch - hi
        ak = jnp.where(lane < 2, 1.0, jnp.where(lane == 2, -hi, jnp.where(lane == 3, -lo, 0.0)))
        ka_ref[:, (2 * h + 1) * HEAD_DIM:(2 * h + 2) * HEAD_DIM] = ak.astype(BF16)
        for r in range(n_sub):
            cr = ct[h:h + 1, r * FOX_BLK:(r + 1) * FOX_BLK]
            hi_r = cr.astype(BF16).astype(F32)
            lo_r = cr - hi_r
            aq = jnp.where(sub == 0, hi_r, jnp.where(sub == 1, lo_r, jnp.where(sub < 4, 1.0, 0.0)))
            qt_ref[r, (2 * h + 1) * HEAD_DIM:(2 * h + 2) * HEAD_DIM, :] = aq.astype(BF16)

    def normed(ph, row_id):
        return _rms(ph, hg_ref[row_id:row_id + 1, :]).astype(BF16)

    heads_per_chunk = MXU_N // HEAD_DIM
    for ck in range(PROJ_W // MXU_N):
        p = jnp.dot(hn, w_ref[:, ck * MXU_N:(ck + 1) * MXU_N], preferred_element_type=F32)
        for e in range(heads_per_chunk):
            col = ck * MXU_N + e * HEAD_DIM
            ph = p[:, e * HEAD_DIM:(e + 1) * HEAD_DIM]
            if col < FOX_W:
                h = col // HEAD_DIM
                qn = _rms(ph, hg_ref[GAIN_FQ:GAIN_FQ + 1, :])
                for r in range(n_sub):
                    qt_ref[r, 2 * h * HEAD_DIM:(2 * h + 1) * HEAD_DIM, :] = (
                        qn[r * FOX_BLK:(r + 1) * FOX_BLK, :].T.astype(BF16))
            elif col < 2 * FOX_W:
                h = (col - FOX_W) // HEAD_DIM
                ka_ref[:, 2 * h * HEAD_DIM:(2 * h + 1) * HEAD_DIM] = normed(ph, GAIN_FK)
            elif col < 3 * FOX_W:
                h = (col - 2 * FOX_W) // HEAD_DIM
                for r in range(n_sub):
                    vt_ref[r, h * HEAD_DIM:(h + 1) * HEAD_DIM, :] = (
                        ph[r * FOX_BLK:(r + 1) * FOX_BLK, :].T.astype(BF16))
            else:
                wc = col - 3 * FOX_W
                if wc < SWA_Q_W:
                    val, rc = normed(ph, GAIN_SQ), REST_SQ + wc
                elif wc < SWA_Q_W + SWA_KV_W:
                    val, rc = normed(ph, GAIN_SK), REST_SK + wc - SWA_Q_W
                elif wc < SWA_Q_W + 2 * SWA_KV_W:
                    val, rc = ph.astype(BF16), REST_SV + wc - SWA_Q_W - SWA_KV_W
                else:
                    val, rc = normed(ph, GAIN_MQ), REST_MQ + wc - SWA_Q_W - 2 * SWA_KV_W
                rest_ref[:, rc:rc + HEAD_DIM] = val


def _in_proj(x, gain, w, wf, head_gain, fbias, *, seq, tm=512):
    t, d = x.shape
    assert seq % tm == 0 and tm % FOX_BLK == 0 and tm // FOX_BLK <= SUBLANES
    const = lambda shape: pl.BlockSpec(shape, lambda i: (0, 0), pipeline_mode=pl.Buffered(1))
    return pl.pallas_call(
        functools.partial(_in_proj_kernel, blocks_per_seq=seq // tm),
        out_shape=(
            jax.ShapeDtypeStruct((t // FOX_BLK, 2 * FOX_W, FOX_BLK), BF16),
            jax.ShapeDtypeStruct((t, 2 * FOX_W), BF16),
            jax.ShapeDtypeStruct((t // FOX_BLK, FOX_W, FOX_BLK), BF16),
            jax.ShapeDtypeStruct((t, REST_W), BF16),
            jax.ShapeDtypeStruct((t // tm, SUBLANES, LANES), F32),
        ),
        grid=(t // tm,),
        in_specs=[
            pl.BlockSpec((tm, d), lambda i: (i, 0)),
            const((1, d)),
            const((d, PROJ_W)),
            const((d, LANES)),
            const((SUBLANES, HEAD_DIM)),
            const((1, LANES)),
        ],
        out_specs=(
            pl.BlockSpec((tm // FOX_BLK, 2 * FOX_W, FOX_BLK), lambda i: (i, 0, 0)),
            pl.BlockSpec((tm, 2 * FOX_W), lambda i: (i, 0)),
            pl.BlockSpec((tm // FOX_BLK, FOX_W, FOX_BLK), lambda i: (i, 0, 0)),
            pl.BlockSpec((tm, REST_W), lambda i: (i, 0)),
            pl.BlockSpec((1, SUBLANES, LANES), lambda i: (i, 0, 0)),
        ),
        scratch_shapes=[pltpu.VMEM((SUBLANES, LANES), F32)],
        compiler_params=_cparams(("arbitrary",)),
        name="in_proj",
    )(x, gain, w, wf, head_gain, fbias)


def _fox_kernel(anc_ref, q_ref, k_ref, vt_ref, o_ref, m_ref, l_ref, acc_ref, st_ref, *, nblk):
    b = pl.program_id(0)
    i = pl.program_id(1)
    blk = q_ref.shape[2]
    aw = 2 * HEAD_DIM
    m_ref[...] = jnp.full(m_ref.shape, NEG_INF, F32)
    l_ref[...] = jnp.zeros(l_ref.shape, F32)
    acc_ref[...] = jnp.zeros(acc_ref.shape, F32)
    krow = lax.broadcasted_iota(jnp.int32, (blk, blk), 0)
    qcol = lax.broadcasted_iota(jnp.int32, (blk, blk), 1)

    def scores(j, slot):
        ks = pl.ds(pl.multiple_of(j * blk, blk), blk)
        for h in range(FOX_HEADS):
            st_ref[slot, h] = jnp.dot(k_ref[ks, h * aw:(h + 1) * aw], q_ref[0, h * aw:(h + 1) * aw, :],
                                      preferred_element_type=F32)

    def softmax_pv(j, slot, masked):
        for h in range(FOX_HEADS):
            st = st_ref[slot, h]
            d = anc_ref[(b * nblk + i) * SUBLANES + h] - anc_ref[(b * nblk + j) * SUBLANES + h]
            if masked:
                st = jnp.where(krow <= qcol, st, NEG_INF)
            m_prev = m_ref[h]
            m_new = jnp.maximum(m_prev, jnp.max(st, axis=0, keepdims=True) + d)
            alpha = jnp.exp2(m_prev - m_new)
            p = jnp.exp2(st - (m_new - d))
            l_ref[h] = alpha * l_ref[h] + jnp.sum(p, axis=0, keepdims=True)
            pv = jnp.dot(vt_ref[j, h * HEAD_DIM:(h + 1) * HEAD_DIM, :], p.astype(BF16),
                         preferred_element_type=F32)
            acc_ref[h] = alpha * acc_ref[h] + pv
            m_ref[h] = m_new

    def body(t, carry):
        j = 2 * t
        scores(j + 1, 1)
        softmax_pv(j, 0, False)
        scores(j + 2, 0)
        softmax_pv(j + 1, 1, False)
        return carry

    scores(0, 0)
    lax.fori_loop(0, i // 2, body, 0)

    @pl.when(i % 2 == 0)
    def _():
        softmax_pv(i, 0, True)

    @pl.when(i % 2 == 1)
    def _():
        scores(i, 1)
        softmax_pv(i - 1, 0, False)
        softmax_pv(i, 1, True)

    for h in range(FOX_HEADS):
        o_ref[:, h * HEAD_DIM:(h + 1) * HEAD_DIM] = (acc_ref[h] / l_ref[h]).T.astype(o_ref.dtype)


def _fox(anchors, qt, ka, vt, *, batch, seq):
    t = ka.shape[0]
    blk = FOX_BLK
    nblk = seq // blk
    return pl.pallas_call(
        functools.partial(_fox_kernel, nblk=nblk),
        out_shape=jax.ShapeDtypeStruct((t, FOX_W), BF16),
        grid=(batch, nblk),
        in_specs=[
            pl.BlockSpec(memory_space=pltpu.SMEM),
            pl.BlockSpec((1, 2 * FOX_W, blk), lambda b, i: (b * nblk + i, 0, 0)),
            pl.BlockSpec((seq, 2 * FOX_W), lambda b, i: (b, 0)),
            pl.BlockSpec((nblk, FOX_W, blk), lambda b, i: (b, 0, 0)),
        ],
        out_specs=pl.BlockSpec((blk, FOX_W), lambda b, i: (b * nblk + i, 0)),
        scratch_shapes=[
            pltpu.VMEM((FOX_HEADS, 1, blk), F32),
            pltpu.VMEM((FOX_HEADS, 1, blk), F32),
            pltpu.VMEM((FOX_HEADS, HEAD_DIM, blk), F32),
            pltpu.VMEM((2, FOX_HEADS, blk, blk), F32),
        ],
        compiler_params=_cparams(("parallel", "arbitrary")),
        name="fox",
    )(anchors, qt, ka, vt)


def _swa_kernel(q_ref, k_ref, v_ref, sink_ref, slope_ref, o_ref):
    qi = pl.program_id(1)
    tq = q_ref.shape[0]
    w = WINDOW
    row = lax.broadcasted_iota(jnp.int32, (w, 2 * w), 0)
    col = lax.broadcasted_iota(jnp.int32, (w, 2 * w), 1)

    for r in range(tq // w):
        n = qi * (tq // w) + r
        kstart = pl.multiple_of(jnp.maximum(n - 1, 0) * w, w)
        dist = (n * w - kstart) + row - col
        valid = (dist >= 0) & (dist < w)
        dist_f = dist.astype(F32)
        for h in range(SWA_HEADS):
            g = h // SWA_GROUP
            q = q_ref[r * w:(r + 1) * w, h * HEAD_DIM:(h + 1) * HEAD_DIM]
            k = k_ref[pl.ds(kstart, 2 * w), g * HEAD_DIM:(g + 1) * HEAD_DIM]
            v = v_ref[pl.ds(kstart, 2 * w), g * HEAD_DIM:(g + 1) * HEAD_DIM]
            s = lax.dot_general(q, k, (((1,), (1,)), ((), ())), preferred_element_type=F32)
            s = s - slope_ref[h] * dist_f
            s = jnp.where(valid, s, NEG_INF)
            sink = sink_ref[h]
            m = jnp.maximum(jnp.max(s, axis=-1, keepdims=True), sink)
            p = jnp.exp(s - m)
            denom = jnp.sum(p, axis=-1, keepdims=True) + jnp.exp(sink - m)
            o = jnp.dot(p.astype(BF16), v, preferred_element_type=F32)
            o_ref[r * w:(r + 1) * w, h * HEAD_DIM:(h + 1) * HEAD_DIM] = (o / denom).astype(o_ref.dtype)


def _swa(rest, sinks, slopes, *, batch, seq, tq=512):
    t = rest.shape[0]
    nq = seq // tq
    smem = pl.BlockSpec(memory_space=pltpu.SMEM)
    return pl.pallas_call(
        _swa_kernel,
        out_shape=jax.ShapeDtypeStruct((t, SWA_Q_W), BF16),
        grid=(batch, nq),
        in_specs=[
            pl.BlockSpec((tq, SWA_Q_W), lambda b, i: (b * nq + i, REST_SQ // SWA_Q_W)),
            pl.BlockSpec((seq, SWA_KV_W), lambda b, i: (b, REST_SK // SWA_KV_W)),
            pl.BlockSpec((seq, SWA_KV_W), lambda b, i: (b, REST_SV // SWA_KV_W)),
            smem, smem,
        ],
        out_specs=pl.BlockSpec((tq, SWA_Q_W), lambda b, i: (b * nq + i, 0)),
        compiler_params=_cparams(("parallel", "arbitrary")),
        name="swa",
    )(rest, rest, rest, sinks, slopes)


def _mem_kv_kernel(mem_ref, g_ref, wk_ref, wv_ref, kg_ref, mk_ref, mv_ref):
    mn = _rms(mem_ref[...], g_ref[...]).astype(BF16)
    k = jnp.dot(mn, wk_ref[...], preferred_element_type=F32)
    v = jnp.dot(mn, wv_ref[...], preferred_element_type=F32)
    for h in range(MEM_HEADS):
        hs = slice(h * HEAD_DIM, (h + 1) * HEAD_DIM)
        mk_ref[:, hs] = _rms(k[:, hs], kg_ref[...]).astype(BF16)
    mv_ref[...] = v.astype(BF16)


def _mem_kv(mem2d, gain, wk, wv, kgain, *, batch, mem_len):
    d = mem2d.shape[1]
    full = lambda shape: pl.BlockSpec(shape, lambda b: (0, 0))
    out = jax.ShapeDtypeStruct((batch * mem_len, MEM_W), BF16)
    return pl.pallas_call(
        _mem_kv_kernel,
        out_shape=(out, out),
        grid=(batch,),
        in_specs=[
            pl.BlockSpec((mem_len, d), lambda b: (b, 0)),
            full((1, d)), full((d, MEM_W)), full((d, MEM_W)), full((1, HEAD_DIM)),
        ],
        out_specs=(pl.BlockSpec((mem_len, MEM_W), lambda b: (b, 0)),
                   pl.BlockSpec((mem_len, MEM_W), lambda b: (b, 0))),
        compiler_params=_cparams(("parallel",)),
        name="mem_kv",
    )(mem2d, gain, wk, wv, kgain)


def _mem_attn_kernel(q_ref, mk_ref, mv_ref, o_ref):
    for h in range(MEM_HEADS):
        hs = slice(h * HEAD_DIM, (h + 1) * HEAD_DIM)
        s = lax.dot_general(q_ref[:, hs], mk_ref[:, hs], (((1,), (1,)), ((), ())),
                            preferred_element_type=F32)
        m = jnp.max(s, axis=-1, keepdims=True)
        p = jnp.exp(s - m)
        denom = jnp.sum(p, axis=-1, keepdims=True)
        o = jnp.dot(p.astype(BF16), mv_ref[:, hs], preferred_element_type=F32)
        o_ref[:, hs] = (o / denom).astype(o_ref.dtype)


def _mem_attn(rest, mk, mv, *, batch, seq, mem_len, tq=512):
    t = rest.shape[0]
    nq = seq // tq
    return pl.pallas_call(
        _mem_attn_kernel,
        out_shape=jax.ShapeDtypeStruct((t, MEM_W), BF16),
        grid=(batch, nq),
        in_specs=[
            pl.BlockSpec((tq, MEM_W), lambda b, i: (b * nq + i, REST_MQ // MEM_W)),
            pl.BlockSpec((mem_len, MEM_W), lambda b, i: (b, 0)),
            pl.BlockSpec((mem_len, MEM_W), lambda b, i: (b, 0)),
        ],
        out_specs=pl.BlockSpec((tq, MEM_W), lambda b, i: (b * nq + i, 0)),
        compiler_params=_cparams(("parallel", "arbitrary")),
        name="mem_attn",
    )(rest, mk, mv)


def _out_proj_kernel(x_ref, a_ref, b_ref, c_ref, w_ref, o_ref):
    acc = jnp.dot(a_ref[...], w_ref[0:FOX_W, :], preferred_element_type=F32)
    acc += jnp.dot(b_ref[...], w_ref[FOX_W:FOX_W + SWA_Q_W, :], preferred_element_type=F32)
    acc += jnp.dot(c_ref[...], w_ref[FOX_W + SWA_Q_W:, :], preferred_element_type=F32)
    o_ref[...] = x_ref[...] + acc


def _out_proj(x, a, b, c, w, *, tm=512):
    t, d = x.shape
    return pl.pallas_call(
        _out_proj_kernel,
        out_shape=jax.ShapeDtypeStruct((t, d), F32),
        grid=(t // tm,),
        in_specs=[
            pl.BlockSpec((tm, d), lambda i: (i, 0)),
            pl.BlockSpec((tm, FOX_W), lambda i: (i, 0)),
            pl.BlockSpec((tm, SWA_Q_W), lambda i: (i, 0)),
            pl.BlockSpec((tm, MEM_W), lambda i: (i, 0)),
            pl.BlockSpec(w.shape, lambda i: (0, 0)),
        ],
        out_specs=pl.BlockSpec((tm, d), lambda i: (i, 0)),
        compiler_params=_cparams(("parallel",)),
        name="out_proj",
    )(x, a, b, c, w)


def _alibi_slopes(n):
    return jnp.asarray([2.0 ** (-8.0 * i / n) for i in range(1, n + 1)], dtype=F32)


def _pack_w_in(w_in):
    fox = w_in[:, :3 * FOX_W].astype(BF16)
    fl = w_in[:, 3 * FOX_W:3 * FOX_W + FOX_HEADS]
    tail = w_in[:, 3 * FOX_W + FOX_HEADS:].astype(BF16)
    w_main = jnp.concatenate([fox, tail], axis=1)
    w_f = jnp.pad(fl, ((0, 0), (0, LANES - FOX_HEADS))).astype(BF16)
    return w_main, w_f


def kernel(x, mem, ffn1_norm, ffn1_gate, ffn1_up, ffn1_down, mix_norm, mem_norm, w_in, forget_bias, w_mem_k, w_mem_v, fox_q_gain, fox_k_gain, swa_q_gain, swa_k_gain, swa_sinks, mem_q_gain, mem_k_gain, w_out, ffn2_norm, ffn2_gate, ffn2_up, ffn2_down):
    batch, seq, d = x.shape
    mem_len = mem.shape[1]
    depth = w_in.shape[0]
    scale = HEAD_DIM ** -0.5
    slopes = _alibi_slopes(SWA_HEADS)
    x2 = x.reshape(batch * seq, d)
    mem2 = mem.reshape(batch * mem_len, d)
    zeros = jnp.zeros((HEAD_DIM,), F32)

    for l in range(depth):
        x2, ffn2_w = _ffn(x2, ffn1_norm[l][None], ffn1_gate[l].astype(BF16), ffn1_up[l].astype(BF16),
                          ffn1_down[l].astype(BF16),
                          next_weights=(ffn2_gate[l], ffn2_up[l], ffn2_down[l]))

        w_main, w_f = _pack_w_in(w_in[l])
        head_gain = jnp.stack([fox_q_gain[l] * (scale * LOG2E), fox_k_gain[l], swa_q_gain[l] * scale,
                               swa_k_gain[l], mem_q_gain[l] * scale, zeros, zeros, zeros])
        fbias = jnp.pad(forget_bias[l], (0, LANES - FOX_HEADS))[None]
        qt, ka, vt, rest, anc = _in_proj(x2, mix_norm[l][None], w_main, w_f, head_gain, fbias, seq=seq)

        n_sub = (x2.shape[0] // FOX_BLK) // anc.shape[0]
        anchors = anc[:, :n_sub, :SUBLANES].reshape(-1)
        out_a = _fox(anchors, qt, ka, vt, batch=batch, seq=seq)
        out_b = _swa(rest, swa_sinks[l], slopes, batch=batch, seq=seq)
        mk, mv = _mem_kv(mem2, mem_norm[l][None], w_mem_k[l].astype(BF16), w_mem_v[l].astype(BF16),
                         mem_k_gain[l][None], batch=batch, mem_len=mem_len)
        out_c = _mem_attn(rest, mk, mv, batch=batch, seq=seq, mem_len=mem_len)

        x2 = _out_proj(x2, out_a, out_b, out_c, w_out[l].astype(BF16))
        x2, _ = _ffn(x2, ffn2_norm[l][None], *ffn2_w)
    return x2.reshape(batch, seq, d)
```

```python
import functools

import jax
import jax.numpy as jnp
from jax import lax
from jax.experimental import pallas as pl
from jax.experimental.pallas import tpu as pltpu

F32 = jnp.float32
BF16 = jnp.bfloat16

HEAD_DIM = 128
FOX_HEADS = 6
SWA_HEADS = 6
SWA_KV_HEADS = 2
SWA_GROUP = SWA_HEADS // SWA_KV_HEADS
MEM_HEADS = 4
WINDOW = 128
EPS = 1e-6
NEG_INF = -1e30
LOG2E = 1.4426950408889634
FOX_BLK = 256

FOX_W = FOX_HEADS * HEAD_DIM
SWA_Q_W = SWA_HEADS * HEAD_DIM
SWA_KV_W = SWA_KV_HEADS * HEAD_DIM
MEM_W = MEM_HEADS * HEAD_DIM

PROJ_W = 3 * FOX_W + SWA_Q_W + 2 * SWA_KV_W + MEM_W
REST_W = SWA_Q_W + 2 * SWA_KV_W + MEM_W
REST_SQ, REST_SK, REST_MQ, REST_SV = 0, SWA_Q_W, SWA_Q_W + SWA_KV_W, SWA_Q_W + SWA_KV_W + MEM_W
GAIN_FQ, GAIN_FK, GAIN_SQ, GAIN_SK, GAIN_MQ = range(5)

LANES = 128
SUBLANES = 8
MXU_N = 256

VMEM_LIMIT = 58 * 1024 * 1024


def _cparams(sem):
    return pltpu.CompilerParams(dimension_semantics=sem, vmem_limit_bytes=VMEM_LIMIT)


def _rms(x, gain):
    ms = jnp.mean(x * x, axis=-1, keepdims=True)
    return x * lax.rsqrt(ms + EPS) * gain


def _ffn_kernel(x_ref, g_ref, wg_ref, wu_ref, wd_ref, *rest, n_chunk, n_cast):
    cast_in, o_ref, cast_out, xn_ref = rest[:n_cast], rest[n_cast], rest[n_cast + 1:-1], rest[-1]
    j = pl.program_id(1)

    for src, dst in zip(cast_in, cast_out):
        dst[...] = src[...].astype(BF16)

    @pl.when(j == 0)
    def _():
        x = x_ref[...]
        xn_ref[...] = _rms(x, g_ref[...]).astype(BF16)
        o_ref[...] = x

    xn = xn_ref[...]
    gate = jnp.dot(xn, wg_ref[...], preferred_element_type=F32)
    up = jnp.dot(xn, wu_ref[...], preferred_element_type=F32)
    h = (gate * jax.nn.sigmoid(gate) * (0.5 * up)).astype(BF16)
    d_model = o_ref.shape[1]
    cw = d_model // n_chunk
    for c in range(n_chunk):
        cs = slice(c * cw, (c + 1) * cw)
        o_ref[:, cs] += jnp.dot(h, wd_ref[:, cs], preferred_element_type=F32)


def _ffn(x, gain, wg, wu, wd, *, next_weights=(), tm=1024, tf=512):
    t, d = x.shape
    dff = wg.shape[1]
    gi, gj = t // tm, dff // tf
    cast_specs, cast_shapes = [], []
    for w in next_weights:
        ri, cj = (gi, gj) if w.shape[1] == dff else (gj, gi)
        blk = (w.shape[0] // ri, w.shape[1] // cj)
        assert blk[0] % SUBLANES == 0 and blk[1] % LANES == 0
        index = (lambda i, j: (i, j)) if w.shape[1] == dff else (lambda i, j: (j, i))
        cast_specs.append(pl.BlockSpec(blk, index))
        cast_shapes.append(jax.ShapeDtypeStruct(w.shape, BF16))
    out = pl.pallas_call(
        functools.partial(_ffn_kernel, n_chunk=4, n_cast=len(next_weights)),
        out_shape=(jax.ShapeDtypeStruct((t, d), F32), *cast_shapes),
        grid=(gi, gj),
        in_specs=[
            pl.BlockSpec((tm, d), lambda i, j: (i, 0)),
            pl.BlockSpec((1, d), lambda i, j: (0, 0)),
            pl.BlockSpec((d, tf), lambda i, j: (0, j)),
            pl.BlockSpec((d, tf), lambda i, j: (0, j)),
            pl.BlockSpec((tf, d), lambda i, j: (j, 0)),
            *cast_specs,
        ],
        out_specs=(pl.BlockSpec((tm, d), lambda i, j: (i, 0)), *cast_specs),
        scratch_shapes=[pltpu.VMEM((tm, d), BF16)],
        compiler_params=_cparams(("parallel", "arbitrary")),
        name="ffn",
    )(x, gain, wg, wu, wd, *next_weights)
    return out[0], out[1:]


def _in_proj_kernel(x_ref, g_ref, w_ref, wf_ref, hg_ref, fb_ref,
                    qt_ref, ka_ref, vt_ref, rest_ref, anc_ref, carry_ref, *, blocks_per_seq):
    i = pl.program_id(0)
    tm = x_ref.shape[0]
    hn = _rms(x_ref[...], g_ref[...]).astype(BF16)

    logit = jnp.dot(hn, wf_ref[...], preferred_element_type=F32)
    c = jax.nn.log_sigmoid(logit + fb_ref[...]) * LOG2E
    row = lax.broadcasted_iota(jnp.int32, c.shape, 0) % FOX_BLK
    sh = 1
    while sh < FOX_BLK:
        c = c + jnp.where(row >= sh, pltpu.roll(c, sh, 0), 0.0)
        sh *= 2

    @pl.when(i % blocks_per_seq == 0)
    def _():
        carry_ref[...] = jnp.zeros_like(carry_ref)

    anchor = carry_ref[0:1, :]
    anc_ref[...] = jnp.zeros_like(anc_ref)
    for r in range(tm // FOX_BLK):
        anc_ref[0, r:r + 1, :] = anchor
        anchor = anchor + c[(r + 1) * FOX_BLK - 1:(r + 1) * FOX_BLK, :]
    carry_ref[0:1, :] = anchor

    n_sub = tm // FOX_BLK
    lane = lax.broadcasted_iota(jnp.int32, c.shape, 1)
    sub = lax.broadcasted_iota(jnp.int32, (HEAD_DIM, FOX_BLK), 0)
    ct = c.T
    for h in range(FOX_HEADS):
        ch = c[:, h:h + 1]
        hi = ch.astype(BF16).astype(F32)
        lo = ch - hi
        ak = jnp.where(lane < 2, 1.0, jnp.where(lane == 2, -hi, jnp.where(lane == 3, -lo, 0.0)))
        ka_ref[:, (2 * h + 1) * HEAD_DIM:(2 * h + 2) * HEAD_DIM] = ak.astype(BF16)
        for r in range(n_sub):
            cr = ct[h:h + 1, r * FOX_BLK:(r + 1) * FOX_BLK]
            hi_r = cr.astype(BF16).astype(F32)
            lo_r = cr - hi_r
            aq = jnp.where(sub == 0, hi_r, jnp.where(sub == 1, lo_r, jnp.where(sub < 4, 1.0, 0.0)))
            qt_ref[r, (2 * h + 1) * HEAD_DIM:(2 * h + 2) * HEAD_DIM, :] = aq.astype(BF16)

    def normed(ph, row_id):
        return _rms(ph, hg_ref[row_id:row_id + 1, :]).astype(BF16)

    heads_per_chunk = MXU_N // HEAD_DIM
    for ck in range(PROJ_W // MXU_N):
        p = jnp.dot(hn, w_ref[:, ck * MXU_N:(ck + 1) * MXU_N], preferred_element_type=F32)
        for e in range(heads_per_chunk):
            col = ck * MXU_N + e * HEAD_DIM
            ph = p[:, e * HEAD_DIM:(e + 1) * HEAD_DIM]
            if col < FOX_W:
                h = col // HEAD_DIM
                qn = _rms(ph, hg_ref[GAIN_FQ:GAIN_FQ + 1, :])
                for r in range(n_sub):
                    qt_ref[r, 2 * h * HEAD_DIM:(2 * h + 1) * HEAD_DIM, :] = (
                        qn[r * FOX_BLK:(r + 1) * FOX_BLK, :].T.astype(BF16))
            elif col < 2 * FOX_W:
                h = (col - FOX_W) // HEAD_DIM
                ka_ref[:, 2 * h * HEAD_DIM:(2 * h + 1) * HEAD_DIM] = normed(ph, GAIN_FK)
            elif col < 3 * FOX_W:
                h = (col - 2 * FOX_W) // HEAD_DIM
                for r in range(n_sub):
                    vt_ref[r, h * HEAD_DIM:(h + 1) * HEAD_DIM, :] = (
                        ph[r * FOX_BLK:(r + 1) * FOX_BLK, :].T.astype(BF16))
            else:
                wc = col - 3 * FOX_W
                if wc < SWA_Q_W:
                    val, rc = normed(ph, GAIN_SQ), REST_SQ + wc
                elif wc < SWA_Q_W + SWA_KV_W:
                    val, rc = normed(ph, GAIN_SK), REST_SK + wc - SWA_Q_W
                elif wc < SWA_Q_W + 2 * SWA_KV_W:
                    val, rc = ph.astype(BF16), REST_SV + wc - SWA_Q_W - SWA_KV_W
                else:
                    val, rc = normed(ph, GAIN_MQ), REST_MQ + wc - SWA_Q_W - 2 * SWA_KV_W
                rest_ref[:, rc:rc + HEAD_DIM] = val


def _in_proj(x, gain, w, wf, head_gain, fbias, *, seq, tm=512):
    t, d = x.shape
    assert seq % tm == 0 and tm % FOX_BLK == 0 and tm // FOX_BLK <= SUBLANES
    const = lambda shape: pl.BlockSpec(shape, lambda i: (0, 0), pipeline_mode=pl.Buffered(1))
    return pl.pallas_call(
        functools.partial(_in_proj_kernel, blocks_per_seq=seq // tm),
        out_shape=(
            jax.ShapeDtypeStruct((t // FOX_BLK, 2 * FOX_W, FOX_BLK), BF16),
            jax.ShapeDtypeStruct((t, 2 * FOX_W), BF16),
            jax.ShapeDtypeStruct((t // FOX_BLK, FOX_W, FOX_BLK), BF16),
            jax.ShapeDtypeStruct((t, REST_W), BF16),
            jax.ShapeDtypeStruct((t // tm, SUBLANES, LANES), F32),
        ),
        grid=(t // tm,),
        in_specs=[
            pl.BlockSpec((tm, d), lambda i: (i, 0)),
            const((1, d)),
            const((d, PROJ_W)),
            const((d, LANES)),
            const((SUBLANES, HEAD_DIM)),
            const((1, LANES)),
        ],
        out_specs=(
            pl.BlockSpec((tm // FOX_BLK, 2 * FOX_W, FOX_BLK), lambda i: (i, 0, 0)),
            pl.BlockSpec((tm, 2 * FOX_W), lambda i: (i, 0)),
            pl.BlockSpec((tm // FOX_BLK, FOX_W, FOX_BLK), lambda i: (i, 0, 0)),
            pl.BlockSpec((tm, REST_W), lambda i: (i, 0)),
            pl.BlockSpec((1, SUBLANES, LANES), lambda i: (i, 0, 0)),
        ),
        scratch_shapes=[pltpu.VMEM((SUBLANES, LANES), F32)],
        compiler_params=_cparams(("arbitrary",)),
        name="in_proj",
    )(x, gain, w, wf, head_gain, fbias)


def _fox_kernel(anc_ref, q_ref, k_ref, vt_ref, o_ref, m_ref, l_ref, acc_ref, st_ref, *, nblk):
    b = pl.program_id(0)
    i = pl.program_id(1)
    blk = q_ref.shape[2]
    aw = 2 * HEAD_DIM
    m_ref[...] = jnp.full(m_ref.shape, NEG_INF, F32)
    l_ref[...] = jnp.zeros(l_ref.shape, F32)
    acc_ref[...] = jnp.zeros(acc_ref.shape, F32)
    krow = lax.broadcasted_iota(jnp.int32, (blk, blk), 0)
    qcol = lax.broadcasted_iota(jnp.int32, (blk, blk), 1)

    def scores(j, slot):
        ks = pl.ds(pl.multiple_of(j * blk, blk), blk)
        for h in range(FOX_HEADS):
            st_ref[slot, h] = jnp.dot(k_ref[ks, h * aw:(h + 1) * aw], q_ref[0, h * aw:(h + 1) * aw, :],
                                      preferred_element_type=F32)

    def softmax_pv(j, slot, masked):
        for h in range(FOX_HEADS):
            st = st_ref[slot, h]
            d = anc_ref[(b * nblk + i) * SUBLANES + h] - anc_ref[(b * nblk + j) * SUBLANES + h]
            if masked:
                st = jnp.where(krow <= qcol, st, NEG_INF)
            m_prev = m_ref[h]
            m_new = jnp.maximum(m_prev, jnp.max(st, axis=0, keepdims=True) + d)
            alpha = jnp.exp2(m_prev - m_new)
            p = jnp.exp2(st - (m_new - d))
            l_ref[h] = alpha * l_ref[h] + jnp.sum(p, axis=0, keepdims=True)
            pv = jnp.dot(vt_ref[j, h * HEAD_DIM:(h + 1) * HEAD_DIM, :], p.astype(BF16),
                         preferred_element_type=F32)
            acc_ref[h] = alpha * acc_ref[h] + pv
            m_ref[h] = m_new

    def body(t, carry):
        j = 2 * t
        scores(j + 1, 1)
        softmax_pv(j, 0, False)
        scores(j + 2, 0)
        softmax_pv(j + 1, 1, False)
        return carry

    scores(0, 0)
    lax.fori_loop(0, i // 2, body, 0)

    @pl.when(i % 2 == 0)
    def _():
        softmax_pv(i, 0, True)

    @pl.when(i % 2 == 1)
    def _():
        scores(i, 1)
        softmax_pv(i - 1, 0, False)
        softmax_pv(i, 1, True)

    for h in range(FOX_HEADS):
        o_ref[:, h * HEAD_DIM:(h + 1) * HEAD_DIM] = (acc_ref[h] / l_ref[h]).T.astype(o_ref.dtype)


def _fox(anchors, qt, ka, vt, *, batch, seq):
    t = ka.shape[0]
    blk = FOX_BLK
    nblk = seq // blk
    return pl.pallas_call(
        functools.partial(_fox_kernel, nblk=nblk),
        out_shape=jax.ShapeDtypeStruct((t, FOX_W), BF16),
        grid=(batch, nblk),
        in_specs=[
            pl.BlockSpec(memory_space=pltpu.SMEM),
            pl.BlockSpec((1, 2 * FOX_W, blk), lambda b, i: (b * nblk + i, 0, 0)),
            pl.BlockSpec((seq, 2 * FOX_W), lambda b, i: (b, 0)),
            pl.BlockSpec((nblk, FOX_W, blk), lambda b, i: (b, 0, 0)),
        ],
        out_specs=pl.BlockSpec((blk, FOX_W), lambda b, i: (b * nblk + i, 0)),
        scratch_shapes=[
            pltpu.VMEM((FOX_HEADS, 1, blk), F32),
            pltpu.VMEM((FOX_HEADS, 1, blk), F32),
            pltpu.VMEM((FOX_HEADS, HEAD_DIM, blk), F32),
            pltpu.VMEM((2, FOX_HEADS, blk, blk), F32),
        ],
        compiler_params=_cparams(("parallel", "arbitrary")),
        name="fox",
    )(anchors, qt, ka, vt)


def _swa_kernel(q_ref, k_ref, v_ref, sink_ref, slope_ref, o_ref):
    qi = pl.program_id(1)
    tq = q_ref.shape[0]
    w = WINDOW
    row = lax.broadcasted_iota(jnp.int32, (w, 2 * w), 0)
    col = lax.broadcasted_iota(jnp.int32, (w, 2 * w), 1)

    for r in range(tq // w):
        n = qi * (tq // w) + r
        kstart = pl.multiple_of(jnp.maximum(n - 1, 0) * w, w)
        dist = (n * w - kstart) + row - col
        valid = (dist >= 0) & (dist < w)
        dist_f = dist.astype(F32)
        for h in range(SWA_HEADS):
            g = h // SWA_GROUP
            q = q_ref[r * w:(r + 1) * w, h * HEAD_DIM:(h + 1) * HEAD_DIM]
            k = k_ref[pl.ds(kstart, 2 * w), g * HEAD_DIM:(g + 1) * HEAD_DIM]
            v = v_ref[pl.ds(kstart, 2 * w), g * HEAD_DIM:(g + 1) * HEAD_DIM]
            s = lax.dot_general(q, k, (((1,), (1,)), ((), ())), preferred_element_type=F32)
            s = s - slope_ref[h] * dist_f
            s = jnp.where(valid, s, NEG_INF)
            sink = sink_ref[h]
            m = jnp.maximum(jnp.max(s, axis=-1, keepdims=True), sink)
            p = jnp.exp(s - m)
            denom = jnp.sum(p, axis=-1, keepdims=True) + jnp.exp(sink - m)
            o = jnp.dot(p.astype(BF16), v, preferred_element_type=F32)
            o_ref[r * w:(r + 1) * w, h * HEAD_DIM:(h + 1) * HEAD_DIM] = (o / denom).astype(o_ref.dtype)


def _swa(rest, sinks, slopes, *, batch, seq, tq=512):
    t = rest.shape[0]
    nq = seq // tq
    smem = pl.BlockSpec(memory_space=pltpu.SMEM)
    return pl.pallas_call(
        _swa_kernel,
        out_shape=jax.ShapeDtypeStruct((t, SWA_Q_W), BF16),
        grid=(batch, nq),
        in_specs=[
            pl.BlockSpec((tq, SWA_Q_W), lambda b, i: (b * nq + i, REST_SQ // SWA_Q_W)),
            pl.BlockSpec((seq, SWA_KV_W), lambda b, i: (b, REST_SK // SWA_KV_W)),
            pl.BlockSpec((seq, SWA_KV_W), lambda b, i: (b, REST_SV // SWA_KV_W)),
            smem, smem,
        ],
        out_specs=pl.BlockSpec((tq, SWA_Q_W), lambda b, i: (b * nq + i, 0)),
        compiler_params=_cparams(("parallel", "arbitrary")),
        name="swa",
    )(rest, rest, rest, sinks, slopes)


def _mem_kv_kernel(mem_ref, g_ref, wk_ref, wv_ref, kg_ref, mk_ref, mv_ref):
    mn = _rms(mem_ref[...], g_ref[...]).astype(BF16)
    k = jnp.dot(mn, wk_ref[...], preferred_element_type=F32)
    v = jnp.dot(mn, wv_ref[...], preferred_element_type=F32)
    for h in range(MEM_HEADS):
        hs = slice(h * HEAD_DIM, (h + 1) * HEAD_DIM)
        mk_ref[:, hs] = _rms(k[:, hs], kg_ref[...]).astype(BF16)
    mv_ref[...] = v.astype(BF16)


def _mem_kv(mem2d, gain, wk, wv, kgain, *, batch, mem_len):
    d = mem2d.shape[1]
    full = lambda shape: pl.BlockSpec(shape, lambda b: (0, 0))
    out = jax.ShapeDtypeStruct((batch * mem_len, MEM_W), BF16)
    return pl.pallas_call(
        _mem_kv_kernel,
        out_shape=(out, out),
        grid=(batch,),
        in_specs=[
            pl.BlockSpec((mem_len, d), lambda b: (b, 0)),
            full((1, d)), full((d, MEM_W)), full((d, MEM_W)), full((1, HEAD_DIM)),
        ],
        out_specs=(pl.BlockSpec((mem_len, MEM_W), lambda b: (b, 0)),
                   pl.BlockSpec((mem_len, MEM_W), lambda b: (b, 0))),
        compiler_params=_cparams(("parallel",)),
        name="mem_kv",
    )(mem2d, gain, wk, wv, kgain)


def _mem_attn_kernel(q_ref, mk_ref, mv_ref, o_ref):
    for h in range(MEM_HEADS):
        hs = slice(h * HEAD_DIM, (h + 1) * HEAD_DIM)
        s = lax.dot_general(q_ref[:, hs], mk_ref[:, hs], (((1,), (1,)), ((), ())),
                            preferred_element_type=F32)
        m = jnp.max(s, axis=-1, keepdims=True)
        p = jnp.exp(s - m)
        denom = jnp.sum(p, axis=-1, keepdims=True)
        o = jnp.dot(p.astype(BF16), mv_ref[:, hs], preferred_element_type=F32)
        o_ref[:, hs] = (o / denom).astype(o_ref.dtype)


def _mem_attn(rest, mk, mv, *, batch, seq, mem_len, tq=512):
    t = rest.shape[0]
    nq = seq // tq
    return pl.pallas_call(
        _mem_attn_kernel,
        out_shape=jax.ShapeDtypeStruct((t, MEM_W), BF16),
        grid=(batch, nq),
        in_specs=[
            pl.BlockSpec((tq, MEM_W), lambda b, i: (b * nq + i, REST_MQ // MEM_W)),
            pl.BlockSpec((mem_len, MEM_W), lambda b, i: (b, 0)),
            pl.BlockSpec((mem_len, MEM_W), lambda b, i: (b, 0)),
        ],
        out_specs=pl.BlockSpec((tq, MEM_W), lambda b, i: (b * nq + i, 0)),
        compiler_params=_cparams(("parallel", "arbitrary")),
        name="mem_attn",
    )(rest, mk, mv)


def _out_proj_kernel(x_ref, a_ref, b_ref, c_ref, w_ref, o_ref):
    acc = jnp.dot(a_ref[...], w_ref[0:FOX_W, :], preferred_element_type=F32)
    acc += jnp.dot(b_ref[...], w_ref[FOX_W:FOX_W + SWA_Q_W, :], preferred_element_type=F32)
    acc += jnp.dot(c_ref[...], w_ref[FOX_W + SWA_Q_W:, :], preferred_element_type=F32)
    o_ref[...] = x_ref[...] + acc


def _out_proj(x, a, b, c, w, *, tm=512):
    t, d = x.shape
    return pl.pallas_call(
        _out_proj_kernel,
        out_shape=jax.ShapeDtypeStruct((t, d), F32),
        grid=(t // tm,),
        in_specs=[
            pl.BlockSpec((tm, d), lambda i: (i, 0)),
            pl.BlockSpec((tm, FOX_W), lambda i: (i, 0)),
            pl.BlockSpec((tm, SWA_Q_W), lambda i: (i, 0)),
            pl.BlockSpec((tm, MEM_W), lambda i: (i, 0)),
            pl.BlockSpec(w.shape, lambda i: (0, 0)),
        ],
        out_specs=pl.BlockSpec((tm, d), lambda i: (i, 0)),
        compiler_params=_cparams(("parallel",)),
        name="out_proj",
    )(x, a, b, c, w)


def _alibi_slopes(n):
    return jnp.asarray([2.0 ** (-8.0 * i / n) for i in range(1, n + 1)], dtype=F32)


def _pack_w_in_kernel(w_ref, main_ref, f_ref):
    x = w_ref[0]
    fl0 = 3 * FOX_W
    main_ref[:, :fl0] = x[:, :fl0].astype(BF16)
    main_ref[:, fl0:] = x[:, fl0 + FOX_HEADS:].astype(BF16)
    lane = lax.broadcasted_iota(jnp.int32, f_ref.shape, 1)
    f_ref[...] = jnp.where(lane < FOX_HEADS, x[:, fl0:fl0 + LANES], 0.0).astype(BF16)


def _pack_w_in(w_in, l, *, rows=256):
    d, n = w_in.shape[1:]
    assert n == PROJ_W + FOX_HEADS
    return pl.pallas_call(
        _pack_w_in_kernel,
        out_shape=(jax.ShapeDtypeStruct((d, PROJ_W), BF16), jax.ShapeDtypeStruct((d, LANES), BF16)),
        grid=(d // rows,),
        in_specs=[pl.BlockSpec((1, rows, n), lambda i: (l, i, 0))],
        out_specs=(pl.BlockSpec((rows, PROJ_W), lambda i: (i, 0)),
                   pl.BlockSpec((rows, LANES), lambda i: (i, 0))),
        compiler_params=_cparams(("parallel",)),
        name="pack_w_in",
    )(w_in)


def kernel(x, mem, ffn1_norm, ffn1_gate, ffn1_up, ffn1_down, mix_norm, mem_norm, w_in, forget_bias, w_mem_k, w_mem_v, fox_q_gain, fox_k_gain, swa_q_gain, swa_k_gain, swa_sinks, mem_q_gain, mem_k_gain, w_out, ffn2_norm, ffn2_gate, ffn2_up, ffn2_down):
    batch, seq, d = x.shape
    mem_len = mem.shape[1]
    depth = w_in.shape[0]
    scale = HEAD_DIM ** -0.5
    slopes = _alibi_slopes(SWA_HEADS)
    x2 = x.reshape(batch * seq, d)
    mem2 = mem.reshape(batch * mem_len, d)
    zeros = jnp.zeros((HEAD_DIM,), F32)

    for l in range(depth):
        x2, ffn2_w = _ffn(x2, ffn1_norm[l][None], ffn1_gate[l].astype(BF16), ffn1_up[l].astype(BF16),
                          ffn1_down[l].astype(BF16),
                          next_weights=(ffn2_gate[l], ffn2_up[l], ffn2_down[l]))

        w_main, w_f = _pack_w_in(w_in, l)
        head_gain = jnp.stack([fox_q_gain[l] * (scale * LOG2E), fox_k_gain[l], swa_q_gain[l] * scale,
                               swa_k_gain[l], mem_q_gain[l] * scale, zeros, zeros, zeros])
        fbias = jnp.pad(forget_bias[l], (0, LANES - FOX_HEADS))[None]
        qt, ka, vt, rest, anc = _in_proj(x2, mix_norm[l][None], w_main, w_f, head_gain, fbias, seq=seq)

        n_sub = (x2.shape[0] // FOX_BLK) // anc.shape[0]
        anchors = anc[:, :n_sub, :SUBLANES].reshape(-1)
        out_a = _fox(anchors, qt, ka, vt, batch=batch, seq=seq)
        out_b = _swa(rest, swa_sinks[l], slopes, batch=batch, seq=seq)
        mk, mv = _mem_kv(mem2, mem_norm[l][None], w_mem_k[l].astype(BF16), w_mem_v[l].astype(BF16),
                         mem_k_gain[l][None], batch=batch, mem_len=mem_len)
        out_c = _mem_attn(rest, mk, mv, batch=batch, seq=seq, mem_len=mem_len)

        x2 = _out_proj(x2, out_a, out_b, out_c, w_out[l].astype(BF16))
        x2, _ = _ffn(x2, ffn2_norm[l][None], *ffn2_w)
    return x2.reshape(batch, seq, d)
```

```python
import functools

import jax
import jax.numpy as jnp
from jax import lax
from jax.experimental import pallas as pl
from jax.experimental.pallas import tpu as pltpu

F32 = jnp.float32
BF16 = jnp.bfloat16

HEAD_DIM = 128
FOX_HEADS = 6
SWA_HEADS = 6
SWA_KV_HEADS = 2
SWA_GROUP = SWA_HEADS // SWA_KV_HEADS
MEM_HEADS = 4
WINDOW = 128
EPS = 1e-6
NEG_INF = -1e30
LOG2E = 1.4426950408889634
FOX_BLK = 256

FOX_W = FOX_HEADS * HEAD_DIM
SWA_Q_W = SWA_HEADS * HEAD_DIM
SWA_KV_W = SWA_KV_HEADS * HEAD_DIM
MEM_W = MEM_HEADS * HEAD_DIM

PROJ_W = 3 * FOX_W + SWA_Q_W + 2 * SWA_KV_W + MEM_W
REST_W = SWA_Q_W + 2 * SWA_KV_W + MEM_W
REST_SQ, REST_SK, REST_MQ, REST_SV = 0, SWA_Q_W, SWA_Q_W + SWA_KV_W, SWA_Q_W + SWA_KV_W + MEM_W
GAIN_FQ, GAIN_FK, GAIN_SQ, GAIN_SK, GAIN_MQ = range(5)

LANES = 128
SUBLANES = 8
MXU_N = 256

VMEM_LIMIT = 58 * 1024 * 1024


def _cparams(sem):
    return pltpu.CompilerParams(dimension_semantics=sem, vmem_limit_bytes=VMEM_LIMIT)


def _rms(x, gain):
    ms = jnp.mean(x * x, axis=-1, keepdims=True)
    return x * lax.rsqrt(ms + EPS) * gain


def _ffn_kernel(x_ref, g_ref, wg_ref, wu_ref, wd_ref, *rest, n_chunk, n_cast):
    cast_in, o_ref, cast_out, xn_ref = rest[:n_cast], rest[n_cast], rest[n_cast + 1:-1], rest[-1]
    j = pl.program_id(1)

    for src, dst in zip(cast_in, cast_out):
        dst[...] = src[...].astype(BF16)

    @pl.when(j == 0)
    def _():
        x = x_ref[...]
        xn_ref[...] = _rms(x, g_ref[...]).astype(BF16)
        o_ref[...] = x

    xn = xn_ref[...]
    gate = jnp.dot(xn, wg_ref[...], preferred_element_type=F32)
    up = jnp.dot(xn, wu_ref[...], preferred_element_type=F32)
    h = (gate * jax.nn.sigmoid(gate) * (0.5 * up)).astype(BF16)
    d_model = o_ref.shape[1]
    cw = d_model // n_chunk
    for c in range(n_chunk):
        cs = slice(c * cw, (c + 1) * cw)
        o_ref[:, cs] += jnp.dot(h, wd_ref[:, cs], preferred_element_type=F32)


def _ffn(x, gain, wg, wu, wd, *, next_weights=(), tm=1024, tf=512):
    t, d = x.shape
    dff = wg.shape[1]
    gi, gj = t // tm, dff // tf
    cast_specs, cast_shapes = [], []
    for w in next_weights:
        ri, cj = (gi, gj) if w.shape[1] == dff else (gj, gi)
        blk = (w.shape[0] // ri, w.shape[1] // cj)
        assert blk[0] % SUBLANES == 0 and blk[1] % LANES == 0
        index = (lambda i, j: (i, j)) if w.shape[1] == dff else (lambda i, j: (j, i))
        cast_specs.append(pl.BlockSpec(blk, index))
        cast_shapes.append(jax.ShapeDtypeStruct(w.shape, BF16))
    out = pl.pallas_call(
        functools.partial(_ffn_kernel, n_chunk=4, n_cast=len(next_weights)),
        out_shape=(jax.ShapeDtypeStruct((t, d), F32), *cast_shapes),
        grid=(gi, gj),
        in_specs=[
            pl.BlockSpec((tm, d), lambda i, j: (i, 0)),
            pl.BlockSpec((1, d), lambda i, j: (0, 0)),
            pl.BlockSpec((d, tf), lambda i, j: (0, j)),
            pl.BlockSpec((d, tf), lambda i, j: (0, j)),
            pl.BlockSpec((tf, d), lambda i, j: (j, 0)),
            *cast_specs,
        ],
        out_specs=(pl.BlockSpec((tm, d), lambda i, j: (i, 0)), *cast_specs),
        scratch_shapes=[pltpu.VMEM((tm, d), BF16)],
        compiler_params=_cparams(("parallel", "arbitrary")),
        name="ffn",
    )(x, gain, wg, wu, wd, *next_weights)
    return out[0], out[1:]


def _in_proj_kernel(x_ref, g_ref, w_ref, wf_ref, hg_ref, fb_ref,
                    qt_ref, ka_ref, vt_ref, rest_ref, anc_ref, carry_ref, *, blocks_per_seq):
    i = pl.program_id(0)
    tm = x_ref.shape[0]
    hn = _rms(x_ref[...], g_ref[...]).astype(BF16)

    nt = (((1,), (1,)), ((), ()))
    logit = lax.dot_general(hn, wf_ref[...], nt, preferred_element_type=F32)
    c = jax.nn.log_sigmoid(logit + fb_ref[...]) * LOG2E
    row = lax.broadcasted_iota(jnp.int32, c.shape, 0) % FOX_BLK
    sh = 1
    while sh < FOX_BLK:
        c = c + jnp.where(row >= sh, pltpu.roll(c, sh, 0), 0.0)
        sh *= 2

    @pl.when(i % blocks_per_seq == 0)
    def _():
        carry_ref[...] = jnp.zeros_like(carry_ref)

    anchor = carry_ref[0:1, :]
    anc_ref[...] = jnp.zeros_like(anc_ref)
    for r in range(tm // FOX_BLK):
        anc_ref[0, r:r + 1, :] = anchor
        anchor = anchor + c[(r + 1) * FOX_BLK - 1:(r + 1) * FOX_BLK, :]
    carry_ref[0:1, :] = anchor

    n_sub = tm // FOX_BLK
    lane = lax.broadcasted_iota(jnp.int32, c.shape, 1)
    sub = lax.broadcasted_iota(jnp.int32, (HEAD_DIM, FOX_BLK), 0)
    ct = c.T
    for h in range(FOX_HEADS):
        ch = c[:, h:h + 1]
        hi = ch.astype(BF16).astype(F32)
        lo = ch - hi
        ak = jnp.where(lane < 2, 1.0, jnp.where(lane == 2, -hi, jnp.where(lane == 3, -lo, 0.0)))
        ka_ref[:, (2 * h + 1) * HEAD_DIM:(2 * h + 2) * HEAD_DIM] = ak.astype(BF16)
        for r in range(n_sub):
            cr = ct[h:h + 1, r * FOX_BLK:(r + 1) * FOX_BLK]
            hi_r = cr.astype(BF16).astype(F32)
            lo_r = cr - hi_r
            aq = jnp.where(sub == 0, hi_r, jnp.where(sub == 1, lo_r, jnp.where(sub < 4, 1.0, 0.0)))
            qt_ref[r, (2 * h + 1) * HEAD_DIM:(2 * h + 2) * HEAD_DIM, :] = aq.astype(BF16)

    def normed(ph, row_id):
        return _rms(ph, hg_ref[row_id:row_id + 1, :]).astype(BF16)

    heads_per_chunk = MXU_N // HEAD_DIM
    for ck in range(PROJ_W // MXU_N):
        p = lax.dot_general(hn, w_ref[ck * MXU_N:(ck + 1) * MXU_N, :], nt, preferred_element_type=F32)
        for e in range(heads_per_chunk):
            col = ck * MXU_N + e * HEAD_DIM
            ph = p[:, e * HEAD_DIM:(e + 1) * HEAD_DIM]
            if col < FOX_W:
                h = col // HEAD_DIM
                qn = _rms(ph, hg_ref[GAIN_FQ:GAIN_FQ + 1, :])
                for r in range(n_sub):
                    qt_ref[r, 2 * h * HEAD_DIM:(2 * h + 1) * HEAD_DIM, :] = (
                        qn[r * FOX_BLK:(r + 1) * FOX_BLK, :].T.astype(BF16))
            elif col < 2 * FOX_W:
                h = (col - FOX_W) // HEAD_DIM
                ka_ref[:, 2 * h * HEAD_DIM:(2 * h + 1) * HEAD_DIM] = normed(ph, GAIN_FK)
            elif col < 3 * FOX_W:
                h = (col - 2 * FOX_W) // HEAD_DIM
                for r in range(n_sub):
                    vt_ref[r, h * HEAD_DIM:(h + 1) * HEAD_DIM, :] = (
                        ph[r * FOX_BLK:(r + 1) * FOX_BLK, :].T.astype(BF16))
            else:
                wc = col - 3 * FOX_W
                if wc < SWA_Q_W:
                    val, rc = normed(ph, GAIN_SQ), REST_SQ + wc
                elif wc < SWA_Q_W + SWA_KV_W:
                    val, rc = normed(ph, GAIN_SK), REST_SK + wc - SWA_Q_W
                elif wc < SWA_Q_W + 2 * SWA_KV_W:
                    val, rc = ph.astype(BF16), REST_SV + wc - SWA_Q_W - SWA_KV_W
                else:
                    val, rc = normed(ph, GAIN_MQ), REST_MQ + wc - SWA_Q_W - 2 * SWA_KV_W
                rest_ref[:, rc:rc + HEAD_DIM] = val


def _in_proj(x, gain, w, wf, head_gain, fbias, *, seq, tm=512):
    t, d = x.shape
    assert seq % tm == 0 and tm % FOX_BLK == 0 and tm // FOX_BLK <= SUBLANES
    const = lambda shape: pl.BlockSpec(shape, lambda i: (0, 0), pipeline_mode=pl.Buffered(1))
    return pl.pallas_call(
        functools.partial(_in_proj_kernel, blocks_per_seq=seq // tm),
        out_shape=(
            jax.ShapeDtypeStruct((t // FOX_BLK, 2 * FOX_W, FOX_BLK), BF16),
            jax.ShapeDtypeStruct((t, 2 * FOX_W), BF16),
            jax.ShapeDtypeStruct((t // FOX_BLK, FOX_W, FOX_BLK), BF16),
            jax.ShapeDtypeStruct((t, REST_W), BF16),
            jax.ShapeDtypeStruct((t // tm, SUBLANES, LANES), F32),
        ),
        grid=(t // tm,),
        in_specs=[
            pl.BlockSpec((tm, d), lambda i: (i, 0)),
            const((1, d)),
            const((PROJ_W, d)),
            const((LANES, d)),
            const((SUBLANES, HEAD_DIM)),
            const((1, LANES)),
        ],
        out_specs=(
            pl.BlockSpec((tm // FOX_BLK, 2 * FOX_W, FOX_BLK), lambda i: (i, 0, 0)),
            pl.BlockSpec((tm, 2 * FOX_W), lambda i: (i, 0)),
            pl.BlockSpec((tm // FOX_BLK, FOX_W, FOX_BLK), lambda i: (i, 0, 0)),
            pl.BlockSpec((tm, REST_W), lambda i: (i, 0)),
            pl.BlockSpec((1, SUBLANES, LANES), lambda i: (i, 0, 0)),
        ),
        scratch_shapes=[pltpu.VMEM((SUBLANES, LANES), F32)],
        compiler_params=_cparams(("arbitrary",)),
        name="in_proj",
    )(x, gain, w, wf, head_gain, fbias)


def _fox_kernel(anc_ref, q_ref, k_ref, vt_ref, o_ref, m_ref, l_ref, acc_ref, st_ref, *, nblk):
    b = pl.program_id(0)
    i = pl.program_id(1)
    blk = q_ref.shape[2]
    aw = 2 * HEAD_DIM
    m_ref[...] = jnp.full(m_ref.shape, NEG_INF, F32)
    l_ref[...] = jnp.zeros(l_ref.shape, F32)
    acc_ref[...] = jnp.zeros(acc_ref.shape, F32)
    krow = lax.broadcasted_iota(jnp.int32, (blk, blk), 0)
    qcol = lax.broadcasted_iota(jnp.int32, (blk, blk), 1)

    def scores(j, slot):
        ks = pl.ds(pl.multiple_of(j * blk, blk), blk)
        for h in range(FOX_HEADS):
            st_ref[slot, h] = jnp.dot(k_ref[ks, h * aw:(h + 1) * aw], q_ref[0, h * aw:(h + 1) * aw, :],
                                      preferred_element_type=F32)

    def softmax_pv(j, slot, masked):
        for h in range(FOX_HEADS):
            st = st_ref[slot, h]
            d = anc_ref[(b * nblk + i) * SUBLANES + h] - anc_ref[(b * nblk + j) * SUBLANES + h]
            if masked:
                st = jnp.where(krow <= qcol, st, NEG_INF)
            m_prev = m_ref[h]
            m_new = jnp.maximum(m_prev, jnp.max(st, axis=0, keepdims=True) + d)
            alpha = jnp.exp2(m_prev - m_new)
            p = jnp.exp2(st - (m_new - d))
            l_ref[h] = alpha * l_ref[h] + jnp.sum(p, axis=0, keepdims=True)
            pv = jnp.dot(vt_ref[j, h * HEAD_DIM:(h + 1) * HEAD_DIM, :], p.astype(BF16),
                         preferred_element_type=F32)
            acc_ref[h] = alpha * acc_ref[h] + pv
            m_ref[h] = m_new

    def body(t, carry):
        j = 2 * t
        scores(j + 1, 1)
        softmax_pv(j, 0, False)
        scores(j + 2, 0)
        softmax_pv(j + 1, 1, False)
        return carry

    scores(0, 0)
    lax.fori_loop(0, i // 2, body, 0)

    @pl.when(i % 2 == 0)
    def _():
        softmax_pv(i, 0, True)

    @pl.when(i % 2 == 1)
    def _():
        scores(i, 1)
        softmax_pv(i - 1, 0, False)
        softmax_pv(i, 1, True)

    for h in range(FOX_HEADS):
        o_ref[:, h * HEAD_DIM:(h + 1) * HEAD_DIM] = (acc_ref[h] / l_ref[h]).T.astype(o_ref.dtype)


def _fox(anchors, qt, ka, vt, *, batch, seq):
    t = ka.shape[0]
    blk = FOX_BLK
    nblk = seq // blk
    return pl.pallas_call(
        functools.partial(_fox_kernel, nblk=nblk),
        out_shape=jax.ShapeDtypeStruct((t, FOX_W), BF16),
        grid=(batch, nblk),
        in_specs=[
            pl.BlockSpec(memory_space=pltpu.SMEM),
            pl.BlockSpec((1, 2 * FOX_W, blk), lambda b, i: (b * nblk + i, 0, 0)),
            pl.BlockSpec((seq, 2 * FOX_W), lambda b, i: (b, 0)),
            pl.BlockSpec((nblk, FOX_W, blk), lambda b, i: (b, 0, 0)),
        ],
        out_specs=pl.BlockSpec((blk, FOX_W), lambda b, i: (b * nblk + i, 0)),
        scratch_shapes=[
            pltpu.VMEM((FOX_HEADS, 1, blk), F32),
            pltpu.VMEM((FOX_HEADS, 1, blk), F32),
            pltpu.VMEM((FOX_HEADS, HEAD_DIM, blk), F32),
            pltpu.VMEM((2, FOX_HEADS, blk, blk), F32),
        ],
        compiler_params=_cparams(("parallel", "arbitrary")),
        name="fox",
    )(anchors, qt, ka, vt)


def _swa_kernel(q_ref, k_ref, v_ref, sink_ref, slope_ref, o_ref):
    qi = pl.program_id(1)
    tq = q_ref.shape[0]
    w = WINDOW
    row = lax.broadcasted_iota(jnp.int32, (w, 2 * w), 0)
    col = lax.broadcasted_iota(jnp.int32, (w, 2 * w), 1)

    for r in range(tq // w):
        n = qi * (tq // w) + r
        kstart = pl.multiple_of(jnp.maximum(n - 1, 0) * w, w)
        dist = (n * w - kstart) + row - col
        valid = (dist >= 0) & (dist < w)
        dist_f = dist.astype(F32)
        for h in range(SWA_HEADS):
            g = h // SWA_GROUP
            q = q_ref[r * w:(r + 1) * w, h * HEAD_DIM:(h + 1) * HEAD_DIM]
            k = k_ref[pl.ds(kstart, 2 * w), g * HEAD_DIM:(g + 1) * HEAD_DIM]
            v = v_ref[pl.ds(kstart, 2 * w), g * HEAD_DIM:(g + 1) * HEAD_DIM]
            s = lax.dot_general(q, k, (((1,), (1,)), ((), ())), preferred_element_type=F32)
            s = s - slope_ref[h] * dist_f
            s = jnp.where(valid, s, NEG_INF)
            sink = sink_ref[h]
            m = jnp.maximum(jnp.max(s, axis=-1, keepdims=True), sink)
            p = jnp.exp(s - m)
            denom = jnp.sum(p, axis=-1, keepdims=True) + jnp.exp(sink - m)
            o = jnp.dot(p.astype(BF16), v, preferred_element_type=F32)
            o_ref[r * w:(r + 1) * w, h * HEAD_DIM:(h + 1) * HEAD_DIM] = (o / denom).astype(o_ref.dtype)


def _swa(rest, sinks, slopes, *, batch, seq, tq=512):
    t = rest.shape[0]
    nq = seq // tq
    smem = pl.BlockSpec(memory_space=pltpu.SMEM)
    return pl.pallas_call(
        _swa_kernel,
        out_shape=jax.ShapeDtypeStruct((t, SWA_Q_W), BF16),
        grid=(batch, nq),
        in_specs=[
            pl.BlockSpec((tq, SWA_Q_W), lambda b, i: (b * nq + i, REST_SQ // SWA_Q_W)),
            pl.BlockSpec((seq, SWA_KV_W), lambda b, i: (b, REST_SK // SWA_KV_W)),
            pl.BlockSpec((seq, SWA_KV_W), lambda b, i: (b, REST_SV // SWA_KV_W)),
            smem, smem,
        ],
        out_specs=pl.BlockSpec((tq, SWA_Q_W), lambda b, i: (b * nq + i, 0)),
        compiler_params=_cparams(("parallel", "arbitrary")),
        name="swa",
    )(rest, rest, rest, sinks, slopes)


def _mem_kv_kernel(mem_ref, g_ref, wk_ref, wv_ref, kg_ref, mk_ref, mv_ref):
    mn = _rms(mem_ref[...], g_ref[...]).astype(BF16)
    k = jnp.dot(mn, wk_ref[...], preferred_element_type=F32)
    v = jnp.dot(mn, wv_ref[...], preferred_element_type=F32)
    for h in range(MEM_HEADS):
        hs = slice(h * HEAD_DIM, (h + 1) * HEAD_DIM)
        mk_ref[:, hs] = _rms(k[:, hs], kg_ref[...]).astype(BF16)
    mv_ref[...] = v.astype(BF16)


def _mem_kv(mem2d, gain, wk, wv, kgain, *, batch, mem_len):
    d = mem2d.shape[1]
    full = lambda shape: pl.BlockSpec(shape, lambda b: (0, 0))
    out = jax.ShapeDtypeStruct((batch * mem_len, MEM_W), BF16)
    return pl.pallas_call(
        _mem_kv_kernel,
        out_shape=(out, out),
        grid=(batch,),
        in_specs=[
            pl.BlockSpec((mem_len, d), lambda b: (b, 0)),
            full((1, d)), full((d, MEM_W)), full((d, MEM_W)), full((1, HEAD_DIM)),
        ],
        out_specs=(pl.BlockSpec((mem_len, MEM_W), lambda b: (b, 0)),
                   pl.BlockSpec((mem_len, MEM_W), lambda b: (b, 0))),
        compiler_params=_cparams(("parallel",)),
        name="mem_kv",
    )(mem2d, gain, wk, wv, kgain)


def _mem_attn_kernel(q_ref, mk_ref, mv_ref, o_ref):
    for h in range(MEM_HEADS):
        hs = slice(h * HEAD_DIM, (h + 1) * HEAD_DIM)
        s = lax.dot_general(q_ref[:, hs], mk_ref[:, hs], (((1,), (1,)), ((), ())),
                            preferred_element_type=F32)
        m = jnp.max(s, axis=-1, keepdims=True)
        p = jnp.exp(s - m)
        denom = jnp.sum(p, axis=-1, keepdims=True)
        o = jnp.dot(p.astype(BF16), mv_ref[:, hs], preferred_element_type=F32)
        o_ref[:, hs] = (o / denom).astype(o_ref.dtype)


def _mem_attn(rest, mk, mv, *, batch, seq, mem_len, tq=512):
    t = rest.shape[0]
    nq = seq // tq
    return pl.pallas_call(
        _mem_attn_kernel,
        out_shape=jax.ShapeDtypeStruct((t, MEM_W), BF16),
        grid=(batch, nq),
        in_specs=[
            pl.BlockSpec((tq, MEM_W), lambda b, i: (b * nq + i, REST_MQ // MEM_W)),
            pl.BlockSpec((mem_len, MEM_W), lambda b, i: (b, 0)),
            pl.BlockSpec((mem_len, MEM_W), lambda b, i: (b, 0)),
        ],
        out_specs=pl.BlockSpec((tq, MEM_W), lambda b, i: (b * nq + i, 0)),
        compiler_params=_cparams(("parallel", "arbitrary")),
        name="mem_attn",
    )(rest, mk, mv)


def _out_proj_kernel(x_ref, a_ref, b_ref, c_ref, w_ref, o_ref):
    acc = jnp.dot(a_ref[...], w_ref[0:FOX_W, :], preferred_element_type=F32)
    acc += jnp.dot(b_ref[...], w_ref[FOX_W:FOX_W + SWA_Q_W, :], preferred_element_type=F32)
    acc += jnp.dot(c_ref[...], w_ref[FOX_W + SWA_Q_W:, :], preferred_element_type=F32)
    o_ref[...] = x_ref[...] + acc


def _out_proj(x, a, b, c, w, *, tm=512):
    t, d = x.shape
    return pl.pallas_call(
        _out_proj_kernel,
        out_shape=jax.ShapeDtypeStruct((t, d), F32),
        grid=(t // tm,),
        in_specs=[
            pl.BlockSpec((tm, d), lambda i: (i, 0)),
            pl.BlockSpec((tm, FOX_W), lambda i: (i, 0)),
            pl.BlockSpec((tm, SWA_Q_W), lambda i: (i, 0)),
            pl.BlockSpec((tm, MEM_W), lambda i: (i, 0)),
            pl.BlockSpec(w.shape, lambda i: (0, 0)),
        ],
        out_specs=pl.BlockSpec((tm, d), lambda i: (i, 0)),
        compiler_params=_cparams(("parallel",)),
        name="out_proj",
    )(x, a, b, c, w)


def _alibi_slopes(n):
    return jnp.asarray([2.0 ** (-8.0 * i / n) for i in range(1, n + 1)], dtype=F32)


def _pack_w_in_kernel(w_ref, main_ref, f_ref, *, n_main):
    i = pl.program_id(0)
    n_a = w_ref.shape[2]

    @pl.when(i < n_main)
    def _():
        for a in range(n_a):
            main_ref[:, a * LANES:(a + 1) * LANES] = w_ref[0, :, a, :].astype(BF16)

    @pl.when(i == n_main)
    def _():
        row = lax.broadcasted_iota(jnp.int32, (f_ref.shape[0], LANES), 0)
        for a in range(n_a):
            f_ref[:, a * LANES:(a + 1) * LANES] = jnp.where(
                row < FOX_HEADS, w_ref[0, 0:f_ref.shape[0], a, :], 0.0).astype(BF16)


def _pack_w_in(w_in, l, *, rows=256):
    d, n = w_in.shape[1:]
    fl0 = 3 * FOX_W
    assert n == PROJ_W + FOX_HEADS and fl0 % rows == 0 and PROJ_W % rows == 0 and d % LANES == 0
    wt = jnp.swapaxes(w_in, 1, 2).reshape(w_in.shape[0], n, d // LANES, LANES)
    n_main = PROJ_W // rows

    def start(i):
        main = jnp.where(i * rows < fl0, i * rows, i * rows + FOX_HEADS)
        return jnp.where(i < n_main, main, fl0)

    return pl.pallas_call(
        functools.partial(_pack_w_in_kernel, n_main=n_main),
        out_shape=(jax.ShapeDtypeStruct((PROJ_W, d), BF16), jax.ShapeDtypeStruct((LANES, d), BF16)),
        grid=(n_main + 1,),
        in_specs=[pl.BlockSpec((pl.Element(1), pl.Element(rows), pl.Element(d // LANES), pl.Element(LANES)),
                               lambda i: (l, start(i), 0, 0))],
        out_specs=(pl.BlockSpec((rows, d), lambda i: (jnp.minimum(i, n_main - 1), 0)),
                   pl.BlockSpec((LANES, d), lambda i: (0, 0))),
        compiler_params=_cparams(("arbitrary",)),
        name="pack_w_in",
    )(wt)


def kernel(x, mem, ffn1_norm, ffn1_gate, ffn1_up, ffn1_down, mix_norm, mem_norm, w_in, forget_bias, w_mem_k, w_mem_v, fox_q_gain, fox_k_gain, swa_q_gain, swa_k_gain, swa_sinks, mem_q_gain, mem_k_gain, w_out, ffn2_norm, ffn2_gate, ffn2_up, ffn2_down):
    batch, seq, d = x.shape
    mem_len = mem.shape[1]
    depth = w_in.shape[0]
    scale = HEAD_DIM ** -0.5
    slopes = _alibi_slopes(SWA_HEADS)
    x2 = x.reshape(batch * seq, d)
    mem2 = mem.reshape(batch * mem_len, d)
    zeros = jnp.zeros((HEAD_DIM,), F32)

    for l in range(depth):
        x2, ffn2_w = _ffn(x2, ffn1_norm[l][None], ffn1_gate[l].astype(BF16), ffn1_up[l].astype(BF16),
                          ffn1_down[l].astype(BF16),
                          next_weights=(ffn2_gate[l], ffn2_up[l], ffn2_down[l]))

        w_main, w_f = _pack_w_in(w_in, l)
        head_gain = jnp.stack([fox_q_gain[l] * (scale * LOG2E), fox_k_gain[l], swa_q_gain[l] * scale,
                               swa_k_gain[l], mem_q_gain[l] * scale, zeros, zeros, zeros])
        fbias = jnp.pad(forget_bias[l], (0, LANES - FOX_HEADS))[None]
        qt, ka, vt, rest, anc = _in_proj(x2, mix_norm[l][None], w_main, w_f, head_gain, fbias, seq=seq)

        n_sub = (x2.shape[0] // FOX_BLK) // anc.shape[0]
        anchors = anc[:, :n_sub, :SUBLANES].reshape(-1)
        out_a = _fox(anchors, qt, ka, vt, batch=batch, seq=seq)
        out_b = _swa(rest, swa_sinks[l], slopes, batch=batch, seq=seq)
        mk, mv = _mem_kv(mem2, mem_norm[l][None], w_mem_k[l].astype(BF16), w_mem_v[l].astype(BF16),
                         mem_k_gain[l][None], batch=batch, mem_len=mem_len)
        out_c = _mem_attn(rest, mk, mv, batch=batch, seq=seq, mem_len=mem_len)

        x2 = _out_proj(x2, out_a, out_b, out_c, w_out[l].astype(BF16))
        x2, _ = _ffn(x2, ffn2_norm[l][None], *ffn2_w)
    return x2.reshape(batch, seq, d)
```

```python
import functools

import jax
import jax.numpy as jnp
from jax import lax
from jax.experimental import pallas as pl
from jax.experimental.pallas import tpu as pltpu

F32 = jnp.float32
BF16 = jnp.bfloat16

HEAD_DIM = 128
FOX_HEADS = 6
SWA_HEADS = 6
SWA_KV_HEADS = 2
SWA_GROUP = SWA_HEADS // SWA_KV_HEADS
MEM_HEADS = 4
WINDOW = 128
EPS = 1e-6
NEG_INF = -1e30
LOG2E = 1.4426950408889634
FOX_BLK = 256

FOX_W = FOX_HEADS * HEAD_DIM
SWA_Q_W = SWA_HEADS * HEAD_DIM
SWA_KV_W = SWA_KV_HEADS * HEAD_DIM
MEM_W = MEM_HEADS * HEAD_DIM

PROJ_W = 3 * FOX_W + SWA_Q_W + 2 * SWA_KV_W + MEM_W
REST_W = SWA_Q_W + 2 * SWA_KV_W + MEM_W
REST_SQ, REST_SK, REST_MQ, REST_SV = 0, SWA_Q_W, SWA_Q_W + SWA_KV_W, SWA_Q_W + SWA_KV_W + MEM_W
GAIN_FQ, GAIN_FK, GAIN_SQ, GAIN_SK, GAIN_MQ = range(5)

LANES = 128
SUBLANES = 8
MXU_N = 256

VMEM_LIMIT = 58 * 1024 * 1024


def _cparams(sem):
    return pltpu.CompilerParams(dimension_semantics=sem, vmem_limit_bytes=VMEM_LIMIT)


def _rms(x, gain):
    ms = jnp.mean(x * x, axis=-1, keepdims=True)
    return x * lax.rsqrt(ms + EPS) * gain


def _ffn_kernel(x_ref, g_ref, wg_ref, wu_ref, wd_ref, *rest, n_chunk, n_cast):
    cast_in, o_ref, cast_out, xn_ref = rest[:n_cast], rest[n_cast], rest[n_cast + 1:-1], rest[-1]
    j = pl.program_id(1)

    for src, dst in zip(cast_in, cast_out):
        dst[...] = src[...].astype(BF16)

    @pl.when(j == 0)
    def _():
        x = x_ref[...]
        xn_ref[...] = _rms(x, g_ref[...]).astype(BF16)
        o_ref[...] = x

    xn = xn_ref[...]
    gate = jnp.dot(xn, wg_ref[...].astype(BF16), preferred_element_type=F32)
    up = jnp.dot(xn, wu_ref[...].astype(BF16), preferred_element_type=F32)
    h = (gate * jax.nn.sigmoid(gate) * (0.5 * up)).astype(BF16)
    d_model = o_ref.shape[1]
    cw = d_model // n_chunk
    for c in range(n_chunk):
        cs = slice(c * cw, (c + 1) * cw)
        o_ref[:, cs] += jnp.dot(h, wd_ref[:, cs].astype(BF16), preferred_element_type=F32)


def _ffn(x, gain, wg, wu, wd, *, next_weights=(), tm=1024, tf=512):
    t, d = x.shape
    dff = wg.shape[1]
    gi, gj = t // tm, dff // tf
    cast_specs, cast_shapes = [], []
    for w in next_weights:
        ri, cj = (gi, gj) if w.shape[1] == dff else (gj, gi)
        blk = (w.shape[0] // ri, w.shape[1] // cj)
        assert blk[0] % SUBLANES == 0 and blk[1] % LANES == 0
        index = (lambda i, j: (i, j)) if w.shape[1] == dff else (lambda i, j: (j, i))
        cast_specs.append(pl.BlockSpec(blk, index))
        cast_shapes.append(jax.ShapeDtypeStruct(w.shape, BF16))
    out = pl.pallas_call(
        functools.partial(_ffn_kernel, n_chunk=4, n_cast=len(next_weights)),
        out_shape=(jax.ShapeDtypeStruct((t, d), F32), *cast_shapes),
        grid=(gi, gj),
        in_specs=[
            pl.BlockSpec((tm, d), lambda i, j: (i, 0)),
            pl.BlockSpec((1, d), lambda i, j: (0, 0)),
            pl.BlockSpec((d, tf), lambda i, j: (0, j)),
            pl.BlockSpec((d, tf), lambda i, j: (0, j)),
            pl.BlockSpec((tf, d), lambda i, j: (j, 0)),
            *cast_specs,
        ],
        out_specs=(pl.BlockSpec((tm, d), lambda i, j: (i, 0)), *cast_specs),
        scratch_shapes=[pltpu.VMEM((tm, d), BF16)],
        compiler_params=_cparams(("parallel", "arbitrary")),
        name="ffn",
    )(x, gain, wg, wu, wd, *next_weights)
    return out[0], out[1:]


def _in_proj_kernel(x_ref, g_ref, w_ref, wf_ref, hg_ref, fb_ref, wo_ref,
                    qt_ref, ka_ref, vt_ref, rest_ref, anc_ref, wo_bf_ref, carry_ref, *, blocks_per_seq):
    i = pl.program_id(0)
    tm = x_ref.shape[0]
    wo_bf_ref[...] = wo_ref[...].astype(BF16)
    hn = _rms(x_ref[...], g_ref[...]).astype(BF16)

    nt = (((1,), (1,)), ((), ()))
    logit = lax.dot_general(hn, wf_ref[...], nt, preferred_element_type=F32)
    c = jax.nn.log_sigmoid(logit + fb_ref[...]) * LOG2E
    row = lax.broadcasted_iota(jnp.int32, c.shape, 0) % FOX_BLK
    sh = 1
    while sh < FOX_BLK:
        c = c + jnp.where(row >= sh, pltpu.roll(c, sh, 0), 0.0)
        sh *= 2

    @pl.when(i % blocks_per_seq == 0)
    def _():
        carry_ref[...] = jnp.zeros_like(carry_ref)

    anchor = carry_ref[0:1, :]
    anc_ref[...] = jnp.zeros_like(anc_ref)
    for r in range(tm // FOX_BLK):
        anc_ref[0, r:r + 1, :] = anchor
        anchor = anchor + c[(r + 1) * FOX_BLK - 1:(r + 1) * FOX_BLK, :]
    carry_ref[0:1, :] = anchor

    n_sub = tm // FOX_BLK
    lane = lax.broadcasted_iota(jnp.int32, c.shape, 1)
    sub = lax.broadcasted_iota(jnp.int32, (HEAD_DIM, FOX_BLK), 0)
    ct = c.T
    for h in range(FOX_HEADS):
        ch = c[:, h:h + 1]
        hi = ch.astype(BF16).astype(F32)
        lo = ch - hi
        ak = jnp.where(lane < 2, 1.0, jnp.where(lane == 2, -hi, jnp.where(lane == 3, -lo, 0.0)))
        ka_ref[:, (2 * h + 1) * HEAD_DIM:(2 * h + 2) * HEAD_DIM] = ak.astype(BF16)
        for r in range(n_sub):
            cr = ct[h:h + 1, r * FOX_BLK:(r + 1) * FOX_BLK]
            hi_r = cr.astype(BF16).astype(F32)
            lo_r = cr - hi_r
            aq = jnp.where(sub == 0, hi_r, jnp.where(sub == 1, lo_r, jnp.where(sub < 4, 1.0, 0.0)))
            qt_ref[r, (2 * h + 1) * HEAD_DIM:(2 * h + 2) * HEAD_DIM, :] = aq.astype(BF16)

    def normed(ph, row_id):
        return _rms(ph, hg_ref[row_id:row_id + 1, :]).astype(BF16)

    heads_per_chunk = MXU_N // HEAD_DIM
    for ck in range(PROJ_W // MXU_N):
        p = lax.dot_general(hn, w_ref[ck * MXU_N:(ck + 1) * MXU_N, :], nt, preferred_element_type=F32)
        for e in range(heads_per_chunk):
            col = ck * MXU_N + e * HEAD_DIM
            ph = p[:, e * HEAD_DIM:(e + 1) * HEAD_DIM]
            if col < FOX_W:
                h = col // HEAD_DIM
                qn = _rms(ph, hg_ref[GAIN_FQ:GAIN_FQ + 1, :])
                for r in range(n_sub):
                    qt_ref[r, 2 * h * HEAD_DIM:(2 * h + 1) * HEAD_DIM, :] = (
                        qn[r * FOX_BLK:(r + 1) * FOX_BLK, :].T.astype(BF16))
            elif col < 2 * FOX_W:
                h = (col - FOX_W) // HEAD_DIM
                ka_ref[:, 2 * h * HEAD_DIM:(2 * h + 1) * HEAD_DIM] = normed(ph, GAIN_FK)
            elif col < 3 * FOX_W:
                h = (col - 2 * FOX_W) // HEAD_DIM
                for r in range(n_sub):
                    vt_ref[r, h * HEAD_DIM:(h + 1) * HEAD_DIM, :] = (
                        ph[r * FOX_BLK:(r + 1) * FOX_BLK, :].T.astype(BF16))
            else:
                wc = col - 3 * FOX_W
                if wc < SWA_Q_W:
                    val, rc = normed(ph, GAIN_SQ), REST_SQ + wc
                elif wc < SWA_Q_W + SWA_KV_W:
                    val, rc = normed(ph, GAIN_SK), REST_SK + wc - SWA_Q_W
                elif wc < SWA_Q_W + 2 * SWA_KV_W:
                    val, rc = ph.astype(BF16), REST_SV + wc - SWA_Q_W - SWA_KV_W
                else:
                    val, rc = normed(ph, GAIN_MQ), REST_MQ + wc - SWA_Q_W - 2 * SWA_KV_W
                rest_ref[:, rc:rc + HEAD_DIM] = val


def _in_proj(x, gain, w, wf, head_gain, fbias, w_out, *, seq, tm=512):
    t, d = x.shape
    wo_rows = w_out.shape[0] // (t // tm)
    assert wo_rows % (2 * SUBLANES) == 0
    assert seq % tm == 0 and tm % FOX_BLK == 0 and tm // FOX_BLK <= SUBLANES
    const = lambda shape: pl.BlockSpec(shape, lambda i: (0, 0), pipeline_mode=pl.Buffered(1))
    return pl.pallas_call(
        functools.partial(_in_proj_kernel, blocks_per_seq=seq // tm),
        out_shape=(
            jax.ShapeDtypeStruct((t // FOX_BLK, 2 * FOX_W, FOX_BLK), BF16),
            jax.ShapeDtypeStruct((t, 2 * FOX_W), BF16),
            jax.ShapeDtypeStruct((t // FOX_BLK, FOX_W, FOX_BLK), BF16),
            jax.ShapeDtypeStruct((t, REST_W), BF16),
            jax.ShapeDtypeStruct((t // tm, SUBLANES, LANES), F32),
            jax.ShapeDtypeStruct(w_out.shape, BF16),
        ),
        grid=(t // tm,),
        in_specs=[
            pl.BlockSpec((tm, d), lambda i: (i, 0)),
            const((1, d)),
            const((PROJ_W, d)),
            const((LANES, d)),
            const((SUBLANES, HEAD_DIM)),
            const((1, LANES)),
            pl.BlockSpec((wo_rows, w_out.shape[1]), lambda i: (i, 0)),
        ],
        out_specs=(
            pl.BlockSpec((tm // FOX_BLK, 2 * FOX_W, FOX_BLK), lambda i: (i, 0, 0)),
            pl.BlockSpec((tm, 2 * FOX_W), lambda i: (i, 0)),
            pl.BlockSpec((tm // FOX_BLK, FOX_W, FOX_BLK), lambda i: (i, 0, 0)),
            pl.BlockSpec((tm, REST_W), lambda i: (i, 0)),
            pl.BlockSpec((1, SUBLANES, LANES), lambda i: (i, 0, 0)),
            pl.BlockSpec((wo_rows, w_out.shape[1]), lambda i: (i, 0)),
        ),
        scratch_shapes=[pltpu.VMEM((SUBLANES, LANES), F32)],
        compiler_params=_cparams(("arbitrary",)),
        name="in_proj",
    )(x, gain, w, wf, head_gain, fbias, w_out)


def _fox_kernel(anc_ref, q_ref, k_ref, vt_ref, o_ref, m_ref, l_ref, acc_ref, st_ref, *, nblk):
    b = pl.program_id(0)
    i = pl.program_id(1)
    blk = q_ref.shape[2]
    aw = 2 * HEAD_DIM
    m_ref[...] = jnp.full(m_ref.shape, NEG_INF, F32)
    l_ref[...] = jnp.zeros(l_ref.shape, F32)
    acc_ref[...] = jnp.zeros(acc_ref.shape, F32)
    krow = lax.broadcasted_iota(jnp.int32, (blk, blk), 0)
    qcol = lax.broadcasted_iota(jnp.int32, (blk, blk), 1)

    def scores(j, slot):
        ks = pl.ds(pl.multiple_of(j * blk, blk), blk)
        for h in range(FOX_HEADS):
            st_ref[slot, h] = jnp.dot(k_ref[ks, h * aw:(h + 1) * aw], q_ref[0, h * aw:(h + 1) * aw, :],
                                      preferred_element_type=F32)

    def softmax_pv(j, slot, masked):
        for h in range(FOX_HEADS):
            st = st_ref[slot, h]
            d = anc_ref[(b * nblk + i) * SUBLANES + h] - anc_ref[(b * nblk + j) * SUBLANES + h]
            if masked:
                st = jnp.where(krow <= qcol, st, NEG_INF)
            m_prev = m_ref[h]
            m_new = jnp.maximum(m_prev, jnp.max(st, axis=0, keepdims=True) + d)
            alpha = jnp.exp2(m_prev - m_new)
            p = jnp.exp2(st - (m_new - d))
            l_ref[h] = alpha * l_ref[h] + jnp.sum(p, axis=0, keepdims=True)
            pv = jnp.dot(vt_ref[j, h * HEAD_DIM:(h + 1) * HEAD_DIM, :], p.astype(BF16),
                         preferred_element_type=F32)
            acc_ref[h] = alpha * acc_ref[h] + pv
            m_ref[h] = m_new

    def body(t, carry):
        j = 2 * t
        scores(j + 1, 1)
        softmax_pv(j, 0, False)
        scores(j + 2, 0)
        softmax_pv(j + 1, 1, False)
        return carry

    scores(0, 0)
    lax.fori_loop(0, i // 2, body, 0)

    @pl.when(i % 2 == 0)
    def _():
        softmax_pv(i, 0, True)

    @pl.when(i % 2 == 1)
    def _():
        scores(i, 1)
        softmax_pv(i - 1, 0, False)
        softmax_pv(i, 1, True)

    for h in range(FOX_HEADS):
        o_ref[:, h * HEAD_DIM:(h + 1) * HEAD_DIM] = (acc_ref[h] / l_ref[h]).T.astype(o_ref.dtype)


def _fox(anchors, qt, ka, vt, *, batch, seq):
    t = ka.shape[0]
    blk = FOX_BLK
    nblk = seq // blk
    return pl.pallas_call(
        functools.partial(_fox_kernel, nblk=nblk),
        out_shape=jax.ShapeDtypeStruct((t, FOX_W), BF16),
        grid=(batch, nblk),
        in_specs=[
            pl.BlockSpec(memory_space=pltpu.SMEM),
            pl.BlockSpec((1, 2 * FOX_W, blk), lambda b, i: (b * nblk + i, 0, 0)),
            pl.BlockSpec((seq, 2 * FOX_W), lambda b, i: (b, 0)),
            pl.BlockSpec((nblk, FOX_W, blk), lambda b, i: (b, 0, 0)),
        ],
        out_specs=pl.BlockSpec((blk, FOX_W), lambda b, i: (b * nblk + i, 0)),
        scratch_shapes=[
            pltpu.VMEM((FOX_HEADS, 1, blk), F32),
            pltpu.VMEM((FOX_HEADS, 1, blk), F32),
            pltpu.VMEM((FOX_HEADS, HEAD_DIM, blk), F32),
            pltpu.VMEM((2, FOX_HEADS, blk, blk), F32),
        ],
        compiler_params=_cparams(("parallel", "arbitrary")),
        name="fox",
    )(anchors, qt, ka, vt)


def _swa_kernel(q_ref, k_ref, v_ref, sink_ref, slope_ref, o_ref):
    qi = pl.program_id(1)
    tq = q_ref.shape[0]
    w = WINDOW
    row = lax.broadcasted_iota(jnp.int32, (w, 2 * w), 0)
    col = lax.broadcasted_iota(jnp.int32, (w, 2 * w), 1)

    for r in range(tq // w):
        n = qi * (tq // w) + r
        kstart = pl.multiple_of(jnp.maximum(n - 1, 0) * w, w)
        dist = (n * w - kstart) + row - col
        valid = (dist >= 0) & (dist < w)
        dist_f = dist.astype(F32)
        for h in range(SWA_HEADS):
            g = h // SWA_GROUP
            q = q_ref[r * w:(r + 1) * w, h * HEAD_DIM:(h + 1) * HEAD_DIM]
            k = k_ref[pl.ds(kstart, 2 * w), g * HEAD_DIM:(g + 1) * HEAD_DIM]
            v = v_ref[pl.ds(kstart, 2 * w), g * HEAD_DIM:(g + 1) * HEAD_DIM]
            s = lax.dot_general(q, k, (((1,), (1,)), ((), ())), preferred_element_type=F32)
            s = s - slope_ref[h] * dist_f
            s = jnp.where(valid, s, NEG_INF)
            sink = sink_ref[h]
            m = jnp.maximum(jnp.max(s, axis=-1, keepdims=True), sink)
            p = jnp.exp(s - m)
            denom = jnp.sum(p, axis=-1, keepdims=True) + jnp.exp(sink - m)
            o = jnp.dot(p.astype(BF16), v, preferred_element_type=F32)
            o_ref[r * w:(r + 1) * w, h * HEAD_DIM:(h + 1) * HEAD_DIM] = (o / denom).astype(o_ref.dtype)


def _swa(rest, sinks, slopes, *, batch, seq, tq=512):
    t = rest.shape[0]
    nq = seq // tq
    smem = pl.BlockSpec(memory_space=pltpu.SMEM)
    return pl.pallas_call(
        _swa_kernel,
        out_shape=jax.ShapeDtypeStruct((t, SWA_Q_W), BF16),
        grid=(batch, nq),
        in_specs=[
            pl.BlockSpec((tq, SWA_Q_W), lambda b, i: (b * nq + i, REST_SQ // SWA_Q_W)),
            pl.BlockSpec((seq, SWA_KV_W), lambda b, i: (b, REST_SK // SWA_KV_W)),
            pl.BlockSpec((seq, SWA_KV_W), lambda b, i: (b, REST_SV // SWA_KV_W)),
            smem, smem,
        ],
        out_specs=pl.BlockSpec((tq, SWA_Q_W), lambda b, i: (b * nq + i, 0)),
        compiler_params=_cparams(("parallel", "arbitrary")),
        name="swa",
    )(rest, rest, rest, sinks, slopes)


def _mem_kv_kernel(mem_ref, g_ref, wk_ref, wv_ref, kg_ref, mk_ref, mv_ref):
    mn = _rms(mem_ref[...], g_ref[...]).astype(BF16)
    k = jnp.dot(mn, wk_ref[...].astype(BF16), preferred_element_type=F32)
    v = jnp.dot(mn, wv_ref[...].astype(BF16), preferred_element_type=F32)
    for h in range(MEM_HEADS):
        hs = slice(h * HEAD_DIM, (h + 1) * HEAD_DIM)
        mk_ref[:, hs] = _rms(k[:, hs], kg_ref[...]).astype(BF16)
    mv_ref[...] = v.astype(BF16)


def _mem_kv(mem2d, gain, wk, wv, kgain, *, batch, mem_len):
    d = mem2d.shape[1]
    full = lambda shape: pl.BlockSpec(shape, lambda b: (0, 0))
    out = jax.ShapeDtypeStruct((batch * mem_len, MEM_W), BF16)
    return pl.pallas_call(
        _mem_kv_kernel,
        out_shape=(out, out),
        grid=(batch,),
        in_specs=[
            pl.BlockSpec((mem_len, d), lambda b: (b, 0)),
            full((1, d)), full((d, MEM_W)), full((d, MEM_W)), full((1, HEAD_DIM)),
        ],
        out_specs=(pl.BlockSpec((mem_len, MEM_W), lambda b: (b, 0)),
                   pl.BlockSpec((mem_len, MEM_W), lambda b: (b, 0))),
        compiler_params=_cparams(("parallel",)),
        name="mem_kv",
    )(mem2d, gain, wk, wv, kgain)


def _mem_attn_kernel(q_ref, mk_ref, mv_ref, o_ref):
    for h in range(MEM_HEADS):
        hs = slice(h * HEAD_DIM, (h + 1) * HEAD_DIM)
        s = lax.dot_general(q_ref[:, hs], mk_ref[:, hs], (((1,), (1,)), ((), ())),
                            preferred_element_type=F32)
        m = jnp.max(s, axis=-1, keepdims=True)
        p = jnp.exp(s - m)
        denom = jnp.sum(p, axis=-1, keepdims=True)
        o = jnp.dot(p.astype(BF16), mv_ref[:, hs], preferred_element_type=F32)
        o_ref[:, hs] = (o / denom).astype(o_ref.dtype)


def _mem_attn(rest, mk, mv, *, batch, seq, mem_len, tq=512):
    t = rest.shape[0]
    nq = seq // tq
    return pl.pallas_call(
        _mem_attn_kernel,
        out_shape=jax.ShapeDtypeStruct((t, MEM_W), BF16),
        grid=(batch, nq),
        in_specs=[
            pl.BlockSpec((tq, MEM_W), lambda b, i: (b * nq + i, REST_MQ // MEM_W)),
            pl.BlockSpec((mem_len, MEM_W), lambda b, i: (b, 0)),
            pl.BlockSpec((mem_len, MEM_W), lambda b, i: (b, 0)),
        ],
        out_specs=pl.BlockSpec((tq, MEM_W), lambda b, i: (b * nq + i, 0)),
        compiler_params=_cparams(("parallel", "arbitrary")),
        name="mem_attn",
    )(rest, mk, mv)


def _out_proj_kernel(x_ref, a_ref, b_ref, c_ref, w_ref, o_ref):
    acc = jnp.dot(a_ref[...], w_ref[0:FOX_W, :], preferred_element_type=F32)
    acc += jnp.dot(b_ref[...], w_ref[FOX_W:FOX_W + SWA_Q_W, :], preferred_element_type=F32)
    acc += jnp.dot(c_ref[...], w_ref[FOX_W + SWA_Q_W:, :], preferred_element_type=F32)
    o_ref[...] = x_ref[...] + acc


def _out_proj(x, a, b, c, w, *, tm=512):
    t, d = x.shape
    return pl.pallas_call(
        _out_proj_kernel,
        out_shape=jax.ShapeDtypeStruct((t, d), F32),
        grid=(t // tm,),
        in_specs=[
            pl.BlockSpec((tm, d), lambda i: (i, 0)),
            pl.BlockSpec((tm, FOX_W), lambda i: (i, 0)),
            pl.BlockSpec((tm, SWA_Q_W), lambda i: (i, 0)),
            pl.BlockSpec((tm, MEM_W), lambda i: (i, 0)),
            pl.BlockSpec(w.shape, lambda i: (0, 0)),
        ],
        out_specs=pl.BlockSpec((tm, d), lambda i: (i, 0)),
        compiler_params=_cparams(("parallel",)),
        name="out_proj",
    )(x, a, b, c, w)


def _alibi_slopes(n):
    return jnp.asarray([2.0 ** (-8.0 * i / n) for i in range(1, n + 1)], dtype=F32)


def _pack_w_in_kernel(w_ref, main_ref, f_ref, *, n_main):
    i = pl.program_id(0)
    n_a = w_ref.shape[2]

    @pl.when(i < n_main)
    def _():
        for a in range(n_a):
            main_ref[:, a * LANES:(a + 1) * LANES] = w_ref[0, :, a, :].astype(BF16)

    @pl.when(i == n_main)
    def _():
        row = lax.broadcasted_iota(jnp.int32, (f_ref.shape[0], LANES), 0)
        for a in range(n_a):
            f_ref[:, a * LANES:(a + 1) * LANES] = jnp.where(
                row < FOX_HEADS, w_ref[0, 0:f_ref.shape[0], a, :], 0.0).astype(BF16)


def _pack_w_in(w_in, l, *, rows=256):
    d, n = w_in.shape[1:]
    fl0 = 3 * FOX_W
    assert n == PROJ_W + FOX_HEADS and fl0 % rows == 0 and PROJ_W % rows == 0 and d % LANES == 0
    wt = jnp.swapaxes(w_in, 1, 2).reshape(w_in.shape[0], n, d // LANES, LANES)
    n_main = PROJ_W // rows

    def start(i):
        main = jnp.where(i * rows < fl0, i * rows, i * rows + FOX_HEADS)
        return jnp.where(i < n_main, main, fl0)

    return pl.pallas_call(
        functools.partial(_pack_w_in_kernel, n_main=n_main),
        out_shape=(jax.ShapeDtypeStruct((PROJ_W, d), BF16), jax.ShapeDtypeStruct((LANES, d), BF16)),
        grid=(n_main + 1,),
        in_specs=[pl.BlockSpec((pl.Element(1), pl.Element(rows), pl.Element(d // LANES), pl.Element(LANES)),
                               lambda i: (l, start(i), 0, 0))],
        out_specs=(pl.BlockSpec((rows, d), lambda i: (jnp.minimum(i, n_main - 1), 0)),
                   pl.BlockSpec((LANES, d), lambda i: (0, 0))),
        compiler_params=_cparams(("arbitrary",)),
        name="pack_w_in",
    )(wt)


def kernel(x, mem, ffn1_norm, ffn1_gate, ffn1_up, ffn1_down, mix_norm, mem_norm, w_in, forget_bias, w_mem_k, w_mem_v, fox_q_gain, fox_k_gain, swa_q_gain, swa_k_gain, swa_sinks, mem_q_gain, mem_k_gain, w_out, ffn2_norm, ffn2_gate, ffn2_up, ffn2_down):
    batch, seq, d = x.shape
    mem_len = mem.shape[1]
    depth = w_in.shape[0]
    scale = HEAD_DIM ** -0.5
    slopes = _alibi_slopes(SWA_HEADS)
    x2 = x.reshape(batch * seq, d)
    mem2 = mem.reshape(batch * mem_len, d)
    zeros = jnp.zeros((HEAD_DIM,), F32)

    for l in range(depth):
        x2, ffn2_w = _ffn(x2, ffn1_norm[l][None], ffn1_gate[l], ffn1_up[l], ffn1_down[l],
                          next_weights=(ffn2_gate[l], ffn2_up[l], ffn2_down[l]), tf=256)

        w_main, w_f = _pack_w_in(w_in, l)
        head_gain = jnp.stack([fox_q_gain[l] * (scale * LOG2E), fox_k_gain[l], swa_q_gain[l] * scale,
                               swa_k_gain[l], mem_q_gain[l] * scale, zeros, zeros, zeros])
        fbias = jnp.pad(forget_bias[l], (0, LANES - FOX_HEADS))[None]
        qt, ka, vt, rest, anc, w_out_bf = _in_proj(x2, mix_norm[l][None], w_main, w_f, head_gain, fbias,
                                                   w_out[l], seq=seq)

        n_sub = (x2.shape[0] // FOX_BLK) // anc.shape[0]
        anchors = anc[:, :n_sub, :SUBLANES].reshape(-1)
        out_a = _fox(anchors, qt, ka, vt, batch=batch, seq=seq)
        out_b = _swa(rest, swa_sinks[l], slopes, batch=batch, seq=seq)
        mk, mv = _mem_kv(mem2, mem_norm[l][None], w_mem_k[l], w_mem_v[l],
                         mem_k_gain[l][None], batch=batch, mem_len=mem_len)
        out_c = _mem_attn(rest, mk, mv, batch=batch, seq=seq, mem_len=mem_len)

        x2 = _out_proj(x2, out_a, out_b, out_c, w_out_bf)
        x2, _ = _ffn(x2, ffn2_norm[l][None], *ffn2_w)
    return x2.reshape(batch, seq, d)
```

```python
import functools

import jax
import jax.numpy as jnp
from jax import lax
from jax.experimental import pallas as pl
from jax.experimental.pallas import tpu as pltpu

F32 = jnp.float32
BF16 = jnp.bfloat16

HEAD_DIM = 128
FOX_HEADS = 6
SWA_HEADS = 6
SWA_KV_HEADS = 2
SWA_GROUP = SWA_HEADS // SWA_KV_HEADS
MEM_HEADS = 4
WINDOW = 128
EPS = 1e-6
NEG_INF = -1e30
LOG2E = 1.4426950408889634
FOX_BLK = 256

FOX_W = FOX_HEADS * HEAD_DIM
SWA_Q_W = SWA_HEADS * HEAD_DIM
SWA_KV_W = SWA_KV_HEADS * HEAD_DIM
MEM_W = MEM_HEADS * HEAD_DIM

PROJ_W = 3 * FOX_W + SWA_Q_W + 2 * SWA_KV_W + MEM_W
REST_W = SWA_Q_W + 2 * SWA_KV_W + MEM_W
REST_SQ, REST_SK, REST_MQ, REST_SV = 0, SWA_Q_W, SWA_Q_W + SWA_KV_W, SWA_Q_W + SWA_KV_W + MEM_W
GAIN_FQ, GAIN_FK, GAIN_SQ, GAIN_SK, GAIN_MQ = range(5)

LANES = 128
SUBLANES = 8
MXU_N = 256

VMEM_LIMIT = 58 * 1024 * 1024


def _cparams(sem):
    return pltpu.CompilerParams(dimension_semantics=sem, vmem_limit_bytes=VMEM_LIMIT)


def _rms(x, gain):
    ms = jnp.mean(x * x, axis=-1, keepdims=True)
    return x * lax.rsqrt(ms + EPS) * gain


def _ffn_kernel(x_ref, g_ref, wg_ref, wu_ref, wd_ref, *rest, n_chunk, n_cast):
    cast_in, o_ref, cast_out, xn_ref = rest[:n_cast], rest[n_cast], rest[n_cast + 1:-1], rest[-1]
    j = pl.program_id(1)

    @pl.when(j == 0)
    def _():
        x = x_ref[...]
        xn_ref[...] = _rms(x, g_ref[...]).astype(BF16)
        o_ref[...] = x

    xn = xn_ref[...]
    gate = jnp.dot(xn, wg_ref[...], preferred_element_type=F32)
    up = jnp.dot(xn, wu_ref[...], preferred_element_type=F32)
    h = (gate * jax.nn.sigmoid(gate) * (0.5 * up)).astype(BF16)
    d_model = o_ref.shape[1]
    cw = d_model // n_chunk
    for c in range(n_chunk):
        cs = slice(c * cw, (c + 1) * cw)
        o_ref[:, cs] += jnp.dot(h, wd_ref[:, cs], preferred_element_type=F32)

    for src, dst in zip(cast_in, cast_out):
        dst[...] = src[...].astype(BF16)


def _ffn(x, gain, wg, wu, wd, *, next_weights=(), tm=1024, tf=512):
    t, d = x.shape
    dff = wg.shape[1]
    gi, gj = t // tm, dff // tf
    cast_specs, cast_shapes = [], []
    for w in next_weights:
        ri, cj = (gi, gj) if w.shape[1] == dff else (gj, gi)
        blk = (w.shape[0] // ri, w.shape[1] // cj)
        assert blk[0] % SUBLANES == 0 and blk[1] % LANES == 0
        index = (lambda i, j: (i, j)) if w.shape[1] == dff else (lambda i, j: (j, i))
        cast_specs.append(pl.BlockSpec(blk, index))
        cast_shapes.append(jax.ShapeDtypeStruct(w.shape, BF16))
    out = pl.pallas_call(
        functools.partial(_ffn_kernel, n_chunk=4, n_cast=len(next_weights)),
        out_shape=(jax.ShapeDtypeStruct((t, d), F32), *cast_shapes),
        grid=(gi, gj),
        in_specs=[
            pl.BlockSpec((tm, d), lambda i, j: (i, 0)),
            pl.BlockSpec((1, d), lambda i, j: (0, 0)),
            pl.BlockSpec((d, tf), lambda i, j: (0, j)),
            pl.BlockSpec((d, tf), lambda i, j: (0, j)),
            pl.BlockSpec((tf, d), lambda i, j: (j, 0)),
            *cast_specs,
        ],
        out_specs=(pl.BlockSpec((tm, d), lambda i, j: (i, 0)), *cast_specs),
        scratch_shapes=[pltpu.VMEM((tm, d), BF16)],
        compiler_params=_cparams(("parallel", "arbitrary")),
        name="ffn",
    )(x, gain, wg, wu, wd, *next_weights)
    return out[0], out[1:]


def _in_proj_kernel(x_ref, g_ref, w_ref, wf_ref, hg_ref, fb_ref, wo_ref,
                    qt_ref, ka_ref, vt_ref, rest_ref, anc_ref, wo_bf_ref, carry_ref, *, blocks_per_seq):
    i = pl.program_id(0)
    tm = x_ref.shape[0]
    wo_bf_ref[...] = wo_ref[...].astype(BF16)
    hn = _rms(x_ref[...], g_ref[...]).astype(BF16)

    nt = (((1,), (1,)), ((), ()))
    logit = lax.dot_general(hn, wf_ref[...], nt, preferred_element_type=F32)
    c = jax.nn.log_sigmoid(logit + fb_ref[...]) * LOG2E
    row = lax.broadcasted_iota(jnp.int32, c.shape, 0) % FOX_BLK
    sh = 1
    while sh < FOX_BLK:
        c = c + jnp.where(row >= sh, pltpu.roll(c, sh, 0), 0.0)
        sh *= 2

    @pl.when(i % blocks_per_seq == 0)
    def _():
        carry_ref[...] = jnp.zeros_like(carry_ref)

    anchor = carry_ref[0:1, :]
    anc_ref[...] = jnp.zeros_like(anc_ref)
    for r in range(tm // FOX_BLK):
        anc_ref[0, r:r + 1, :] = anchor
        anchor = anchor + c[(r + 1) * FOX_BLK - 1:(r + 1) * FOX_BLK, :]
    carry_ref[0:1, :] = anchor

    n_sub = tm // FOX_BLK
    lane = lax.broadcasted_iota(jnp.int32, c.shape, 1)
    sub = lax.broadcasted_iota(jnp.int32, (HEAD_DIM, FOX_BLK), 0)
    ct = c.T
    for h in range(FOX_HEADS):
        ch = c[:, h:h + 1]
        hi = ch.astype(BF16).astype(F32)
        lo = ch - hi
        ak = jnp.where(lane < 2, 1.0, jnp.where(lane == 2, -hi, jnp.where(lane == 3, -lo, 0.0)))
        ka_ref[:, (2 * h + 1) * HEAD_DIM:(2 * h + 2) * HEAD_DIM] = ak.astype(BF16)
        for r in range(n_sub):
            cr = ct[h:h + 1, r * FOX_BLK:(r + 1) * FOX_BLK]
            hi_r = cr.astype(BF16).astype(F32)
            lo_r = cr - hi_r
            aq = jnp.where(sub == 0, hi_r, jnp.where(sub == 1, lo_r, jnp.where(sub < 4, 1.0, 0.0)))
            qt_ref[r, (2 * h + 1) * HEAD_DIM:(2 * h + 2) * HEAD_DIM, :] = aq.astype(BF16)

    def normed(ph, row_id):
        return _rms(ph, hg_ref[row_id:row_id + 1, :]).astype(BF16)

    heads_per_chunk = MXU_N // HEAD_DIM
    for ck in range(PROJ_W // MXU_N):
        p = lax.dot_general(hn, w_ref[ck * MXU_N:(ck + 1) * MXU_N, :], nt, preferred_element_type=F32)
        for e in range(heads_per_chunk):
            col = ck * MXU_N + e * HEAD_DIM
            ph = p[:, e * HEAD_DIM:(e + 1) * HEAD_DIM]
            if col < FOX_W:
                h = col // HEAD_DIM
                qn = _rms(ph, hg_ref[GAIN_FQ:GAIN_FQ + 1, :])
                for r in range(n_sub):
                    qt_ref[r, 2 * h * HEAD_DIM:(2 * h + 1) * HEAD_DIM, :] = (
                        qn[r * FOX_BLK:(r + 1) * FOX_BLK, :].T.astype(BF16))
            elif col < 2 * FOX_W:
                h = (col - FOX_W) // HEAD_DIM
                ka_ref[:, 2 * h * HEAD_DIM:(2 * h + 1) * HEAD_DIM] = normed(ph, GAIN_FK)
            elif col < 3 * FOX_W:
                h = (col - 2 * FOX_W) // HEAD_DIM
                for r in range(n_sub):
                    vt_ref[r, h * HEAD_DIM:(h + 1) * HEAD_DIM, :] = (
                        ph[r * FOX_BLK:(r + 1) * FOX_BLK, :].T.astype(BF16))
            else:
                wc = col - 3 * FOX_W
                if wc < SWA_Q_W:
                    val, rc = normed(ph, GAIN_SQ), REST_SQ + wc
                elif wc < SWA_Q_W + SWA_KV_W:
                    val, rc = normed(ph, GAIN_SK), REST_SK + wc - SWA_Q_W
                elif wc < SWA_Q_W + 2 * SWA_KV_W:
                    val, rc = ph.astype(BF16), REST_SV + wc - SWA_Q_W - SWA_KV_W
                else:
                    val, rc = normed(ph, GAIN_MQ), REST_MQ + wc - SWA_Q_W - 2 * SWA_KV_W
                rest_ref[:, rc:rc + HEAD_DIM] = val


def _in_proj(x, gain, w, wf, head_gain, fbias, w_out, *, seq, tm=512):
    t, d = x.shape
    wo_rows = w_out.shape[0] // (t // tm)
    assert wo_rows % (2 * SUBLANES) == 0
    assert seq % tm == 0 and tm % FOX_BLK == 0 and tm // FOX_BLK <= SUBLANES
    const = lambda shape: pl.BlockSpec(shape, lambda i: (0, 0), pipeline_mode=pl.Buffered(1))
    return pl.pallas_call(
        functools.partial(_in_proj_kernel, blocks_per_seq=seq // tm),
        out_shape=(
            jax.ShapeDtypeStruct((t // FOX_BLK, 2 * FOX_W, FOX_BLK), BF16),
            jax.ShapeDtypeStruct((t, 2 * FOX_W), BF16),
            jax.ShapeDtypeStruct((t // FOX_BLK, FOX_W, FOX_BLK), BF16),
            jax.ShapeDtypeStruct((t, REST_W), BF16),
            jax.ShapeDtypeStruct((t // tm, SUBLANES, LANES), F32),
            jax.ShapeDtypeStruct(w_out.shape, BF16),
        ),
        grid=(t // tm,),
        in_specs=[
            pl.BlockSpec((tm, d), lambda i: (i, 0)),
            const((1, d)),
            const((PROJ_W, d)),
            const((LANES, d)),
            const((SUBLANES, HEAD_DIM)),
            const((1, LANES)),
            pl.BlockSpec((wo_rows, w_out.shape[1]), lambda i: (i, 0)),
        ],
        out_specs=(
            pl.BlockSpec((tm // FOX_BLK, 2 * FOX_W, FOX_BLK), lambda i: (i, 0, 0)),
            pl.BlockSpec((tm, 2 * FOX_W), lambda i: (i, 0)),
            pl.BlockSpec((tm // FOX_BLK, FOX_W, FOX_BLK), lambda i: (i, 0, 0)),
            pl.BlockSpec((tm, REST_W), lambda i: (i, 0)),
            pl.BlockSpec((1, SUBLANES, LANES), lambda i: (i, 0, 0)),
            pl.BlockSpec((wo_rows, w_out.shape[1]), lambda i: (i, 0)),
        ),
        scratch_shapes=[pltpu.VMEM((SUBLANES, LANES), F32)],
        compiler_params=_cparams(("arbitrary",)),
        name="in_proj",
    )(x, gain, w, wf, head_gain, fbias, w_out)


def _fox_kernel(anc_ref, q_ref, k_ref, vt_ref, o_ref, m_ref, l_ref, acc_ref, st_ref, *, nblk):
    b = pl.program_id(0)
    i = pl.program_id(1)
    blk = q_ref.shape[2]
    aw = 2 * HEAD_DIM
    m_ref[...] = jnp.full(m_ref.shape, NEG_INF, F32)
    l_ref[...] = jnp.zeros(l_ref.shape, F32)
    acc_ref[...] = jnp.zeros(acc_ref.shape, F32)
    krow = lax.broadcasted_iota(jnp.int32, (blk, blk), 0)
    qcol = lax.broadcasted_iota(jnp.int32, (blk, blk), 1)

    def scores(j, slot):
        ks = pl.ds(pl.multiple_of(j * blk, blk), blk)
        for h in range(FOX_HEADS):
            st_ref[slot, h] = jnp.dot(k_ref[ks, h * aw:(h + 1) * aw], q_ref[0, h * aw:(h + 1) * aw, :],
                                      preferred_element_type=F32)

    def softmax_pv(j, slot, masked):
        for h in range(FOX_HEADS):
            st = st_ref[slot, h]
            d = anc_ref[(b * nblk + i) * SUBLANES + h] - anc_ref[(b * nblk + j) * SUBLANES + h]
            if masked:
                st = jnp.where(krow <= qcol, st, NEG_INF)
            m_prev = m_ref[h]
            m_new = jnp.maximum(m_prev, jnp.max(st, axis=0, keepdims=True) + d)
            alpha = jnp.exp2(m_prev - m_new)
            p = jnp.exp2(st - (m_new - d))
            l_ref[h] = alpha * l_ref[h] + jnp.sum(p, axis=0, keepdims=True)
            pv = jnp.dot(vt_ref[j, h * HEAD_DIM:(h + 1) * HEAD_DIM, :], p.astype(BF16),
                         preferred_element_type=F32)
            acc_ref[h] = alpha * acc_ref[h] + pv
            m_ref[h] = m_new

    def body(t, carry):
        j = 2 * t
        scores(j + 1, 1)
        softmax_pv(j, 0, False)
        scores(j + 2, 0)
        softmax_pv(j + 1, 1, False)
        return carry

    scores(0, 0)
    lax.fori_loop(0, i // 2, body, 0)

    @pl.when(i % 2 == 0)
    def _():
        softmax_pv(i, 0, True)

    @pl.when(i % 2 == 1)
    def _():
        scores(i, 1)
        softmax_pv(i - 1, 0, False)
        softmax_pv(i, 1, True)

    for h in range(FOX_HEADS):
        o_ref[:, h * HEAD_DIM:(h + 1) * HEAD_DIM] = (acc_ref[h] / l_ref[h]).T.astype(o_ref.dtype)


def _fox(anchors, qt, ka, vt, *, batch, seq):
    t = ka.shape[0]
    blk = FOX_BLK
    nblk = seq // blk
    return pl.pallas_call(
        functools.partial(_fox_kernel, nblk=nblk),
        out_shape=jax.ShapeDtypeStruct((t, FOX_W), BF16),
        grid=(batch, nblk),
        in_specs=[
            pl.BlockSpec(memory_space=pltpu.SMEM),
            pl.BlockSpec((1, 2 * FOX_W, blk), lambda b, i: (b * nblk + i, 0, 0)),
            pl.BlockSpec((seq, 2 * FOX_W), lambda b, i: (b, 0)),
            pl.BlockSpec((nblk, FOX_W, blk), lambda b, i: (b, 0, 0)),
        ],
        out_specs=pl.BlockSpec((blk, FOX_W), lambda b, i: (b * nblk + i, 0)),
        scratch_shapes=[
            pltpu.VMEM((FOX_HEADS, 1, blk), F32),
            pltpu.VMEM((FOX_HEADS, 1, blk), F32),
            pltpu.VMEM((FOX_HEADS, HEAD_DIM, blk), F32),
            pltpu.VMEM((2, FOX_HEADS, blk, blk), F32),
        ],
        compiler_params=_cparams(("parallel", "arbitrary")),
        name="fox",
    )(anchors, qt, ka, vt)


def _swa_kernel(q_ref, k_ref, v_ref, sink_ref, slope_ref, o_ref):
    qi = pl.program_id(1)
    tq = q_ref.shape[0]
    w = WINDOW
    row = lax.broadcasted_iota(jnp.int32, (w, 2 * w), 0)
    col = lax.broadcasted_iota(jnp.int32, (w, 2 * w), 1)

    for r in range(tq // w):
        n = qi * (tq // w) + r
        kstart = pl.multiple_of(jnp.maximum(n - 1, 0) * w, w)
        dist = (n * w - kstart) + row - col
        valid = (dist >= 0) & (dist < w)
        dist_f = dist.astype(F32)
        for h in range(SWA_HEADS):
            g = h // SWA_GROUP
            q = q_ref[r * w:(r + 1) * w, h * HEAD_DIM:(h + 1) * HEAD_DIM]
            k = k_ref[pl.ds(kstart, 2 * w), g * HEAD_DIM:(g + 1) * HEAD_DIM]
            v = v_ref[pl.ds(kstart, 2 * w), g * HEAD_DIM:(g + 1) * HEAD_DIM]
            s = lax.dot_general(q, k, (((1,), (1,)), ((), ())), preferred_element_type=F32)
            s = s - slope_ref[h] * dist_f
            s = jnp.where(valid, s, NEG_INF)
            sink = sink_ref[h]
            m = jnp.maximum(jnp.max(s, axis=-1, keepdims=True), sink)
            p = jnp.exp(s - m)
            denom = jnp.sum(p, axis=-1, keepdims=True) + jnp.exp(sink - m)
            o = jnp.dot(p.astype(BF16), v, preferred_element_type=F32)
            o_ref[r * w:(r + 1) * w, h * HEAD_DIM:(h + 1) * HEAD_DIM] = (o / denom).astype(o_ref.dtype)


def _swa(rest, sinks, slopes, *, batch, seq, tq=512):
    t = rest.shape[0]
    nq = seq // tq
    smem = pl.BlockSpec(memory_space=pltpu.SMEM)
    return pl.pallas_call(
        _swa_kernel,
        out_shape=jax.ShapeDtypeStruct((t, SWA_Q_W), BF16),
        grid=(batch, nq),
        in_specs=[
            pl.BlockSpec((tq, SWA_Q_W), lambda b, i: (b * nq + i, REST_SQ // SWA_Q_W)),
            pl.BlockSpec((seq, SWA_KV_W), lambda b, i: (b, REST_SK // SWA_KV_W)),
            pl.BlockSpec((seq, SWA_KV_W), lambda b, i: (b, REST_SV // SWA_KV_W)),
            smem, smem,
        ],
        out_specs=pl.BlockSpec((tq, SWA_Q_W), lambda b, i: (b * nq + i, 0)),
        compiler_params=_cparams(("parallel", "arbitrary")),
        name="swa",
    )(rest, rest, rest, sinks, slopes)


def _mem_kv_kernel(mem_ref, g_ref, wk_ref, wv_ref, kg_ref, mk_ref, mv_ref):
    mn = _rms(mem_ref[...], g_ref[...]).astype(BF16)
    k = jnp.dot(mn, wk_ref[...].astype(BF16), preferred_element_type=F32)
    v = jnp.dot(mn, wv_ref[...].astype(BF16), preferred_element_type=F32)
    for h in range(MEM_HEADS):
        hs = slice(h * HEAD_DIM, (h + 1) * HEAD_DIM)
        mk_ref[:, hs] = _rms(k[:, hs], kg_ref[...]).astype(BF16)
    mv_ref[...] = v.astype(BF16)


def _mem_kv(mem2d, gain, wk, wv, kgain, *, batch, mem_len):
    d = mem2d.shape[1]
    full = lambda shape: pl.BlockSpec(shape, lambda b: (0, 0))
    out = jax.ShapeDtypeStruct((batch * mem_len, MEM_W), BF16)
    return pl.pallas_call(
        _mem_kv_kernel,
        out_shape=(out, out),
        grid=(batch,),
        in_specs=[
            pl.BlockSpec((mem_len, d), lambda b: (b, 0)),
            full((1, d)), full((d, MEM_W)), full((d, MEM_W)), full((1, HEAD_DIM)),
        ],
        out_specs=(pl.BlockSpec((mem_len, MEM_W), lambda b: (b, 0)),
                   pl.BlockSpec((mem_len, MEM_W), lambda b: (b, 0))),
        compiler_params=_cparams(("parallel",)),
        name="mem_kv",
    )(mem2d, gain, wk, wv, kgain)


def _mem_attn_kernel(q_ref, mk_ref, mv_ref, o_ref):
    for h in range(MEM_HEADS):
        hs = slice(h * HEAD_DIM, (h + 1) * HEAD_DIM)
        s = lax.dot_general(q_ref[:, hs], mk_ref[:, hs], (((1,), (1,)), ((), ())),
                            preferred_element_type=F32)
        m = jnp.max(s, axis=-1, keepdims=True)
        p = jnp.exp(s - m)
        denom = jnp.sum(p, axis=-1, keepdims=True)
        o = jnp.dot(p.astype(BF16), mv_ref[:, hs], preferred_element_type=F32)
        o_ref[:, hs] = (o / denom).astype(o_ref.dtype)


def _mem_attn(rest, mk, mv, *, batch, seq, mem_len, tq=512):
    t = rest.shape[0]
    nq = seq // tq
    return pl.pallas_call(
        _mem_attn_kernel,
        out_shape=jax.ShapeDtypeStruct((t, MEM_W), BF16),
        grid=(batch, nq),
        in_specs=[
            pl.BlockSpec((tq, MEM_W), lambda b, i: (b * nq + i, REST_MQ // MEM_W)),
            pl.BlockSpec((mem_len, MEM_W), lambda b, i: (b, 0)),
            pl.BlockSpec((mem_len, MEM_W), lambda b, i: (b, 0)),
        ],
        out_specs=pl.BlockSpec((tq, MEM_W), lambda b, i: (b * nq + i, 0)),
        compiler_params=_cparams(("parallel", "arbitrary")),
        name="mem_attn",
    )(rest, mk, mv)


def _out_proj_kernel(x_ref, a_ref, b_ref, c_ref, w_ref, o_ref):
    acc = jnp.dot(a_ref[...], w_ref[0:FOX_W, :], preferred_element_type=F32)
    acc += jnp.dot(b_ref[...], w_ref[FOX_W:FOX_W + SWA_Q_W, :], preferred_element_type=F32)
    acc += jnp.dot(c_ref[...], w_ref[FOX_W + SWA_Q_W:, :], preferred_element_type=F32)
    o_ref[...] = x_ref[...] + acc


def _out_proj(x, a, b, c, w, *, tm=512):
    t, d = x.shape
    return pl.pallas_call(
        _out_proj_kernel,
        out_shape=jax.ShapeDtypeStruct((t, d), F32),
        grid=(t // tm,),
        in_specs=[
            pl.BlockSpec((tm, d), lambda i: (i, 0)),
            pl.BlockSpec((tm, FOX_W), lambda i: (i, 0)),
            pl.BlockSpec((tm, SWA_Q_W), lambda i: (i, 0)),
            pl.BlockSpec((tm, MEM_W), lambda i: (i, 0)),
            pl.BlockSpec(w.shape, lambda i: (0, 0)),
        ],
        out_specs=pl.BlockSpec((tm, d), lambda i: (i, 0)),
        compiler_params=_cparams(("parallel",)),
        name="out_proj",
    )(x, a, b, c, w)


def _alibi_slopes(n):
    return jnp.asarray([2.0 ** (-8.0 * i / n) for i in range(1, n + 1)], dtype=F32)


def _pack_w_in_kernel(w_ref, main_ref, f_ref, *, n_main):
    i = pl.program_id(0)
    n_a = w_ref.shape[2]

    @pl.when(i < n_main)
    def _():
        for a in range(n_a):
            main_ref[:, a * LANES:(a + 1) * LANES] = w_ref[0, :, a, :].astype(BF16)

    @pl.when(i == n_main)
    def _():
        row = lax.broadcasted_iota(jnp.int32, (f_ref.shape[0], LANES), 0)
        for a in range(n_a):
            f_ref[:, a * LANES:(a + 1) * LANES] = jnp.where(
                row < FOX_HEADS, w_ref[0, 0:f_ref.shape[0], a, :], 0.0).astype(BF16)


def _pack_w_in(w_in, l, *, rows=256):
    d, n = w_in.shape[1:]
    fl0 = 3 * FOX_W
    assert n == PROJ_W + FOX_HEADS and fl0 % rows == 0 and PROJ_W % rows == 0 and d % LANES == 0
    wt = jnp.swapaxes(w_in, 1, 2).reshape(w_in.shape[0], n, d // LANES, LANES)
    n_main = PROJ_W // rows

    def start(i):
        main = jnp.where(i * rows < fl0, i * rows, i * rows + FOX_HEADS)
        return jnp.where(i < n_main, main, fl0)

    return pl.pallas_call(
        functools.partial(_pack_w_in_kernel, n_main=n_main),
        out_shape=(jax.ShapeDtypeStruct((PROJ_W, d), BF16), jax.ShapeDtypeStruct((LANES, d), BF16)),
        grid=(n_main + 1,),
        in_specs=[pl.BlockSpec((pl.Element(1), pl.Element(rows), pl.Element(d // LANES), pl.Element(LANES)),
                               lambda i: (l, start(i), 0, 0))],
        out_specs=(pl.BlockSpec((rows, d), lambda i: (jnp.minimum(i, n_main - 1), 0)),
                   pl.BlockSpec((LANES, d), lambda i: (0, 0))),
        compiler_params=_cparams(("arbitrary",)),
        name="pack_w_in",
    )(wt)


def kernel(x, mem, ffn1_norm, ffn1_gate, ffn1_up, ffn1_down, mix_norm, mem_norm, w_in, forget_bias, w_mem_k, w_mem_v, fox_q_gain, fox_k_gain, swa_q_gain, swa_k_gain, swa_sinks, mem_q_gain, mem_k_gain, w_out, ffn2_norm, ffn2_gate, ffn2_up, ffn2_down):
    batch, seq, d = x.shape
    mem_len = mem.shape[1]
    depth = w_in.shape[0]
    scale = HEAD_DIM ** -0.5
    slopes = _alibi_slopes(SWA_HEADS)
    x2 = x.reshape(batch * seq, d)
    mem2 = mem.reshape(batch * mem_len, d)
    zeros = jnp.zeros((HEAD_DIM,), F32)

    for l in range(depth):
        x2, ffn2_w = _ffn(x2, ffn1_norm[l][None], ffn1_gate[l].astype(BF16), ffn1_up[l].astype(BF16),
                          ffn1_down[l].astype(BF16),
                          next_weights=(ffn2_gate[l], ffn2_up[l], ffn2_down[l]))

        w_main, w_f = _pack_w_in(w_in, l)
        head_gain = jnp.stack([fox_q_gain[l] * (scale * LOG2E), fox_k_gain[l], swa_q_gain[l] * scale,
                               swa_k_gain[l], mem_q_gain[l] * scale, zeros, zeros, zeros])
        fbias = jnp.pad(forget_bias[l], (0, LANES - FOX_HEADS))[None]
        qt, ka, vt, rest, anc, w_out_bf = _in_proj(x2, mix_norm[l][None], w_main, w_f, head_gain, fbias,
                                                   w_out[l], seq=seq)

        n_sub = (x2.shape[0] // FOX_BLK) // anc.shape[0]
        anchors = anc[:, :n_sub, :SUBLANES].reshape(-1)
        out_a = _fox(anchors, qt, ka, vt, batch=batch, seq=seq)
        out_b = _swa(rest, swa_sinks[l], slopes, batch=batch, seq=seq)
        mk, mv = _mem_kv(mem2, mem_norm[l][None], w_mem_k[l], w_mem_v[l],
                         mem_k_gain[l][None], batch=batch, mem_len=mem_len)
        out_c = _mem_attn(rest, mk, mv, batch=batch, seq=seq, mem_len=mem_len)

        x2 = _out_proj(x2, out_a, out_b, out_c, w_out_bf)
        x2, _ = _ffn(x2, ffn2_norm[l][None], *ffn2_w)
    return x2.reshape(batch, seq, d)
```

```python
import functools

import jax
import jax.numpy as jnp
from jax import lax
from jax.experimental import pallas as pl
from jax.experimental.pallas import tpu as pltpu

F32 = jnp.float32
BF16 = jnp.bfloat16

HEAD_DIM = 128
FOX_HEADS = 6
SWA_HEADS = 6
SWA_KV_HEADS = 2
SWA_GROUP = SWA_HEADS // SWA_KV_HEADS
MEM_HEADS = 4
WINDOW = 128
EPS = 1e-6
NEG_INF = -1e30
LOG2E = 1.4426950408889634
FOX_BLK = 256

FOX_W = FOX_HEADS * HEAD_DIM
SWA_Q_W = SWA_HEADS * HEAD_DIM
SWA_KV_W = SWA_KV_HEADS * HEAD_DIM
MEM_W = MEM_HEADS * HEAD_DIM

PROJ_W = 3 * FOX_W + SWA_Q_W + 2 * SWA_KV_W + MEM_W
REST_W = SWA_Q_W + 2 * SWA_KV_W + MEM_W
REST_SQ, REST_SK, REST_MQ, REST_SV = 0, SWA_Q_W, SWA_Q_W + SWA_KV_W, SWA_Q_W + SWA_KV_W + MEM_W
GAIN_FQ, GAIN_FK, GAIN_SQ, GAIN_SK, GAIN_MQ = range(5)

LANES = 128
SUBLANES = 8
MXU_N = 256

VMEM_LIMIT = 58 * 1024 * 1024


def _cparams(sem):
    return pltpu.CompilerParams(dimension_semantics=sem, vmem_limit_bytes=VMEM_LIMIT)


def _rms(x, gain):
    ms = jnp.mean(x * x, axis=-1, keepdims=True)
    return x * lax.rsqrt(ms + EPS) * gain


def _ffn_kernel(x_ref, g_ref, wg_ref, wu_ref, wd_ref, *rest, n_chunk, n_cast):
    cast_in, o_ref, cast_out, xn_ref = rest[:n_cast], rest[n_cast], rest[n_cast + 1:-1], rest[-1]
    j = pl.program_id(1)

    @pl.when(j == 0)
    def _():
        x = x_ref[...]
        xn_ref[...] = _rms(x, g_ref[...]).astype(BF16)
        o_ref[...] = x

    xn = xn_ref[...]
    gate = jnp.dot(xn, wg_ref[...], preferred_element_type=F32)
    up = jnp.dot(xn, wu_ref[...], preferred_element_type=F32)
    h = (gate * jax.nn.sigmoid(gate) * (0.5 * up)).astype(BF16)
    d_model = o_ref.shape[1]
    cw = d_model // n_chunk
    for c in range(n_chunk):
        cs = slice(c * cw, (c + 1) * cw)
        o_ref[:, cs] += jnp.dot(h, wd_ref[:, cs], preferred_element_type=F32)

    for src, dst in zip(cast_in, cast_out):
        dst[...] = src[...].astype(BF16)


def _ffn(x, gain, wg, wu, wd, *, next_weights=(), tm=1024, tf=512):
    t, d = x.shape
    dff = wg.shape[1]
    gi, gj = t // tm, dff // tf
    cast_specs, cast_shapes = [], []
    for w in next_weights:
        ri, cj = (gi, gj) if w.shape[1] == dff else (gj, gi)
        blk = (w.shape[0] // ri, w.shape[1] // cj)
        assert blk[0] % SUBLANES == 0 and blk[1] % LANES == 0
        index = (lambda i, j: (i, j)) if w.shape[1] == dff else (lambda i, j: (j, i))
        cast_specs.append(pl.BlockSpec(blk, index))
        cast_shapes.append(jax.ShapeDtypeStruct(w.shape, BF16))
    out = pl.pallas_call(
        functools.partial(_ffn_kernel, n_chunk=4, n_cast=len(next_weights)),
        out_shape=(jax.ShapeDtypeStruct((t, d), F32), *cast_shapes),
        grid=(gi, gj),
        in_specs=[
            pl.BlockSpec((tm, d), lambda i, j: (i, 0)),
            pl.BlockSpec((1, d), lambda i, j: (0, 0)),
            pl.BlockSpec((d, tf), lambda i, j: (0, j)),
            pl.BlockSpec((d, tf), lambda i, j: (0, j)),
            pl.BlockSpec((tf, d), lambda i, j: (j, 0)),
            *cast_specs,
        ],
        out_specs=(pl.BlockSpec((tm, d), lambda i, j: (i, 0)), *cast_specs),
        scratch_shapes=[pltpu.VMEM((tm, d), BF16)],
        compiler_params=_cparams(("parallel", "arbitrary")),
        name="ffn",
    )(x, gain, wg, wu, wd, *next_weights)
    return out[0], out[1:]


def _in_proj_kernel(x_ref, g_ref, w_ref, wf_ref, hg_ref, fb_ref, wo_ref,
                    qt_ref, ka_ref, vt_ref, rest_ref, anc_ref, wo_bf_ref, carry_ref, *, blocks_per_seq):
    i = pl.program_id(0)
    tm = x_ref.shape[0]
    @pl.when(i == 0)
    def _():
        carry_ref[...] = jnp.zeros_like(carry_ref)

    wo_bf_ref[...] = wo_ref[...].astype(BF16)
    hn = _rms(x_ref[...], g_ref[...]).astype(BF16)

    nt = (((1,), (1,)), ((), ()))
    logit = lax.dot_general(hn, wf_ref[...], nt, preferred_element_type=F32)
    c = jax.nn.log_sigmoid(logit + fb_ref[...]) * LOG2E
    row = lax.broadcasted_iota(jnp.int32, c.shape, 0) % FOX_BLK
    sh = 1
    while sh < FOX_BLK:
        c = c + jnp.where(row >= sh, pltpu.roll(c, sh, 0), 0.0)
        sh *= 2

    anchor = jnp.where(i % blocks_per_seq == 0, 0.0, carry_ref[0:1, :])
    anc_ref[...] = jnp.zeros_like(anc_ref)
    for r in range(tm // FOX_BLK):
        anc_ref[0, r:r + 1, :] = anchor
        anchor = anchor + c[(r + 1) * FOX_BLK - 1:(r + 1) * FOX_BLK, :]
    carry_ref[0:1, :] = anchor

    n_sub = tm // FOX_BLK
    lane = lax.broadcasted_iota(jnp.int32, c.shape, 1)
    sub = lax.broadcasted_iota(jnp.int32, (HEAD_DIM, FOX_BLK), 0)
    ct = c.T
    for h in range(FOX_HEADS):
        ch = c[:, h:h + 1]
        hi = ch.astype(BF16).astype(F32)
        lo = ch - hi
        ak = jnp.where(lane < 2, 1.0, jnp.where(lane == 2, -hi, jnp.where(lane == 3, -lo, 0.0)))
        ka_ref[:, (2 * h + 1) * HEAD_DIM:(2 * h + 2) * HEAD_DIM] = ak.astype(BF16)
        for r in range(n_sub):
            cr = ct[h:h + 1, r * FOX_BLK:(r + 1) * FOX_BLK]
            hi_r = cr.astype(BF16).astype(F32)
            lo_r = cr - hi_r
            aq = jnp.where(sub == 0, hi_r, jnp.where(sub == 1, lo_r, jnp.where(sub < 4, 1.0, 0.0)))
            qt_ref[r, (2 * h + 1) * HEAD_DIM:(2 * h + 2) * HEAD_DIM, :] = aq.astype(BF16)

    def normed(ph, row_id):
        return _rms(ph, hg_ref[row_id:row_id + 1, :]).astype(BF16)

    heads_per_chunk = MXU_N // HEAD_DIM
    for ck in range(PROJ_W // MXU_N):
        p = lax.dot_general(hn, w_ref[ck * MXU_N:(ck + 1) * MXU_N, :], nt, preferred_element_type=F32)
        for e in range(heads_per_chunk):
            col = ck * MXU_N + e * HEAD_DIM
            ph = p[:, e * HEAD_DIM:(e + 1) * HEAD_DIM]
            if col < FOX_W:
                h = col // HEAD_DIM
                qn = _rms(ph, hg_ref[GAIN_FQ:GAIN_FQ + 1, :])
                for r in range(n_sub):
                    qt_ref[r, 2 * h * HEAD_DIM:(2 * h + 1) * HEAD_DIM, :] = (
                        qn[r * FOX_BLK:(r + 1) * FOX_BLK, :].T.astype(BF16))
            elif col < 2 * FOX_W:
                h = (col - FOX_W) // HEAD_DIM
                ka_ref[:, 2 * h * HEAD_DIM:(2 * h + 1) * HEAD_DIM] = normed(ph, GAIN_FK)
            elif col < 3 * FOX_W:
                h = (col - 2 * FOX_W) // HEAD_DIM
                for r in range(n_sub):
                    vt_ref[r, h * HEAD_DIM:(h + 1) * HEAD_DIM, :] = (
                        ph[r * FOX_BLK:(r + 1) * FOX_BLK, :].T.astype(BF16))
            else:
                wc = col - 3 * FOX_W
                if wc < SWA_Q_W:
                    val, rc = normed(ph, GAIN_SQ), REST_SQ + wc
                elif wc < SWA_Q_W + SWA_KV_W:
                    val, rc = normed(ph, GAIN_SK), REST_SK + wc - SWA_Q_W
                elif wc < SWA_Q_W + 2 * SWA_KV_W:
                    val, rc = ph.astype(BF16), REST_SV + wc - SWA_Q_W - SWA_KV_W
                else:
                    val, rc = normed(ph, GAIN_MQ), REST_MQ + wc - SWA_Q_W - 2 * SWA_KV_W
                rest_ref[:, rc:rc + HEAD_DIM] = val


def _in_proj(x, gain, w, wf, head_gain, fbias, w_out, *, seq, tm=512):
    t, d = x.shape
    wo_rows = w_out.shape[0] // (t // tm)
    assert wo_rows % (2 * SUBLANES) == 0
    assert seq % tm == 0 and tm % FOX_BLK == 0 and tm // FOX_BLK <= SUBLANES
    const = lambda shape: pl.BlockSpec(shape, lambda i: (0, 0), pipeline_mode=pl.Buffered(1))
    return pl.pallas_call(
        functools.partial(_in_proj_kernel, blocks_per_seq=seq // tm),
        out_shape=(
            jax.ShapeDtypeStruct((t // FOX_BLK, 2 * FOX_W, FOX_BLK), BF16),
            jax.ShapeDtypeStruct((t, 2 * FOX_W), BF16),
            jax.ShapeDtypeStruct((t // FOX_BLK, FOX_W, FOX_BLK), BF16),
            jax.ShapeDtypeStruct((t, REST_W), BF16),
            jax.ShapeDtypeStruct((t // tm, SUBLANES, LANES), F32),
            jax.ShapeDtypeStruct(w_out.shape, BF16),
        ),
        grid=(t // tm,),
        in_specs=[
            pl.BlockSpec((tm, d), lambda i: (i, 0)),
            const((1, d)),
            const((PROJ_W, d)),
            const((LANES, d)),
            const((SUBLANES, HEAD_DIM)),
            const((1, LANES)),
            pl.BlockSpec((wo_rows, w_out.shape[1]), lambda i: (i, 0)),
        ],
        out_specs=(
            pl.BlockSpec((tm // FOX_BLK, 2 * FOX_W, FOX_BLK), lambda i: (i, 0, 0)),
            pl.BlockSpec((tm, 2 * FOX_W), lambda i: (i, 0)),
            pl.BlockSpec((tm // FOX_BLK, FOX_W, FOX_BLK), lambda i: (i, 0, 0)),
            pl.BlockSpec((tm, REST_W), lambda i: (i, 0)),
            pl.BlockSpec((1, SUBLANES, LANES), lambda i: (i, 0, 0)),
            pl.BlockSpec((wo_rows, w_out.shape[1]), lambda i: (i, 0)),
        ),
        scratch_shapes=[pltpu.VMEM((SUBLANES, LANES), F32)],
        compiler_params=_cparams(("arbitrary",)),
        name="in_proj",
    )(x, gain, w, wf, head_gain, fbias, w_out)


def _fox_kernel(anc_ref, q_ref, k_ref, vt_ref, o_ref, m_ref, l_ref, acc_ref, st_ref, *, nblk):
    b = pl.program_id(0)
    q_per_step, _, blk = q_ref.shape
    aw = 2 * HEAD_DIM
    krow = lax.broadcasted_iota(jnp.int32, (blk, blk), 0)
    qcol = lax.broadcasted_iota(jnp.int32, (blk, blk), 1)
    lax.fori_loop(0, q_per_step, functools.partial(
        _fox_query_block, anc_ref, q_ref, k_ref, vt_ref, o_ref, m_ref, l_ref, acc_ref, st_ref,
        b, nblk, krow, qcol), 0)


def _fox_query_block(anc_ref, q_ref, k_ref, vt_ref, o_ref, m_ref, l_ref, acc_ref, st_ref,
                     b, nblk, krow, qcol, sub, carry):
    q_per_step, _, blk = q_ref.shape
    aw = 2 * HEAD_DIM
    i = pl.program_id(1) * q_per_step + sub
    m_ref[...] = jnp.full(m_ref.shape, NEG_INF, F32)
    l_ref[...] = jnp.zeros(l_ref.shape, F32)
    acc_ref[...] = jnp.zeros(acc_ref.shape, F32)

    def scores(j, slot):
        ks = pl.ds(pl.multiple_of(j * blk, blk), blk)
        for h in range(FOX_HEADS):
            st_ref[slot, h] = jnp.dot(k_ref[ks, h * aw:(h + 1) * aw], q_ref[sub, h * aw:(h + 1) * aw, :],
                                      preferred_element_type=F32)

    def softmax_pv(j, slot, masked):
        for h in range(FOX_HEADS):
            st = st_ref[slot, h]
            d = anc_ref[(b * nblk + i) * SUBLANES + h] - anc_ref[(b * nblk + j) * SUBLANES + h]
            if masked:
                st = jnp.where(krow <= qcol, st, NEG_INF)
            m_prev = m_ref[h]
            m_new = jnp.maximum(m_prev, jnp.max(st, axis=0, keepdims=True) + d)
            alpha = jnp.exp2(m_prev - m_new)
            p = jnp.exp2(st - (m_new - d))
            l_ref[h] = alpha * l_ref[h] + jnp.sum(p, axis=0, keepdims=True)
            pv = jnp.dot(vt_ref[j, h * HEAD_DIM:(h + 1) * HEAD_DIM, :], p.astype(BF16),
                         preferred_element_type=F32)
            acc_ref[h] = alpha * acc_ref[h] + pv
            m_ref[h] = m_new

    def body(t, c):
        j = 2 * t
        scores(j + 1, 1)
        softmax_pv(j, 0, False)
        scores(j + 2, 0)
        softmax_pv(j + 1, 1, False)
        return c

    scores(0, 0)
    lax.fori_loop(0, i // 2, body, 0)

    @pl.when(i % 2 == 0)
    def _():
        softmax_pv(i, 0, True)

    @pl.when(i % 2 == 1)
    def _():
        scores(i, 1)
        softmax_pv(i - 1, 0, False)
        softmax_pv(i, 1, True)

    rows = pl.ds(pl.multiple_of(sub * blk, blk), blk)
    for h in range(FOX_HEADS):
        o_ref[rows, h * HEAD_DIM:(h + 1) * HEAD_DIM] = (acc_ref[h] / l_ref[h]).T.astype(o_ref.dtype)
    return carry


def _fox(anchors, qt, ka, vt, *, batch, seq, q_per_step=4):
    t = ka.shape[0]
    blk = FOX_BLK
    nblk = seq // blk
    steps = nblk // q_per_step
    return pl.pallas_call(
        functools.partial(_fox_kernel, nblk=nblk),
        out_shape=jax.ShapeDtypeStruct((t, FOX_W), BF16),
        grid=(batch, steps),
        in_specs=[
            pl.BlockSpec(memory_space=pltpu.SMEM),
            pl.BlockSpec((q_per_step, 2 * FOX_W, blk), lambda b, i: (b * steps + i, 0, 0)),
            pl.BlockSpec((seq, 2 * FOX_W), lambda b, i: (b, 0)),
            pl.BlockSpec((nblk, FOX_W, blk), lambda b, i: (b, 0, 0)),
        ],
        out_specs=pl.BlockSpec((q_per_step * blk, FOX_W), lambda b, i: (b * steps + i, 0)),
        scratch_shapes=[
            pltpu.VMEM((FOX_HEADS, 1, blk), F32),
            pltpu.VMEM((FOX_HEADS, 1, blk), F32),
            pltpu.VMEM((FOX_HEADS, HEAD_DIM, blk), F32),
            pltpu.VMEM((2, FOX_HEADS, blk, blk), F32),
        ],
        compiler_params=_cparams(("parallel", "arbitrary")),
        name="fox",
    )(anchors, qt, ka, vt)


def _swa_kernel(q_ref, k_ref, v_ref, sink_ref, slope_ref, o_ref):
    qi = pl.program_id(1)
    tq = q_ref.shape[0]
    w = WINDOW
    row = lax.broadcasted_iota(jnp.int32, (w, 2 * w), 0)
    col = lax.broadcasted_iota(jnp.int32, (w, 2 * w), 1)

    for r in range(tq // w):
        n = qi * (tq // w) + r
        kstart = pl.multiple_of(jnp.maximum(n - 1, 0) * w, w)
        dist = (n * w - kstart) + row - col
        valid = (dist >= 0) & (dist < w)
        dist_f = dist.astype(F32)
        for h in range(SWA_HEADS):
            g = h // SWA_GROUP
            q = q_ref[r * w:(r + 1) * w, h * HEAD_DIM:(h + 1) * HEAD_DIM]
            k = k_ref[pl.ds(kstart, 2 * w), g * HEAD_DIM:(g + 1) * HEAD_DIM]
            v = v_ref[pl.ds(kstart, 2 * w), g * HEAD_DIM:(g + 1) * HEAD_DIM]
            s = lax.dot_general(q, k, (((1,), (1,)), ((), ())), preferred_element_type=F32)
            s = s - slope_ref[h] * dist_f
            s = jnp.where(valid, s, NEG_INF)
            sink = sink_ref[h]
            m = jnp.maximum(jnp.max(s, axis=-1, keepdims=True), sink)
            p = jnp.exp(s - m)
            denom = jnp.sum(p, axis=-1, keepdims=True) + jnp.exp(sink - m)
            o = jnp.dot(p.astype(BF16), v, preferred_element_type=F32)
            o_ref[r * w:(r + 1) * w, h * HEAD_DIM:(h + 1) * HEAD_DIM] = (o / denom).astype(o_ref.dtype)


def _swa(rest, sinks, slopes, *, batch, seq, tq=512):
    t = rest.shape[0]
    nq = seq // tq
    smem = pl.BlockSpec(memory_space=pltpu.SMEM)
    return pl.pallas_call(
        _swa_kernel,
        out_shape=jax.ShapeDtypeStruct((t, SWA_Q_W), BF16),
        grid=(batch, nq),
        in_specs=[
            pl.BlockSpec((tq, SWA_Q_W), lambda b, i: (b * nq + i, REST_SQ // SWA_Q_W)),
            pl.BlockSpec((seq, SWA_KV_W), lambda b, i: (b, REST_SK // SWA_KV_W)),
            pl.BlockSpec((seq, SWA_KV_W), lambda b, i: (b, REST_SV // SWA_KV_W)),
            smem, smem,
        ],
        out_specs=pl.BlockSpec((tq, SWA_Q_W), lambda b, i: (b * nq + i, 0)),
        compiler_params=_cparams(("parallel", "arbitrary")),
        name="swa",
    )(rest, rest, rest, sinks, slopes)


def _mem_kv_kernel(mem_ref, g_ref, wk_ref, wv_ref, kg_ref, mk_ref, mv_ref):
    mn = _rms(mem_ref[...], g_ref[...]).astype(BF16)
    k = jnp.dot(mn, wk_ref[...].astype(BF16), preferred_element_type=F32)
    v = jnp.dot(mn, wv_ref[...].astype(BF16), preferred_element_type=F32)
    for h in range(MEM_HEADS):
        hs = slice(h * HEAD_DIM, (h + 1) * HEAD_DIM)
        mk_ref[:, hs] = _rms(k[:, hs], kg_ref[...]).astype(BF16)
    mv_ref[...] = v.astype(BF16)


def _mem_kv(mem2d, gain, wk, wv, kgain, *, batch, mem_len):
    d = mem2d.shape[1]
    full = lambda shape: pl.BlockSpec(shape, lambda b: (0, 0))
    out = jax.ShapeDtypeStruct((batch * mem_len, MEM_W), BF16)
    return pl.pallas_call(
        _mem_kv_kernel,
        out_shape=(out, out),
        grid=(batch,),
        in_specs=[
            pl.BlockSpec((mem_len, d), lambda b: (b, 0)),
            full((1, d)), full((d, MEM_W)), full((d, MEM_W)), full((1, HEAD_DIM)),
        ],
        out_specs=(pl.BlockSpec((mem_len, MEM_W), lambda b: (b, 0)),
                   pl.BlockSpec((mem_len, MEM_W), lambda b: (b, 0))),
        compiler_params=_cparams(("parallel",)),
        name="mem_kv",
    )(mem2d, gain, wk, wv, kgain)


def _mem_attn_kernel(q_ref, mk_ref, mv_ref, o_ref):
    for h in range(MEM_HEADS):
        hs = slice(h * HEAD_DIM, (h + 1) * HEAD_DIM)
        s = lax.dot_general(q_ref[:, hs], mk_ref[:, hs], (((1,), (1,)), ((), ())),
                            preferred_element_type=F32)
        m = jnp.max(s, axis=-1, keepdims=True)
        p = jnp.exp(s - m)
        denom = jnp.sum(p, axis=-1, keepdims=True)
        o = jnp.dot(p.astype(BF16), mv_ref[:, hs], preferred_element_type=F32)
        o_ref[:, hs] = (o / denom).astype(o_ref.dtype)


def _mem_attn(rest, mk, mv, *, batch, seq, mem_len, tq=512):
    t = rest.shape[0]
    nq = seq // tq
    return pl.pallas_call(
        _mem_attn_kernel,
        out_shape=jax.ShapeDtypeStruct((t, MEM_W), BF16),
        grid=(batch, nq),
        in_specs=[
            pl.BlockSpec((tq, MEM_W), lambda b, i: (b * nq + i, REST_MQ // MEM_W)),
            pl.BlockSpec((mem_len, MEM_W), lambda b, i: (b, 0)),
            pl.BlockSpec((mem_len, MEM_W), lambda b, i: (b, 0)),
        ],
        out_specs=pl.BlockSpec((tq, MEM_W), lambda b, i: (b * nq + i, 0)),
        compiler_params=_cparams(("parallel", "arbitrary")),
        name="mem_attn",
    )(rest, mk, mv)


def _out_proj_kernel(x_ref, a_ref, b_ref, c_ref, w_ref, o_ref):
    acc = jnp.dot(a_ref[...], w_ref[0:FOX_W, :], preferred_element_type=F32)
    acc += jnp.dot(b_ref[...], w_ref[FOX_W:FOX_W + SWA_Q_W, :], preferred_element_type=F32)
    acc += jnp.dot(c_ref[...], w_ref[FOX_W + SWA_Q_W:, :], preferred_element_type=F32)
    o_ref[...] = x_ref[...] + acc


def _out_proj(x, a, b, c, w, *, tm=512):
    t, d = x.shape
    return pl.pallas_call(
        _out_proj_kernel,
        out_shape=jax.ShapeDtypeStruct((t, d), F32),
        grid=(t // tm,),
        in_specs=[
            pl.BlockSpec((tm, d), lambda i: (i, 0)),
            pl.BlockSpec((tm, FOX_W), lambda i: (i, 0)),
            pl.BlockSpec((tm, SWA_Q_W), lambda i: (i, 0)),
            pl.BlockSpec((tm, MEM_W), lambda i: (i, 0)),
            pl.BlockSpec(w.shape, lambda i: (0, 0)),
        ],
        out_specs=pl.BlockSpec((tm, d), lambda i: (i, 0)),
        compiler_params=_cparams(("parallel",)),
        name="out_proj",
    )(x, a, b, c, w)


def _alibi_slopes(n):
    return jnp.asarray([2.0 ** (-8.0 * i / n) for i in range(1, n + 1)], dtype=F32)


def _pack_w_in_kernel(w_ref, main_ref, f_ref, *, n_main):
    i = pl.program_id(0)
    n_a = w_ref.shape[2]

    @pl.when(i < n_main)
    def _():
        for a in range(n_a):
            main_ref[:, a * LANES:(a + 1) * LANES] = w_ref[0, :, a, :].astype(BF16)

    @pl.when(i == n_main)
    def _():
        row = lax.broadcasted_iota(jnp.int32, (f_ref.shape[0], LANES), 0)
        for a in range(n_a):
            f_ref[:, a * LANES:(a + 1) * LANES] = jnp.where(
                row < FOX_HEADS, w_ref[0, 0:f_ref.shape[0], a, :], 0.0).astype(BF16)


def _pack_w_in(w_in, l, *, rows=256):
    d, n = w_in.shape[1:]
    fl0 = 3 * FOX_W
    assert n == PROJ_W + FOX_HEADS and fl0 % rows == 0 and PROJ_W % rows == 0 and d % LANES == 0
    wt = jnp.swapaxes(w_in, 1, 2).reshape(w_in.shape[0], n, d // LANES, LANES)
    n_main = PROJ_W // rows

    def start(i):
        main = jnp.where(i * rows < fl0, i * rows, i * rows + FOX_HEADS)
        return jnp.where(i < n_main, main, fl0)

    return pl.pallas_call(
        functools.partial(_pack_w_in_kernel, n_main=n_main),
        out_shape=(jax.ShapeDtypeStruct((PROJ_W, d), BF16), jax.ShapeDtypeStruct((LANES, d), BF16)),
        grid=(n_main + 1,),
        in_specs=[pl.BlockSpec((pl.Element(1), pl.Element(rows), pl.Element(d // LANES), pl.Element(LANES)),
                               lambda i: (l, start(i), 0, 0))],
        out_specs=(pl.BlockSpec((rows, d), lambda i: (jnp.minimum(i, n_main - 1), 0)),
                   pl.BlockSpec((LANES, d), lambda i: (0, 0))),
        compiler_params=_cparams(("arbitrary",)),
        name="pack_w_in",
    )(wt)


def kernel(x, mem, ffn1_norm, ffn1_gate, ffn1_up, ffn1_down, mix_norm, mem_norm, w_in, forget_bias, w_mem_k, w_mem_v, fox_q_gain, fox_k_gain, swa_q_gain, swa_k_gain, swa_sinks, mem_q_gain, mem_k_gain, w_out, ffn2_norm, ffn2_gate, ffn2_up, ffn2_down):
    batch, seq, d = x.shape
    mem_len = mem.shape[1]
    depth = w_in.shape[0]
    scale = HEAD_DIM ** -0.5
    slopes = _alibi_slopes(SWA_HEADS)
    x2 = x.reshape(batch * seq, d)
    mem2 = mem.reshape(batch * mem_len, d)
    zeros = jnp.zeros((HEAD_DIM,), F32)

    for l in range(depth):
        x2, ffn2_w = _ffn(x2, ffn1_norm[l][None], ffn1_gate[l].astype(BF16), ffn1_up[l].astype(BF16),
                          ffn1_down[l].astype(BF16),
                          next_weights=(ffn2_gate[l], ffn2_up[l], ffn2_down[l]))

        w_main, w_f = _pack_w_in(w_in, l)
        head_gain = jnp.stack([fox_q_gain[l] * (scale * LOG2E), fox_k_gain[l], swa_q_gain[l] * scale,
                               swa_k_gain[l], mem_q_gain[l] * scale, zeros, zeros, zeros])
        fbias = jnp.pad(forget_bias[l], (0, LANES - FOX_HEADS))[None]
        qt, ka, vt, rest, anc, w_out_bf = _in_proj(x2, mix_norm[l][None], w_main, w_f, head_gain, fbias,
                                                   w_out[l], seq=seq)

        n_sub = (x2.shape[0] // FOX_BLK) // anc.shape[0]
        anchors = anc[:, :n_sub, :SUBLANES].reshape(-1)
        out_a = _fox(anchors, qt, ka, vt, batch=batch, seq=seq)
        out_b = _swa(rest, swa_sinks[l], slopes, batch=batch, seq=seq)
        mk, mv = _mem_kv(mem2, mem_norm[l][None], w_mem_k[l], w_mem_v[l],
                         mem_k_gain[l][None], batch=batch, mem_len=mem_len)
        out_c = _mem_attn(rest, mk, mv, batch=batch, seq=seq, mem_len=mem_len)

        x2 = _out_proj(x2, out_a, out_b, out_c, w_out_bf)
        x2, _ = _ffn(x2, ffn2_norm[l][None], *ffn2_w)
    return x2.reshape(batch, seq, d)
```

```python
import functools

import jax
import jax.numpy as jnp
from jax import lax
from jax.experimental import pallas as pl
from jax.experimental.pallas import tpu as pltpu

F32 = jnp.float32
BF16 = jnp.bfloat16

HEAD_DIM = 128
FOX_HEADS = 6
SWA_HEADS = 6
SWA_KV_HEADS = 2
SWA_GROUP = SWA_HEADS // SWA_KV_HEADS
MEM_HEADS = 4
WINDOW = 128
EPS = 1e-6
NEG_INF = -1e30
LOG2E = 1.4426950408889634
FOX_BLK = 512

FOX_W = FOX_HEADS * HEAD_DIM
SWA_Q_W = SWA_HEADS * HEAD_DIM
SWA_KV_W = SWA_KV_HEADS * HEAD_DIM
MEM_W = MEM_HEADS * HEAD_DIM

PROJ_W = 3 * FOX_W + SWA_Q_W + 2 * SWA_KV_W + MEM_W
REST_W = SWA_Q_W + 2 * SWA_KV_W + MEM_W
REST_SQ, REST_SK, REST_MQ, REST_SV = 0, SWA_Q_W, SWA_Q_W + SWA_KV_W, SWA_Q_W + SWA_KV_W + MEM_W
GAIN_FQ, GAIN_FK, GAIN_SQ, GAIN_SK, GAIN_MQ = range(5)

LANES = 128
SUBLANES = 8
MXU_N = 256

VMEM_LIMIT = 58 * 1024 * 1024


def _cparams(sem):
    return pltpu.CompilerParams(dimension_semantics=sem, vmem_limit_bytes=VMEM_LIMIT)


def _rms(x, gain):
    ms = jnp.mean(x * x, axis=-1, keepdims=True)
    return x * lax.rsqrt(ms + EPS) * gain


def _ffn_kernel(x_ref, g_ref, wg_ref, wu_ref, wd_ref, *rest, n_chunk, n_cast):
    cast_in, o_ref, cast_out, xn_ref = rest[:n_cast], rest[n_cast], rest[n_cast + 1:-1], rest[-1]
    j = pl.program_id(1)

    @pl.when(j == 0)
    def _():
        x = x_ref[...]
        xn_ref[...] = _rms(x, g_ref[...]).astype(BF16)
        o_ref[...] = x

    xn = xn_ref[...]
    gate = jnp.dot(xn, wg_ref[...], preferred_element_type=F32)
    up = jnp.dot(xn, wu_ref[...], preferred_element_type=F32)
    h = (gate * jax.nn.sigmoid(gate) * (0.5 * up)).astype(BF16)
    d_model = o_ref.shape[1]
    cw = d_model // n_chunk
    for c in range(n_chunk):
        cs = slice(c * cw, (c + 1) * cw)
        o_ref[:, cs] += jnp.dot(h, wd_ref[:, cs], preferred_element_type=F32)

    for src, dst in zip(cast_in, cast_out):
        dst[...] = src[...].astype(BF16)


def _ffn(x, gain, wg, wu, wd, *, next_weights=(), tm=1024, tf=512):
    t, d = x.shape
    dff = wg.shape[1]
    gi, gj = t // tm, dff // tf
    cast_specs, cast_shapes = [], []
    for w in next_weights:
        ri, cj = (gi, gj) if w.shape[1] == dff else (gj, gi)
        blk = (w.shape[0] // ri, w.shape[1] // cj)
        assert blk[0] % SUBLANES == 0 and blk[1] % LANES == 0
        index = (lambda i, j: (i, j)) if w.shape[1] == dff else (lambda i, j: (j, i))
        cast_specs.append(pl.BlockSpec(blk, index))
        cast_shapes.append(jax.ShapeDtypeStruct(w.shape, BF16))
    out = pl.pallas_call(
        functools.partial(_ffn_kernel, n_chunk=4, n_cast=len(next_weights)),
        out_shape=(jax.ShapeDtypeStruct((t, d), F32), *cast_shapes),
        grid=(gi, gj),
        in_specs=[
            pl.BlockSpec((tm, d), lambda i, j: (i, 0)),
            pl.BlockSpec((1, d), lambda i, j: (0, 0)),
            pl.BlockSpec((d, tf), lambda i, j: (0, j)),
            pl.BlockSpec((d, tf), lambda i, j: (0, j)),
            pl.BlockSpec((tf, d), lambda i, j: (j, 0)),
            *cast_specs,
        ],
        out_specs=(pl.BlockSpec((tm, d), lambda i, j: (i, 0)), *cast_specs),
        scratch_shapes=[pltpu.VMEM((tm, d), BF16)],
        compiler_params=_cparams(("parallel", "arbitrary")),
        name="ffn",
    )(x, gain, wg, wu, wd, *next_weights)
    return out[0], out[1:]


def _in_proj_kernel(x_ref, g_ref, w_ref, wf_ref, hg_ref, fb_ref, wo_ref,
                    qt_ref, ka_ref, vt_ref, rest_ref, anc_ref, wo_bf_ref, carry_ref, *, blocks_per_seq):
    i = pl.program_id(0)
    tm = x_ref.shape[0]
    @pl.when(i == 0)
    def _():
        carry_ref[...] = jnp.zeros_like(carry_ref)

    wo_bf_ref[...] = wo_ref[...].astype(BF16)
    hn = _rms(x_ref[...], g_ref[...]).astype(BF16)

    nt = (((1,), (1,)), ((), ()))
    logit = lax.dot_general(hn, wf_ref[...], nt, preferred_element_type=F32)
    c = jax.nn.log_sigmoid(logit + fb_ref[...]) * LOG2E
    row = lax.broadcasted_iota(jnp.int32, c.shape, 0) % FOX_BLK
    sh = 1
    while sh < FOX_BLK:
        c = c + jnp.where(row >= sh, pltpu.roll(c, sh, 0), 0.0)
        sh *= 2

    anchor = jnp.where(i % blocks_per_seq == 0, 0.0, carry_ref[0:1, :])
    anc_ref[...] = jnp.zeros_like(anc_ref)
    for r in range(tm // FOX_BLK):
        anc_ref[0, r:r + 1, :] = anchor
        anchor = anchor + c[(r + 1) * FOX_BLK - 1:(r + 1) * FOX_BLK, :]
    carry_ref[0:1, :] = anchor

    n_sub = tm // FOX_BLK
    lane = lax.broadcasted_iota(jnp.int32, c.shape, 1)
    sub = lax.broadcasted_iota(jnp.int32, (HEAD_DIM, FOX_BLK), 0)
    ct = c.T
    for h in range(FOX_HEADS):
        ch = c[:, h:h + 1]
        hi = ch.astype(BF16).astype(F32)
        lo = ch - hi
        ak = jnp.where(lane < 2, 1.0, jnp.where(lane == 2, -hi, jnp.where(lane == 3, -lo, 0.0)))
        ka_ref[:, (2 * h + 1) * HEAD_DIM:(2 * h + 2) * HEAD_DIM] = ak.astype(BF16)
        for r in range(n_sub):
            cr = ct[h:h + 1, r * FOX_BLK:(r + 1) * FOX_BLK]
            hi_r = cr.astype(BF16).astype(F32)
            lo_r = cr - hi_r
            aq = jnp.where(sub == 0, hi_r, jnp.where(sub == 1, lo_r, jnp.where(sub < 4, 1.0, 0.0)))
            qt_ref[r, (2 * h + 1) * HEAD_DIM:(2 * h + 2) * HEAD_DIM, :] = aq.astype(BF16)

    def normed(ph, row_id):
        return _rms(ph, hg_ref[row_id:row_id + 1, :]).astype(BF16)

    heads_per_chunk = MXU_N // HEAD_DIM
    for ck in range(PROJ_W // MXU_N):
        p = lax.dot_general(hn, w_ref[ck * MXU_N:(ck + 1) * MXU_N, :], nt, preferred_element_type=F32)
        for e in range(heads_per_chunk):
            col = ck * MXU_N + e * HEAD_DIM
            ph = p[:, e * HEAD_DIM:(e + 1) * HEAD_DIM]
            if col < FOX_W:
                h = col // HEAD_DIM
                qn = _rms(ph, hg_ref[GAIN_FQ:GAIN_FQ + 1, :])
                for r in range(n_sub):
                    qt_ref[r, 2 * h * HEAD_DIM:(2 * h + 1) * HEAD_DIM, :] = (
                        qn[r * FOX_BLK:(r + 1) * FOX_BLK, :].T.astype(BF16))
            elif col < 2 * FOX_W:
                h = (col - FOX_W) // HEAD_DIM
                ka_ref[:, 2 * h * HEAD_DIM:(2 * h + 1) * HEAD_DIM] = normed(ph, GAIN_FK)
            elif col < 3 * FOX_W:
                h = (col - 2 * FOX_W) // HEAD_DIM
                for r in range(n_sub):
                    vt_ref[r, h * HEAD_DIM:(h + 1) * HEAD_DIM, :] = (
                        ph[r * FOX_BLK:(r + 1) * FOX_BLK, :].T.astype(BF16))
            else:
                wc = col - 3 * FOX_W
                if wc < SWA_Q_W:
                    val, rc = normed(ph, GAIN_SQ), REST_SQ + wc
                elif wc < SWA_Q_W + SWA_KV_W:
                    val, rc = normed(ph, GAIN_SK), REST_SK + wc - SWA_Q_W
                elif wc < SWA_Q_W + 2 * SWA_KV_W:
                    val, rc = ph.astype(BF16), REST_SV + wc - SWA_Q_W - SWA_KV_W
                else:
                    val, rc = normed(ph, GAIN_MQ), REST_MQ + wc - SWA_Q_W - 2 * SWA_KV_W
                rest_ref[:, rc:rc + HEAD_DIM] = val


def _in_proj(x, gain, w, wf, head_gain, fbias, w_out, *, seq, tm=512):
    t, d = x.shape
    wo_rows = w_out.shape[0] // (t // tm)
    assert wo_rows % (2 * SUBLANES) == 0
    assert seq % tm == 0 and tm % FOX_BLK == 0 and tm // FOX_BLK <= SUBLANES
    const = lambda shape: pl.BlockSpec(shape, lambda i: (0, 0), pipeline_mode=pl.Buffered(1))
    return pl.pallas_call(
        functools.partial(_in_proj_kernel, blocks_per_seq=seq // tm),
        out_shape=(
            jax.ShapeDtypeStruct((t // FOX_BLK, 2 * FOX_W, FOX_BLK), BF16),
            jax.ShapeDtypeStruct((t, 2 * FOX_W), BF16),
            jax.ShapeDtypeStruct((t // FOX_BLK, FOX_W, FOX_BLK), BF16),
            jax.ShapeDtypeStruct((t, REST_W), BF16),
            jax.ShapeDtypeStruct((t // tm, SUBLANES, LANES), F32),
            jax.ShapeDtypeStruct(w_out.shape, BF16),
        ),
        grid=(t // tm,),
        in_specs=[
            pl.BlockSpec((tm, d), lambda i: (i, 0)),
            const((1, d)),
            const((PROJ_W, d)),
            const((LANES, d)),
            const((SUBLANES, HEAD_DIM)),
            const((1, LANES)),
            pl.BlockSpec((wo_rows, w_out.shape[1]), lambda i: (i, 0)),
        ],
        out_specs=(
            pl.BlockSpec((tm // FOX_BLK, 2 * FOX_W, FOX_BLK), lambda i: (i, 0, 0)),
            pl.BlockSpec((tm, 2 * FOX_W), lambda i: (i, 0)),
            pl.BlockSpec((tm // FOX_BLK, FOX_W, FOX_BLK), lambda i: (i, 0, 0)),
            pl.BlockSpec((tm, REST_W), lambda i: (i, 0)),
            pl.BlockSpec((1, SUBLANES, LANES), lambda i: (i, 0, 0)),
            pl.BlockSpec((wo_rows, w_out.shape[1]), lambda i: (i, 0)),
        ),
        scratch_shapes=[pltpu.VMEM((SUBLANES, LANES), F32)],
        compiler_params=_cparams(("arbitrary",)),
        name="in_proj",
    )(x, gain, w, wf, head_gain, fbias, w_out)


def _fox_kernel(anc_ref, q_ref, k_ref, vt_ref, o_ref, m_ref, l_ref, acc_ref, st_ref, *, nblk):
    b = pl.program_id(0)
    q_per_step, _, blk = q_ref.shape
    aw = 2 * HEAD_DIM
    krow = lax.broadcasted_iota(jnp.int32, (blk, blk), 0)
    qcol = lax.broadcasted_iota(jnp.int32, (blk, blk), 1)
    lax.fori_loop(0, q_per_step, functools.partial(
        _fox_query_block, anc_ref, q_ref, k_ref, vt_ref, o_ref, m_ref, l_ref, acc_ref, st_ref,
        b, nblk, krow, qcol), 0)


def _fox_query_block(anc_ref, q_ref, k_ref, vt_ref, o_ref, m_ref, l_ref, acc_ref, st_ref,
                     b, nblk, krow, qcol, sub, carry):
    q_per_step, _, blk = q_ref.shape
    aw = 2 * HEAD_DIM
    i = pl.program_id(1) * q_per_step + sub
    m_ref[...] = jnp.full(m_ref.shape, NEG_INF, F32)
    l_ref[...] = jnp.zeros(l_ref.shape, F32)
    acc_ref[...] = jnp.zeros(acc_ref.shape, F32)

    def scores(j, slot):
        ks = pl.ds(pl.multiple_of(j * blk, blk), blk)
        for h in range(FOX_HEADS):
            st_ref[slot, h] = jnp.dot(k_ref[ks, h * aw:(h + 1) * aw], q_ref[sub, h * aw:(h + 1) * aw, :],
                                      preferred_element_type=F32)

    def softmax_pv(j, slot, masked):
        for h in range(FOX_HEADS):
            st = st_ref[slot, h]
            d = anc_ref[(b * nblk + i) * SUBLANES + h] - anc_ref[(b * nblk + j) * SUBLANES + h]
            if masked:
                st = jnp.where(krow <= qcol, st, NEG_INF)
            m_prev = m_ref[h]
            m_new = jnp.maximum(m_prev, jnp.max(st, axis=0, keepdims=True) + d)
            alpha = jnp.exp2(m_prev - m_new)
            p = jnp.exp2(st - (m_new - d))
            l_ref[h] = alpha * l_ref[h] + jnp.sum(p, axis=0, keepdims=True)
            pv = jnp.dot(vt_ref[j, h * HEAD_DIM:(h + 1) * HEAD_DIM, :], p.astype(BF16),
                         preferred_element_type=F32)
            acc_ref[h] = alpha * acc_ref[h] + pv
            m_ref[h] = m_new

    def body(t, c):
        j = 2 * t
        scores(j + 1, 1)
        softmax_pv(j, 0, False)
        scores(j + 2, 0)
        softmax_pv(j + 1, 1, False)
        return c

    scores(0, 0)
    lax.fori_loop(0, i // 2, body, 0)

    @pl.when(i % 2 == 0)
    def _():
        softmax_pv(i, 0, True)

    @pl.when(i % 2 == 1)
    def _():
        scores(i, 1)
        softmax_pv(i - 1, 0, False)
        softmax_pv(i, 1, True)

    rows = pl.ds(pl.multiple_of(sub * blk, blk), blk)
    for h in range(FOX_HEADS):
        o_ref[rows, h * HEAD_DIM:(h + 1) * HEAD_DIM] = (acc_ref[h] / l_ref[h]).T.astype(o_ref.dtype)
    return carry


def _fox(anchors, qt, ka, vt, *, batch, seq, q_per_step=2):
    t = ka.shape[0]
    blk = FOX_BLK
    nblk = seq // blk
    steps = nblk // q_per_step
    return pl.pallas_call(
        functools.partial(_fox_kernel, nblk=nblk),
        out_shape=jax.ShapeDtypeStruct((t, FOX_W), BF16),
        grid=(batch, steps),
        in_specs=[
            pl.BlockSpec(memory_space=pltpu.SMEM),
            pl.BlockSpec((q_per_step, 2 * FOX_W, blk), lambda b, i: (b * steps + i, 0, 0)),
            pl.BlockSpec((seq, 2 * FOX_W), lambda b, i: (b, 0), pipeline_mode=pl.Buffered(1)),
            pl.BlockSpec((nblk, FOX_W, blk), lambda b, i: (b, 0, 0), pipeline_mode=pl.Buffered(1)),
        ],
        out_specs=pl.BlockSpec((q_per_step * blk, FOX_W), lambda b, i: (b * steps + i, 0)),
        scratch_shapes=[
            pltpu.VMEM((FOX_HEADS, 1, blk), F32),
            pltpu.VMEM((FOX_HEADS, 1, blk), F32),
            pltpu.VMEM((FOX_HEADS, HEAD_DIM, blk), F32),
            pltpu.VMEM((2, FOX_HEADS, blk, blk), F32),
        ],
        compiler_params=_cparams(("parallel", "arbitrary")),
        name="fox",
    )(anchors, qt, ka, vt)


def _swa_kernel(q_ref, k_ref, v_ref, sink_ref, slope_ref, o_ref):
    qi = pl.program_id(1)
    tq = q_ref.shape[0]
    w = WINDOW
    row = lax.broadcasted_iota(jnp.int32, (w, 2 * w), 0)
    col = lax.broadcasted_iota(jnp.int32, (w, 2 * w), 1)

    for r in range(tq // w):
        n = qi * (tq // w) + r
        kstart = pl.multiple_of(jnp.maximum(n - 1, 0) * w, w)
        dist = (n * w - kstart) + row - col
        valid = (dist >= 0) & (dist < w)
        dist_f = dist.astype(F32)
        for h in range(SWA_HEADS):
            g = h // SWA_GROUP
            q = q_ref[r * w:(r + 1) * w, h * HEAD_DIM:(h + 1) * HEAD_DIM]
            k = k_ref[pl.ds(kstart, 2 * w), g * HEAD_DIM:(g + 1) * HEAD_DIM]
            v = v_ref[pl.ds(kstart, 2 * w), g * HEAD_DIM:(g + 1) * HEAD_DIM]
            s = lax.dot_general(q, k, (((1,), (1,)), ((), ())), preferred_element_type=F32)
            s = s - slope_ref[h] * dist_f
            s = jnp.where(valid, s, NEG_INF)
            sink = sink_ref[h]
            m = jnp.maximum(jnp.max(s, axis=-1, keepdims=True), sink)
            p = jnp.exp(s - m)
            denom = jnp.sum(p, axis=-1, keepdims=True) + jnp.exp(sink - m)
            o = jnp.dot(p.astype(BF16), v, preferred_element_type=F32)
            o_ref[r * w:(r + 1) * w, h * HEAD_DIM:(h + 1) * HEAD_DIM] = (o / denom).astype(o_ref.dtype)


def _swa(rest, sinks, slopes, *, batch, seq, tq=512):
    t = rest.shape[0]
    nq = seq // tq
    smem = pl.BlockSpec(memory_space=pltpu.SMEM)
    return pl.pallas_call(
        _swa_kernel,
        out_shape=jax.ShapeDtypeStruct((t, SWA_Q_W), BF16),
        grid=(batch, nq),
        in_specs=[
            pl.BlockSpec((tq, SWA_Q_W), lambda b, i: (b * nq + i, REST_SQ // SWA_Q_W)),
            pl.BlockSpec((seq, SWA_KV_W), lambda b, i: (b, REST_SK // SWA_KV_W)),
            pl.BlockSpec((seq, SWA_KV_W), lambda b, i: (b, REST_SV // SWA_KV_W)),
            smem, smem,
        ],
        out_specs=pl.BlockSpec((tq, SWA_Q_W), lambda b, i: (b * nq + i, 0)),
        compiler_params=_cparams(("parallel", "arbitrary")),
        name="swa",
    )(rest, rest, rest, sinks, slopes)


def _mem_kv_kernel(mem_ref, g_ref, wk_ref, wv_ref, kg_ref, mk_ref, mv_ref):
    mn = _rms(mem_ref[...], g_ref[...]).astype(BF16)
    k = jnp.dot(mn, wk_ref[...].astype(BF16), preferred_element_type=F32)
    v = jnp.dot(mn, wv_ref[...].astype(BF16), preferred_element_type=F32)
    for h in range(MEM_HEADS):
        hs = slice(h * HEAD_DIM, (h + 1) * HEAD_DIM)
        mk_ref[:, hs] = _rms(k[:, hs], kg_ref[...]).astype(BF16)
    mv_ref[...] = v.astype(BF16)


def _mem_kv(mem2d, gain, wk, wv, kgain, *, batch, mem_len):
    d = mem2d.shape[1]
    full = lambda shape: pl.BlockSpec(shape, lambda b: (0, 0))
    out = jax.ShapeDtypeStruct((batch * mem_len, MEM_W), BF16)
    return pl.pallas_call(
        _mem_kv_kernel,
        out_shape=(out, out),
        grid=(batch,),
        in_specs=[
            pl.BlockSpec((mem_len, d), lambda b: (b, 0)),
            full((1, d)), full((d, MEM_W)), full((d, MEM_W)), full((1, HEAD_DIM)),
        ],
        out_specs=(pl.BlockSpec((mem_len, MEM_W), lambda b: (b, 0)),
                   pl.BlockSpec((mem_len, MEM_W), lambda b: (b, 0))),
        compiler_params=_cparams(("parallel",)),
        name="mem_kv",
    )(mem2d, gain, wk, wv, kgain)


def _mem_attn_kernel(q_ref, mk_ref, mv_ref, o_ref):
    for h in range(MEM_HEADS):
        hs = slice(h * HEAD_DIM, (h + 1) * HEAD_DIM)
        s = lax.dot_general(q_ref[:, hs], mk_ref[:, hs], (((1,), (1,)), ((), ())),
                            preferred_element_type=F32)
        m = jnp.max(s, axis=-1, keepdims=True)
        p = jnp.exp(s - m)
        denom = jnp.sum(p, axis=-1, keepdims=True)
        o = jnp.dot(p.astype(BF16), mv_ref[:, hs], preferred_element_type=F32)
        o_ref[:, hs] = (o / denom).astype(o_ref.dtype)


def _mem_attn(rest, mk, mv, *, batch, seq, mem_len, tq=512):
    t = rest.shape[0]
    nq = seq // tq
    return pl.pallas_call(
        _mem_attn_kernel,
        out_shape=jax.ShapeDtypeStruct((t, MEM_W), BF16),
        grid=(batch, nq),
        in_specs=[
            pl.BlockSpec((tq, MEM_W), lambda b, i: (b * nq + i, REST_MQ // MEM_W)),
            pl.BlockSpec((mem_len, MEM_W), lambda b, i: (b, 0)),
            pl.BlockSpec((mem_len, MEM_W), lambda b, i: (b, 0)),
        ],
        out_specs=pl.BlockSpec((tq, MEM_W), lambda b, i: (b * nq + i, 0)),
        compiler_params=_cparams(("parallel", "arbitrary")),
        name="mem_attn",
    )(rest, mk, mv)


def _out_proj_kernel(x_ref, a_ref, b_ref, c_ref, w_ref, o_ref):
    acc = jnp.dot(a_ref[...], w_ref[0:FOX_W, :], preferred_element_type=F32)
    acc += jnp.dot(b_ref[...], w_ref[FOX_W:FOX_W + SWA_Q_W, :], preferred_element_type=F32)
    acc += jnp.dot(c_ref[...], w_ref[FOX_W + SWA_Q_W:, :], preferred_element_type=F32)
    o_ref[...] = x_ref[...] + acc


def _out_proj(x, a, b, c, w, *, tm=512):
    t, d = x.shape
    return pl.pallas_call(
        _out_proj_kernel,
        out_shape=jax.ShapeDtypeStruct((t, d), F32),
        grid=(t // tm,),
        in_specs=[
            pl.BlockSpec((tm, d), lambda i: (i, 0)),
            pl.BlockSpec((tm, FOX_W), lambda i: (i, 0)),
            pl.BlockSpec((tm, SWA_Q_W), lambda i: (i, 0)),
            pl.BlockSpec((tm, MEM_W), lambda i: (i, 0)),
            pl.BlockSpec(w.shape, lambda i: (0, 0)),
        ],
        out_specs=pl.BlockSpec((tm, d), lambda i: (i, 0)),
        compiler_params=_cparams(("parallel",)),
        name="out_proj",
    )(x, a, b, c, w)


def _alibi_slopes(n):
    return jnp.asarray([2.0 ** (-8.0 * i / n) for i in range(1, n + 1)], dtype=F32)


def _pack_w_in_kernel(w_ref, main_ref, f_ref, *, n_main):
    i = pl.program_id(0)
    n_a = w_ref.shape[2]

    @pl.when(i < n_main)
    def _():
        for a in range(n_a):
            main_ref[:, a * LANES:(a + 1) * LANES] = w_ref[0, :, a, :].astype(BF16)

    @pl.when(i == n_main)
    def _():
        row = lax.broadcasted_iota(jnp.int32, (f_ref.shape[0], LANES), 0)
        for a in range(n_a):
            f_ref[:, a * LANES:(a + 1) * LANES] = jnp.where(
                row < FOX_HEADS, w_ref[0, 0:f_ref.shape[0], a, :], 0.0).astype(BF16)


def _pack_w_in(w_in, l, *, rows=256):
    d, n = w_in.shape[1:]
    fl0 = 3 * FOX_W
    assert n == PROJ_W + FOX_HEADS and fl0 % rows == 0 and PROJ_W % rows == 0 and d % LANES == 0
    wt = jnp.swapaxes(w_in, 1, 2).reshape(w_in.shape[0], n, d // LANES, LANES)
    n_main = PROJ_W // rows

    def start(i):
        main = jnp.where(i * rows < fl0, i * rows, i * rows + FOX_HEADS)
        return jnp.where(i < n_main, main, fl0)

    return pl.pallas_call(
        functools.partial(_pack_w_in_kernel, n_main=n_main),
        out_shape=(jax.ShapeDtypeStruct((PROJ_W, d), BF16), jax.ShapeDtypeStruct((LANES, d), BF16)),
        grid=(n_main + 1,),
        in_specs=[pl.BlockSpec((pl.Element(1), pl.Element(rows), pl.Element(d // LANES), pl.Element(LANES)),
                               lambda i: (l, start(i), 0, 0))],
        out_specs=(pl.BlockSpec((rows, d), lambda i: (jnp.minimum(i, n_main - 1), 0)),
                   pl.BlockSpec((LANES, d), lambda i: (0, 0))),
        compiler_params=_cparams(("arbitrary",)),
        name="pack_w_in",
    )(wt)


def kernel(x, mem, ffn1_norm, ffn1_gate, ffn1_up, ffn1_down, mix_norm, mem_norm, w_in, forget_bias, w_mem_k, w_mem_v, fox_q_gain, fox_k_gain, swa_q_gain, swa_k_gain, swa_sinks, mem_q_gain, mem_k_gain, w_out, ffn2_norm, ffn2_gate, ffn2_up, ffn2_down):
    batch, seq, d = x.shape
    mem_len = mem.shape[1]
    depth = w_in.shape[0]
    scale = HEAD_DIM ** -0.5
    slopes = _alibi_slopes(SWA_HEADS)
    x2 = x.reshape(batch * seq, d)
    mem2 = mem.reshape(batch * mem_len, d)
    zeros = jnp.zeros((HEAD_DIM,), F32)

    for l in range(depth):
        x2, ffn2_w = _ffn(x2, ffn1_norm[l][None], ffn1_gate[l].astype(BF16), ffn1_up[l].astype(BF16),
                          ffn1_down[l].astype(BF16),
                          next_weights=(ffn2_gate[l], ffn2_up[l], ffn2_down[l]))

        w_main, w_f = _pack_w_in(w_in, l)
        head_gain = jnp.stack([fox_q_gain[l] * (scale * LOG2E), fox_k_gain[l], swa_q_gain[l] * scale,
                               swa_k_gain[l], mem_q_gain[l] * scale, zeros, zeros, zeros])
        fbias = jnp.pad(forget_bias[l], (0, LANES - FOX_HEADS))[None]
        qt, ka, vt, rest, anc, w_out_bf = _in_proj(x2, mix_norm[l][None], w_main, w_f, head_gain, fbias,
                                                   w_out[l], seq=seq)

        n_sub = (x2.shape[0] // FOX_BLK) // anc.shape[0]
        anchors = anc[:, :n_sub, :SUBLANES].reshape(-1)
        out_a = _fox(anchors, qt, ka, vt, batch=batch, seq=seq)
        out_b = _swa(rest, swa_sinks[l], slopes, batch=batch, seq=seq)
        mk, mv = _mem_kv(mem2, mem_norm[l][None], w_mem_k[l], w_mem_v[l],
                         mem_k_gain[l][None], batch=batch, mem_len=mem_len)
        out_c = _mem_attn(rest, mk, mv, batch=batch, seq=seq, mem_len=mem_len)

        x2 = _out_proj(x2, out_a, out_b, out_c, w_out_bf)
        x2, _ = _ffn(x2, ffn2_norm[l][None], *ffn2_w)
    return x2.reshape(batch, seq, d)
```

```python
import functools

import jax
import jax.numpy as jnp
from jax import lax
from jax.experimental import pallas as pl
from jax.experimental.pallas import tpu as pltpu

F32 = jnp.float32
BF16 = jnp.bfloat16

HEAD_DIM = 128
FOX_HEADS = 6
SWA_HEADS = 6
SWA_KV_HEADS = 2
SWA_GROUP = SWA_HEADS // SWA_KV_HEADS
MEM_HEADS = 4
WINDOW = 128
EPS = 1e-6
NEG_INF = -1e30
LOG2E = 1.4426950408889634
FOX_BLK = 512

FOX_W = FOX_HEADS * HEAD_DIM
SWA_Q_W = SWA_HEADS * HEAD_DIM
SWA_KV_W = SWA_KV_HEADS * HEAD_DIM
MEM_W = MEM_HEADS * HEAD_DIM

PROJ_W = 3 * FOX_W + SWA_Q_W + 2 * SWA_KV_W + MEM_W
REST_W = SWA_Q_W + 2 * SWA_KV_W + MEM_W
REST_SQ, REST_SK, REST_MQ, REST_SV = 0, SWA_Q_W, SWA_Q_W + SWA_KV_W, SWA_Q_W + SWA_KV_W + MEM_W
GAIN_FQ, GAIN_FK, GAIN_SQ, GAIN_SK, GAIN_MQ = range(5)

LANES = 128
SUBLANES = 8
MXU_N = 256

VMEM_LIMIT = 58 * 1024 * 1024


def _cparams(sem):
    return pltpu.CompilerParams(dimension_semantics=sem, vmem_limit_bytes=VMEM_LIMIT)


def _rms(x, gain):
    ms = jnp.mean(x * x, axis=-1, keepdims=True)
    return x * lax.rsqrt(ms + EPS) * gain


def _ffn_kernel(x_ref, g_ref, wg_ref, wu_ref, wd_ref, *rest, n_chunk, n_cast):
    cast_in, o_ref, cast_out, xn_ref = rest[:n_cast], rest[n_cast], rest[n_cast + 1:-1], rest[-1]
    j = pl.program_id(1)

    @pl.when(j == 0)
    def _():
        x = x_ref[...]
        xn_ref[...] = _rms(x, g_ref[...]).astype(BF16)
        o_ref[...] = x

    xn = xn_ref[...]
    gate = jnp.dot(xn, wg_ref[...], preferred_element_type=F32)
    up = jnp.dot(xn, wu_ref[...], preferred_element_type=F32)
    h = (gate * jax.nn.sigmoid(gate) * (0.5 * up)).astype(BF16)
    d_model = o_ref.shape[1]
    cw = d_model // n_chunk
    for c in range(n_chunk):
        cs = slice(c * cw, (c + 1) * cw)
        o_ref[:, cs] += jnp.dot(h, wd_ref[:, cs], preferred_element_type=F32)

    for src, dst in zip(cast_in, cast_out):
        dst[...] = src[...].astype(BF16)


def _ffn(x, gain, wg, wu, wd, *, next_weights=(), tm=1024, tf=512):
    t, d = x.shape
    dff = wg.shape[1]
    gi, gj = t // tm, dff // tf
    cast_specs, cast_shapes = [], []
    for w in next_weights:
        ri, cj = (gi, gj) if w.shape[1] == dff else (gj, gi)
        blk = (w.shape[0] // ri, w.shape[1] // cj)
        assert blk[0] % SUBLANES == 0 and blk[1] % LANES == 0
        index = (lambda i, j: (i, j)) if w.shape[1] == dff else (lambda i, j: (j, i))
        cast_specs.append(pl.BlockSpec(blk, index))
        cast_shapes.append(jax.ShapeDtypeStruct(w.shape, BF16))
    out = pl.pallas_call(
        functools.partial(_ffn_kernel, n_chunk=4, n_cast=len(next_weights)),
        out_shape=(jax.ShapeDtypeStruct((t, d), F32), *cast_shapes),
        grid=(gi, gj),
        in_specs=[
            pl.BlockSpec((tm, d), lambda i, j: (i, 0)),
            pl.BlockSpec((1, d), lambda i, j: (0, 0)),
            pl.BlockSpec((d, tf), lambda i, j: (0, j)),
            pl.BlockSpec((d, tf), lambda i, j: (0, j)),
            pl.BlockSpec((tf, d), lambda i, j: (j, 0)),
            *cast_specs,
        ],
        out_specs=(pl.BlockSpec((tm, d), lambda i, j: (i, 0)), *cast_specs),
        scratch_shapes=[pltpu.VMEM((tm, d), BF16)],
        compiler_params=_cparams(("parallel", "arbitrary")),
        name="ffn",
    )(x, gain, wg, wu, wd, *next_weights)
    return out[0], out[1:]


def _in_proj_kernel(x_ref, g_ref, w_ref, wf_ref, hg_ref, fb_ref, wo_ref,
                    qt_ref, ka_ref, vt_ref, rest_ref, anc_ref, wo_bf_ref, carry_ref, *, blocks_per_seq):
    i = pl.program_id(0)
    tm = x_ref.shape[0]

    @pl.when(i == 0)
    def _():
        carry_ref[...] = jnp.zeros_like(carry_ref)

    wo_bf_ref[...] = wo_ref[...].astype(BF16)
    hn = _rms(x_ref[...], g_ref[...]).astype(BF16)

    nt = (((1,), (1,)), ((), ()))
    logit = lax.dot_general(hn, wf_ref[...], nt, preferred_element_type=F32)
    c = jax.nn.log_sigmoid(logit + fb_ref[...]) * LOG2E
    row = lax.broadcasted_iota(jnp.int32, c.shape, 0) % FOX_BLK
    sh = 1
    while sh < FOX_BLK:
        c = c + jnp.where(row >= sh, pltpu.roll(c, sh, 0), 0.0)
        sh *= 2

    anchor = jnp.where(i % blocks_per_seq == 0, 0.0, carry_ref[0:1, :])
    anc_ref[...] = jnp.zeros_like(anc_ref)
    for r in range(tm // FOX_BLK):
        anc_ref[0, r:r + 1, :] = anchor
        anchor = anchor + c[(r + 1) * FOX_BLK - 1:(r + 1) * FOX_BLK, :]
    carry_ref[0:1, :] = anchor

    n_sub = tm // FOX_BLK
    lane = lax.broadcasted_iota(jnp.int32, c.shape, 1)
    sub = lax.broadcasted_iota(jnp.int32, (HEAD_DIM, FOX_BLK), 0)
    ct = c.T
    for h in range(FOX_HEADS):
        ch = c[:, h:h + 1]
        hi = ch.astype(BF16).astype(F32)
        lo = ch - hi
        ak = jnp.where(lane < 2, 1.0, jnp.where(lane == 2, -hi, jnp.where(lane == 3, -lo, 0.0)))
        ka_ref[:, (2 * h + 1) * HEAD_DIM:(2 * h + 2) * HEAD_DIM] = ak.astype(BF16)
        for r in range(n_sub):
            cr = ct[h:h + 1, r * FOX_BLK:(r + 1) * FOX_BLK]
            hi_r = cr.astype(BF16).astype(F32)
            lo_r = cr - hi_r
            aq = jnp.where(sub == 0, hi_r, jnp.where(sub == 1, lo_r, jnp.where(sub < 4, 1.0, 0.0)))
            qt_ref[r, (2 * h + 1) * HEAD_DIM:(2 * h + 2) * HEAD_DIM, :] = aq.astype(BF16)

    def normed(ph, row_id):
        return _rms(ph, hg_ref[row_id:row_id + 1, :]).astype(BF16)

    heads_per_chunk = MXU_N // HEAD_DIM
    sv_chunk = (3 * FOX_W + SWA_Q_W + SWA_KV_W) // MXU_N
    chunks = [ck for ck in range(PROJ_W // MXU_N) if ck != sv_chunk] + [sv_chunk]
    for ck in chunks:
        p = lax.dot_general(hn, w_ref[ck * MXU_N:(ck + 1) * MXU_N, :], nt, preferred_element_type=F32)
        for e in range(heads_per_chunk):
            col = ck * MXU_N + e * HEAD_DIM
            ph = p[:, e * HEAD_DIM:(e + 1) * HEAD_DIM]
            if col < FOX_W:
                h = col // HEAD_DIM
                qn = _rms(ph, hg_ref[GAIN_FQ:GAIN_FQ + 1, :])
                for r in range(n_sub):
                    qt_ref[r, 2 * h * HEAD_DIM:(2 * h + 1) * HEAD_DIM, :] = (
                        qn[r * FOX_BLK:(r + 1) * FOX_BLK, :].T.astype(BF16))
            elif col < 2 * FOX_W:
                h = (col - FOX_W) // HEAD_DIM
                ka_ref[:, 2 * h * HEAD_DIM:(2 * h + 1) * HEAD_DIM] = normed(ph, GAIN_FK)
            elif col < 3 * FOX_W:
                h = (col - 2 * FOX_W) // HEAD_DIM
                for r in range(n_sub):
                    vt_ref[r, h * HEAD_DIM:(h + 1) * HEAD_DIM, :] = (
                        ph[r * FOX_BLK:(r + 1) * FOX_BLK, :].T.astype(BF16))
            else:
                wc = col - 3 * FOX_W
                if wc < SWA_Q_W:
                    val, rc = normed(ph, GAIN_SQ), REST_SQ + wc
                elif wc < SWA_Q_W + SWA_KV_W:
                    val, rc = normed(ph, GAIN_SK), REST_SK + wc - SWA_Q_W
                elif wc < SWA_Q_W + 2 * SWA_KV_W:
                    val, rc = ph.astype(BF16), REST_SV + wc - SWA_Q_W - SWA_KV_W
                else:
                    val, rc = normed(ph, GAIN_MQ), REST_MQ + wc - SWA_Q_W - 2 * SWA_KV_W
                rest_ref[:, rc:rc + HEAD_DIM] = val


def _in_proj(x, gain, w, wf, head_gain, fbias, w_out, *, seq, tm=512):
    t, d = x.shape
    wo_rows = w_out.shape[0] // (t // tm)
    assert wo_rows % (2 * SUBLANES) == 0
    assert seq % tm == 0 and tm % FOX_BLK == 0 and tm // FOX_BLK <= SUBLANES
    const = lambda shape: pl.BlockSpec(shape, lambda i: (0, 0), pipeline_mode=pl.Buffered(1))
    return pl.pallas_call(
        functools.partial(_in_proj_kernel, blocks_per_seq=seq // tm),
        out_shape=(
            jax.ShapeDtypeStruct((t // FOX_BLK, 2 * FOX_W, FOX_BLK), BF16),
            jax.ShapeDtypeStruct((t, 2 * FOX_W), BF16),
            jax.ShapeDtypeStruct((t // FOX_BLK, FOX_W, FOX_BLK), BF16),
            jax.ShapeDtypeStruct((t, REST_W), BF16),
            jax.ShapeDtypeStruct((t // tm, SUBLANES, LANES), F32),
            jax.ShapeDtypeStruct(w_out.shape, BF16),
        ),
        grid=(t // tm,),
        in_specs=[
            pl.BlockSpec((tm, d), lambda i: (i, 0)),
            const((1, d)),
            const((PROJ_W, d)),
            const((LANES, d)),
            const((SUBLANES, HEAD_DIM)),
            const((1, LANES)),
            pl.BlockSpec((wo_rows, w_out.shape[1]), lambda i: (i, 0)),
        ],
        out_specs=(
            pl.BlockSpec((tm // FOX_BLK, 2 * FOX_W, FOX_BLK), lambda i: (i, 0, 0)),
            pl.BlockSpec((tm, 2 * FOX_W), lambda i: (i, 0)),
            pl.BlockSpec((tm // FOX_BLK, FOX_W, FOX_BLK), lambda i: (i, 0, 0)),
            pl.BlockSpec((tm, REST_W), lambda i: (i, 0)),
            pl.BlockSpec((1, SUBLANES, LANES), lambda i: (i, 0, 0)),
            pl.BlockSpec((wo_rows, w_out.shape[1]), lambda i: (i, 0)),
        ),
        scratch_shapes=[pltpu.VMEM((SUBLANES, LANES), F32)],
        compiler_params=_cparams(("arbitrary",)),
        name="in_proj",
    )(x, gain, w, wf, head_gain, fbias, w_out)


def _fox_kernel(anc_ref, q_ref, k_ref, vt_ref, o_ref, m_ref, l_ref, acc_ref, st_ref, *, nblk):
    b = pl.program_id(0)
    q_per_step, _, blk = q_ref.shape
    krow = lax.broadcasted_iota(jnp.int32, (blk, blk), 0)
    qcol = lax.broadcasted_iota(jnp.int32, (blk, blk), 1)
    lax.fori_loop(0, q_per_step, functools.partial(
        _fox_query_block, anc_ref, q_ref, k_ref, vt_ref, o_ref, m_ref, l_ref, acc_ref, st_ref,
        b, nblk, krow, qcol), 0)


def _fox_query_block(anc_ref, q_ref, k_ref, vt_ref, o_ref, m_ref, l_ref, acc_ref, st_ref,
                     b, nblk, krow, qcol, sub, carry):
    q_per_step, _, blk = q_ref.shape
    aw = 2 * HEAD_DIM
    i = pl.program_id(1) * q_per_step + sub
    m_ref[...] = jnp.full(m_ref.shape, NEG_INF, F32)
    l_ref[...] = jnp.zeros(l_ref.shape, F32)
    acc_ref[...] = jnp.zeros(acc_ref.shape, F32)

    def scores(j, slot):
        ks = pl.ds(pl.multiple_of(j * blk, blk), blk)
        for h in range(FOX_HEADS):
            st_ref[slot, h] = jnp.dot(k_ref[ks, h * aw:(h + 1) * aw], q_ref[sub, h * aw:(h + 1) * aw, :],
                                      preferred_element_type=F32)

    def softmax_pv(j, slot, masked):
        for h in range(FOX_HEADS):
            st = st_ref[slot, h]
            d = anc_ref[(b * nblk + i) * SUBLANES + h] - anc_ref[(b * nblk + j) * SUBLANES + h]
            if masked:
                st = jnp.where(krow <= qcol, st, NEG_INF)
            m_prev = m_ref[h]
            m_new = jnp.maximum(m_prev, jnp.max(st, axis=0, keepdims=True) + d)
            alpha = jnp.exp2(m_prev - m_new)
            p = jnp.exp2(st - (m_new - d))
            l_ref[h] = alpha * l_ref[h] + jnp.sum(p, axis=0, keepdims=True)
            pv = jnp.dot(vt_ref[j, h * HEAD_DIM:(h + 1) * HEAD_DIM, :], p.astype(BF16),
                         preferred_element_type=F32)
            acc_ref[h] = alpha * acc_ref[h] + pv
            m_ref[h] = m_new

    def body(t, c):
        j = 2 * t
        scores(j + 1, 1)
        softmax_pv(j, 0, False)
        scores(j + 2, 0)
        softmax_pv(j + 1, 1, False)
        return c

    scores(0, 0)
    lax.fori_loop(0, i // 2, body, 0)

    @pl.when(i % 2 == 0)
    def _():
        softmax_pv(i, 0, True)

    @pl.when(i % 2 == 1)
    def _():
        scores(i, 1)
        softmax_pv(i - 1, 0, False)
        softmax_pv(i, 1, True)

    rows = pl.ds(pl.multiple_of(sub * blk, blk), blk)
    for h in range(FOX_HEADS):
        o_ref[rows, h * HEAD_DIM:(h + 1) * HEAD_DIM] = (acc_ref[h] / l_ref[h]).T.astype(o_ref.dtype)
    return carry


def _fox(anchors, qt, ka, vt, *, batch, seq, q_per_step=2):
    t = ka.shape[0]
    blk = FOX_BLK
    nblk = seq // blk
    steps = nblk // q_per_step
    return pl.pallas_call(
        functools.partial(_fox_kernel, nblk=nblk),
        out_shape=jax.ShapeDtypeStruct((t, FOX_W), BF16),
        grid=(batch, steps),
        in_specs=[
            pl.BlockSpec(memory_space=pltpu.SMEM),
            pl.BlockSpec((q_per_step, 2 * FOX_W, blk), lambda b, i: (b * steps + i, 0, 0)),
            pl.BlockSpec((seq, 2 * FOX_W), lambda b, i: (b, 0), pipeline_mode=pl.Buffered(1)),
            pl.BlockSpec((nblk, FOX_W, blk), lambda b, i: (b, 0, 0), pipeline_mode=pl.Buffered(1)),
        ],
        out_specs=pl.BlockSpec((q_per_step * blk, FOX_W), lambda b, i: (b * steps + i, 0)),
        scratch_shapes=[
            pltpu.VMEM((FOX_HEADS, 1, blk), F32),
            pltpu.VMEM((FOX_HEADS, 1, blk), F32),
            pltpu.VMEM((FOX_HEADS, HEAD_DIM, blk), F32),
            pltpu.VMEM((2, FOX_HEADS, blk, blk), F32),
        ],
        compiler_params=_cparams(("parallel", "arbitrary")),
        name="fox",
    )(anchors, qt, ka, vt)


def _swa_kernel(q_ref, k_ref, v_ref, sink_ref, slope_ref, o_ref):
    qi = pl.program_id(1)
    tq = q_ref.shape[0]
    w = WINDOW
    n_sub = tq // w
    row = lax.broadcasted_iota(jnp.int32, (w, 2 * w), 0)
    col = lax.broadcasted_iota(jnp.int32, (w, 2 * w), 1)

    def bias_tables(offset):
        dist = offset + row - col
        valid = (dist >= 0) & (dist < w)
        dist_f = dist.astype(F32)
        return [jnp.where(valid, -(slope_ref[h] * LOG2E) * dist_f, NEG_INF) for h in range(SWA_HEADS)]

    later_bias = bias_tables(w)
    for r in range(n_sub):
        n = qi * n_sub + r
        kstart = pl.multiple_of(jnp.maximum(n - 1, 0) * w, w)
        bias = bias_tables(n * w - kstart) if r == 0 else later_bias
        for h in range(SWA_HEADS):
            g = h // SWA_GROUP
            q = q_ref[r * w:(r + 1) * w, h * HEAD_DIM:(h + 1) * HEAD_DIM]
            k = k_ref[pl.ds(kstart, 2 * w), g * HEAD_DIM:(g + 1) * HEAD_DIM]
            v = v_ref[pl.ds(kstart, 2 * w), g * HEAD_DIM:(g + 1) * HEAD_DIM]
            s = lax.dot_general(q, k, (((1,), (1,)), ((), ())), preferred_element_type=F32) + bias[h]
            sink = sink_ref[h] * LOG2E
            m = jnp.maximum(jnp.max(s, axis=-1, keepdims=True), sink)
            p = jnp.exp2(s - m)
            denom = jnp.sum(p, axis=-1, keepdims=True) + jnp.exp2(sink - m)
            o = jnp.dot(p.astype(BF16), v, preferred_element_type=F32)
            o_ref[r * w:(r + 1) * w, h * HEAD_DIM:(h + 1) * HEAD_DIM] = (o / denom).astype(o_ref.dtype)


def _swa(rest, sinks, slopes, *, batch, seq, tq=512):
    t = rest.shape[0]
    nq = seq // tq
    smem = pl.BlockSpec(memory_space=pltpu.SMEM)
    return pl.pallas_call(
        _swa_kernel,
        out_shape=jax.ShapeDtypeStruct((t, SWA_Q_W), BF16),
        grid=(batch, nq),
        in_specs=[
            pl.BlockSpec((tq, SWA_Q_W), lambda b, i: (b * nq + i, REST_SQ // SWA_Q_W)),
            pl.BlockSpec((seq, SWA_KV_W), lambda b, i: (b, REST_SK // SWA_KV_W)),
            pl.BlockSpec((seq, SWA_KV_W), lambda b, i: (b, REST_SV // SWA_KV_W)),
            smem, smem,
        ],
        out_specs=pl.BlockSpec((tq, SWA_Q_W), lambda b, i: (b * nq + i, 0)),
        compiler_params=_cparams(("parallel", "arbitrary")),
        name="swa",
    )(rest, rest, rest, sinks, slopes)


def _mem_kv_kernel(mem_ref, g_ref, wk_ref, wv_ref, kg_ref, mk_ref, mv_ref):
    mn = _rms(mem_ref[...], g_ref[...]).astype(BF16)
    k = jnp.dot(mn, wk_ref[...].astype(BF16), preferred_element_type=F32)
    v = jnp.dot(mn, wv_ref[...].astype(BF16), preferred_element_type=F32)
    for h in range(MEM_HEADS):
        hs = slice(h * HEAD_DIM, (h + 1) * HEAD_DIM)
        mk_ref[:, hs] = _rms(k[:, hs], kg_ref[...]).astype(BF16)
    mv_ref[...] = v.astype(BF16)


def _mem_kv(mem2d, gain, wk, wv, kgain, *, batch, mem_len):
    d = mem2d.shape[1]
    full = lambda shape: pl.BlockSpec(shape, lambda b: (0, 0))
    out = jax.ShapeDtypeStruct((batch * mem_len, MEM_W), BF16)
    return pl.pallas_call(
        _mem_kv_kernel,
        out_shape=(out, out),
        grid=(batch,),
        in_specs=[
            pl.BlockSpec((mem_len, d), lambda b: (b, 0)),
            full((1, d)), full((d, MEM_W)), full((d, MEM_W)), full((1, HEAD_DIM)),
        ],
        out_specs=(pl.BlockSpec((mem_len, MEM_W), lambda b: (b, 0)),
                   pl.BlockSpec((mem_len, MEM_W), lambda b: (b, 0))),
        compiler_params=_cparams(("parallel",)),
        name="mem_kv",
    )(mem2d, gain, wk, wv, kgain)


def _mem_attn_kernel(q_ref, mk_ref, mv_ref, o_ref):
    for h in range(MEM_HEADS):
        hs = slice(h * HEAD_DIM, (h + 1) * HEAD_DIM)
        s = lax.dot_general(q_ref[:, hs], mk_ref[:, hs], (((1,), (1,)), ((), ())),
                            preferred_element_type=F32)
        m = jnp.max(s, axis=-1, keepdims=True)
        p = jnp.exp2(s - m)
        denom = jnp.sum(p, axis=-1, keepdims=True)
        o = jnp.dot(p.astype(BF16), mv_ref[:, hs], preferred_element_type=F32)
        o_ref[:, hs] = (o / denom).astype(o_ref.dtype)


def _mem_attn(rest, mk, mv, *, batch, seq, mem_len, tq=512):
    t = rest.shape[0]
    nq = seq // tq
    return pl.pallas_call(
        _mem_attn_kernel,
        out_shape=jax.ShapeDtypeStruct((t, MEM_W), BF16),
        grid=(batch, nq),
        in_specs=[
            pl.BlockSpec((tq, MEM_W), lambda b, i: (b * nq + i, REST_MQ // MEM_W)),
            pl.BlockSpec((mem_len, MEM_W), lambda b, i: (b, 0)),
            pl.BlockSpec((mem_len, MEM_W), lambda b, i: (b, 0)),
        ],
        out_specs=pl.BlockSpec((tq, MEM_W), lambda b, i: (b * nq + i, 0)),
        compiler_params=_cparams(("parallel", "arbitrary")),
        name="mem_attn",
    )(rest, mk, mv)


def _out_proj_kernel(x_ref, a_ref, b_ref, c_ref, w_ref, o_ref):
    acc = jnp.dot(a_ref[...], w_ref[0:FOX_W, :], preferred_element_type=F32)
    acc += jnp.dot(b_ref[...], w_ref[FOX_W:FOX_W + SWA_Q_W, :], preferred_element_type=F32)
    acc += jnp.dot(c_ref[...], w_ref[FOX_W + SWA_Q_W:, :], preferred_element_type=F32)
    o_ref[...] = x_ref[...] + acc


def _out_proj(x, a, b, c, w, *, tm=1024):
    t, d = x.shape
    return pl.pallas_call(
        _out_proj_kernel,
        out_shape=jax.ShapeDtypeStruct((t, d), F32),
        grid=(t // tm,),
        in_specs=[
            pl.BlockSpec((tm, d), lambda i: (i, 0)),
            pl.BlockSpec((tm, FOX_W), lambda i: (i, 0)),
            pl.BlockSpec((tm, SWA_Q_W), lambda i: (i, 0)),
            pl.BlockSpec((tm, MEM_W), lambda i: (i, 0)),
            pl.BlockSpec(w.shape, lambda i: (0, 0), pipeline_mode=pl.Buffered(1)),
        ],
        out_specs=pl.BlockSpec((tm, d), lambda i: (i, 0)),
        compiler_params=_cparams(("parallel",)),
        name="out_proj",
    )(x, a, b, c, w)


def _alibi_slopes(n):
    return jnp.asarray([2.0 ** (-8.0 * i / n) for i in range(1, n + 1)], dtype=F32)


def _pack_w_in_kernel(w_ref, main_ref, f_ref, *, n_main):
    i = pl.program_id(0)
    n_a = w_ref.shape[2]

    @pl.when(i < n_main)
    def _():
        for a in range(n_a):
            main_ref[:, a * LANES:(a + 1) * LANES] = w_ref[0, :, a, :].astype(BF16)

    @pl.when(i == n_main)
    def _():
        row = lax.broadcasted_iota(jnp.int32, (f_ref.shape[0], LANES), 0)
        for a in range(n_a):
            f_ref[:, a * LANES:(a + 1) * LANES] = jnp.where(
                row < FOX_HEADS, w_ref[0, 0:f_ref.shape[0], a, :], 0.0).astype(BF16)


def _pack_w_in(w_in, l, *, rows=256):
    d, n = w_in.shape[1:]
    fl0 = 3 * FOX_W
    assert n == PROJ_W + FOX_HEADS and fl0 % rows == 0 and PROJ_W % rows == 0 and d % LANES == 0
    wt = jnp.swapaxes(w_in, 1, 2).reshape(w_in.shape[0], n, d // LANES, LANES)
    n_main = PROJ_W // rows

    def start(i):
        main = jnp.where(i * rows < fl0, i * rows, i * rows + FOX_HEADS)
        return jnp.where(i < n_main, main, fl0)

    return pl.pallas_call(
        functools.partial(_pack_w_in_kernel, n_main=n_main),
        out_shape=(jax.ShapeDtypeStruct((PROJ_W, d), BF16), jax.ShapeDtypeStruct((LANES, d), BF16)),
        grid=(n_main + 1,),
        in_specs=[pl.BlockSpec((pl.Element(1), pl.Element(rows), pl.Element(d // LANES), pl.Element(LANES)),
                               lambda i: (l, start(i), 0, 0))],
        out_specs=(pl.BlockSpec((rows, d), lambda i: (jnp.minimum(i, n_main - 1), 0)),
                   pl.BlockSpec((LANES, d), lambda i: (0, 0))),
        compiler_params=_cparams(("arbitrary",)),
        name="pack_w_in",
    )(wt)


def kernel(x, mem, ffn1_norm, ffn1_gate, ffn1_up, ffn1_down, mix_norm, mem_norm, w_in, forget_bias, w_mem_k, w_mem_v, fox_q_gain, fox_k_gain, swa_q_gain, swa_k_gain, swa_sinks, mem_q_gain, mem_k_gain, w_out, ffn2_norm, ffn2_gate, ffn2_up, ffn2_down):
    batch, seq, d = x.shape
    mem_len = mem.shape[1]
    depth = w_in.shape[0]
    scale = HEAD_DIM ** -0.5
    slopes = _alibi_slopes(SWA_HEADS)
    x2 = x.reshape(batch * seq, d)
    mem2 = mem.reshape(batch * mem_len, d)
    zeros = jnp.zeros((HEAD_DIM,), F32)

    for l in range(depth):
        x2, ffn2_w = _ffn(x2, ffn1_norm[l][None], ffn1_gate[l].astype(BF16), ffn1_up[l].astype(BF16),
                          ffn1_down[l].astype(BF16),
                          next_weights=(ffn2_gate[l], ffn2_up[l], ffn2_down[l]))

        w_main, w_f = _pack_w_in(w_in, l)
        qs = scale * LOG2E
        head_gain = jnp.stack([fox_q_gain[l] * qs, fox_k_gain[l], swa_q_gain[l] * qs,
                               swa_k_gain[l], mem_q_gain[l] * qs, zeros, zeros, zeros])
        fbias = jnp.pad(forget_bias[l], (0, LANES - FOX_HEADS))[None]
        qt, ka, vt, rest, anc, w_out_bf = _in_proj(x2, mix_norm[l][None], w_main, w_f, head_gain, fbias,
                                                   w_out[l], seq=seq)

        n_sub = (x2.shape[0] // FOX_BLK) // anc.shape[0]
        anchors = anc[:, :n_sub, :SUBLANES].reshape(-1)
        out_a = _fox(anchors, qt, ka, vt, batch=batch, seq=seq)
        out_b = _swa(rest, swa_sinks[l], slopes, batch=batch, seq=seq)
        mk, mv = _mem_kv(mem2, mem_norm[l][None], w_mem_k[l], w_mem_v[l],
                         mem_k_gain[l][None], batch=batch, mem_len=mem_len)
        out_c = _mem_attn(rest, mk, mv, batch=batch, seq=seq, mem_len=mem_len)

        x2 = _out_proj(x2, out_a, out_b, out_c, w_out_bf)
        x2, _ = _ffn(x2, ffn2_norm[l][None], *ffn2_w)
    return x2.reshape(batch, seq, d)
```

```python
import functools

import jax
import jax.numpy as jnp
from jax import lax
from jax.experimental import pallas as pl
from jax.experimental.pallas import tpu as pltpu

F32 = jnp.float32
BF16 = jnp.bfloat16

HEAD_DIM = 128
FOX_HEADS = 6
SWA_HEADS = 6
SWA_KV_HEADS = 2
SWA_GROUP = SWA_HEADS // SWA_KV_HEADS
MEM_HEADS = 4
WINDOW = 128
EPS = 1e-6
NEG_INF = -1e30
LOG2E = 1.4426950408889634
FOX_BLK = 512

FOX_W = FOX_HEADS * HEAD_DIM
SWA_Q_W = SWA_HEADS * HEAD_DIM
SWA_KV_W = SWA_KV_HEADS * HEAD_DIM
MEM_W = MEM_HEADS * HEAD_DIM

PROJ_W = 3 * FOX_W + SWA_Q_W + 2 * SWA_KV_W + MEM_W
REST_W = SWA_Q_W + 2 * SWA_KV_W + MEM_W
REST_SQ, REST_SK, REST_MQ, REST_SV = 0, SWA_Q_W, SWA_Q_W + SWA_KV_W, SWA_Q_W + SWA_KV_W + MEM_W
GAIN_FQ, GAIN_FK, GAIN_SQ, GAIN_SK, GAIN_MQ = range(5)

LANES = 128
SUBLANES = 8
MXU_N = 256

VMEM_LIMIT = 58 * 1024 * 1024


def _cparams(sem):
    return pltpu.CompilerParams(dimension_semantics=sem, vmem_limit_bytes=VMEM_LIMIT)


def _rms(x, gain):
    ms = jnp.mean(x * x, axis=-1, keepdims=True)
    return x * lax.rsqrt(ms + EPS) * gain


def _ffn_kernel(x_ref, g_ref, wg_ref, wu_ref, wd_ref, *rest, n_cast):
    cast_in, o_ref, cast_out, xn_ref = rest[:n_cast], rest[n_cast], rest[n_cast + 1:-1], rest[-1]
    j = pl.program_id(1)

    @pl.when(j == 0)
    def _():
        x = x_ref[...]
        xn_ref[...] = _rms(x, g_ref[...]).astype(BF16)
        o_ref[...] = x

    xn = xn_ref[...]
    gate = jnp.dot(xn, wg_ref[...], preferred_element_type=F32)
    up = jnp.dot(xn, wu_ref[...], preferred_element_type=F32)
    h = (gate * jax.nn.sigmoid(gate) * (0.5 * up)).astype(BF16)
    o_ref[...] += jnp.dot(h, wd_ref[...], preferred_element_type=F32)

    for src, dst in zip(cast_in, cast_out):
        dst[...] = src[...].astype(BF16)


def _ffn(x, gain, wg, wu, wd, *, next_weights=(), tm=1024, tf=512):
    t, d = x.shape
    dff = wg.shape[1]
    gi, gj = t // tm, dff // tf
    cast_specs, cast_shapes = [], []
    for w in next_weights:
        ri, cj = (gi, gj) if w.shape[1] == dff else (gj, gi)
        blk = (w.shape[0] // ri, w.shape[1] // cj)
        assert blk[0] % SUBLANES == 0 and blk[1] % LANES == 0
        index = (lambda i, j: (i, j)) if w.shape[1] == dff else (lambda i, j: (j, i))
        cast_specs.append(pl.BlockSpec(blk, index))
        cast_shapes.append(jax.ShapeDtypeStruct(w.shape, BF16))
    out = pl.pallas_call(
        functools.partial(_ffn_kernel, n_cast=len(next_weights)),
        out_shape=(jax.ShapeDtypeStruct((t, d), F32), *cast_shapes),
        grid=(gi, gj),
        in_specs=[
            pl.BlockSpec((tm, d), lambda i, j: (i, 0)),
            pl.BlockSpec((1, d), lambda i, j: (0, 0)),
            pl.BlockSpec((d, tf), lambda i, j: (0, j)),
            pl.BlockSpec((d, tf), lambda i, j: (0, j)),
            pl.BlockSpec((tf, d), lambda i, j: (j, 0)),
            *cast_specs,
        ],
        out_specs=(pl.BlockSpec((tm, d), lambda i, j: (i, 0)), *cast_specs),
        scratch_shapes=[pltpu.VMEM((tm, d), BF16)],
        compiler_params=_cparams(("parallel", "arbitrary")),
        name="ffn",
    )(x, gain, wg, wu, wd, *next_weights)
    return out[0], out[1:]


def _in_proj_kernel(x_ref, g_ref, w_ref, wf_ref, hg_ref, fb_ref, wo_ref,
                    qt_ref, ka_ref, vt_ref, rest_ref, anc_ref, wo_bf_ref, carry_ref, *, blocks_per_seq):
    i = pl.program_id(0)
    tm = x_ref.shape[0]

    @pl.when(i == 0)
    def _():
        carry_ref[...] = jnp.zeros_like(carry_ref)

    wo_bf_ref[...] = wo_ref[...].astype(BF16)
    hn = _rms(x_ref[...], g_ref[...]).astype(BF16)

    nt = (((1,), (1,)), ((), ()))
    logit = lax.dot_general(hn, wf_ref[...], nt, preferred_element_type=F32)
    c = jax.nn.log_sigmoid(logit + fb_ref[...]) * LOG2E
    row = lax.broadcasted_iota(jnp.int32, c.shape, 0) % FOX_BLK
    sh = 1
    while sh < FOX_BLK:
        c = c + jnp.where(row >= sh, pltpu.roll(c, sh, 0), 0.0)
        sh *= 2

    anchor = jnp.where(i % blocks_per_seq == 0, 0.0, carry_ref[0:1, :])
    anc_ref[...] = jnp.zeros_like(anc_ref)
    for r in range(tm // FOX_BLK):
        anc_ref[0, r:r + 1, :] = anchor
        anchor = anchor + c[(r + 1) * FOX_BLK - 1:(r + 1) * FOX_BLK, :]
    carry_ref[0:1, :] = anchor

    n_sub = tm // FOX_BLK
    lane = lax.broadcasted_iota(jnp.int32, c.shape, 1)
    sub = lax.broadcasted_iota(jnp.int32, (HEAD_DIM, FOX_BLK), 0)
    ct = c.T
    ak = jnp.zeros(c.shape, F32)
    for h in range(FOX_HEADS):
        ch = c[:, h:h + 1]
        hi = ch.astype(BF16).astype(F32)
        lo = ch - hi
        own = lane - 4 * h
        ak = jnp.where((own == 0) | (own == 1), 1.0, jnp.where(own == 2, -hi, jnp.where(own == 3, -lo, ak)))
        for r in range(n_sub):
            cr = ct[h:h + 1, r * FOX_BLK:(r + 1) * FOX_BLK]
            hi_r = cr.astype(BF16).astype(F32)
            lo_r = cr - hi_r
            own_r = sub - 4 * h
            aq = jnp.where(own_r == 0, hi_r, jnp.where(own_r == 1, lo_r,
                                                       jnp.where((own_r == 2) | (own_r == 3), 1.0, 0.0)))
            qt_ref[r, (2 * h + 1) * HEAD_DIM:(2 * h + 2) * HEAD_DIM, :] = aq.astype(BF16)
    ka_ref[:, FOX_W:] = ak.astype(BF16)

    def normed(ph, row_id):
        return _rms(ph, hg_ref[row_id:row_id + 1, :]).astype(BF16)

    heads_per_chunk = MXU_N // HEAD_DIM
    sv_chunk = (3 * FOX_W + SWA_Q_W + SWA_KV_W) // MXU_N
    chunks = [ck for ck in range(PROJ_W // MXU_N) if ck != sv_chunk] + [sv_chunk]
    for ck in chunks:
        p = lax.dot_general(hn, w_ref[ck * MXU_N:(ck + 1) * MXU_N, :], nt, preferred_element_type=F32)
        for e in range(heads_per_chunk):
            col = ck * MXU_N + e * HEAD_DIM
            ph = p[:, e * HEAD_DIM:(e + 1) * HEAD_DIM]
            if col < FOX_W:
                h = col // HEAD_DIM
                qn = _rms(ph, hg_ref[GAIN_FQ:GAIN_FQ + 1, :])
                for r in range(n_sub):
                    qt_ref[r, 2 * h * HEAD_DIM:(2 * h + 1) * HEAD_DIM, :] = (
                        qn[r * FOX_BLK:(r + 1) * FOX_BLK, :].T.astype(BF16))
            elif col < 2 * FOX_W:
                h = (col - FOX_W) // HEAD_DIM
                ka_ref[:, h * HEAD_DIM:(h + 1) * HEAD_DIM] = normed(ph, GAIN_FK)
            elif col < 3 * FOX_W:
                h = (col - 2 * FOX_W) // HEAD_DIM
                for r in range(n_sub):
                    vt_ref[r, h * HEAD_DIM:(h + 1) * HEAD_DIM, :] = (
                        ph[r * FOX_BLK:(r + 1) * FOX_BLK, :].T.astype(BF16))
            else:
                wc = col - 3 * FOX_W
                if wc < SWA_Q_W:
                    val, rc = normed(ph, GAIN_SQ), REST_SQ + wc
                elif wc < SWA_Q_W + SWA_KV_W:
                    val, rc = normed(ph, GAIN_SK), REST_SK + wc - SWA_Q_W
                elif wc < SWA_Q_W + 2 * SWA_KV_W:
                    val, rc = ph.astype(BF16), REST_SV + wc - SWA_Q_W - SWA_KV_W
                else:
                    val, rc = normed(ph, GAIN_MQ), REST_MQ + wc - SWA_Q_W - 2 * SWA_KV_W
                rest_ref[:, rc:rc + HEAD_DIM] = val


def _in_proj(x, gain, w, wf, head_gain, fbias, w_out, *, seq, tm=512):
    t, d = x.shape
    wo_rows = w_out.shape[0] // (t // tm)
    assert wo_rows % (2 * SUBLANES) == 0
    assert seq % tm == 0 and tm % FOX_BLK == 0 and tm // FOX_BLK <= SUBLANES
    const = lambda shape: pl.BlockSpec(shape, lambda i: (0, 0), pipeline_mode=pl.Buffered(1))
    return pl.pallas_call(
        functools.partial(_in_proj_kernel, blocks_per_seq=seq // tm),
        out_shape=(
            jax.ShapeDtypeStruct((t // FOX_BLK, 2 * FOX_W, FOX_BLK), BF16),
            jax.ShapeDtypeStruct((t, FOX_W + HEAD_DIM), BF16),
            jax.ShapeDtypeStruct((t // FOX_BLK, FOX_W, FOX_BLK), BF16),
            jax.ShapeDtypeStruct((t, REST_W), BF16),
            jax.ShapeDtypeStruct((t // tm, SUBLANES, LANES), F32),
            jax.ShapeDtypeStruct(w_out.shape, BF16),
        ),
        grid=(t // tm,),
        in_specs=[
            pl.BlockSpec((tm, d), lambda i: (i, 0)),
            const((1, d)),
            const((PROJ_W, d)),
            const((LANES, d)),
            const((SUBLANES, HEAD_DIM)),
            const((1, LANES)),
            pl.BlockSpec((wo_rows, w_out.shape[1]), lambda i: (i, 0)),
        ],
        out_specs=(
            pl.BlockSpec((tm // FOX_BLK, 2 * FOX_W, FOX_BLK), lambda i: (i, 0, 0)),
            pl.BlockSpec((tm, FOX_W + HEAD_DIM), lambda i: (i, 0)),
            pl.BlockSpec((tm // FOX_BLK, FOX_W, FOX_BLK), lambda i: (i, 0, 0)),
            pl.BlockSpec((tm, REST_W), lambda i: (i, 0)),
            pl.BlockSpec((1, SUBLANES, LANES), lambda i: (i, 0, 0)),
            pl.BlockSpec((wo_rows, w_out.shape[1]), lambda i: (i, 0)),
        ),
        scratch_shapes=[pltpu.VMEM((SUBLANES, LANES), F32)],
        compiler_params=_cparams(("arbitrary",)),
        name="in_proj",
    )(x, gain, w, wf, head_gain, fbias, w_out)


def _fox_kernel(anc_ref, q_ref, k_ref, vt_ref, o_ref, m_ref, l_ref, acc_ref, st_ref, *, nblk):
    b = pl.program_id(0)
    q_per_step, _, blk = q_ref.shape
    krow = lax.broadcasted_iota(jnp.int32, (blk, blk), 0)
    qcol = lax.broadcasted_iota(jnp.int32, (blk, blk), 1)
    lax.fori_loop(0, q_per_step, functools.partial(
        _fox_query_block, anc_ref, q_ref, k_ref, vt_ref, o_ref, m_ref, l_ref, acc_ref, st_ref,
        b, nblk, krow, qcol), 0)


def _fox_query_block(anc_ref, q_ref, k_ref, vt_ref, o_ref, m_ref, l_ref, acc_ref, st_ref,
                     b, nblk, krow, qcol, sub, carry):
    q_per_step, _, blk = q_ref.shape
    aw = 2 * HEAD_DIM
    i = pl.program_id(1) * q_per_step + sub
    m_ref[...] = jnp.full(m_ref.shape, NEG_INF, F32)
    l_ref[...] = jnp.zeros(l_ref.shape, F32)
    acc_ref[...] = jnp.zeros(acc_ref.shape, F32)

    def scores(j, slot):
        ks = pl.ds(pl.multiple_of(j * blk, blk), blk)
        k_bias = k_ref[ks, FOX_W:]
        for h in range(FOX_HEADS):
            k_aug = jnp.concatenate([k_ref[ks, h * HEAD_DIM:(h + 1) * HEAD_DIM], k_bias], axis=1)
            st_ref[slot, h] = jnp.dot(k_aug, q_ref[sub, h * aw:(h + 1) * aw, :],
                                      preferred_element_type=F32)

    def softmax_pv(j, slot, masked):
        for h in range(FOX_HEADS):
            st = st_ref[slot, h]
            d = anc_ref[(b * nblk + i) * SUBLANES + h] - anc_ref[(b * nblk + j) * SUBLANES + h]
            if masked:
                st = jnp.where(krow <= qcol, st, NEG_INF)
            m_prev = m_ref[h]
            m_new = jnp.maximum(m_prev, jnp.max(st, axis=0, keepdims=True) + d)
            alpha = jnp.exp2(m_prev - m_new)
            p = jnp.exp2(st - (m_new - d))
            l_ref[h] = alpha * l_ref[h] + jnp.sum(p, axis=0, keepdims=True)
            pv = jnp.dot(vt_ref[j, h * HEAD_DIM:(h + 1) * HEAD_DIM, :], p.astype(BF16),
                         preferred_element_type=F32)
            acc_ref[h] = alpha * acc_ref[h] + pv
            m_ref[h] = m_new

    def body(t, c):
        j = 2 * t
        scores(j + 1, 1)
        softmax_pv(j, 0, False)
        scores(j + 2, 0)
        softmax_pv(j + 1, 1, False)
        return c

    scores(0, 0)
    lax.fori_loop(0, i // 2, body, 0)

    @pl.when(i % 2 == 0)
    def _():
        softmax_pv(i, 0, True)

    @pl.when(i % 2 == 1)
    def _():
        scores(i, 1)
        softmax_pv(i - 1, 0, False)
        softmax_pv(i, 1, True)

    rows = pl.ds(pl.multiple_of(sub * blk, blk), blk)
    for h in range(FOX_HEADS):
        o_ref[rows, h * HEAD_DIM:(h + 1) * HEAD_DIM] = (acc_ref[h] / l_ref[h]).T.astype(o_ref.dtype)
    return carry


def _fox(anchors, qt, ka, vt, *, batch, seq, q_per_step=2):
    t = ka.shape[0]
    blk = FOX_BLK
    nblk = seq // blk
    steps = nblk // q_per_step
    return pl.pallas_call(
        functools.partial(_fox_kernel, nblk=nblk),
        out_shape=jax.ShapeDtypeStruct((t, FOX_W), BF16),
        grid=(batch, steps),
        in_specs=[
            pl.BlockSpec(memory_space=pltpu.SMEM),
            pl.BlockSpec((q_per_step, 2 * FOX_W, blk), lambda b, i: (b * steps + i, 0, 0)),
            pl.BlockSpec((seq, FOX_W + HEAD_DIM), lambda b, i: (b, 0)),
            pl.BlockSpec((nblk, FOX_W, blk), lambda b, i: (b, 0, 0)),
        ],
        out_specs=pl.BlockSpec((q_per_step * blk, FOX_W), lambda b, i: (b * steps + i, 0)),
        scratch_shapes=[
            pltpu.VMEM((FOX_HEADS, 1, blk), F32),
            pltpu.VMEM((FOX_HEADS, 1, blk), F32),
            pltpu.VMEM((FOX_HEADS, HEAD_DIM, blk), F32),
            pltpu.VMEM((2, FOX_HEADS, blk, blk), F32),
        ],
        compiler_params=_cparams(("parallel", "arbitrary")),
        name="fox",
    )(anchors, qt, ka, vt)


def _swa_kernel(q_ref, k_ref, v_ref, sink_ref, slope_ref, o_ref):
    qi = pl.program_id(1)
    tq = q_ref.shape[0]
    w = WINDOW
    n_sub = tq // w
    row = lax.broadcasted_iota(jnp.int32, (w, 2 * w), 0)
    col = lax.broadcasted_iota(jnp.int32, (w, 2 * w), 1)

    def bias_tables(offset):
        dist = offset + row - col
        valid = (dist >= 0) & (dist < w)
        dist_f = dist.astype(F32)
        return [jnp.where(valid, -(slope_ref[h] * LOG2E) * dist_f, NEG_INF) for h in range(SWA_HEADS)]

    later_bias = bias_tables(w)
    for r in range(n_sub):
        n = qi * n_sub + r
        kstart = pl.multiple_of(jnp.maximum(n - 1, 0) * w, w)
        bias = bias_tables(n * w - kstart) if r == 0 else later_bias
        for h in range(SWA_HEADS):
            g = h // SWA_GROUP
            q = q_ref[r * w:(r + 1) * w, h * HEAD_DIM:(h + 1) * HEAD_DIM]
            k = k_ref[pl.ds(kstart, 2 * w), g * HEAD_DIM:(g + 1) * HEAD_DIM]
            v = v_ref[pl.ds(kstart, 2 * w), g * HEAD_DIM:(g + 1) * HEAD_DIM]
            s = lax.dot_general(q, k, (((1,), (1,)), ((), ())), preferred_element_type=F32) + bias[h]
            sink = sink_ref[h] * LOG2E
            m = jnp.maximum(jnp.max(s, axis=-1, keepdims=True), sink)
            p = jnp.exp2(s - m)
            denom = jnp.sum(p, axis=-1, keepdims=True) + jnp.exp2(sink - m)
            o = jnp.dot(p.astype(BF16), v, preferred_element_type=F32)
            o_ref[r * w:(r + 1) * w, h * HEAD_DIM:(h + 1) * HEAD_DIM] = (o / denom).astype(o_ref.dtype)


def _swa(rest, sinks, slopes, *, batch, seq, tq=512):
    t = rest.shape[0]
    nq = seq // tq
    smem = pl.BlockSpec(memory_space=pltpu.SMEM)
    return pl.pallas_call(
        _swa_kernel,
        out_shape=jax.ShapeDtypeStruct((t, SWA_Q_W), BF16),
        grid=(batch, nq),
        in_specs=[
            pl.BlockSpec((tq, SWA_Q_W), lambda b, i: (b * nq + i, REST_SQ // SWA_Q_W)),
            pl.BlockSpec((seq, SWA_KV_W), lambda b, i: (b, REST_SK // SWA_KV_W)),
            pl.BlockSpec((seq, SWA_KV_W), lambda b, i: (b, REST_SV // SWA_KV_W)),
            smem, smem,
        ],
        out_specs=pl.BlockSpec((tq, SWA_Q_W), lambda b, i: (b * nq + i, 0)),
        compiler_params=_cparams(("parallel", "arbitrary")),
        name="swa",
    )(rest, rest, rest, sinks, slopes)


def _mem_kv_kernel(mem_ref, g_ref, wk_ref, wv_ref, kg_ref, mk_ref, mv_ref):
    mn = _rms(mem_ref[...], g_ref[...]).astype(BF16)
    k = jnp.dot(mn, wk_ref[...].astype(BF16), preferred_element_type=F32)
    v = jnp.dot(mn, wv_ref[...].astype(BF16), preferred_element_type=F32)
    for h in range(MEM_HEADS):
        hs = slice(h * HEAD_DIM, (h + 1) * HEAD_DIM)
        mk_ref[:, hs] = _rms(k[:, hs], kg_ref[...]).astype(BF16)
    mv_ref[...] = v.astype(BF16)


def _mem_kv(mem2d, gain, wk, wv, kgain, *, batch, mem_len):
    d = mem2d.shape[1]
    full = lambda shape: pl.BlockSpec(shape, lambda b: (0, 0))
    out = jax.ShapeDtypeStruct((batch * mem_len, MEM_W), BF16)
    return pl.pallas_call(
        _mem_kv_kernel,
        out_shape=(out, out),
        grid=(batch,),
        in_specs=[
            pl.BlockSpec((mem_len, d), lambda b: (b, 0)),
            full((1, d)), full((d, MEM_W)), full((d, MEM_W)), full((1, HEAD_DIM)),
        ],
        out_specs=(pl.BlockSpec((mem_len, MEM_W), lambda b: (b, 0)),
                   pl.BlockSpec((mem_len, MEM_W), lambda b: (b, 0))),
        compiler_params=_cparams(("parallel",)),
        name="mem_kv",
    )(mem2d, gain, wk, wv, kgain)


def _mem_attn_kernel(q_ref, mk_ref, mv_ref, o_ref):
    for h in range(MEM_HEADS):
        hs = slice(h * HEAD_DIM, (h + 1) * HEAD_DIM)
        s = lax.dot_general(q_ref[:, hs], mk_ref[:, hs], (((1,), (1,)), ((), ())),
                            preferred_element_type=F32)
        m = jnp.max(s, axis=-1, keepdims=True)
        p = jnp.exp2(s - m)
        denom = jnp.sum(p, axis=-1, keepdims=True)
        o = jnp.dot(p.astype(BF16), mv_ref[:, hs], preferred_element_type=F32)
        o_ref[:, hs] = (o / denom).astype(o_ref.dtype)


def _mem_attn(rest, mk, mv, *, batch, seq, mem_len, tq=512):
    t = rest.shape[0]
    nq = seq // tq
    return pl.pallas_call(
        _mem_attn_kernel,
        out_shape=jax.ShapeDtypeStruct((t, MEM_W), BF16),
        grid=(batch, nq),
        in_specs=[
            pl.BlockSpec((tq, MEM_W), lambda b, i: (b * nq + i, REST_MQ // MEM_W)),
            pl.BlockSpec((mem_len, MEM_W), lambda b, i: (b, 0)),
            pl.BlockSpec((mem_len, MEM_W), lambda b, i: (b, 0)),
        ],
        out_specs=pl.BlockSpec((tq, MEM_W), lambda b, i: (b * nq + i, 0)),
        compiler_params=_cparams(("parallel", "arbitrary")),
        name="mem_attn",
    )(rest, mk, mv)


def _out_proj_kernel(x_ref, a_ref, b_ref, c_ref, w_ref, o_ref):
    acc = jnp.dot(a_ref[...], w_ref[0:FOX_W, :], preferred_element_type=F32)
    acc += jnp.dot(b_ref[...], w_ref[FOX_W:FOX_W + SWA_Q_W, :], preferred_element_type=F32)
    acc += jnp.dot(c_ref[...], w_ref[FOX_W + SWA_Q_W:, :], preferred_element_type=F32)
    o_ref[...] = x_ref[...] + acc


def _out_proj(x, a, b, c, w, *, tm=1024):
    t, d = x.shape
    return pl.pallas_call(
        _out_proj_kernel,
        out_shape=jax.ShapeDtypeStruct((t, d), F32),
        grid=(t // tm,),
        in_specs=[
            pl.BlockSpec((tm, d), lambda i: (i, 0)),
            pl.BlockSpec((tm, FOX_W), lambda i: (i, 0)),
            pl.BlockSpec((tm, SWA_Q_W), lambda i: (i, 0)),
            pl.BlockSpec((tm, MEM_W), lambda i: (i, 0)),
            pl.BlockSpec(w.shape, lambda i: (0, 0), pipeline_mode=pl.Buffered(1)),
        ],
        out_specs=pl.BlockSpec((tm, d), lambda i: (i, 0)),
        compiler_params=_cparams(("parallel",)),
        name="out_proj",
    )(x, a, b, c, w)


def _alibi_slopes(n):
    return jnp.asarray([2.0 ** (-8.0 * i / n) for i in range(1, n + 1)], dtype=F32)


def _pack_w_in_kernel(w_ref, main_ref, f_ref, *, n_main):
    i = pl.program_id(0)
    n_a = w_ref.shape[2]

    @pl.when(i < n_main)
    def _():
        for a in range(n_a):
            main_ref[:, a * LANES:(a + 1) * LANES] = w_ref[0, :, a, :].astype(BF16)

    @pl.when(i == n_main)
    def _():
        row = lax.broadcasted_iota(jnp.int32, (f_ref.shape[0], LANES), 0)
        for a in range(n_a):
            f_ref[:, a * LANES:(a + 1) * LANES] = jnp.where(
                row < FOX_HEADS, w_ref[0, 0:f_ref.shape[0], a, :], 0.0).astype(BF16)


def _pack_w_in(w_in, l, *, rows=256):
    d, n = w_in.shape[1:]
    fl0 = 3 * FOX_W
    assert n == PROJ_W + FOX_HEADS and fl0 % rows == 0 and PROJ_W % rows == 0 and d % LANES == 0
    wt = jnp.swapaxes(w_in, 1, 2).reshape(w_in.shape[0], n, d // LANES, LANES)
    n_main = PROJ_W // rows

    def start(i):
        main = jnp.where(i * rows < fl0, i * rows, i * rows + FOX_HEADS)
        return jnp.where(i < n_main, main, fl0)

    return pl.pallas_call(
        functools.partial(_pack_w_in_kernel, n_main=n_main),
        out_shape=(jax.ShapeDtypeStruct((PROJ_W, d), BF16), jax.ShapeDtypeStruct((LANES, d), BF16)),
        grid=(n_main + 1,),
        in_specs=[pl.BlockSpec((pl.Element(1), pl.Element(rows), pl.Element(d // LANES), pl.Element(LANES)),
                               lambda i: (l, start(i), 0, 0))],
        out_specs=(pl.BlockSpec((rows, d), lambda i: (jnp.minimum(i, n_main - 1), 0)),
                   pl.BlockSpec((LANES, d), lambda i: (0, 0))),
        compiler_params=_cparams(("arbitrary",)),
        name="pack_w_in",
    )(wt)


def kernel(x, mem, ffn1_norm, ffn1_gate, ffn1_up, ffn1_down, mix_norm, mem_norm, w_in, forget_bias, w_mem_k, w_mem_v, fox_q_gain, fox_k_gain, swa_q_gain, swa_k_gain, swa_sinks, mem_q_gain, mem_k_gain, w_out, ffn2_norm, ffn2_gate, ffn2_up, ffn2_down):
    batch, seq, d = x.shape
    mem_len = mem.shape[1]
    depth = w_in.shape[0]
    scale = HEAD_DIM ** -0.5
    slopes = _alibi_slopes(SWA_HEADS)
    x2 = x.reshape(batch * seq, d)
    mem2 = mem.reshape(batch * mem_len, d)
    zeros = jnp.zeros((HEAD_DIM,), F32)

    for l in range(depth):
        x2, ffn2_w = _ffn(x2, ffn1_norm[l][None], ffn1_gate[l].astype(BF16), ffn1_up[l].astype(BF16),
                          ffn1_down[l].astype(BF16),
                          next_weights=(ffn2_gate[l], ffn2_up[l], ffn2_down[l]))

        w_main, w_f = _pack_w_in(w_in, l)
        qs = scale * LOG2E
        head_gain = jnp.stack([fox_q_gain[l] * qs, fox_k_gain[l], swa_q_gain[l] * qs,
                               swa_k_gain[l], mem_q_gain[l] * qs, zeros, zeros, zeros])
        fbias = jnp.pad(forget_bias[l], (0, LANES - FOX_HEADS))[None]
        qt, ka, vt, rest, anc, w_out_bf = _in_proj(x2, mix_norm[l][None], w_main, w_f, head_gain, fbias,
                                                   w_out[l], seq=seq)

        n_sub = (x2.shape[0] // FOX_BLK) // anc.shape[0]
        anchors = anc[:, :n_sub, :SUBLANES].reshape(-1)
        out_a = _fox(anchors, qt, ka, vt, batch=batch, seq=seq)
        out_b = _swa(rest, swa_sinks[l], slopes, batch=batch, seq=seq)
        mk, mv = _mem_kv(mem2, mem_norm[l][None], w_mem_k[l], w_mem_v[l],
                         mem_k_gain[l][None], batch=batch, mem_len=mem_len)
        out_c = _mem_attn(rest, mk, mv, batch=batch, seq=seq, mem_len=mem_len)

        x2 = _out_proj(x2, out_a, out_b, out_c, w_out_bf)
        x2, _ = _ffn(x2, ffn2_norm[l][None], *ffn2_w)
    return x2.reshape(batch, seq, d)
```

```python
import functools

import jax
import jax.numpy as jnp
from jax import lax
from jax.experimental import pallas as pl
from jax.experimental.pallas import tpu as pltpu

F32 = jnp.float32
BF16 = jnp.bfloat16

HEAD_DIM = 128
FOX_HEADS = 6
SWA_HEADS = 6
SWA_KV_HEADS = 2
SWA_GROUP = SWA_HEADS // SWA_KV_HEADS
MEM_HEADS = 4
WINDOW = 128
EPS = 1e-6
NEG_INF = -1e30
LOG2E = 1.4426950408889634
FOX_BLK = 512

FOX_W = FOX_HEADS * HEAD_DIM
SWA_Q_W = SWA_HEADS * HEAD_DIM
SWA_KV_W = SWA_KV_HEADS * HEAD_DIM
MEM_W = MEM_HEADS * HEAD_DIM

PROJ_W = 3 * FOX_W + SWA_Q_W + 2 * SWA_KV_W + MEM_W
REST_W = SWA_Q_W + 2 * SWA_KV_W + MEM_W
REST_SQ, REST_SK, REST_MQ, REST_SV = 0, SWA_Q_W, SWA_Q_W + SWA_KV_W, SWA_Q_W + SWA_KV_W + MEM_W
GAIN_FQ, GAIN_FK, GAIN_SQ, GAIN_SK, GAIN_MQ = range(5)

LANES = 128
SUBLANES = 8
MXU_N = 256

VMEM_LIMIT = 58 * 1024 * 1024


def _cparams(sem):
    return pltpu.CompilerParams(dimension_semantics=sem, vmem_limit_bytes=VMEM_LIMIT)


def _rms(x, gain):
    ms = jnp.mean(x * x, axis=-1, keepdims=True)
    return x * lax.rsqrt(ms + EPS) * gain


def _ffn_kernel(x_ref, g_ref, wg_ref, wu_ref, wd_ref, *rest, n_cast):
    cast_in, o_ref, cast_out, xn_ref = rest[:n_cast], rest[n_cast], rest[n_cast + 1:-1], rest[-1]
    j = pl.program_id(1)

    @pl.when(j == 0)
    def _():
        x = x_ref[...]
        xn_ref[...] = _rms(x, g_ref[...]).astype(BF16)
        o_ref[...] = x

    xn = xn_ref[...]
    gate = jnp.dot(xn, wg_ref[...], preferred_element_type=F32)
    up = jnp.dot(xn, wu_ref[...], preferred_element_type=F32)
    h = (gate * jax.nn.sigmoid(gate) * (0.5 * up)).astype(BF16)
    o_ref[...] += jnp.dot(h, wd_ref[...], preferred_element_type=F32)

    for src, dst in zip(cast_in, cast_out):
        dst[...] = src[...].astype(BF16)


def _ffn(x, gain, wg, wu, wd, *, next_weights=(), tm=1024, tf=512):
    t, d = x.shape
    dff = wg.shape[1]
    gi, gj = t // tm, dff // tf
    cast_specs, cast_shapes = [], []
    for w in next_weights:
        ri, cj = (gi, gj) if w.shape[1] == dff else (gj, gi)
        blk = (w.shape[0] // ri, w.shape[1] // cj)
        assert blk[0] % SUBLANES == 0 and blk[1] % LANES == 0
        index = (lambda i, j: (i, j)) if w.shape[1] == dff else (lambda i, j: (j, i))
        cast_specs.append(pl.BlockSpec(blk, index))
        cast_shapes.append(jax.ShapeDtypeStruct(w.shape, BF16))
    out = pl.pallas_call(
        functools.partial(_ffn_kernel, n_cast=len(next_weights)),
        out_shape=(jax.ShapeDtypeStruct((t, d), F32), *cast_shapes),
        grid=(gi, gj),
        in_specs=[
            pl.BlockSpec((tm, d), lambda i, j: (i, 0)),
            pl.BlockSpec((1, d), lambda i, j: (0, 0)),
            pl.BlockSpec((d, tf), lambda i, j: (0, j)),
            pl.BlockSpec((d, tf), lambda i, j: (0, j)),
            pl.BlockSpec((tf, d), lambda i, j: (j, 0)),
            *cast_specs,
        ],
        out_specs=(pl.BlockSpec((tm, d), lambda i, j: (i, 0)), *cast_specs),
        scratch_shapes=[pltpu.VMEM((tm, d), BF16)],
        compiler_params=_cparams(("parallel", "arbitrary")),
        name="ffn",
    )(x, gain, wg, wu, wd, *next_weights)
    return out[0], out[1:]


def _in_proj_kernel(x_ref, g_ref, w_ref, wf_ref, hg_ref, fb_ref, wo_ref,
                    qt_ref, ka_ref, vt_ref, rest_ref, anc_ref, wo_bf_ref, carry_ref, *, blocks_per_seq):
    i = pl.program_id(0)
    tm = x_ref.shape[0]

    @pl.when(i == 0)
    def _():
        carry_ref[...] = jnp.zeros_like(carry_ref)

    wo_bf_ref[...] = wo_ref[...].astype(BF16)
    hn = _rms(x_ref[...], g_ref[...]).astype(BF16)

    nt = (((1,), (1,)), ((), ()))
    logit = lax.dot_general(hn, wf_ref[...], nt, preferred_element_type=F32)
    c = jax.nn.log_sigmoid(logit + fb_ref[...]) * LOG2E
    row = lax.broadcasted_iota(jnp.int32, c.shape, 0) % FOX_BLK
    sh = 1
    while sh < FOX_BLK:
        c = c + jnp.where(row >= sh, pltpu.roll(c, sh, 0), 0.0)
        sh *= 2

    anchor = jnp.where(i % blocks_per_seq == 0, 0.0, carry_ref[0:1, :])
    anc_ref[...] = jnp.zeros_like(anc_ref)
    for r in range(tm // FOX_BLK):
        anc_ref[0, r:r + 1, :] = anchor
        anchor = anchor + c[(r + 1) * FOX_BLK - 1:(r + 1) * FOX_BLK, :]
    carry_ref[0:1, :] = anchor

    n_sub = tm // FOX_BLK
    lane = lax.broadcasted_iota(jnp.int32, c.shape, 1)
    sub = lax.broadcasted_iota(jnp.int32, (HEAD_DIM, FOX_BLK), 0)
    ct = c.T
    ak = jnp.zeros(c.shape, F32)
    for h in range(FOX_HEADS):
        ch = c[:, h:h + 1]
        hi = ch.astype(BF16).astype(F32)
        lo = ch - hi
        own = lane - 4 * h
        ak = jnp.where((own == 0) | (own == 1), 1.0, jnp.where(own == 2, -hi, jnp.where(own == 3, -lo, ak)))
        for r in range(n_sub):
            cr = ct[h:h + 1, r * FOX_BLK:(r + 1) * FOX_BLK]
            hi_r = cr.astype(BF16).astype(F32)
            lo_r = cr - hi_r
            own_r = sub - 4 * h
            aq = jnp.where(own_r == 0, hi_r, jnp.where(own_r == 1, lo_r,
                                                       jnp.where((own_r == 2) | (own_r == 3), 1.0, 0.0)))
            qt_ref[r, (2 * h + 1) * HEAD_DIM:(2 * h + 2) * HEAD_DIM, :] = aq.astype(BF16)
    ka_ref[:, FOX_W:] = ak.astype(BF16)

    def normed(ph, row_id):
        return _rms(ph, hg_ref[row_id:row_id + 1, :]).astype(BF16)

    heads_per_chunk = MXU_N // HEAD_DIM
    sv_chunk = (3 * FOX_W + SWA_Q_W + SWA_KV_W) // MXU_N
    chunks = [ck for ck in range(PROJ_W // MXU_N) if ck != sv_chunk] + [sv_chunk]
    for ck in chunks:
        p = lax.dot_general(hn, w_ref[ck * MXU_N:(ck + 1) * MXU_N, :], nt, preferred_element_type=F32)
        for e in range(heads_per_chunk):
            col = ck * MXU_N + e * HEAD_DIM
            ph = p[:, e * HEAD_DIM:(e + 1) * HEAD_DIM]
            if col < FOX_W:
                h = col // HEAD_DIM
                qn = _rms(ph, hg_ref[GAIN_FQ:GAIN_FQ + 1, :])
                for r in range(n_sub):
                    qt_ref[r, 2 * h * HEAD_DIM:(2 * h + 1) * HEAD_DIM, :] = (
                        qn[r * FOX_BLK:(r + 1) * FOX_BLK, :].T.astype(BF16))
            elif col < 2 * FOX_W:
                h = (col - FOX_W) // HEAD_DIM
                ka_ref[:, h * HEAD_DIM:(h + 1) * HEAD_DIM] = normed(ph, GAIN_FK)
            elif col < 3 * FOX_W:
                h = (col - 2 * FOX_W) // HEAD_DIM
                for r in range(n_sub):
                    vt_ref[r, h * HEAD_DIM:(h + 1) * HEAD_DIM, :] = (
                        ph[r * FOX_BLK:(r + 1) * FOX_BLK, :].T.astype(BF16))
            else:
                wc = col - 3 * FOX_W
                if wc < SWA_Q_W:
                    val, rc = normed(ph, GAIN_SQ), REST_SQ + wc
                elif wc < SWA_Q_W + SWA_KV_W:
                    val, rc = normed(ph, GAIN_SK), REST_SK + wc - SWA_Q_W
                elif wc < SWA_Q_W + 2 * SWA_KV_W:
                    val, rc = ph.astype(BF16), REST_SV + wc - SWA_Q_W - SWA_KV_W
                else:
                    val, rc = normed(ph, GAIN_MQ), REST_MQ + wc - SWA_Q_W - 2 * SWA_KV_W
                rest_ref[:, rc:rc + HEAD_DIM] = val


def _in_proj(x, gain, w, wf, head_gain, fbias, w_out, *, seq, tm=512):
    t, d = x.shape
    wo_rows = w_out.shape[0] // (t // tm)
    assert wo_rows % (2 * SUBLANES) == 0
    assert seq % tm == 0 and tm % FOX_BLK == 0 and tm // FOX_BLK <= SUBLANES
    const = lambda shape: pl.BlockSpec(shape, lambda i: (0, 0), pipeline_mode=pl.Buffered(1))
    return pl.pallas_call(
        functools.partial(_in_proj_kernel, blocks_per_seq=seq // tm),
        out_shape=(
            jax.ShapeDtypeStruct((t // FOX_BLK, 2 * FOX_W, FOX_BLK), BF16),
            jax.ShapeDtypeStruct((t, FOX_W + HEAD_DIM), BF16),
            jax.ShapeDtypeStruct((t // FOX_BLK, FOX_W, FOX_BLK), BF16),
            jax.ShapeDtypeStruct((t, REST_W), BF16),
            jax.ShapeDtypeStruct((t // tm, SUBLANES, LANES), F32),
            jax.ShapeDtypeStruct(w_out.shape, BF16),
        ),
        grid=(t // tm,),
        in_specs=[
            pl.BlockSpec((tm, d), lambda i: (i, 0)),
            const((1, d)),
            const((PROJ_W, d)),
            const((LANES, d)),
            const((SUBLANES, HEAD_DIM)),
            const((1, LANES)),
            pl.BlockSpec((wo_rows, w_out.shape[1]), lambda i: (i, 0)),
        ],
        out_specs=(
            pl.BlockSpec((tm // FOX_BLK, 2 * FOX_W, FOX_BLK), lambda i: (i, 0, 0)),
            pl.BlockSpec((tm, FOX_W + HEAD_DIM), lambda i: (i, 0)),
            pl.BlockSpec((tm // FOX_BLK, FOX_W, FOX_BLK), lambda i: (i, 0, 0)),
            pl.BlockSpec((tm, REST_W), lambda i: (i, 0)),
            pl.BlockSpec((1, SUBLANES, LANES), lambda i: (i, 0, 0)),
            pl.BlockSpec((wo_rows, w_out.shape[1]), lambda i: (i, 0)),
        ),
        scratch_shapes=[pltpu.VMEM((SUBLANES, LANES), F32)],
        compiler_params=_cparams(("arbitrary",)),
        name="in_proj",
    )(x, gain, w, wf, head_gain, fbias, w_out)


def _fox_kernel(anc_ref, q_ref, k_ref, vt_ref, o_ref, m_ref, l_ref, acc_ref, st_ref, *, nblk):
    b = pl.program_id(0)
    q_per_step, _, blk = q_ref.shape
    krow = lax.broadcasted_iota(jnp.int32, (blk, blk), 0)
    qcol = lax.broadcasted_iota(jnp.int32, (blk, blk), 1)
    lax.fori_loop(0, q_per_step, functools.partial(
        _fox_query_block, anc_ref, q_ref, k_ref, vt_ref, o_ref, m_ref, l_ref, acc_ref, st_ref,
        b, nblk, krow, qcol), 0)


def _fox_query_block(anc_ref, q_ref, k_ref, vt_ref, o_ref, m_ref, l_ref, acc_ref, st_ref,
                     b, nblk, krow, qcol, sub, carry):
    q_per_step, _, blk = q_ref.shape
    aw = 2 * HEAD_DIM
    i = pl.program_id(1) * q_per_step + sub
    m_ref[...] = jnp.full(m_ref.shape, NEG_INF, F32)
    l_ref[...] = jnp.zeros(l_ref.shape, F32)
    acc_ref[...] = jnp.zeros(acc_ref.shape, F32)

    def scores(j, slot):
        ks = pl.ds(pl.multiple_of(j * blk, blk), blk)
        k_bias = k_ref[ks, FOX_W:]
        for h in range(FOX_HEADS):
            k_aug = jnp.concatenate([k_ref[ks, h * HEAD_DIM:(h + 1) * HEAD_DIM], k_bias], axis=1)
            st_ref[slot, h] = jnp.dot(k_aug, q_ref[sub, h * aw:(h + 1) * aw, :],
                                      preferred_element_type=F32)

    def softmax_pv(j, slot, masked):
        for h in range(FOX_HEADS):
            st = st_ref[slot, h]
            d = anc_ref[(b * nblk + i) * SUBLANES + h] - anc_ref[(b * nblk + j) * SUBLANES + h]
            if masked:
                st = jnp.where(krow <= qcol, st, NEG_INF)
            m_prev = m_ref[h]
            m_new = jnp.maximum(m_prev, jnp.max(st, axis=0, keepdims=True) + d)
            alpha = jnp.exp2(m_prev - m_new)
            p = jnp.exp2(st - (m_new - d))
            l_ref[h] = alpha * l_ref[h] + jnp.sum(p, axis=0, keepdims=True)
            pv = jnp.dot(vt_ref[j, h * HEAD_DIM:(h + 1) * HEAD_DIM, :], p.astype(BF16),
                         preferred_element_type=F32)
            acc_ref[h] = alpha * acc_ref[h] + pv
            m_ref[h] = m_new

    def body(t, c):
        j = 2 * t
        scores(j + 1, 1)
        softmax_pv(j, 0, False)
        scores(j + 2, 0)
        softmax_pv(j + 1, 1, False)
        return c

    scores(0, 0)
    lax.fori_loop(0, i // 2, body, 0)

    @pl.when(i % 2 == 0)
    def _():
        softmax_pv(i, 0, True)

    @pl.when(i % 2 == 1)
    def _():
        scores(i, 1)
        softmax_pv(i - 1, 0, False)
        softmax_pv(i, 1, True)

    rows = pl.ds(pl.multiple_of(sub * blk, blk), blk)
    for h in range(FOX_HEADS):
        o_ref[rows, h * HEAD_DIM:(h + 1) * HEAD_DIM] = (acc_ref[h] / l_ref[h]).T.astype(o_ref.dtype)
    return carry


def _fox(anchors, qt, ka, vt, *, batch, seq, q_per_step=2):
    t = ka.shape[0]
    blk = FOX_BLK
    nblk = seq // blk
    steps = nblk // q_per_step
    return pl.pallas_call(
        functools.partial(_fox_kernel, nblk=nblk),
        out_shape=jax.ShapeDtypeStruct((t, FOX_W), BF16),
        grid=(batch, steps),
        in_specs=[
            pl.BlockSpec(memory_space=pltpu.SMEM),
            pl.BlockSpec((q_per_step, 2 * FOX_W, blk), lambda b, i: (b * steps + i, 0, 0)),
            pl.BlockSpec((seq, FOX_W + HEAD_DIM), lambda b, i: (b, 0)),
            pl.BlockSpec((nblk, FOX_W, blk), lambda b, i: (b, 0, 0)),
        ],
        out_specs=pl.BlockSpec((q_per_step * blk, FOX_W), lambda b, i: (b * steps + i, 0)),
        scratch_shapes=[
            pltpu.VMEM((FOX_HEADS, 1, blk), F32),
            pltpu.VMEM((FOX_HEADS, 1, blk), F32),
            pltpu.VMEM((FOX_HEADS, HEAD_DIM, blk), F32),
            pltpu.VMEM((2, FOX_HEADS, blk, blk), F32),
        ],
        compiler_params=_cparams(("parallel", "arbitrary")),
        name="fox",
    )(anchors, qt, ka, vt)


def _swa_kernel(q_ref, k_ref, v_ref, sink_ref, slope_ref, o_ref):
    qi = pl.program_id(1)
    tq = q_ref.shape[0]
    w = WINDOW
    n_sub = tq // w
    row = lax.broadcasted_iota(jnp.int32, (w, 2 * w), 0)
    col = lax.broadcasted_iota(jnp.int32, (w, 2 * w), 1)

    def bias_tables(offset):
        dist = offset + row - col
        valid = (dist >= 0) & (dist < w)
        dist_f = dist.astype(F32)
        return [jnp.where(valid, -(slope_ref[h] * LOG2E) * dist_f, NEG_INF) for h in range(SWA_HEADS)]

    later_bias = bias_tables(w)
    for r in range(n_sub):
        n = qi * n_sub + r
        kstart = pl.multiple_of(jnp.maximum(n - 1, 0) * w, w)
        bias = bias_tables(n * w - kstart) if r == 0 else later_bias
        for h in range(SWA_HEADS):
            g = h // SWA_GROUP
            q = q_ref[r * w:(r + 1) * w, h * HEAD_DIM:(h + 1) * HEAD_DIM]
            k = k_ref[pl.ds(kstart, 2 * w), g * HEAD_DIM:(g + 1) * HEAD_DIM]
            v = v_ref[pl.ds(kstart, 2 * w), g * HEAD_DIM:(g + 1) * HEAD_DIM]
            s = lax.dot_general(q, k, (((1,), (1,)), ((), ())), preferred_element_type=F32) + bias[h]
            sink = sink_ref[h] * LOG2E
            m = jnp.maximum(jnp.max(s, axis=-1, keepdims=True), sink)
            p = jnp.exp2(s - m)
            denom = jnp.sum(p, axis=-1, keepdims=True) + jnp.exp2(sink - m)
            o = jnp.dot(p.astype(BF16), v, preferred_element_type=F32)
            o_ref[r * w:(r + 1) * w, h * HEAD_DIM:(h + 1) * HEAD_DIM] = (o / denom).astype(o_ref.dtype)


def _swa(rest, sinks, slopes, *, batch, seq, tq=512):
    t = rest.shape[0]
    nq = seq // tq
    smem = pl.BlockSpec(memory_space=pltpu.SMEM)
    return pl.pallas_call(
        _swa_kernel,
        out_shape=jax.ShapeDtypeStruct((t, SWA_Q_W), BF16),
        grid=(batch, nq),
        in_specs=[
            pl.BlockSpec((tq, SWA_Q_W), lambda b, i: (b * nq + i, REST_SQ // SWA_Q_W)),
            pl.BlockSpec((seq, SWA_KV_W), lambda b, i: (b, REST_SK // SWA_KV_W)),
            pl.BlockSpec((seq, SWA_KV_W), lambda b, i: (b, REST_SV // SWA_KV_W)),
            smem, smem,
        ],
        out_specs=pl.BlockSpec((tq, SWA_Q_W), lambda b, i: (b * nq + i, 0)),
        compiler_params=_cparams(("parallel", "arbitrary")),
        name="swa",
    )(rest, rest, rest, sinks, slopes)


def _mem_kv_kernel(mem_ref, g_ref, wk_ref, wv_ref, kg_ref, mk_ref, mv_ref):
    mn = _rms(mem_ref[...], g_ref[...]).astype(BF16)
    k = jnp.dot(mn, wk_ref[...].astype(BF16), preferred_element_type=F32)
    v = jnp.dot(mn, wv_ref[...].astype(BF16), preferred_element_type=F32)
    for h in range(MEM_HEADS):
        hs = slice(h * HEAD_DIM, (h + 1) * HEAD_DIM)
        mk_ref[:, hs] = _rms(k[:, hs], kg_ref[...]).astype(BF16)
    mv_ref[...] = v.astype(BF16)


def _mem_kv(mem2d, gain, wk, wv, kgain, *, batch, mem_len):
    d = mem2d.shape[1]
    full = lambda shape: pl.BlockSpec(shape, lambda b: (0, 0))
    out = jax.ShapeDtypeStruct((batch * mem_len, MEM_W), BF16)
    return pl.pallas_call(
        _mem_kv_kernel,
        out_shape=(out, out),
        grid=(batch,),
        in_specs=[
            pl.BlockSpec((mem_len, d), lambda b: (b, 0)),
            full((1, d)), full((d, MEM_W)), full((d, MEM_W)), full((1, HEAD_DIM)),
        ],
        out_specs=(pl.BlockSpec((mem_len, MEM_W), lambda b: (b, 0)),
                   pl.BlockSpec((mem_len, MEM_W), lambda b: (b, 0))),
        compiler_params=_cparams(("parallel",)),
        name="mem_kv",
    )(mem2d, gain, wk, wv, kgain)


def _mem_attn_kernel(q_ref, mk_ref, mv_ref, o_ref):
    for h in range(MEM_HEADS):
        hs = slice(h * HEAD_DIM, (h + 1) * HEAD_DIM)
        s = lax.dot_general(q_ref[:, hs], mk_ref[:, hs], (((1,), (1,)), ((), ())),
                            preferred_element_type=F32)
        m = jnp.max(s, axis=-1, keepdims=True)
        p = jnp.exp2(s - m)
        denom = jnp.sum(p, axis=-1, keepdims=True)
        o = jnp.dot(p.astype(BF16), mv_ref[:, hs], preferred_element_type=F32)
        o_ref[:, hs] = (o / denom).astype(o_ref.dtype)


def _mem_attn(rest, mk, mv, *, batch, seq, mem_len, tq=512):
    t = rest.shape[0]
    nq = seq // tq
    return pl.pallas_call(
        _mem_attn_kernel,
        out_shape=jax.ShapeDtypeStruct((t, MEM_W), BF16),
        grid=(batch, nq),
        in_specs=[
            pl.BlockSpec((tq, MEM_W), lambda b, i: (b * nq + i, REST_MQ // MEM_W)),
            pl.BlockSpec((mem_len, MEM_W), lambda b, i: (b, 0)),
            pl.BlockSpec((mem_len, MEM_W), lambda b, i: (b, 0)),
        ],
        out_specs=pl.BlockSpec((tq, MEM_W), lambda b, i: (b * nq + i, 0)),
        compiler_params=_cparams(("parallel", "arbitrary")),
        name="mem_attn",
    )(rest, mk, mv)


def _out_proj_kernel(x_ref, a_ref, b_ref, c_ref, w_ref, o_ref):
    acc = jnp.dot(a_ref[...], w_ref[0:FOX_W, :], preferred_element_type=F32)
    acc += jnp.dot(b_ref[...], w_ref[FOX_W:FOX_W + SWA_Q_W, :], preferred_element_type=F32)
    acc += jnp.dot(c_ref[...], w_ref[FOX_W + SWA_Q_W:, :], preferred_element_type=F32)
    o_ref[...] = x_ref[...] + acc


def _out_proj(x, a, b, c, w, *, tm=1024):
    t, d = x.shape
    return pl.pallas_call(
        _out_proj_kernel,
        out_shape=jax.ShapeDtypeStruct((t, d), F32),
        grid=(t // tm,),
        in_specs=[
            pl.BlockSpec((tm, d), lambda i: (i, 0)),
            pl.BlockSpec((tm, FOX_W), lambda i: (i, 0)),
            pl.BlockSpec((tm, SWA_Q_W), lambda i: (i, 0)),
            pl.BlockSpec((tm, MEM_W), lambda i: (i, 0)),
            pl.BlockSpec(w.shape, lambda i: (0, 0), pipeline_mode=pl.Buffered(1)),
        ],
        out_specs=pl.BlockSpec((tm, d), lambda i: (i, 0)),
        compiler_params=_cparams(("parallel",)),
        name="out_proj",
    )(x, a, b, c, w)


def _alibi_slopes(n):
    return jnp.asarray([2.0 ** (-8.0 * i / n) for i in range(1, n + 1)], dtype=F32)


def _pack_w_in_kernel(w_ref, *rest, n_main):
    n_ride = (len(rest) - 2) // 2
    ride_in, (main_ref, f_ref), ride_out = rest[:n_ride], rest[n_ride:n_ride + 2], rest[n_ride + 2:]
    i = pl.program_id(0)
    n_a = w_ref.shape[2]

    @pl.when(i < n_main)
    def _():
        for a in range(n_a):
            main_ref[:, a * LANES:(a + 1) * LANES] = w_ref[0, :, a, :].astype(BF16)
        for src, dst in zip(ride_in, ride_out):
            dst[...] = src[...].astype(BF16)

    @pl.when(i == n_main)
    def _():
        row = lax.broadcasted_iota(jnp.int32, (f_ref.shape[0], LANES), 0)
        for a in range(n_a):
            f_ref[:, a * LANES:(a + 1) * LANES] = jnp.where(
                row < FOX_HEADS, w_ref[0, 0:f_ref.shape[0], a, :], 0.0).astype(BF16)


def _pack_w_in(w_in, l, *, ride=(), rows=256):
    d, n = w_in.shape[1:]
    fl0 = 3 * FOX_W
    assert n == PROJ_W + FOX_HEADS and fl0 % rows == 0 and PROJ_W % rows == 0 and d % LANES == 0
    wt = jnp.swapaxes(w_in, 1, 2).reshape(w_in.shape[0], n, d // LANES, LANES)
    n_main = PROJ_W // rows

    def start(i):
        main = jnp.where(i * rows < fl0, i * rows, i * rows + FOX_HEADS)
        return jnp.where(i < n_main, main, fl0)

    last_main = lambda i: (jnp.minimum(i, n_main - 1), 0)
    ride_specs = []
    for w in ride:
        assert w.shape[0] % (n_main * 2 * SUBLANES) == 0
        ride_specs.append(pl.BlockSpec((w.shape[0] // n_main, w.shape[1]), last_main))
    out = pl.pallas_call(
        functools.partial(_pack_w_in_kernel, n_main=n_main),
        out_shape=(jax.ShapeDtypeStruct((PROJ_W, d), BF16), jax.ShapeDtypeStruct((LANES, d), BF16),
                   *[jax.ShapeDtypeStruct(w.shape, BF16) for w in ride]),
        grid=(n_main + 1,),
        in_specs=[pl.BlockSpec((pl.Element(1), pl.Element(rows), pl.Element(d // LANES), pl.Element(LANES)),
                               lambda i: (l, start(i), 0, 0)),
                  *ride_specs],
        out_specs=(pl.BlockSpec((rows, d), last_main), pl.BlockSpec((LANES, d), lambda i: (0, 0)),
                   *ride_specs),
        compiler_params=_cparams(("arbitrary",)),
        name="pack_w_in",
    )(wt, *ride)
    return out[0], out[1], out[2:]


def kernel(x, mem, ffn1_norm, ffn1_gate, ffn1_up, ffn1_down, mix_norm, mem_norm, w_in, forget_bias, w_mem_k, w_mem_v, fox_q_gain, fox_k_gain, swa_q_gain, swa_k_gain, swa_sinks, mem_q_gain, mem_k_gain, w_out, ffn2_norm, ffn2_gate, ffn2_up, ffn2_down):
    batch, seq, d = x.shape
    mem_len = mem.shape[1]
    depth = w_in.shape[0]
    scale = HEAD_DIM ** -0.5
    slopes = _alibi_slopes(SWA_HEADS)
    x2 = x.reshape(batch * seq, d)
    mem2 = mem.reshape(batch * mem_len, d)
    zeros = jnp.zeros((HEAD_DIM,), F32)

    for l in range(depth):
        w_main, w_f, ffn1_w = _pack_w_in(w_in, l, ride=(ffn1_gate[l], ffn1_up[l], ffn1_down[l]))
        x2, ffn2_w = _ffn(x2, ffn1_norm[l][None], *ffn1_w,
                          next_weights=(ffn2_gate[l], ffn2_up[l], ffn2_down[l]))

        qs = scale * LOG2E
        head_gain = jnp.stack([fox_q_gain[l] * qs, fox_k_gain[l], swa_q_gain[l] * qs,
                               swa_k_gain[l], mem_q_gain[l] * qs, zeros, zeros, zeros])
        fbias = jnp.pad(forget_bias[l], (0, LANES - FOX_HEADS))[None]
        qt, ka, vt, rest, anc, w_out_bf = _in_proj(x2, mix_norm[l][None], w_main, w_f, head_gain, fbias,
                                                   w_out[l], seq=seq)

        n_sub = (x2.shape[0] // FOX_BLK) // anc.shape[0]
        anchors = anc[:, :n_sub, :SUBLANES].reshape(-1)
        out_a = _fox(anchors, qt, ka, vt, batch=batch, seq=seq)
        out_b = _swa(rest, swa_sinks[l], slopes, batch=batch, seq=seq)
        mk, mv = _mem_kv(mem2, mem_norm[l][None], w_mem_k[l], w_mem_v[l],
                         mem_k_gain[l][None], batch=batch, mem_len=mem_len)
        out_c = _mem_attn(rest, mk, mv, batch=batch, seq=seq, mem_len=mem_len)

        x2 = _out_proj(x2, out_a, out_b, out_c, w_out_bf)
        x2, _ = _ffn(x2, ffn2_norm[l][None], *ffn2_w)
    return x2.reshape(batch, seq, d)
```

```python
import functools

import jax
import jax.numpy as jnp
from jax import lax
from jax.experimental import pallas as pl
from jax.experimental.pallas import tpu as pltpu

F32 = jnp.float32
BF16 = jnp.bfloat16

HEAD_DIM = 128
FOX_HEADS = 6
SWA_HEADS = 6
SWA_KV_HEADS = 2
SWA_GROUP = SWA_HEADS // SWA_KV_HEADS
MEM_HEADS = 4
WINDOW = 128
EPS = 1e-6
NEG_INF = -1e30
LOG2E = 1.4426950408889634
FOX_BLK = 512

FOX_W = FOX_HEADS * HEAD_DIM
SWA_Q_W = SWA_HEADS * HEAD_DIM
SWA_KV_W = SWA_KV_HEADS * HEAD_DIM
MEM_W = MEM_HEADS * HEAD_DIM

PROJ_W = 3 * FOX_W + SWA_Q_W + 2 * SWA_KV_W + MEM_W
REST_W = SWA_Q_W + 2 * SWA_KV_W + MEM_W
REST_SQ, REST_SK, REST_MQ, REST_SV = 0, SWA_Q_W, SWA_Q_W + SWA_KV_W, SWA_Q_W + SWA_KV_W + MEM_W
GAIN_FQ, GAIN_FK, GAIN_SQ, GAIN_SK, GAIN_MQ = range(5)

LANES = 128
SUBLANES = 8
MXU_N = 256

VMEM_LIMIT = 58 * 1024 * 1024


def _cparams(sem):
    return pltpu.CompilerParams(dimension_semantics=sem, vmem_limit_bytes=VMEM_LIMIT)


def _rms(x, gain):
    ms = jnp.mean(x * x, axis=-1, keepdims=True)
    return x * lax.rsqrt(ms + EPS) * gain


def _ffn_kernel(x_ref, g_ref, wg_ref, wu_ref, wd_ref, *rest, n_cast):
    cast_in, o_ref, cast_out, xn_ref = rest[:n_cast], rest[n_cast], rest[n_cast + 1:-1], rest[-1]
    j = pl.program_id(1)

    @pl.when(j == 0)
    def _():
        x = x_ref[...]
        xn_ref[...] = _rms(x, g_ref[...]).astype(BF16)
        o_ref[...] = x

    xn = xn_ref[...]
    acc = None
    for c0 in range(0, wg_ref.shape[1], MXU_N):
        cs = slice(c0, c0 + MXU_N)
        gate = jnp.dot(xn, wg_ref[:, cs], preferred_element_type=F32)
        up = jnp.dot(xn, wu_ref[:, cs], preferred_element_type=F32)
        h = (gate * jax.nn.sigmoid(gate) * (0.5 * up)).astype(BF16)
        part = jnp.dot(h, wd_ref[cs, :], preferred_element_type=F32)
        acc = part if acc is None else acc + part
    o_ref[...] += acc

    for src, dst in zip(cast_in, cast_out):
        dst[...] = src[...].astype(BF16)


def _ffn(x, gain, wg, wu, wd, *, next_weights=(), tm=1024, tf=512):
    t, d = x.shape
    dff = wg.shape[1]
    gi, gj = t // tm, dff // tf
    cast_specs, cast_shapes = [], []
    for w in next_weights:
        ri, cj = (gi, gj) if w.shape[1] == dff else (gj, gi)
        blk = (w.shape[0] // ri, w.shape[1] // cj)
        assert blk[0] % SUBLANES == 0 and blk[1] % LANES == 0
        index = (lambda i, j: (i, j)) if w.shape[1] == dff else (lambda i, j: (j, i))
        cast_specs.append(pl.BlockSpec(blk, index))
        cast_shapes.append(jax.ShapeDtypeStruct(w.shape, BF16))
    out = pl.pallas_call(
        functools.partial(_ffn_kernel, n_cast=len(next_weights)),
        out_shape=(jax.ShapeDtypeStruct((t, d), F32), *cast_shapes),
        grid=(gi, gj),
        in_specs=[
            pl.BlockSpec((tm, d), lambda i, j: (i, 0)),
            pl.BlockSpec((1, d), lambda i, j: (0, 0)),
            pl.BlockSpec((d, tf), lambda i, j: (0, j)),
            pl.BlockSpec((d, tf), lambda i, j: (0, j)),
            pl.BlockSpec((tf, d), lambda i, j: (j, 0)),
            *cast_specs,
        ],
        out_specs=(pl.BlockSpec((tm, d), lambda i, j: (i, 0)), *cast_specs),
        scratch_shapes=[pltpu.VMEM((tm, d), BF16)],
        compiler_params=_cparams(("parallel", "arbitrary")),
        name="ffn",
    )(x, gain, wg, wu, wd, *next_weights)
    return out[0], out[1:]


def _in_proj_kernel(x_ref, g_ref, w_ref, wf_ref, hg_ref, fb_ref, wo_ref,
                    qt_ref, ka_ref, vt_ref, rest_ref, anc_ref, wo_bf_ref, carry_ref, *, blocks_per_seq):
    i = pl.program_id(0)
    tm = x_ref.shape[0]

    @pl.when(i == 0)
    def _():
        carry_ref[...] = jnp.zeros_like(carry_ref)

    wo_bf_ref[...] = wo_ref[...].astype(BF16)
    hn = _rms(x_ref[...], g_ref[...]).astype(BF16)

    nt = (((1,), (1,)), ((), ()))
    logit = lax.dot_general(hn, wf_ref[...], nt, preferred_element_type=F32)
    c = jax.nn.log_sigmoid(logit + fb_ref[...]) * LOG2E
    row = lax.broadcasted_iota(jnp.int32, c.shape, 0) % FOX_BLK
    sh = 1
    while sh < FOX_BLK:
        c = c + jnp.where(row >= sh, pltpu.roll(c, sh, 0), 0.0)
        sh *= 2

    anchor = jnp.where(i % blocks_per_seq == 0, 0.0, carry_ref[0:1, :])
    anc_ref[...] = jnp.zeros_like(anc_ref)
    for r in range(tm // FOX_BLK):
        anc_ref[0, r:r + 1, :] = anchor
        anchor = anchor + c[(r + 1) * FOX_BLK - 1:(r + 1) * FOX_BLK, :]
    carry_ref[0:1, :] = anchor

    n_sub = tm // FOX_BLK
    lane = lax.broadcasted_iota(jnp.int32, c.shape, 1)
    sub = lax.broadcasted_iota(jnp.int32, (HEAD_DIM, FOX_BLK), 0)
    ct = c.T
    ak = jnp.zeros(c.shape, F32)
    for h in range(FOX_HEADS):
        ch = c[:, h:h + 1]
        hi = ch.astype(BF16).astype(F32)
        lo = ch - hi
        own = lane - 4 * h
        ak = jnp.where((own == 0) | (own == 1), 1.0, jnp.where(own == 2, -hi, jnp.where(own == 3, -lo, ak)))
        for r in range(n_sub):
            cr = ct[h:h + 1, r * FOX_BLK:(r + 1) * FOX_BLK]
            hi_r = cr.astype(BF16).astype(F32)
            lo_r = cr - hi_r
            own_r = sub - 4 * h
            aq = jnp.where(own_r == 0, hi_r, jnp.where(own_r == 1, lo_r,
                                                       jnp.where((own_r == 2) | (own_r == 3), 1.0, 0.0)))
            qt_ref[r, (2 * h + 1) * HEAD_DIM:(2 * h + 2) * HEAD_DIM, :] = aq.astype(BF16)
    ka_ref[:, FOX_W:] = ak.astype(BF16)

    def normed(ph, row_id):
        return _rms(ph, hg_ref[row_id:row_id + 1, :]).astype(BF16)

    heads_per_chunk = MXU_N // HEAD_DIM
    sv_chunk = (3 * FOX_W + SWA_Q_W + SWA_KV_W) // MXU_N
    chunks = [ck for ck in range(PROJ_W // MXU_N) if ck != sv_chunk] + [sv_chunk]
    for ck in chunks:
        p = lax.dot_general(hn, w_ref[ck * MXU_N:(ck + 1) * MXU_N, :], nt, preferred_element_type=F32)
        for e in range(heads_per_chunk):
            col = ck * MXU_N + e * HEAD_DIM
            ph = p[:, e * HEAD_DIM:(e + 1) * HEAD_DIM]
            if col < FOX_W:
                h = col // HEAD_DIM
                qn = _rms(ph, hg_ref[GAIN_FQ:GAIN_FQ + 1, :])
                for r in range(n_sub):
                    qt_ref[r, 2 * h * HEAD_DIM:(2 * h + 1) * HEAD_DIM, :] = (
                        qn[r * FOX_BLK:(r + 1) * FOX_BLK, :].T.astype(BF16))
            elif col < 2 * FOX_W:
                h = (col - FOX_W) // HEAD_DIM
                ka_ref[:, h * HEAD_DIM:(h + 1) * HEAD_DIM] = normed(ph, GAIN_FK)
            elif col < 3 * FOX_W:
                h = (col - 2 * FOX_W) // HEAD_DIM
                for r in range(n_sub):
                    vt_ref[r, h * HEAD_DIM:(h + 1) * HEAD_DIM, :] = (
                        ph[r * FOX_BLK:(r + 1) * FOX_BLK, :].T.astype(BF16))
            else:
                wc = col - 3 * FOX_W
                if wc < SWA_Q_W:
                    val, rc = normed(ph, GAIN_SQ), REST_SQ + wc
                elif wc < SWA_Q_W + SWA_KV_W:
                    val, rc = normed(ph, GAIN_SK), REST_SK + wc - SWA_Q_W
                elif wc < SWA_Q_W + 2 * SWA_KV_W:
                    val, rc = ph.astype(BF16), REST_SV + wc - SWA_Q_W - SWA_KV_W
                else:
                    val, rc = normed(ph, GAIN_MQ), REST_MQ + wc - SWA_Q_W - 2 * SWA_KV_W
                rest_ref[:, rc:rc + HEAD_DIM] = val


def _in_proj(x, gain, w, wf, head_gain, fbias, w_out, *, seq, tm=512):
    t, d = x.shape
    wo_rows = w_out.shape[0] // (t // tm)
    assert wo_rows % (2 * SUBLANES) == 0
    assert seq % tm == 0 and tm % FOX_BLK == 0 and tm // FOX_BLK <= SUBLANES
    const = lambda shape: pl.BlockSpec(shape, lambda i: (0, 0), pipeline_mode=pl.Buffered(1))
    return pl.pallas_call(
        functools.partial(_in_proj_kernel, blocks_per_seq=seq // tm),
        out_shape=(
            jax.ShapeDtypeStruct((t // FOX_BLK, 2 * FOX_W, FOX_BLK), BF16),
            jax.ShapeDtypeStruct((t, FOX_W + HEAD_DIM), BF16),
            jax.ShapeDtypeStruct((t // FOX_BLK, FOX_W, FOX_BLK), BF16),
            jax.ShapeDtypeStruct((t, REST_W), BF16),
            jax.ShapeDtypeStruct((t // tm, SUBLANES, LANES), F32),
            jax.ShapeDtypeStruct(w_out.shape, BF16),
        ),
        grid=(t // tm,),
        in_specs=[
            pl.BlockSpec((tm, d), lambda i: (i, 0)),
            const((1, d)),
            const((PROJ_W, d)),
            const((LANES, d)),
            const((SUBLANES, HEAD_DIM)),
            const((1, LANES)),
            pl.BlockSpec((wo_rows, w_out.shape[1]), lambda i: (i, 0)),
        ],
        out_specs=(
            pl.BlockSpec((tm // FOX_BLK, 2 * FOX_W, FOX_BLK), lambda i: (i, 0, 0)),
            pl.BlockSpec((tm, FOX_W + HEAD_DIM), lambda i: (i, 0)),
            pl.BlockSpec((tm // FOX_BLK, FOX_W, FOX_BLK), lambda i: (i, 0, 0)),
            pl.BlockSpec((tm, REST_W), lambda i: (i, 0)),
            pl.BlockSpec((1, SUBLANES, LANES), lambda i: (i, 0, 0)),
            pl.BlockSpec((wo_rows, w_out.shape[1]), lambda i: (i, 0)),
        ),
        scratch_shapes=[pltpu.VMEM((SUBLANES, LANES), F32)],
        compiler_params=_cparams(("arbitrary",)),
        name="in_proj",
    )(x, gain, w, wf, head_gain, fbias, w_out)


def _fox_kernel(anc_ref, q_ref, k_ref, vt_ref, o_ref, m_ref, l_ref, acc_ref, st_ref, *, nblk):
    b = pl.program_id(0)
    q_per_step, _, blk = q_ref.shape
    krow = lax.broadcasted_iota(jnp.int32, (blk, blk), 0)
    qcol = lax.broadcasted_iota(jnp.int32, (blk, blk), 1)
    lax.fori_loop(0, q_per_step, functools.partial(
        _fox_query_block, anc_ref, q_ref, k_ref, vt_ref, o_ref, m_ref, l_ref, acc_ref, st_ref,
        b, nblk, krow, qcol), 0)


def _fox_query_block(anc_ref, q_ref, k_ref, vt_ref, o_ref, m_ref, l_ref, acc_ref, st_ref,
                     b, nblk, krow, qcol, sub, carry):
    q_per_step, _, blk = q_ref.shape
    aw = 2 * HEAD_DIM
    i = pl.program_id(1) * q_per_step + sub
    m_ref[...] = jnp.full(m_ref.shape, NEG_INF, F32)
    l_ref[...] = jnp.zeros(l_ref.shape, F32)
    acc_ref[...] = jnp.zeros(acc_ref.shape, F32)

    def scores(j, slot):
        ks = pl.ds(pl.multiple_of(j * blk, blk), blk)
        k_bias = k_ref[ks, FOX_W:]
        for h in range(FOX_HEADS):
            k_aug = jnp.concatenate([k_ref[ks, h * HEAD_DIM:(h + 1) * HEAD_DIM], k_bias], axis=1)
            st_ref[slot, h] = jnp.dot(k_aug, q_ref[sub, h * aw:(h + 1) * aw, :],
                                      preferred_element_type=F32)

    def softmax_pv(j, slot, masked):
        for h in range(FOX_HEADS):
            st = st_ref[slot, h]
            d = anc_ref[(b * nblk + i) * SUBLANES + h] - anc_ref[(b * nblk + j) * SUBLANES + h]
            if masked:
                st = jnp.where(krow <= qcol, st, NEG_INF)
            m_prev = m_ref[h]
            m_new = jnp.maximum(m_prev, jnp.max(st, axis=0, keepdims=True) + d)
            alpha = jnp.exp2(m_prev - m_new)
            p = jnp.exp2(st - (m_new - d))
            l_ref[h] = alpha * l_ref[h] + jnp.sum(p, axis=0, keepdims=True)
            pv = jnp.dot(vt_ref[j, h * HEAD_DIM:(h + 1) * HEAD_DIM, :], p.astype(BF16),
                         preferred_element_type=F32)
            acc_ref[h] = alpha * acc_ref[h] + pv
            m_ref[h] = m_new

    def body(t, c):
        j = 2 * t
        scores(j + 1, 1)
        softmax_pv(j, 0, False)
        scores(j + 2, 0)
        softmax_pv(j + 1, 1, False)
        return c

    scores(0, 0)
    lax.fori_loop(0, i // 2, body, 0)

    @pl.when(i % 2 == 0)
    def _():
        softmax_pv(i, 0, True)

    @pl.when(i % 2 == 1)
    def _():
        scores(i, 1)
        softmax_pv(i - 1, 0, False)
        softmax_pv(i, 1, True)

    rows = pl.ds(pl.multiple_of(sub * blk, blk), blk)
    for h in range(FOX_HEADS):
        o_ref[rows, h * HEAD_DIM:(h + 1) * HEAD_DIM] = (acc_ref[h] / l_ref[h]).T.astype(o_ref.dtype)
    return carry


def _fox(anchors, qt, ka, vt, *, batch, seq, q_per_step=2):
    t = ka.shape[0]
    blk = FOX_BLK
    nblk = seq // blk
    steps = nblk // q_per_step
    return pl.pallas_call(
        functools.partial(_fox_kernel, nblk=nblk),
        out_shape=jax.ShapeDtypeStruct((t, FOX_W), BF16),
        grid=(batch, steps),
        in_specs=[
            pl.BlockSpec(memory_space=pltpu.SMEM),
            pl.BlockSpec((q_per_step, 2 * FOX_W, blk), lambda b, i: (b * steps + i, 0, 0)),
            pl.BlockSpec((seq, FOX_W + HEAD_DIM), lambda b, i: (b, 0)),
            pl.BlockSpec((nblk, FOX_W, blk), lambda b, i: (b, 0, 0)),
        ],
        out_specs=pl.BlockSpec((q_per_step * blk, FOX_W), lambda b, i: (b * steps + i, 0)),
        scratch_shapes=[
            pltpu.VMEM((FOX_HEADS, 1, blk), F32),
            pltpu.VMEM((FOX_HEADS, 1, blk), F32),
            pltpu.VMEM((FOX_HEADS, HEAD_DIM, blk), F32),
            pltpu.VMEM((2, FOX_HEADS, blk, blk), F32),
        ],
        compiler_params=_cparams(("parallel", "arbitrary")),
        name="fox",
    )(anchors, qt, ka, vt)


def _swa_kernel(q_ref, k_ref, v_ref, sink_ref, slope_ref, o_ref):
    qi = pl.program_id(1)
    tq = q_ref.shape[0]
    w = WINDOW
    n_sub = tq // w
    row = lax.broadcasted_iota(jnp.int32, (w, 2 * w), 0)
    col = lax.broadcasted_iota(jnp.int32, (w, 2 * w), 1)

    def bias_tables(offset):
        dist = offset + row - col
        valid = (dist >= 0) & (dist < w)
        dist_f = dist.astype(F32)
        return [jnp.where(valid, -(slope_ref[h] * LOG2E) * dist_f, NEG_INF) for h in range(SWA_HEADS)]

    later_bias = bias_tables(w)
    for r in range(n_sub):
        n = qi * n_sub + r
        kstart = pl.multiple_of(jnp.maximum(n - 1, 0) * w, w)
        bias = bias_tables(n * w - kstart) if r == 0 else later_bias
        for h in range(SWA_HEADS):
            g = h // SWA_GROUP
            q = q_ref[r * w:(r + 1) * w, h * HEAD_DIM:(h + 1) * HEAD_DIM]
            k = k_ref[pl.ds(kstart, 2 * w), g * HEAD_DIM:(g + 1) * HEAD_DIM]
            v = v_ref[pl.ds(kstart, 2 * w), g * HEAD_DIM:(g + 1) * HEAD_DIM]
            s = lax.dot_general(q, k, (((1,), (1,)), ((), ())), preferred_element_type=F32) + bias[h]
            sink = sink_ref[h] * LOG2E
            m = jnp.maximum(jnp.max(s, axis=-1, keepdims=True), sink)
            p = jnp.exp2(s - m)
            denom = jnp.sum(p, axis=-1, keepdims=True) + jnp.exp2(sink - m)
            o = jnp.dot(p.astype(BF16), v, preferred_element_type=F32)
            o_ref[r * w:(r + 1) * w, h * HEAD_DIM:(h + 1) * HEAD_DIM] = (o / denom).astype(o_ref.dtype)


def _swa(rest, sinks, slopes, *, batch, seq, tq=512):
    t = rest.shape[0]
    nq = seq // tq
    smem = pl.BlockSpec(memory_space=pltpu.SMEM)
    return pl.pallas_call(
        _swa_kernel,
        out_shape=jax.ShapeDtypeStruct((t, SWA_Q_W), BF16),
        grid=(batch, nq),
        in_specs=[
            pl.BlockSpec((tq, SWA_Q_W), lambda b, i: (b * nq + i, REST_SQ // SWA_Q_W)),
            pl.BlockSpec((seq, SWA_KV_W), lambda b, i: (b, REST_SK // SWA_KV_W)),
            pl.BlockSpec((seq, SWA_KV_W), lambda b, i: (b, REST_SV // SWA_KV_W)),
            smem, smem,
        ],
        out_specs=pl.BlockSpec((tq, SWA_Q_W), lambda b, i: (b * nq + i, 0)),
        compiler_params=_cparams(("parallel", "arbitrary")),
        name="swa",
    )(rest, rest, rest, sinks, slopes)


def _mem_kv_kernel(mem_ref, g_ref, wk_ref, wv_ref, kg_ref, mk_ref, mv_ref):
    mn = _rms(mem_ref[...], g_ref[...]).astype(BF16)
    k = jnp.dot(mn, wk_ref[...].astype(BF16), preferred_element_type=F32)
    v = jnp.dot(mn, wv_ref[...].astype(BF16), preferred_element_type=F32)
    for h in range(MEM_HEADS):
        hs = slice(h * HEAD_DIM, (h + 1) * HEAD_DIM)
        mk_ref[:, hs] = _rms(k[:, hs], kg_ref[...]).astype(BF16)
    mv_ref[...] = v.astype(BF16)


def _mem_kv(mem2d, gain, wk, wv, kgain, *, batch, mem_len):
    d = mem2d.shape[1]
    full = lambda shape: pl.BlockSpec(shape, lambda b: (0, 0))
    out = jax.ShapeDtypeStruct((batch * mem_len, MEM_W), BF16)
    return pl.pallas_call(
        _mem_kv_kernel,
        out_shape=(out, out),
        grid=(batch,),
        in_specs=[
            pl.BlockSpec((mem_len, d), lambda b: (b, 0)),
            full((1, d)), full((d, MEM_W)), full((d, MEM_W)), full((1, HEAD_DIM)),
        ],
        out_specs=(pl.BlockSpec((mem_len, MEM_W), lambda b: (b, 0)),
                   pl.BlockSpec((mem_len, MEM_W), lambda b: (b, 0))),
        compiler_params=_cparams(("parallel",)),
        name="mem_kv",
    )(mem2d, gain, wk, wv, kgain)


def _mem_attn_kernel(q_ref, mk_ref, mv_ref, o_ref):
    for h in range(MEM_HEADS):
        hs = slice(h * HEAD_DIM, (h + 1) * HEAD_DIM)
        s = lax.dot_general(q_ref[:, hs], mk_ref[:, hs], (((1,), (1,)), ((), ())),
                            preferred_element_type=F32)
        m = jnp.max(s, axis=-1, keepdims=True)
        p = jnp.exp2(s - m)
        denom = jnp.sum(p, axis=-1, keepdims=True)
        o = jnp.dot(p.astype(BF16), mv_ref[:, hs], preferred_element_type=F32)
        o_ref[:, hs] = (o / denom).astype(o_ref.dtype)


def _mem_attn(rest, mk, mv, *, batch, seq, mem_len, tq=512):
    t = rest.shape[0]
    nq = seq // tq
    return pl.pallas_call(
        _mem_attn_kernel,
        out_shape=jax.ShapeDtypeStruct((t, MEM_W), BF16),
        grid=(batch, nq),
        in_specs=[
            pl.BlockSpec((tq, MEM_W), lambda b, i: (b * nq + i, REST_MQ // MEM_W)),
            pl.BlockSpec((mem_len, MEM_W), lambda b, i: (b, 0)),
            pl.BlockSpec((mem_len, MEM_W), lambda b, i: (b, 0)),
        ],
        out_specs=pl.BlockSpec((tq, MEM_W), lambda b, i: (b * nq + i, 0)),
        compiler_params=_cparams(("parallel", "arbitrary")),
        name="mem_attn",
    )(rest, mk, mv)


def _out_proj_kernel(x_ref, a_ref, b_ref, c_ref, w_ref, o_ref):
    acc = jnp.dot(a_ref[...], w_ref[0:FOX_W, :], preferred_element_type=F32)
    acc += jnp.dot(b_ref[...], w_ref[FOX_W:FOX_W + SWA_Q_W, :], preferred_element_type=F32)
    acc += jnp.dot(c_ref[...], w_ref[FOX_W + SWA_Q_W:, :], preferred_element_type=F32)
    o_ref[...] = x_ref[...] + acc


def _out_proj(x, a, b, c, w, *, tm=1024):
    t, d = x.shape
    return pl.pallas_call(
        _out_proj_kernel,
        out_shape=jax.ShapeDtypeStruct((t, d), F32),
        grid=(t // tm,),
        in_specs=[
            pl.BlockSpec((tm, d), lambda i: (i, 0)),
            pl.BlockSpec((tm, FOX_W), lambda i: (i, 0)),
            pl.BlockSpec((tm, SWA_Q_W), lambda i: (i, 0)),
            pl.BlockSpec((tm, MEM_W), lambda i: (i, 0)),
            pl.BlockSpec(w.shape, lambda i: (0, 0), pipeline_mode=pl.Buffered(1)),
        ],
        out_specs=pl.BlockSpec((tm, d), lambda i: (i, 0)),
        compiler_params=_cparams(("parallel",)),
        name="out_proj",
    )(x, a, b, c, w)


def _alibi_slopes(n):
    return jnp.asarray([2.0 ** (-8.0 * i / n) for i in range(1, n + 1)], dtype=F32)


def _pack_w_in_kernel(w_ref, *rest, n_main):
    n_ride = (len(rest) - 2) // 2
    ride_in, (main_ref, f_ref), ride_out = rest[:n_ride], rest[n_ride:n_ride + 2], rest[n_ride + 2:]
    i = pl.program_id(0)
    n_a = w_ref.shape[2]

    @pl.when(i < n_main)
    def _():
        for a in range(n_a):
            main_ref[:, a * LANES:(a + 1) * LANES] = w_ref[0, :, a, :].astype(BF16)
        for src, dst in zip(ride_in, ride_out):
            dst[...] = src[...].astype(BF16)

    @pl.when(i == n_main)
    def _():
        row = lax.broadcasted_iota(jnp.int32, (f_ref.shape[0], LANES), 0)
        for a in range(n_a):
            f_ref[:, a * LANES:(a + 1) * LANES] = jnp.where(
                row < FOX_HEADS, w_ref[0, 0:f_ref.shape[0], a, :], 0.0).astype(BF16)


def _pack_w_in(w_in, l, *, ride=(), rows=256):
    d, n = w_in.shape[1:]
    fl0 = 3 * FOX_W
    assert n == PROJ_W + FOX_HEADS and fl0 % rows == 0 and PROJ_W % rows == 0 and d % LANES == 0
    wt = jnp.swapaxes(w_in, 1, 2).reshape(w_in.shape[0], n, d // LANES, LANES)
    n_main = PROJ_W // rows

    def start(i):
        main = jnp.where(i * rows < fl0, i * rows, i * rows + FOX_HEADS)
        return jnp.where(i < n_main, main, fl0)

    last_main = lambda i: (jnp.minimum(i, n_main - 1), 0)
    ride_specs = []
    for w in ride:
        assert w.shape[0] % (n_main * 2 * SUBLANES) == 0
        ride_specs.append(pl.BlockSpec((w.shape[0] // n_main, w.shape[1]), last_main))
    out = pl.pallas_call(
        functools.partial(_pack_w_in_kernel, n_main=n_main),
        out_shape=(jax.ShapeDtypeStruct((PROJ_W, d), BF16), jax.ShapeDtypeStruct((LANES, d), BF16),
                   *[jax.ShapeDtypeStruct(w.shape, BF16) for w in ride]),
        grid=(n_main + 1,),
        in_specs=[pl.BlockSpec((pl.Element(1), pl.Element(rows), pl.Element(d // LANES), pl.Element(LANES)),
                               lambda i: (l, start(i), 0, 0)),
                  *ride_specs],
        out_specs=(pl.BlockSpec((rows, d), last_main), pl.BlockSpec((LANES, d), lambda i: (0, 0)),
                   *ride_specs),
        compiler_params=_cparams(("arbitrary",)),
        name="pack_w_in",
    )(wt, *ride)
    return out[0], out[1], out[2:]


def kernel(x, mem, ffn1_norm, ffn1_gate, ffn1_up, ffn1_down, mix_norm, mem_norm, w_in, forget_bias, w_mem_k, w_mem_v, fox_q_gain, fox_k_gain, swa_q_gain, swa_k_gain, swa_sinks, mem_q_gain, mem_k_gain, w_out, ffn2_norm, ffn2_gate, ffn2_up, ffn2_down):
    batch, seq, d = x.shape
    mem_len = mem.shape[1]
    depth = w_in.shape[0]
    scale = HEAD_DIM ** -0.5
    slopes = _alibi_slopes(SWA_HEADS)
    x2 = x.reshape(batch * seq, d)
    mem2 = mem.reshape(batch * mem_len, d)
    zeros = jnp.zeros((HEAD_DIM,), F32)

    for l in range(depth):
        w_main, w_f, ffn1_w = _pack_w_in(w_in, l, ride=(ffn1_gate[l], ffn1_up[l], ffn1_down[l]))
        x2, ffn2_w = _ffn(x2, ffn1_norm[l][None], *ffn1_w,
                          next_weights=(ffn2_gate[l], ffn2_up[l], ffn2_down[l]))

        qs = scale * LOG2E
        head_gain = jnp.stack([fox_q_gain[l] * qs, fox_k_gain[l], swa_q_gain[l] * qs,
                               swa_k_gain[l], mem_q_gain[l] * qs, zeros, zeros, zeros])
        fbias = jnp.pad(forget_bias[l], (0, LANES - FOX_HEADS))[None]
        qt, ka, vt, rest, anc, w_out_bf = _in_proj(x2, mix_norm[l][None], w_main, w_f, head_gain, fbias,
                                                   w_out[l], seq=seq)

        n_sub = (x2.shape[0] // FOX_BLK) // anc.shape[0]
        anchors = anc[:, :n_sub, :SUBLANES].reshape(-1)
        out_a = _fox(anchors, qt, ka, vt, batch=batch, seq=seq)
        out_b = _swa(rest, swa_sinks[l], slopes, batch=batch, seq=seq)
        mk, mv = _mem_kv(mem2, mem_norm[l][None], w_mem_k[l], w_mem_v[l],
                         mem_k_gain[l][None], batch=batch, mem_len=mem_len)
        out_c = _mem_attn(rest, mk, mv, batch=batch, seq=seq, mem_len=mem_len)

        x2 = _out_proj(x2, out_a, out_b, out_c, w_out_bf)
        x2, _ = _ffn(x2, ffn2_norm[l][None], *ffn2_w)
    return x2.reshape(batch, seq, d)
```

```python
import functools

import jax
import jax.numpy as jnp
from jax import lax
from jax.experimental import pallas as pl
from jax.experimental.pallas import tpu as pltpu

F32 = jnp.float32
BF16 = jnp.bfloat16

HEAD_DIM = 128
FOX_HEADS = 6
SWA_HEADS = 6
SWA_KV_HEADS = 2
SWA_GROUP = SWA_HEADS // SWA_KV_HEADS
MEM_HEADS = 4
WINDOW = 128
EPS = 1e-6
NEG_INF = -1e30
LOG2E = 1.4426950408889634
FOX_BLK = 512

FOX_W = FOX_HEADS * HEAD_DIM
SWA_Q_W = SWA_HEADS * HEAD_DIM
SWA_KV_W = SWA_KV_HEADS * HEAD_DIM
MEM_W = MEM_HEADS * HEAD_DIM

PROJ_W = 3 * FOX_W + SWA_Q_W + 2 * SWA_KV_W + MEM_W
REST_W = SWA_Q_W + 3 * SWA_KV_W + MEM_W
REST_SQ, REST_SK, REST_MQ, REST_SV = 0, SWA_Q_W, SWA_Q_W + SWA_KV_W, SWA_Q_W + SWA_KV_W + MEM_W
GAIN_FQ, GAIN_FK, GAIN_SQ, GAIN_SK, GAIN_MQ = range(5)

LANES = 128
SUBLANES = 8
MXU_N = 256

VMEM_LIMIT = 58 * 1024 * 1024


def _cparams(sem):
    return pltpu.CompilerParams(dimension_semantics=sem, vmem_limit_bytes=VMEM_LIMIT)


def _rms(x, gain):
    ms = jnp.mean(x * x, axis=-1, keepdims=True)
    return x * lax.rsqrt(ms + EPS) * gain


def _ffn_kernel(x_ref, g_ref, wg_ref, wu_ref, wd_ref, *rest, n_cast):
    cast_in, o_ref, cast_out, xn_ref = rest[:n_cast], rest[n_cast], rest[n_cast + 1:-1], rest[-1]
    j = pl.program_id(1)

    @pl.when(j == 0)
    def _():
        x = x_ref[...]
        xn_ref[...] = _rms(x, g_ref[...]).astype(BF16)
        o_ref[...] = x

    xn = xn_ref[...]
    acc = None
    for c0 in range(0, wg_ref.shape[1], MXU_N):
        cs = slice(c0, c0 + MXU_N)
        gate = jnp.dot(xn, wg_ref[:, cs], preferred_element_type=F32)
        up = jnp.dot(xn, wu_ref[:, cs], preferred_element_type=F32)
        h = (gate * jax.nn.sigmoid(gate) * (0.5 * up)).astype(BF16)
        part = jnp.dot(h, wd_ref[cs, :], preferred_element_type=F32)
        acc = part if acc is None else acc + part
    o_ref[...] += acc

    for src, dst in zip(cast_in, cast_out):
        dst[...] = src[...].astype(BF16)


def _ffn(x, gain, wg, wu, wd, *, next_weights=(), tm=1024, tf=512):
    t, d = x.shape
    dff = wg.shape[1]
    gi, gj = t // tm, dff // tf
    cast_specs, cast_shapes = [], []
    for w in next_weights:
        ri, cj = (gi, gj) if w.shape[1] == dff else (gj, gi)
        blk = (w.shape[0] // ri, w.shape[1] // cj)
        assert blk[0] % SUBLANES == 0 and blk[1] % LANES == 0
        index = (lambda i, j: (i, j)) if w.shape[1] == dff else (lambda i, j: (j, i))
        cast_specs.append(pl.BlockSpec(blk, index))
        cast_shapes.append(jax.ShapeDtypeStruct(w.shape, BF16))
    out = pl.pallas_call(
        functools.partial(_ffn_kernel, n_cast=len(next_weights)),
        out_shape=(jax.ShapeDtypeStruct((t, d), F32), *cast_shapes),
        grid=(gi, gj),
        in_specs=[
            pl.BlockSpec((tm, d), lambda i, j: (i, 0)),
            pl.BlockSpec((1, d), lambda i, j: (0, 0)),
            pl.BlockSpec((d, tf), lambda i, j: (0, j)),
            pl.BlockSpec((d, tf), lambda i, j: (0, j)),
            pl.BlockSpec((tf, d), lambda i, j: (j, 0)),
            *cast_specs,
        ],
        out_specs=(pl.BlockSpec((tm, d), lambda i, j: (i, 0)), *cast_specs),
        scratch_shapes=[pltpu.VMEM((tm, d), BF16)],
        compiler_params=_cparams(("parallel", "arbitrary")),
        name="ffn",
    )(x, gain, wg, wu, wd, *next_weights)
    return out[0], out[1:]


def _in_proj_kernel(x_ref, g_ref, w_ref, wf_ref, hg_ref, fb_ref, wo_ref,
                    qt_ref, ka_ref, vt_ref, rest_ref, anc_ref, wo_bf_ref, carry_ref, *, blocks_per_seq):
    i = pl.program_id(0)
    tm = x_ref.shape[0]

    @pl.when(i == 0)
    def _():
        carry_ref[...] = jnp.zeros_like(carry_ref)

    wo_bf_ref[...] = wo_ref[...].astype(BF16)
    hn = _rms(x_ref[...], g_ref[...]).astype(BF16)

    nt = (((1,), (1,)), ((), ()))
    logit = lax.dot_general(hn, wf_ref[...], nt, preferred_element_type=F32)
    c = jax.nn.log_sigmoid(logit + fb_ref[...]) * LOG2E
    row = lax.broadcasted_iota(jnp.int32, c.shape, 0) % FOX_BLK
    sh = 1
    while sh < FOX_BLK:
        c = c + jnp.where(row >= sh, pltpu.roll(c, sh, 0), 0.0)
        sh *= 2

    anchor = jnp.where(i % blocks_per_seq == 0, 0.0, carry_ref[0:1, :])
    anc_ref[...] = jnp.zeros_like(anc_ref)
    for r in range(tm // FOX_BLK):
        anc_ref[0, r:r + 1, :] = anchor
        anchor = anchor + c[(r + 1) * FOX_BLK - 1:(r + 1) * FOX_BLK, :]
    carry_ref[0:1, :] = anchor

    n_sub = tm // FOX_BLK
    lane = lax.broadcasted_iota(jnp.int32, c.shape, 1)
    sub = lax.broadcasted_iota(jnp.int32, (HEAD_DIM, FOX_BLK), 0)
    ct = c.T
    ak = jnp.zeros(c.shape, F32)
    for h in range(FOX_HEADS):
        ch = c[:, h:h + 1]
        hi = ch.astype(BF16).astype(F32)
        lo = ch - hi
        own = lane - 4 * h
        ak = jnp.where((own == 0) | (own == 1), 1.0, jnp.where(own == 2, -hi, jnp.where(own == 3, -lo, ak)))
        for r in range(n_sub):
            cr = ct[h:h + 1, r * FOX_BLK:(r + 1) * FOX_BLK]
            hi_r = cr.astype(BF16).astype(F32)
            lo_r = cr - hi_r
            own_r = sub - 4 * h
            aq = jnp.where(own_r == 0, hi_r, jnp.where(own_r == 1, lo_r,
                                                       jnp.where((own_r == 2) | (own_r == 3), 1.0, 0.0)))
            qt_ref[r, (2 * h + 1) * HEAD_DIM:(2 * h + 2) * HEAD_DIM, :] = aq.astype(BF16)
    ka_ref[:, FOX_W:] = ak.astype(BF16)

    def normed(ph, row_id):
        return _rms(ph, hg_ref[row_id:row_id + 1, :]).astype(BF16)

    heads_per_chunk = MXU_N // HEAD_DIM
    sv_chunk = (3 * FOX_W + SWA_Q_W + SWA_KV_W) // MXU_N
    chunks = [ck for ck in range(PROJ_W // MXU_N) if ck != sv_chunk] + [sv_chunk]
    for ck in chunks:
        p = lax.dot_general(hn, w_ref[ck * MXU_N:(ck + 1) * MXU_N, :], nt, preferred_element_type=F32)
        for e in range(heads_per_chunk):
            col = ck * MXU_N + e * HEAD_DIM
            ph = p[:, e * HEAD_DIM:(e + 1) * HEAD_DIM]
            if col < FOX_W:
                h = col // HEAD_DIM
                qn = _rms(ph, hg_ref[GAIN_FQ:GAIN_FQ + 1, :])
                for r in range(n_sub):
                    qt_ref[r, 2 * h * HEAD_DIM:(2 * h + 1) * HEAD_DIM, :] = (
                        qn[r * FOX_BLK:(r + 1) * FOX_BLK, :].T.astype(BF16))
            elif col < 2 * FOX_W:
                h = (col - FOX_W) // HEAD_DIM
                ka_ref[:, h * HEAD_DIM:(h + 1) * HEAD_DIM] = normed(ph, GAIN_FK)
            elif col < 3 * FOX_W:
                h = (col - 2 * FOX_W) // HEAD_DIM
                for r in range(n_sub):
                    vt_ref[r, h * HEAD_DIM:(h + 1) * HEAD_DIM, :] = (
                        ph[r * FOX_BLK:(r + 1) * FOX_BLK, :].T.astype(BF16))
            else:
                wc = col - 3 * FOX_W
                if wc < SWA_Q_W:
                    val, rc = normed(ph, GAIN_SQ), REST_SQ + wc
                elif wc < SWA_Q_W + SWA_KV_W:
                    val, rc = normed(ph, GAIN_SK), REST_SK + wc - SWA_Q_W
                elif wc < SWA_Q_W + 2 * SWA_KV_W:
                    val, rc = ph.astype(BF16), REST_SV + 2 * (wc - SWA_Q_W - SWA_KV_W)
                    rest_ref[:, rc + HEAD_DIM:rc + 2 * HEAD_DIM] = jnp.ones((tm, HEAD_DIM), BF16)
                else:
                    val, rc = normed(ph, GAIN_MQ), REST_MQ + wc - SWA_Q_W - 2 * SWA_KV_W
                rest_ref[:, rc:rc + HEAD_DIM] = val


def _in_proj(x, gain, w, wf, head_gain, fbias, w_out, *, seq, tm=512):
    t, d = x.shape
    wo_rows = w_out.shape[0] // (t // tm)
    assert wo_rows % (2 * SUBLANES) == 0
    assert seq % tm == 0 and tm % FOX_BLK == 0 and tm // FOX_BLK <= SUBLANES
    const = lambda shape: pl.BlockSpec(shape, lambda i: (0, 0), pipeline_mode=pl.Buffered(1))
    return pl.pallas_call(
        functools.partial(_in_proj_kernel, blocks_per_seq=seq // tm),
        out_shape=(
            jax.ShapeDtypeStruct((t // FOX_BLK, 2 * FOX_W, FOX_BLK), BF16),
            jax.ShapeDtypeStruct((t, FOX_W + HEAD_DIM), BF16),
            jax.ShapeDtypeStruct((t // FOX_BLK, FOX_W, FOX_BLK), BF16),
            jax.ShapeDtypeStruct((t, REST_W), BF16),
            jax.ShapeDtypeStruct((t // tm, SUBLANES, LANES), F32),
            jax.ShapeDtypeStruct(w_out.shape, BF16),
        ),
        grid=(t // tm,),
        in_specs=[
            pl.BlockSpec((tm, d), lambda i: (i, 0)),
            const((1, d)),
            const((PROJ_W, d)),
            const((LANES, d)),
            const((SUBLANES, HEAD_DIM)),
            const((1, LANES)),
            pl.BlockSpec((wo_rows, w_out.shape[1]), lambda i: (i, 0)),
        ],
        out_specs=(
            pl.BlockSpec((tm // FOX_BLK, 2 * FOX_W, FOX_BLK), lambda i: (i, 0, 0)),
            pl.BlockSpec((tm, FOX_W + HEAD_DIM), lambda i: (i, 0)),
            pl.BlockSpec((tm // FOX_BLK, FOX_W, FOX_BLK), lambda i: (i, 0, 0)),
            pl.BlockSpec((tm, REST_W), lambda i: (i, 0)),
            pl.BlockSpec((1, SUBLANES, LANES), lambda i: (i, 0, 0)),
            pl.BlockSpec((wo_rows, w_out.shape[1]), lambda i: (i, 0)),
        ),
        scratch_shapes=[pltpu.VMEM((SUBLANES, LANES), F32)],
        compiler_params=_cparams(("arbitrary",)),
        name="in_proj",
    )(x, gain, w, wf, head_gain, fbias, w_out)


def _fox_kernel(anc_ref, q_ref, k_ref, vt_ref, o_ref, m_ref, l_ref, acc_ref, st_ref, *, nblk):
    b = pl.program_id(0)
    q_per_step, _, blk = q_ref.shape
    krow = lax.broadcasted_iota(jnp.int32, (blk, blk), 0)
    qcol = lax.broadcasted_iota(jnp.int32, (blk, blk), 1)
    lax.fori_loop(0, q_per_step, functools.partial(
        _fox_query_block, anc_ref, q_ref, k_ref, vt_ref, o_ref, m_ref, l_ref, acc_ref, st_ref,
        b, nblk, krow, qcol), 0)


def _fox_query_block(anc_ref, q_ref, k_ref, vt_ref, o_ref, m_ref, l_ref, acc_ref, st_ref,
                     b, nblk, krow, qcol, sub, carry):
    q_per_step, _, blk = q_ref.shape
    aw = 2 * HEAD_DIM
    i = pl.program_id(1) * q_per_step + sub
    m_ref[...] = jnp.full(m_ref.shape, NEG_INF, F32)
    l_ref[...] = jnp.zeros(l_ref.shape, F32)
    acc_ref[...] = jnp.zeros(acc_ref.shape, F32)

    def scores(j, slot):
        ks = pl.ds(pl.multiple_of(j * blk, blk), blk)
        k_bias = k_ref[ks, FOX_W:]
        for h in range(FOX_HEADS):
            k_aug = jnp.concatenate([k_ref[ks, h * HEAD_DIM:(h + 1) * HEAD_DIM], k_bias], axis=1)
            st_ref[slot, h] = jnp.dot(k_aug, q_ref[sub, h * aw:(h + 1) * aw, :],
                                      preferred_element_type=F32)

    def softmax_pv(j, slot, masked):
        for h in range(FOX_HEADS):
            st = st_ref[slot, h]
            d = anc_ref[(b * nblk + i) * SUBLANES + h] - anc_ref[(b * nblk + j) * SUBLANES + h]
            if masked:
                st = jnp.where(krow <= qcol, st, NEG_INF)
            m_prev = m_ref[h]
            m_new = jnp.maximum(m_prev, jnp.max(st, axis=0, keepdims=True) + d)
            alpha = jnp.exp2(m_prev - m_new)
            p = jnp.exp2(st - (m_new - d))
            l_ref[h] = alpha * l_ref[h] + jnp.sum(p, axis=0, keepdims=True)
            pv = jnp.dot(vt_ref[j, h * HEAD_DIM:(h + 1) * HEAD_DIM, :], p.astype(BF16),
                         preferred_element_type=F32)
            acc_ref[h] = alpha * acc_ref[h] + pv
            m_ref[h] = m_new

    def body(t, c):
        j = 2 * t
        scores(j + 1, 1)
        softmax_pv(j, 0, False)
        scores(j + 2, 0)
        softmax_pv(j + 1, 1, False)
        return c

    scores(0, 0)
    lax.fori_loop(0, i // 2, body, 0)

    @pl.when(i % 2 == 0)
    def _():
        softmax_pv(i, 0, True)

    @pl.when(i % 2 == 1)
    def _():
        scores(i, 1)
        softmax_pv(i - 1, 0, False)
        softmax_pv(i, 1, True)

    rows = pl.ds(pl.multiple_of(sub * blk, blk), blk)
    for h in range(FOX_HEADS):
        o_ref[rows, h * HEAD_DIM:(h + 1) * HEAD_DIM] = (acc_ref[h] / l_ref[h]).T.astype(o_ref.dtype)
    return carry


def _fox(anchors, qt, ka, vt, *, batch, seq, q_per_step=2):
    t = ka.shape[0]
    blk = FOX_BLK
    nblk = seq // blk
    steps = nblk // q_per_step
    return pl.pallas_call(
        functools.partial(_fox_kernel, nblk=nblk),
        out_shape=jax.ShapeDtypeStruct((t, FOX_W), BF16),
        grid=(batch, steps),
        in_specs=[
            pl.BlockSpec(memory_space=pltpu.SMEM),
            pl.BlockSpec((q_per_step, 2 * FOX_W, blk), lambda b, i: (b * steps + i, 0, 0)),
            pl.BlockSpec((seq, FOX_W + HEAD_DIM), lambda b, i: (b, 0)),
            pl.BlockSpec((nblk, FOX_W, blk), lambda b, i: (b, 0, 0)),
        ],
        out_specs=pl.BlockSpec((q_per_step * blk, FOX_W), lambda b, i: (b * steps + i, 0)),
        scratch_shapes=[
            pltpu.VMEM((FOX_HEADS, 1, blk), F32),
            pltpu.VMEM((FOX_HEADS, 1, blk), F32),
            pltpu.VMEM((FOX_HEADS, HEAD_DIM, blk), F32),
            pltpu.VMEM((2, FOX_HEADS, blk, blk), F32),
        ],
        compiler_params=_cparams(("parallel", "arbitrary")),
        name="fox",
    )(anchors, qt, ka, vt)


def _swa_kernel(q_ref, k_ref, v_ref, sink_ref, slope_ref, o_ref):
    qi = pl.program_id(1)
    tq = q_ref.shape[0]
    w = WINDOW
    n_sub = tq // w
    row = lax.broadcasted_iota(jnp.int32, (w, 2 * w), 0)
    col = lax.broadcasted_iota(jnp.int32, (w, 2 * w), 1)

    def bias_tables(offset):
        dist = offset + row - col
        valid = (dist >= 0) & (dist < w)
        dist_f = dist.astype(F32)
        return [jnp.where(valid, -(slope_ref[h] * LOG2E) * dist_f, NEG_INF) for h in range(SWA_HEADS)]

    later_bias = bias_tables(w)
    for r in range(n_sub):
        n = qi * n_sub + r
        kstart = pl.multiple_of(jnp.maximum(n - 1, 0) * w, w)
        bias = bias_tables(n * w - kstart) if r == 0 else later_bias
        for h in range(SWA_HEADS):
            g = h // SWA_GROUP
            q = q_ref[r * w:(r + 1) * w, h * HEAD_DIM:(h + 1) * HEAD_DIM]
            k = k_ref[pl.ds(kstart, 2 * w), g * HEAD_DIM:(g + 1) * HEAD_DIM]
            v1 = v_ref[pl.ds(kstart, 2 * w), 2 * g * HEAD_DIM:2 * (g + 1) * HEAD_DIM]
            s = lax.dot_general(q, k, (((1,), (1,)), ((), ())), preferred_element_type=F32) + bias[h]
            sink = sink_ref[h] * LOG2E
            m = jnp.maximum(jnp.max(s, axis=-1, keepdims=True), sink)
            p = jnp.exp2(s - m)
            o1 = jnp.dot(p.astype(BF16), v1, preferred_element_type=F32)
            denom = o1[:, HEAD_DIM:] + jnp.exp2(sink - m)
            o_ref[r * w:(r + 1) * w, h * HEAD_DIM:(h + 1) * HEAD_DIM] = (
                o1[:, :HEAD_DIM] / denom).astype(o_ref.dtype)


def _swa(rest, sinks, slopes, *, batch, seq, tq=512):
    t = rest.shape[0]
    nq = seq // tq
    smem = pl.BlockSpec(memory_space=pltpu.SMEM)
    return pl.pallas_call(
        _swa_kernel,
        out_shape=jax.ShapeDtypeStruct((t, SWA_Q_W), BF16),
        grid=(batch, nq),
        in_specs=[
            pl.BlockSpec((tq, SWA_Q_W), lambda b, i: (b * nq + i, REST_SQ // SWA_Q_W)),
            pl.BlockSpec((seq, SWA_KV_W), lambda b, i: (b, REST_SK // SWA_KV_W)),
            pl.BlockSpec((seq, 2 * SWA_KV_W), lambda b, i: (b, REST_SV // (2 * SWA_KV_W))),
            smem, smem,
        ],
        out_specs=pl.BlockSpec((tq, SWA_Q_W), lambda b, i: (b * nq + i, 0)),
        compiler_params=_cparams(("parallel", "arbitrary")),
        name="swa",
    )(rest, rest, rest, sinks, slopes)


def _mem_kv_kernel(mem_ref, g_ref, wk_ref, wv_ref, kg_ref, mk_ref, mv_ref):
    mn = _rms(mem_ref[...], g_ref[...]).astype(BF16)
    k = jnp.dot(mn, wk_ref[...].astype(BF16), preferred_element_type=F32)
    v = jnp.dot(mn, wv_ref[...].astype(BF16), preferred_element_type=F32)
    ones = jnp.ones((mem_ref.shape[0], HEAD_DIM), BF16)
    for h in range(MEM_HEADS):
        hs = slice(h * HEAD_DIM, (h + 1) * HEAD_DIM)
        mk_ref[:, hs] = _rms(k[:, hs], kg_ref[...]).astype(BF16)
        mv_ref[:, 2 * h * HEAD_DIM:(2 * h + 1) * HEAD_DIM] = v[:, hs].astype(BF16)
        mv_ref[:, (2 * h + 1) * HEAD_DIM:(2 * h + 2) * HEAD_DIM] = ones


def _mem_kv(mem2d, gain, wk, wv, kgain, *, batch, mem_len):
    d = mem2d.shape[1]
    full = lambda shape: pl.BlockSpec(shape, lambda b: (0, 0))
    return pl.pallas_call(
        _mem_kv_kernel,
        out_shape=(jax.ShapeDtypeStruct((batch * mem_len, MEM_W), BF16),
                   jax.ShapeDtypeStruct((batch * mem_len, 2 * MEM_W), BF16)),
        grid=(batch,),
        in_specs=[
            pl.BlockSpec((mem_len, d), lambda b: (b, 0)),
            full((1, d)), full((d, MEM_W)), full((d, MEM_W)), full((1, HEAD_DIM)),
        ],
        out_specs=(pl.BlockSpec((mem_len, MEM_W), lambda b: (b, 0)),
                   pl.BlockSpec((mem_len, 2 * MEM_W), lambda b: (b, 0))),
        compiler_params=_cparams(("parallel",)),
        name="mem_kv",
    )(mem2d, gain, wk, wv, kgain)


def _mem_attn_kernel(q_ref, mk_ref, mv_ref, o_ref):
    for h in range(MEM_HEADS):
        hs = slice(h * HEAD_DIM, (h + 1) * HEAD_DIM)
        s = lax.dot_general(q_ref[:, hs], mk_ref[:, hs], (((1,), (1,)), ((), ())),
                            preferred_element_type=F32)
        m = jnp.max(s, axis=-1, keepdims=True)
        p = jnp.exp2(s - m)
        o1 = jnp.dot(p.astype(BF16), mv_ref[:, 2 * h * HEAD_DIM:(2 * h + 2) * HEAD_DIM],
                     preferred_element_type=F32)
        o_ref[:, hs] = (o1[:, :HEAD_DIM] / o1[:, HEAD_DIM:]).astype(o_ref.dtype)


def _mem_attn(rest, mk, mv, *, batch, seq, mem_len, tq=512):
    t = rest.shape[0]
    nq = seq // tq
    return pl.pallas_call(
        _mem_attn_kernel,
        out_shape=jax.ShapeDtypeStruct((t, MEM_W), BF16),
        grid=(batch, nq),
        in_specs=[
            pl.BlockSpec((tq, MEM_W), lambda b, i: (b * nq + i, REST_MQ // MEM_W)),
            pl.BlockSpec((mem_len, MEM_W), lambda b, i: (b, 0)),
            pl.BlockSpec((mem_len, 2 * MEM_W), lambda b, i: (b, 0)),
        ],
        out_specs=pl.BlockSpec((tq, MEM_W), lambda b, i: (b * nq + i, 0)),
        compiler_params=_cparams(("parallel", "arbitrary")),
        name="mem_attn",
    )(rest, mk, mv)


def _out_proj_kernel(x_ref, a_ref, b_ref, c_ref, w_ref, o_ref):
    acc = jnp.dot(a_ref[...], w_ref[0:FOX_W, :], preferred_element_type=F32)
    acc += jnp.dot(b_ref[...], w_ref[FOX_W:FOX_W + SWA_Q_W, :], preferred_element_type=F32)
    acc += jnp.dot(c_ref[...], w_ref[FOX_W + SWA_Q_W:, :], preferred_element_type=F32)
    o_ref[...] = x_ref[...] + acc


def _out_proj(x, a, b, c, w, *, tm=1024):
    t, d = x.shape
    return pl.pallas_call(
        _out_proj_kernel,
        out_shape=jax.ShapeDtypeStruct((t, d), F32),
        grid=(t // tm,),
        in_specs=[
            pl.BlockSpec((tm, d), lambda i: (i, 0)),
            pl.BlockSpec((tm, FOX_W), lambda i: (i, 0)),
            pl.BlockSpec((tm, SWA_Q_W), lambda i: (i, 0)),
            pl.BlockSpec((tm, MEM_W), lambda i: (i, 0)),
            pl.BlockSpec(w.shape, lambda i: (0, 0), pipeline_mode=pl.Buffered(1)),
        ],
        out_specs=pl.BlockSpec((tm, d), lambda i: (i, 0)),
        compiler_params=_cparams(("parallel",)),
        name="out_proj",
    )(x, a, b, c, w)


def _alibi_slopes(n):
    return jnp.asarray([2.0 ** (-8.0 * i / n) for i in range(1, n + 1)], dtype=F32)


def _pack_w_in_kernel(w_ref, *rest, n_main):
    n_ride = (len(rest) - 2) // 2
    ride_in, (main_ref, f_ref), ride_out = rest[:n_ride], rest[n_ride:n_ride + 2], rest[n_ride + 2:]
    i = pl.program_id(0)
    n_a = w_ref.shape[2]

    @pl.when(i < n_main)
    def _():
        for a in range(n_a):
            main_ref[:, a * LANES:(a + 1) * LANES] = w_ref[0, :, a, :].astype(BF16)
        for src, dst in zip(ride_in, ride_out):
            dst[...] = src[...].astype(BF16)

    @pl.when(i == n_main)
    def _():
        row = lax.broadcasted_iota(jnp.int32, (f_ref.shape[0], LANES), 0)
        for a in range(n_a):
            f_ref[:, a * LANES:(a + 1) * LANES] = jnp.where(
                row < FOX_HEADS, w_ref[0, 0:f_ref.shape[0], a, :], 0.0).astype(BF16)


def _pack_w_in(w_in, l, *, ride=(), rows=256):
    d, n = w_in.shape[1:]
    fl0 = 3 * FOX_W
    assert n == PROJ_W + FOX_HEADS and fl0 % rows == 0 and PROJ_W % rows == 0 and d % LANES == 0
    wt = jnp.swapaxes(w_in, 1, 2).reshape(w_in.shape[0], n, d // LANES, LANES)
    n_main = PROJ_W // rows

    def start(i):
        main = jnp.where(i * rows < fl0, i * rows, i * rows + FOX_HEADS)
        return jnp.where(i < n_main, main, fl0)

    last_main = lambda i: (jnp.minimum(i, n_main - 1), 0)
    ride_specs = []
    for w in ride:
        assert w.shape[0] % (n_main * 2 * SUBLANES) == 0
        ride_specs.append(pl.BlockSpec((w.shape[0] // n_main, w.shape[1]), last_main))
    out = pl.pallas_call(
        functools.partial(_pack_w_in_kernel, n_main=n_main),
        out_shape=(jax.ShapeDtypeStruct((PROJ_W, d), BF16), jax.ShapeDtypeStruct((LANES, d), BF16),
                   *[jax.ShapeDtypeStruct(w.shape, BF16) for w in ride]),
        grid=(n_main + 1,),
        in_specs=[pl.BlockSpec((pl.Element(1), pl.Element(rows), pl.Element(d // LANES), pl.Element(LANES)),
                               lambda i: (l, start(i), 0, 0)),
                  *ride_specs],
        out_specs=(pl.BlockSpec((rows, d), last_main), pl.BlockSpec((LANES, d), lambda i: (0, 0)),
                   *ride_specs),
        compiler_params=_cparams(("arbitrary",)),
        name="pack_w_in",
    )(wt, *ride)
    return out[0], out[1], out[2:]


def kernel(x, mem, ffn1_norm, ffn1_gate, ffn1_up, ffn1_down, mix_norm, mem_norm, w_in, forget_bias, w_mem_k, w_mem_v, fox_q_gain, fox_k_gain, swa_q_gain, swa_k_gain, swa_sinks, mem_q_gain, mem_k_gain, w_out, ffn2_norm, ffn2_gate, ffn2_up, ffn2_down):
    batch, seq, d = x.shape
    mem_len = mem.shape[1]
    depth = w_in.shape[0]
    scale = HEAD_DIM ** -0.5
    slopes = _alibi_slopes(SWA_HEADS)
    x2 = x.reshape(batch * seq, d)
    mem2 = mem.reshape(batch * mem_len, d)
    zeros = jnp.zeros((HEAD_DIM,), F32)

    for l in range(depth):
        w_main, w_f, ffn1_w = _pack_w_in(w_in, l, ride=(ffn1_gate[l], ffn1_up[l], ffn1_down[l]))
        x2, ffn2_w = _ffn(x2, ffn1_norm[l][None], *ffn1_w,
                          next_weights=(ffn2_gate[l], ffn2_up[l], ffn2_down[l]))

        qs = scale * LOG2E
        head_gain = jnp.stack([fox_q_gain[l] * qs, fox_k_gain[l], swa_q_gain[l] * qs,
                               swa_k_gain[l], mem_q_gain[l] * qs, zeros, zeros, zeros])
        fbias = jnp.pad(forget_bias[l], (0, LANES - FOX_HEADS))[None]
        qt, ka, vt, rest, anc, w_out_bf = _in_proj(x2, mix_norm[l][None], w_main, w_f, head_gain, fbias,
                                                   w_out[l], seq=seq)

        n_sub = (x2.shape[0] // FOX_BLK) // anc.shape[0]
        anchors = anc[:, :n_sub, :SUBLANES].reshape(-1)
        out_a = _fox(anchors, qt, ka, vt, batch=batch, seq=seq)
        out_b = _swa(rest, swa_sinks[l], slopes, batch=batch, seq=seq)
        mk, mv = _mem_kv(mem2, mem_norm[l][None], w_mem_k[l], w_mem_v[l],
                         mem_k_gain[l][None], batch=batch, mem_len=mem_len)
        out_c = _mem_attn(rest, mk, mv, batch=batch, seq=seq, mem_len=mem_len)

        x2 = _out_proj(x2, out_a, out_b, out_c, w_out_bf)
        x2, _ = _ffn(x2, ffn2_norm[l][None], *ffn2_w)
    return x2.reshape(batch, seq, d)
```

```python
import functools

import jax
import jax.numpy as jnp
from jax import lax
from jax.experimental import pallas as pl
from jax.experimental.pallas import tpu as pltpu

F32 = jnp.float32
BF16 = jnp.bfloat16

HEAD_DIM = 128
FOX_HEADS = 6
SWA_HEADS = 6
SWA_KV_HEADS = 2
SWA_GROUP = SWA_HEADS // SWA_KV_HEADS
MEM_HEADS = 4
WINDOW = 128
EPS = 1e-6
NEG_INF = -1e30
LOG2E = 1.4426950408889634
FOX_BLK = 512
BIAS_LANES = 4

FOX_W = FOX_HEADS * HEAD_DIM
SWA_Q_W = SWA_HEADS * HEAD_DIM
SWA_KV_W = SWA_KV_HEADS * HEAD_DIM
MEM_W = MEM_HEADS * HEAD_DIM

PROJ_W = 3 * FOX_W + SWA_Q_W + 2 * SWA_KV_W + MEM_W
REST_W = SWA_Q_W + 3 * SWA_KV_W + MEM_W
REST_SQ, REST_SK, REST_MQ, REST_SV = 0, SWA_Q_W, SWA_Q_W + SWA_KV_W, SWA_Q_W + SWA_KV_W + MEM_W
GAIN_FQ, GAIN_FK, GAIN_SQ, GAIN_SK, GAIN_MQ = range(5)

LANES = 128
SUBLANES = 8
MXU_N = 256

VMEM_LIMIT = 58 * 1024 * 1024


def _cparams(sem):
    return pltpu.CompilerParams(dimension_semantics=sem, vmem_limit_bytes=VMEM_LIMIT)


def _rms(x, gain):
    ms = jnp.mean(x * x, axis=-1, keepdims=True)
    return x * lax.rsqrt(ms + EPS) * gain


def _ffn_kernel(x_ref, g_ref, wg_ref, wu_ref, wd_ref, *rest, n_cast):
    cast_in, o_ref, cast_out, xn_ref = rest[:n_cast], rest[n_cast], rest[n_cast + 1:-1], rest[-1]
    j = pl.program_id(1)

    @pl.when(j == 0)
    def _():
        x = x_ref[...]
        xn_ref[...] = _rms(x, g_ref[...]).astype(BF16)
        o_ref[...] = x

    xn = xn_ref[...]
    acc = None
    for c0 in range(0, wg_ref.shape[1], MXU_N):
        cs = slice(c0, c0 + MXU_N)
        gate = jnp.dot(xn, wg_ref[:, cs], preferred_element_type=F32)
        up = jnp.dot(xn, wu_ref[:, cs], preferred_element_type=F32)
        h = (gate * jax.nn.sigmoid(gate) * (0.5 * up)).astype(BF16)
        part = jnp.dot(h, wd_ref[cs, :], preferred_element_type=F32)
        acc = part if acc is None else acc + part
    o_ref[...] += acc

    for src, dst in zip(cast_in, cast_out):
        dst[...] = src[...].astype(BF16)


def _ffn(x, gain, wg, wu, wd, *, next_weights=(), tm=1024, tf=512):
    t, d = x.shape
    dff = wg.shape[1]
    gi, gj = t // tm, dff // tf
    cast_specs, cast_shapes = [], []
    for w in next_weights:
        ri, cj = (gi, gj) if w.shape[1] == dff else (gj, gi)
        blk = (w.shape[0] // ri, w.shape[1] // cj)
        assert blk[0] % SUBLANES == 0 and blk[1] % LANES == 0
        index = (lambda i, j: (i, j)) if w.shape[1] == dff else (lambda i, j: (j, i))
        cast_specs.append(pl.BlockSpec(blk, index))
        cast_shapes.append(jax.ShapeDtypeStruct(w.shape, BF16))
    out = pl.pallas_call(
        functools.partial(_ffn_kernel, n_cast=len(next_weights)),
        out_shape=(jax.ShapeDtypeStruct((t, d), F32), *cast_shapes),
        grid=(gi, gj),
        in_specs=[
            pl.BlockSpec((tm, d), lambda i, j: (i, 0)),
            pl.BlockSpec((1, d), lambda i, j: (0, 0)),
            pl.BlockSpec((d, tf), lambda i, j: (0, j)),
            pl.BlockSpec((d, tf), lambda i, j: (0, j)),
            pl.BlockSpec((tf, d), lambda i, j: (j, 0)),
            *cast_specs,
        ],
        out_specs=(pl.BlockSpec((tm, d), lambda i, j: (i, 0)), *cast_specs),
        scratch_shapes=[pltpu.VMEM((tm, d), BF16)],
        compiler_params=_cparams(("parallel", "arbitrary")),
        name="ffn",
    )(x, gain, wg, wu, wd, *next_weights)
    return out[0], out[1:]


def _in_proj_kernel(x_ref, g_ref, w_ref, wf_ref, hg_ref, fb_ref, wo_ref,
                    qt_ref, ka_ref, vt_ref, rest_ref, anc_ref, wo_bf_ref, carry_ref, *, blocks_per_seq):
    i = pl.program_id(0)
    tm = x_ref.shape[0]

    @pl.when(i == 0)
    def _():
        carry_ref[...] = jnp.zeros_like(carry_ref)

    wo_bf_ref[...] = wo_ref[...].astype(BF16)
    hn = _rms(x_ref[...], g_ref[...]).astype(BF16)

    nt = (((1,), (1,)), ((), ()))
    logit = lax.dot_general(hn, wf_ref[...], nt, preferred_element_type=F32)
    c = jax.nn.log_sigmoid(logit + fb_ref[...]) * LOG2E
    row = lax.broadcasted_iota(jnp.int32, c.shape, 0) % FOX_BLK
    sh = 1
    while sh < FOX_BLK:
        c = c + jnp.where(row >= sh, pltpu.roll(c, sh, 0), 0.0)
        sh *= 2

    anchor = jnp.where(i % blocks_per_seq == 0, 0.0, carry_ref[0:1, :])
    anc_ref[...] = jnp.zeros_like(anc_ref)
    for r in range(tm // FOX_BLK):
        anc_ref[0, r:r + 1, :] = anchor
        anchor = anchor + c[(r + 1) * FOX_BLK - 1:(r + 1) * FOX_BLK, :]
    carry_ref[0:1, :] = anchor

    n_sub = tm // FOX_BLK
    lane = lax.broadcasted_iota(jnp.int32, c.shape, 1)
    sub = lax.broadcasted_iota(jnp.int32, (HEAD_DIM, FOX_BLK), 0)
    ct = c.T
    ak = jnp.zeros(c.shape, F32)
    for h in range(FOX_HEADS):
        ch = c[:, h:h + 1]
        hi = ch.astype(BF16).astype(F32)
        lo = ch - hi
        own = lane - BIAS_LANES * h
        ak = jnp.where((own == 0) | (own == 1), 1.0, jnp.where(own == 2, -hi, jnp.where(own == 3, -lo, ak)))
        for r in range(n_sub):
            cr = ct[h:h + 1, r * FOX_BLK:(r + 1) * FOX_BLK]
            hi_r = cr.astype(BF16).astype(F32)
            lo_r = cr - hi_r
            own_r = sub - BIAS_LANES * h
            aq = jnp.where(own_r == 0, hi_r, jnp.where(own_r == 1, lo_r,
                                                       jnp.where((own_r == 2) | (own_r == 3), 1.0, 0.0)))
            qt_ref[r, (2 * h + 1) * HEAD_DIM:(2 * h + 2) * HEAD_DIM, :] = aq.astype(BF16)
    ka_ref[:, FOX_W:] = ak.astype(BF16)

    def normed(ph, row_id):
        return _rms(ph, hg_ref[row_id:row_id + 1, :]).astype(BF16)

    heads_per_chunk = MXU_N // HEAD_DIM
    sv_chunk = (3 * FOX_W + SWA_Q_W + SWA_KV_W) // MXU_N
    chunks = [ck for ck in range(PROJ_W // MXU_N) if ck != sv_chunk] + [sv_chunk]
    for ck in chunks:
        p = lax.dot_general(hn, w_ref[ck * MXU_N:(ck + 1) * MXU_N, :], nt, preferred_element_type=F32)
        for e in range(heads_per_chunk):
            col = ck * MXU_N + e * HEAD_DIM
            ph = p[:, e * HEAD_DIM:(e + 1) * HEAD_DIM]
            if col < FOX_W:
                h = col // HEAD_DIM
                qn = _rms(ph, hg_ref[GAIN_FQ:GAIN_FQ + 1, :])
                for r in range(n_sub):
                    qt_ref[r, 2 * h * HEAD_DIM:(2 * h + 1) * HEAD_DIM, :] = (
                        qn[r * FOX_BLK:(r + 1) * FOX_BLK, :].T.astype(BF16))
            elif col < 2 * FOX_W:
                h = (col - FOX_W) // HEAD_DIM
                ka_ref[:, h * HEAD_DIM:(h + 1) * HEAD_DIM] = normed(ph, GAIN_FK)
            elif col < 3 * FOX_W:
                h = (col - 2 * FOX_W) // HEAD_DIM
                for r in range(n_sub):
                    vt_ref[r, h * HEAD_DIM:(h + 1) * HEAD_DIM, :] = (
                        ph[r * FOX_BLK:(r + 1) * FOX_BLK, :].T.astype(BF16))
            else:
                wc = col - 3 * FOX_W
                if wc < SWA_Q_W:
                    val, rc = normed(ph, GAIN_SQ), REST_SQ + wc
                elif wc < SWA_Q_W + SWA_KV_W:
                    val, rc = normed(ph, GAIN_SK), REST_SK + wc - SWA_Q_W
                elif wc < SWA_Q_W + 2 * SWA_KV_W:
                    val, rc = ph.astype(BF16), REST_SV + 2 * (wc - SWA_Q_W - SWA_KV_W)
                    rest_ref[:, rc + HEAD_DIM:rc + 2 * HEAD_DIM] = jnp.ones((tm, HEAD_DIM), BF16)
                else:
                    val, rc = normed(ph, GAIN_MQ), REST_MQ + wc - SWA_Q_W - 2 * SWA_KV_W
                rest_ref[:, rc:rc + HEAD_DIM] = val


def _in_proj(x, gain, w, wf, head_gain, fbias, w_out, *, seq, tm=512):
    t, d = x.shape
    wo_rows = w_out.shape[0] // (t // tm)
    assert wo_rows % (2 * SUBLANES) == 0
    assert seq % tm == 0 and tm % FOX_BLK == 0 and tm // FOX_BLK <= SUBLANES
    const = lambda shape: pl.BlockSpec(shape, lambda i: (0, 0), pipeline_mode=pl.Buffered(1))
    return pl.pallas_call(
        functools.partial(_in_proj_kernel, blocks_per_seq=seq // tm),
        out_shape=(
            jax.ShapeDtypeStruct((t // FOX_BLK, 2 * FOX_W, FOX_BLK), BF16),
            jax.ShapeDtypeStruct((t, FOX_W + HEAD_DIM), BF16),
            jax.ShapeDtypeStruct((t // FOX_BLK, FOX_W, FOX_BLK), BF16),
            jax.ShapeDtypeStruct((t, REST_W), BF16),
            jax.ShapeDtypeStruct((t // tm, SUBLANES, LANES), F32),
            jax.ShapeDtypeStruct(w_out.shape, BF16),
        ),
        grid=(t // tm,),
        in_specs=[
            pl.BlockSpec((tm, d), lambda i: (i, 0)),
            const((1, d)),
            const((PROJ_W, d)),
            const((LANES, d)),
            const((SUBLANES, HEAD_DIM)),
            const((1, LANES)),
            pl.BlockSpec((wo_rows, w_out.shape[1]), lambda i: (i, 0)),
        ],
        out_specs=(
            pl.BlockSpec((tm // FOX_BLK, 2 * FOX_W, FOX_BLK), lambda i: (i, 0, 0)),
            pl.BlockSpec((tm, FOX_W + HEAD_DIM), lambda i: (i, 0)),
            pl.BlockSpec((tm // FOX_BLK, FOX_W, FOX_BLK), lambda i: (i, 0, 0)),
            pl.BlockSpec((tm, REST_W), lambda i: (i, 0)),
            pl.BlockSpec((1, SUBLANES, LANES), lambda i: (i, 0, 0)),
            pl.BlockSpec((wo_rows, w_out.shape[1]), lambda i: (i, 0)),
        ),
        scratch_shapes=[pltpu.VMEM((SUBLANES, LANES), F32)],
        compiler_params=_cparams(("arbitrary",)),
        name="in_proj",
    )(x, gain, w, wf, head_gain, fbias, w_out)


def _fox_kernel(anc_ref, q_ref, k_ref, vt_ref, o_ref, m_ref, l_ref, acc_ref, st_ref, *, nblk):
    lax.fori_loop(0, q_ref.shape[0], functools.partial(
        _fox_query_block, anc_ref, q_ref, k_ref, vt_ref, o_ref, m_ref, l_ref, acc_ref, st_ref, nblk), 0)


def _fox_query_block(anc_ref, q_ref, k_ref, vt_ref, o_ref, m_ref, l_ref, acc_ref, st_ref, nblk, sub, carry):
    q_per_step, _, blk = q_ref.shape
    aw = 2 * HEAD_DIM
    b = pl.program_id(0)
    i = pl.program_id(1) * q_per_step + sub
    m_ref[...] = jnp.full(m_ref.shape, NEG_INF, F32)
    l_ref[...] = jnp.zeros(l_ref.shape, F32)
    acc_ref[...] = jnp.zeros(acc_ref.shape, F32)

    def scores(j, slot):
        ks = pl.ds(pl.multiple_of(j * blk, blk), blk)
        k_bias = k_ref[ks, FOX_W:]
        for h in range(FOX_HEADS):
            k_aug = jnp.concatenate([k_ref[ks, h * HEAD_DIM:(h + 1) * HEAD_DIM], k_bias], axis=1)
            st_ref[slot, h] = jnp.dot(k_aug, q_ref[sub, h * aw:(h + 1) * aw, :],
                                      preferred_element_type=F32)

    def softmax_pv(j, slot, masked):
        if masked:
            causal = (lax.broadcasted_iota(jnp.int32, (blk, blk), 0)
                      <= lax.broadcasted_iota(jnp.int32, (blk, blk), 1))
        for h in range(FOX_HEADS):
            st = st_ref[slot, h]
            d = anc_ref[(b * nblk + i) * SUBLANES + h] - anc_ref[(b * nblk + j) * SUBLANES + h]
            if masked:
                st = jnp.where(causal, st, NEG_INF)
            m_prev = m_ref[h]
            m_new = jnp.maximum(m_prev, jnp.max(st, axis=0, keepdims=True) + d)
            alpha = jnp.exp2(m_prev - m_new)
            p = jnp.exp2(st - (m_new - d))
            l_ref[h] = alpha * l_ref[h] + jnp.sum(p, axis=0, keepdims=True)
            pv = jnp.dot(vt_ref[j, h * HEAD_DIM:(h + 1) * HEAD_DIM, :], p.astype(BF16),
                         preferred_element_type=F32)
            acc_ref[h] = alpha * acc_ref[h] + pv
            m_ref[h] = m_new

    def body(t, c):
        j = 2 * t
        scores(j + 1, 1)
        softmax_pv(j, 0, False)
        scores(j + 2, 0)
        softmax_pv(j + 1, 1, False)
        return c

    scores(0, 0)
    lax.fori_loop(0, i // 2, body, 0)

    @pl.when(i % 2 == 0)
    def _():
        softmax_pv(i, 0, True)

    @pl.when(i % 2 == 1)
    def _():
        scores(i, 1)
        softmax_pv(i - 1, 0, False)
        softmax_pv(i, 1, True)

    rows = pl.ds(pl.multiple_of(sub * blk, blk), blk)
    for h in range(FOX_HEADS):
        o_ref[rows, h * HEAD_DIM:(h + 1) * HEAD_DIM] = (acc_ref[h] / l_ref[h]).T.astype(o_ref.dtype)
    return carry


def _fox(anchors, qt, ka, vt, *, batch, seq, q_per_step=2):
    t = ka.shape[0]
    blk = FOX_BLK
    nblk = seq // blk
    steps = nblk // q_per_step
    return pl.pallas_call(
        functools.partial(_fox_kernel, nblk=nblk),
        out_shape=jax.ShapeDtypeStruct((t, FOX_W), BF16),
        grid=(batch, steps),
        in_specs=[
            pl.BlockSpec(memory_space=pltpu.SMEM),
            pl.BlockSpec((q_per_step, 2 * FOX_W, blk), lambda b, i: (b * steps + i, 0, 0)),
            pl.BlockSpec((seq, FOX_W + HEAD_DIM), lambda b, i: (b, 0)),
            pl.BlockSpec((nblk, FOX_W, blk), lambda b, i: (b, 0, 0)),
        ],
        out_specs=pl.BlockSpec((q_per_step * blk, FOX_W), lambda b, i: (b * steps + i, 0)),
        scratch_shapes=[
            pltpu.VMEM((FOX_HEADS, 1, blk), F32),
            pltpu.VMEM((FOX_HEADS, 1, blk), F32),
            pltpu.VMEM((FOX_HEADS, HEAD_DIM, blk), F32),
            pltpu.VMEM((2, FOX_HEADS, blk, blk), F32),
        ],
        compiler_params=_cparams(("parallel", "arbitrary")),
        name="fox",
    )(anchors, qt, ka, vt)


def _swa_kernel(q_ref, k_ref, v_ref, sink_ref, slope_ref, o_ref):
    qi = pl.program_id(1)
    tq = q_ref.shape[0]
    w = WINDOW
    n_sub = tq // w
    row = lax.broadcasted_iota(jnp.int32, (w, 2 * w), 0)
    col = lax.broadcasted_iota(jnp.int32, (w, 2 * w), 1)

    def bias_tables(offset):
        dist = offset + row - col
        valid = (dist >= 0) & (dist < w)
        dist_f = dist.astype(F32)
        return [jnp.where(valid, -(slope_ref[h] * LOG2E) * dist_f, NEG_INF) for h in range(SWA_HEADS)]

    later_bias = bias_tables(w)
    for r in range(n_sub):
        n = qi * n_sub + r
        kstart = pl.multiple_of(jnp.maximum(n - 1, 0) * w, w)
        bias = bias_tables(n * w - kstart) if r == 0 else later_bias
        for h in range(SWA_HEADS):
            g = h // SWA_GROUP
            q = q_ref[r * w:(r + 1) * w, h * HEAD_DIM:(h + 1) * HEAD_DIM]
            k = k_ref[pl.ds(kstart, 2 * w), g * HEAD_DIM:(g + 1) * HEAD_DIM]
            v1 = v_ref[pl.ds(kstart, 2 * w), 2 * g * HEAD_DIM:2 * (g + 1) * HEAD_DIM]
            s = lax.dot_general(q, k, (((1,), (1,)), ((), ())), preferred_element_type=F32) + bias[h]
            sink = sink_ref[h] * LOG2E
            m = jnp.maximum(jnp.max(s, axis=-1, keepdims=True), sink)
            p = jnp.exp2(s - m)
            o1 = jnp.dot(p.astype(BF16), v1, preferred_element_type=F32)
            denom = o1[:, HEAD_DIM:] + jnp.exp2(sink - m)
            o_ref[r * w:(r + 1) * w, h * HEAD_DIM:(h + 1) * HEAD_DIM] = (
                o1[:, :HEAD_DIM] / denom).astype(o_ref.dtype)


def _swa(rest, sinks, slopes, *, batch, seq, tq=1024):
    t = rest.shape[0]
    nq = seq // tq
    smem = pl.BlockSpec(memory_space=pltpu.SMEM)
    return pl.pallas_call(
        _swa_kernel,
        out_shape=jax.ShapeDtypeStruct((t, SWA_Q_W), BF16),
        grid=(batch, nq),
        in_specs=[
            pl.BlockSpec((tq, SWA_Q_W), lambda b, i: (b * nq + i, REST_SQ // SWA_Q_W)),
            pl.BlockSpec((seq, SWA_KV_W), lambda b, i: (b, REST_SK // SWA_KV_W)),
            pl.BlockSpec((seq, 2 * SWA_KV_W), lambda b, i: (b, REST_SV // (2 * SWA_KV_W))),
            smem, smem,
        ],
        out_specs=pl.BlockSpec((tq, SWA_Q_W), lambda b, i: (b * nq + i, 0)),
        compiler_params=_cparams(("parallel", "arbitrary")),
        name="swa",
    )(rest, rest, rest, sinks, slopes)


def _mem_kv_kernel(mem_ref, g_ref, wk_ref, wv_ref, kg_ref, mk_ref, mv_ref):
    mn = _rms(mem_ref[...], g_ref[...]).astype(BF16)
    k = jnp.dot(mn, wk_ref[...].astype(BF16), preferred_element_type=F32)
    v = jnp.dot(mn, wv_ref[...].astype(BF16), preferred_element_type=F32)
    ones = jnp.ones((mem_ref.shape[0], HEAD_DIM), BF16)
    for h in range(MEM_HEADS):
        hs = slice(h * HEAD_DIM, (h + 1) * HEAD_DIM)
        mk_ref[:, hs] = _rms(k[:, hs], kg_ref[...]).astype(BF16)
        mv_ref[:, 2 * h * HEAD_DIM:(2 * h + 1) * HEAD_DIM] = v[:, hs].astype(BF16)
        mv_ref[:, (2 * h + 1) * HEAD_DIM:(2 * h + 2) * HEAD_DIM] = ones


def _mem_kv(mem2d, gain, wk, wv, kgain, *, batch, mem_len):
    d = mem2d.shape[1]
    full = lambda shape: pl.BlockSpec(shape, lambda b: (0, 0))
    return pl.pallas_call(
        _mem_kv_kernel,
        out_shape=(jax.ShapeDtypeStruct((batch * mem_len, MEM_W), BF16),
                   jax.ShapeDtypeStruct((batch * mem_len, 2 * MEM_W), BF16)),
        grid=(batch,),
        in_specs=[
            pl.BlockSpec((mem_len, d), lambda b: (b, 0)),
            full((1, d)), full((d, MEM_W)), full((d, MEM_W)), full((1, HEAD_DIM)),
        ],
        out_specs=(pl.BlockSpec((mem_len, MEM_W), lambda b: (b, 0)),
                   pl.BlockSpec((mem_len, 2 * MEM_W), lambda b: (b, 0))),
        compiler_params=_cparams(("parallel",)),
        name="mem_kv",
    )(mem2d, gain, wk, wv, kgain)


def _mem_attn_kernel(q_ref, mk_ref, mv_ref, o_ref):
    for h in range(MEM_HEADS):
        hs = slice(h * HEAD_DIM, (h + 1) * HEAD_DIM)
        s = lax.dot_general(q_ref[:, hs], mk_ref[:, hs], (((1,), (1,)), ((), ())),
                            preferred_element_type=F32)
        m = jnp.max(s, axis=-1, keepdims=True)
        p = jnp.exp2(s - m)
        o1 = jnp.dot(p.astype(BF16), mv_ref[:, 2 * h * HEAD_DIM:(2 * h + 2) * HEAD_DIM],
                     preferred_element_type=F32)
        o_ref[:, hs] = (o1[:, :HEAD_DIM] / o1[:, HEAD_DIM:]).astype(o_ref.dtype)


def _mem_attn(rest, mk, mv, *, batch, seq, mem_len, tq=1024):
    t = rest.shape[0]
    nq = seq // tq
    return pl.pallas_call(
        _mem_attn_kernel,
        out_shape=jax.ShapeDtypeStruct((t, MEM_W), BF16),
        grid=(batch, nq),
        in_specs=[
            pl.BlockSpec((tq, MEM_W), lambda b, i: (b * nq + i, REST_MQ // MEM_W)),
            pl.BlockSpec((mem_len, MEM_W), lambda b, i: (b, 0)),
            pl.BlockSpec((mem_len, 2 * MEM_W), lambda b, i: (b, 0)),
        ],
        out_specs=pl.BlockSpec((tq, MEM_W), lambda b, i: (b * nq + i, 0)),
        compiler_params=_cparams(("parallel", "arbitrary")),
        name="mem_attn",
    )(rest, mk, mv)


def _out_proj_kernel(x_ref, a_ref, b_ref, c_ref, w_ref, o_ref):
    acc = jnp.dot(a_ref[...], w_ref[0:FOX_W, :], preferred_element_type=F32)
    acc += jnp.dot(b_ref[...], w_ref[FOX_W:FOX_W + SWA_Q_W, :], preferred_element_type=F32)
    acc += jnp.dot(c_ref[...], w_ref[FOX_W + SWA_Q_W:, :], preferred_element_type=F32)
    o_ref[...] = x_ref[...] + acc


def _out_proj(x, a, b, c, w, *, tm=1024):
    t, d = x.shape
    return pl.pallas_call(
        _out_proj_kernel,
        out_shape=jax.ShapeDtypeStruct((t, d), F32),
        grid=(t // tm,),
        in_specs=[
            pl.BlockSpec((tm, d), lambda i: (i, 0)),
            pl.BlockSpec((tm, FOX_W), lambda i: (i, 0)),
            pl.BlockSpec((tm, SWA_Q_W), lambda i: (i, 0)),
            pl.BlockSpec((tm, MEM_W), lambda i: (i, 0)),
            pl.BlockSpec(w.shape, lambda i: (0, 0), pipeline_mode=pl.Buffered(1)),
        ],
        out_specs=pl.BlockSpec((tm, d), lambda i: (i, 0)),
        compiler_params=_cparams(("parallel",)),
        name="out_proj",
    )(x, a, b, c, w)


def _alibi_slopes(n):
    return jnp.asarray([2.0 ** (-8.0 * i / n) for i in range(1, n + 1)], dtype=F32)


def _pack_w_in_kernel(w_ref, *rest, n_main):
    n_ride = (len(rest) - 2) // 2
    ride_in, (main_ref, f_ref), ride_out = rest[:n_ride], rest[n_ride:n_ride + 2], rest[n_ride + 2:]
    i = pl.program_id(0)
    n_a = w_ref.shape[2]

    @pl.when(i < n_main)
    def _():
        for a in range(n_a):
            main_ref[:, a * LANES:(a + 1) * LANES] = w_ref[0, :, a, :].astype(BF16)
        for src, dst in zip(ride_in, ride_out):
            dst[...] = src[...].astype(BF16)

    @pl.when(i == n_main)
    def _():
        row = lax.broadcasted_iota(jnp.int32, (f_ref.shape[0], LANES), 0)
        for a in range(n_a):
            f_ref[:, a * LANES:(a + 1) * LANES] = jnp.where(
                row < FOX_HEADS, w_ref[0, 0:f_ref.shape[0], a, :], 0.0).astype(BF16)


def _pack_w_in(w_in, l, *, ride=(), rows=256):
    d, n = w_in.shape[1:]
    fl0 = 3 * FOX_W
    assert n == PROJ_W + FOX_HEADS and fl0 % rows == 0 and PROJ_W % rows == 0 and d % LANES == 0
    wt = jnp.swapaxes(w_in, 1, 2).reshape(w_in.shape[0], n, d // LANES, LANES)
    n_main = PROJ_W // rows

    def start(i):
        main = jnp.where(i * rows < fl0, i * rows, i * rows + FOX_HEADS)
        return jnp.where(i < n_main, main, fl0)

    last_main = lambda i: (jnp.minimum(i, n_main - 1), 0)
    ride_specs = []
    for w in ride:
        assert w.shape[0] % (n_main * 2 * SUBLANES) == 0
        ride_specs.append(pl.BlockSpec((w.shape[0] // n_main, w.shape[1]), last_main))
    out = pl.pallas_call(
        functools.partial(_pack_w_in_kernel, n_main=n_main),
        out_shape=(jax.ShapeDtypeStruct((PROJ_W, d), BF16), jax.ShapeDtypeStruct((LANES, d), BF16),
                   *[jax.ShapeDtypeStruct(w.shape, BF16) for w in ride]),
        grid=(n_main + 1,),
        in_specs=[pl.BlockSpec((pl.Element(1), pl.Element(rows), pl.Element(d // LANES), pl.Element(LANES)),
                               lambda i: (l, start(i), 0, 0)),
                  *ride_specs],
        out_specs=(pl.BlockSpec((rows, d), last_main), pl.BlockSpec((LANES, d), lambda i: (0, 0)),
                   *ride_specs),
        compiler_params=_cparams(("arbitrary",)),
        name="pack_w_in",
    )(wt, *ride)
    return out[0], out[1], out[2:]


def kernel(x, mem, ffn1_norm, ffn1_gate, ffn1_up, ffn1_down, mix_norm, mem_norm, w_in, forget_bias, w_mem_k, w_mem_v, fox_q_gain, fox_k_gain, swa_q_gain, swa_k_gain, swa_sinks, mem_q_gain, mem_k_gain, w_out, ffn2_norm, ffn2_gate, ffn2_up, ffn2_down):
    batch, seq, d = x.shape
    mem_len = mem.shape[1]
    depth = w_in.shape[0]
    scale = HEAD_DIM ** -0.5
    slopes = _alibi_slopes(SWA_HEADS)
    x2 = x.reshape(batch * seq, d)
    mem2 = mem.reshape(batch * mem_len, d)
    zeros = jnp.zeros((HEAD_DIM,), F32)

    for l in range(depth):
        w_main, w_f, ffn1_w = _pack_w_in(w_in, l, ride=(ffn1_gate[l], ffn1_up[l], ffn1_down[l]))
        x2, ffn2_w = _ffn(x2, ffn1_norm[l][None], *ffn1_w,
                          next_weights=(ffn2_gate[l], ffn2_up[l], ffn2_down[l]))

        qs = scale * LOG2E
        head_gain = jnp.stack([fox_q_gain[l] * qs, fox_k_gain[l], swa_q_gain[l] * qs,
                               swa_k_gain[l], mem_q_gain[l] * qs, zeros, zeros, zeros])
        fbias = jnp.pad(forget_bias[l], (0, LANES - FOX_HEADS))[None]
        qt, ka, vt, rest, anc, w_out_bf = _in_proj(x2, mix_norm[l][None], w_main, w_f, head_gain, fbias,
                                                   w_out[l], seq=seq)

        n_sub = (x2.shape[0] // FOX_BLK) // anc.shape[0]
        anchors = anc[:, :n_sub, :SUBLANES].reshape(-1)
        out_a = _fox(anchors, qt, ka, vt, batch=batch, seq=seq)
        out_b = _swa(rest, swa_sinks[l], slopes, batch=batch, seq=seq)
        mk, mv = _mem_kv(mem2, mem_norm[l][None], w_mem_k[l], w_mem_v[l],
                         mem_k_gain[l][None], batch=batch, mem_len=mem_len)
        out_c = _mem_attn(rest, mk, mv, batch=batch, seq=seq, mem_len=mem_len)

        x2 = _out_proj(x2, out_a, out_b, out_c, w_out_bf)
        x2, _ = _ffn(x2, ffn2_norm[l][None], *ffn2_w)
    return x2.reshape(batch, seq, d)
```

```python
import functools

import jax
import jax.numpy as jnp
from jax import lax
from jax.experimental import pallas as pl
from jax.experimental.pallas import tpu as pltpu

F32 = jnp.float32
BF16 = jnp.bfloat16

HEAD_DIM = 128
FOX_HEADS = 6
SWA_HEADS = 6
SWA_KV_HEADS = 2
SWA_GROUP = SWA_HEADS // SWA_KV_HEADS
MEM_HEADS = 4
WINDOW = 128
EPS = 1e-6
NEG_INF = -1e30
LOG2E = 1.4426950408889634
FOX_BLK = 512
BIAS_LANES = 4

FOX_W = FOX_HEADS * HEAD_DIM
SWA_Q_W = SWA_HEADS * HEAD_DIM
SWA_KV_W = SWA_KV_HEADS * HEAD_DIM
MEM_W = MEM_HEADS * HEAD_DIM

PROJ_W = 3 * FOX_W + SWA_Q_W + 2 * SWA_KV_W + MEM_W
REST_W = SWA_Q_W + 3 * SWA_KV_W + MEM_W
REST_SQ, REST_SK, REST_MQ, REST_SV = 0, SWA_Q_W, SWA_Q_W + SWA_KV_W, SWA_Q_W + SWA_KV_W + MEM_W
GAIN_FQ, GAIN_FK, GAIN_SQ, GAIN_SK, GAIN_MQ = range(5)

LANES = 128
SUBLANES = 8
MXU_N = 256

VMEM_LIMIT = 58 * 1024 * 1024


def _cparams(sem):
    return pltpu.CompilerParams(dimension_semantics=sem, vmem_limit_bytes=VMEM_LIMIT)


def _rms(x, gain):
    ms = jnp.mean(x * x, axis=-1, keepdims=True)
    return x * lax.rsqrt(ms + EPS) * gain


def _ffn_kernel(x_ref, g_ref, wg_ref, wu_ref, wd_ref, *rest, n_cast):
    cast_in, o_ref, cast_out, xn_ref = rest[:n_cast], rest[n_cast], rest[n_cast + 1:-1], rest[-1]
    j = pl.program_id(1)

    @pl.when(j == 0)
    def _():
        x = x_ref[...]
        xn_ref[...] = _rms(x, g_ref[...]).astype(BF16)
        o_ref[...] = x

    xn = xn_ref[...]
    acc = None
    for c0 in range(0, wg_ref.shape[1], MXU_N):
        cs = slice(c0, c0 + MXU_N)
        gate = jnp.dot(xn, wg_ref[:, cs], preferred_element_type=F32)
        up = jnp.dot(xn, wu_ref[:, cs], preferred_element_type=F32)
        h = (gate * jax.nn.sigmoid(gate) * (0.5 * up)).astype(BF16)
        part = jnp.dot(h, wd_ref[cs, :], preferred_element_type=F32)
        acc = part if acc is None else acc + part
    o_ref[...] += acc

    for src, dst in zip(cast_in, cast_out):
        dst[...] = src[...].astype(BF16)


def _ffn(x, gain, wg, wu, wd, *, next_weights=(), tm=1024, tf=512):
    t, d = x.shape
    dff = wg.shape[1]
    gi, gj = t // tm, dff // tf
    cast_specs, cast_shapes = [], []
    for w in next_weights:
        ri, cj = (gi, gj) if w.shape[1] == dff else (gj, gi)
        blk = (w.shape[0] // ri, w.shape[1] // cj)
        assert blk[0] % SUBLANES == 0 and blk[1] % LANES == 0
        index = (lambda i, j: (i, j)) if w.shape[1] == dff else (lambda i, j: (j, i))
        cast_specs.append(pl.BlockSpec(blk, index))
        cast_shapes.append(jax.ShapeDtypeStruct(w.shape, BF16))
    out = pl.pallas_call(
        functools.partial(_ffn_kernel, n_cast=len(next_weights)),
        out_shape=(jax.ShapeDtypeStruct((t, d), F32), *cast_shapes),
        grid=(gi, gj),
        in_specs=[
            pl.BlockSpec((tm, d), lambda i, j: (i, 0)),
            pl.BlockSpec((1, d), lambda i, j: (0, 0)),
            pl.BlockSpec((d, tf), lambda i, j: (0, j)),
            pl.BlockSpec((d, tf), lambda i, j: (0, j)),
            pl.BlockSpec((tf, d), lambda i, j: (j, 0)),
            *cast_specs,
        ],
        out_specs=(pl.BlockSpec((tm, d), lambda i, j: (i, 0)), *cast_specs),
        scratch_shapes=[pltpu.VMEM((tm, d), BF16)],
        compiler_params=_cparams(("parallel", "arbitrary")),
        name="ffn",
    )(x, gain, wg, wu, wd, *next_weights)
    return out[0], out[1:]


def _in_proj_kernel(x_ref, g_ref, w_ref, wf_ref, hg_ref, fb_ref, wo_ref,
                    qt_ref, ka_ref, vt_ref, rest_ref, anc_ref, wo_bf_ref, carry_ref, *, blocks_per_seq):
    i = pl.program_id(0)
    tm = x_ref.shape[0]

    @pl.when(i == 0)
    def _():
        carry_ref[...] = jnp.zeros_like(carry_ref)

    wo_bf_ref[...] = wo_ref[...].astype(BF16)
    hn = _rms(x_ref[...], g_ref[...]).astype(BF16)

    nt = (((1,), (1,)), ((), ()))
    logit = lax.dot_general(hn, wf_ref[...], nt, preferred_element_type=F32)
    c = jax.nn.log_sigmoid(logit + fb_ref[...]) * LOG2E
    row = lax.broadcasted_iota(jnp.int32, c.shape, 0) % FOX_BLK
    sh = 1
    while sh < FOX_BLK:
        c = c + jnp.where(row >= sh, pltpu.roll(c, sh, 0), 0.0)
        sh *= 2

    anchor = jnp.where(i % blocks_per_seq == 0, 0.0, carry_ref[0:1, :])
    anc_ref[...] = jnp.zeros_like(anc_ref)
    for r in range(tm // FOX_BLK):
        anc_ref[0, r:r + 1, :] = anchor
        anchor = anchor + c[(r + 1) * FOX_BLK - 1:(r + 1) * FOX_BLK, :]
    carry_ref[0:1, :] = anchor

    n_sub = tm // FOX_BLK
    lane = lax.broadcasted_iota(jnp.int32, c.shape, 1)
    sub = lax.broadcasted_iota(jnp.int32, (HEAD_DIM, FOX_BLK), 0)
    ct = c.T
    ak = jnp.zeros(c.shape, F32)
    for h in range(FOX_HEADS):
        ch = c[:, h:h + 1]
        hi = ch.astype(BF16).astype(F32)
        lo = ch - hi
        own = lane - BIAS_LANES * h
        ak = jnp.where((own == 0) | (own == 1), 1.0, jnp.where(own == 2, -hi, jnp.where(own == 3, -lo, ak)))
        for r in range(n_sub):
            cr = ct[h:h + 1, r * FOX_BLK:(r + 1) * FOX_BLK]
            hi_r = cr.astype(BF16).astype(F32)
            lo_r = cr - hi_r
            own_r = sub - BIAS_LANES * h
            aq = jnp.where(own_r == 0, hi_r, jnp.where(own_r == 1, lo_r,
                                                       jnp.where((own_r == 2) | (own_r == 3), 1.0, 0.0)))
            qt_ref[r, (2 * h + 1) * HEAD_DIM:(2 * h + 2) * HEAD_DIM, :] = aq.astype(BF16)
    ka_ref[:, FOX_W:] = ak.astype(BF16)

    def normed(ph, row_id):
        return _rms(ph, hg_ref[row_id:row_id + 1, :]).astype(BF16)

    heads_per_chunk = MXU_N // HEAD_DIM
    sv_chunk = (3 * FOX_W + SWA_Q_W + SWA_KV_W) // MXU_N
    chunks = [ck for ck in range(PROJ_W // MXU_N) if ck != sv_chunk] + [sv_chunk]
    for ck in chunks:
        p = lax.dot_general(hn, w_ref[ck * MXU_N:(ck + 1) * MXU_N, :], nt, preferred_element_type=F32)
        for e in range(heads_per_chunk):
            col = ck * MXU_N + e * HEAD_DIM
            ph = p[:, e * HEAD_DIM:(e + 1) * HEAD_DIM]
            if col < FOX_W:
                h = col // HEAD_DIM
                qn = _rms(ph, hg_ref[GAIN_FQ:GAIN_FQ + 1, :])
                for r in range(n_sub):
                    qt_ref[r, 2 * h * HEAD_DIM:(2 * h + 1) * HEAD_DIM, :] = (
                        qn[r * FOX_BLK:(r + 1) * FOX_BLK, :].T.astype(BF16))
            elif col < 2 * FOX_W:
                h = (col - FOX_W) // HEAD_DIM
                ka_ref[:, h * HEAD_DIM:(h + 1) * HEAD_DIM] = normed(ph, GAIN_FK)
            elif col < 3 * FOX_W:
                h = (col - 2 * FOX_W) // HEAD_DIM
                for r in range(n_sub):
                    vt_ref[r, h * HEAD_DIM:(h + 1) * HEAD_DIM, :] = (
                        ph[r * FOX_BLK:(r + 1) * FOX_BLK, :].T.astype(BF16))
            else:
                wc = col - 3 * FOX_W
                if wc < SWA_Q_W:
                    val, rc = normed(ph, GAIN_SQ), REST_SQ + wc
                elif wc < SWA_Q_W + SWA_KV_W:
                    val, rc = normed(ph, GAIN_SK), REST_SK + wc - SWA_Q_W
                elif wc < SWA_Q_W + 2 * SWA_KV_W:
                    val, rc = ph.astype(BF16), REST_SV + 2 * (wc - SWA_Q_W - SWA_KV_W)
                    rest_ref[:, rc + HEAD_DIM:rc + 2 * HEAD_DIM] = jnp.ones((tm, HEAD_DIM), BF16)
                else:
                    val, rc = normed(ph, GAIN_MQ), REST_MQ + wc - SWA_Q_W - 2 * SWA_KV_W
                rest_ref[:, rc:rc + HEAD_DIM] = val


def _in_proj(x, gain, w, wf, head_gain, fbias, w_out, *, seq, tm=512):
    t, d = x.shape
    wo_rows = w_out.shape[0] // (t // tm)
    assert wo_rows % (2 * SUBLANES) == 0
    assert seq % tm == 0 and tm % FOX_BLK == 0 and tm // FOX_BLK <= SUBLANES
    const = lambda shape: pl.BlockSpec(shape, lambda i: (0, 0), pipeline_mode=pl.Buffered(1))
    return pl.pallas_call(
        functools.partial(_in_proj_kernel, blocks_per_seq=seq // tm),
        out_shape=(
            jax.ShapeDtypeStruct((t // FOX_BLK, 2 * FOX_W, FOX_BLK), BF16),
            jax.ShapeDtypeStruct((t, FOX_W + HEAD_DIM), BF16),
            jax.ShapeDtypeStruct((t // FOX_BLK, FOX_W, FOX_BLK), BF16),
            jax.ShapeDtypeStruct((t, REST_W), BF16),
            jax.ShapeDtypeStruct((t // tm, SUBLANES, LANES), F32),
            jax.ShapeDtypeStruct(w_out.shape, BF16),
        ),
        grid=(t // tm,),
        in_specs=[
            pl.BlockSpec((tm, d), lambda i: (i, 0)),
            const((1, d)),
            const((PROJ_W, d)),
            const((LANES, d)),
            const((SUBLANES, HEAD_DIM)),
            const((1, LANES)),
            pl.BlockSpec((wo_rows, w_out.shape[1]), lambda i: (i, 0)),
        ],
        out_specs=(
            pl.BlockSpec((tm // FOX_BLK, 2 * FOX_W, FOX_BLK), lambda i: (i, 0, 0)),
            pl.BlockSpec((tm, FOX_W + HEAD_DIM), lambda i: (i, 0)),
            pl.BlockSpec((tm // FOX_BLK, FOX_W, FOX_BLK), lambda i: (i, 0, 0)),
            pl.BlockSpec((tm, REST_W), lambda i: (i, 0)),
            pl.BlockSpec((1, SUBLANES, LANES), lambda i: (i, 0, 0)),
            pl.BlockSpec((wo_rows, w_out.shape[1]), lambda i: (i, 0)),
        ),
        scratch_shapes=[pltpu.VMEM((SUBLANES, LANES), F32)],
        compiler_params=_cparams(("arbitrary",)),
        name="in_proj",
    )(x, gain, w, wf, head_gain, fbias, w_out)


def _fox_kernel(anc_ref, q_ref, k_ref, vt_ref, o_ref, m_ref, l_ref, acc_ref, st_ref, *, nblk):
    lax.fori_loop(0, q_ref.shape[0], functools.partial(
        _fox_query_block, anc_ref, q_ref, k_ref, vt_ref, o_ref, m_ref, l_ref, acc_ref, st_ref, nblk), 0)


def _fox_query_block(anc_ref, q_ref, k_ref, vt_ref, o_ref, m_ref, l_ref, acc_ref, st_ref, nblk, sub, carry):
    q_per_step, _, blk = q_ref.shape
    aw = 2 * HEAD_DIM
    b = pl.program_id(0)
    i = pl.program_id(1) * q_per_step + sub
    m_ref[...] = jnp.full(m_ref.shape, NEG_INF, F32)
    l_ref[...] = jnp.zeros(l_ref.shape, F32)
    acc_ref[...] = jnp.zeros(acc_ref.shape, F32)

    def scores(j, slot):
        ks = pl.ds(pl.multiple_of(j * blk, blk), blk)
        k_bias = k_ref[ks, FOX_W:]
        for h in range(FOX_HEADS):
            k_aug = jnp.concatenate([k_ref[ks, h * HEAD_DIM:(h + 1) * HEAD_DIM], k_bias], axis=1)
            st_ref[slot, h] = jnp.dot(k_aug, q_ref[sub, h * aw:(h + 1) * aw, :],
                                      preferred_element_type=F32)

    def softmax_pv(j, slot, masked):
        if masked:
            causal = (lax.broadcasted_iota(jnp.int32, (blk, blk), 0)
                      <= lax.broadcasted_iota(jnp.int32, (blk, blk), 1))
        for h in range(FOX_HEADS):
            st = st_ref[slot, h]
            d = anc_ref[(b * nblk + i) * SUBLANES + h] - anc_ref[(b * nblk + j) * SUBLANES + h]
            if masked:
                st = jnp.where(causal, st, NEG_INF)
            m_prev = m_ref[h]
            m_new = jnp.maximum(m_prev, jnp.max(st, axis=0, keepdims=True) + d)
            alpha = jnp.exp2(m_prev - m_new)
            p = jnp.exp2(st - (m_new - d))
            l_ref[h] = alpha * l_ref[h] + jnp.sum(p, axis=0, keepdims=True)
            pv = jnp.dot(vt_ref[j, h * HEAD_DIM:(h + 1) * HEAD_DIM, :], p.astype(BF16),
                         preferred_element_type=F32)
            acc_ref[h] = alpha * acc_ref[h] + pv
            m_ref[h] = m_new

    def body(t, c):
        j = 2 * t
        scores(j + 1, 1)
        softmax_pv(j, 0, False)
        scores(j + 2, 0)
        softmax_pv(j + 1, 1, False)
        return c

    scores(0, 0)
    lax.fori_loop(0, i // 2, body, 0)

    @pl.when(i % 2 == 0)
    def _():
        softmax_pv(i, 0, True)

    @pl.when(i % 2 == 1)
    def _():
        scores(i, 1)
        softmax_pv(i - 1, 0, False)
        softmax_pv(i, 1, True)

    rows = pl.ds(pl.multiple_of(sub * blk, blk), blk)
    for h in range(FOX_HEADS):
        o_ref[rows, h * HEAD_DIM:(h + 1) * HEAD_DIM] = (acc_ref[h] / l_ref[h]).T.astype(o_ref.dtype)
    return carry


def _fox(anchors, qt, ka, vt, *, batch, seq, q_per_step=2):
    t = ka.shape[0]
    blk = FOX_BLK
    nblk = seq // blk
    steps = nblk // q_per_step
    return pl.pallas_call(
        functools.partial(_fox_kernel, nblk=nblk),
        out_shape=jax.ShapeDtypeStruct((t, FOX_W), BF16),
        grid=(batch, steps),
        in_specs=[
            pl.BlockSpec(memory_space=pltpu.SMEM),
            pl.BlockSpec((q_per_step, 2 * FOX_W, blk), lambda b, i: (b * steps + i, 0, 0)),
            pl.BlockSpec((seq, FOX_W + HEAD_DIM), lambda b, i: (b, 0)),
            pl.BlockSpec((nblk, FOX_W, blk), lambda b, i: (b, 0, 0)),
        ],
        out_specs=pl.BlockSpec((q_per_step * blk, FOX_W), lambda b, i: (b * steps + i, 0)),
        scratch_shapes=[
            pltpu.VMEM((FOX_HEADS, 1, blk), F32),
            pltpu.VMEM((FOX_HEADS, 1, blk), F32),
            pltpu.VMEM((FOX_HEADS, HEAD_DIM, blk), F32),
            pltpu.VMEM((2, FOX_HEADS, blk, blk), F32),
        ],
        compiler_params=_cparams(("parallel", "arbitrary")),
        name="fox",
    )(anchors, qt, ka, vt)


def _swa_kernel(q_ref, k_ref, v_ref, sink_ref, slope_ref, o_ref):
    qi = pl.program_id(1)
    tq = q_ref.shape[0]
    w = WINDOW
    n_sub = tq // w
    row = lax.broadcasted_iota(jnp.int32, (w, 2 * w), 0)
    col = lax.broadcasted_iota(jnp.int32, (w, 2 * w), 1)

    def bias_tables(offset):
        dist = offset + row - col
        valid = (dist >= 0) & (dist < w)
        dist_f = dist.astype(F32)
        return [jnp.where(valid, -(slope_ref[h] * LOG2E) * dist_f, NEG_INF) for h in range(SWA_HEADS)]

    later_bias = bias_tables(w)
    for r in range(n_sub):
        n = qi * n_sub + r
        kstart = pl.multiple_of(jnp.maximum(n - 1, 0) * w, w)
        bias = bias_tables(n * w - kstart) if r == 0 else later_bias
        for h in range(SWA_HEADS):
            g = h // SWA_GROUP
            q = q_ref[r * w:(r + 1) * w, h * HEAD_DIM:(h + 1) * HEAD_DIM]
            k = k_ref[pl.ds(kstart, 2 * w), g * HEAD_DIM:(g + 1) * HEAD_DIM]
            v1 = v_ref[pl.ds(kstart, 2 * w), 2 * g * HEAD_DIM:2 * (g + 1) * HEAD_DIM]
            s = lax.dot_general(q, k, (((1,), (1,)), ((), ())), preferred_element_type=F32) + bias[h]
            sink = sink_ref[h] * LOG2E
            m = jnp.maximum(jnp.max(s, axis=-1, keepdims=True), sink)
            p = jnp.exp2(s - m)
            o1 = jnp.dot(p.astype(BF16), v1, preferred_element_type=F32)
            denom = o1[:, HEAD_DIM:] + jnp.exp2(sink - m)
            o_ref[r * w:(r + 1) * w, h * HEAD_DIM:(h + 1) * HEAD_DIM] = (
                o1[:, :HEAD_DIM] / denom).astype(o_ref.dtype)


def _swa(rest, sinks, slopes, *, batch, seq, tq=2048):
    t = rest.shape[0]
    nq = seq // tq
    smem = pl.BlockSpec(memory_space=pltpu.SMEM)
    return pl.pallas_call(
        _swa_kernel,
        out_shape=jax.ShapeDtypeStruct((t, SWA_Q_W), BF16),
        grid=(batch, nq),
        in_specs=[
            pl.BlockSpec((tq, SWA_Q_W), lambda b, i: (b * nq + i, REST_SQ // SWA_Q_W)),
            pl.BlockSpec((seq, SWA_KV_W), lambda b, i: (b, REST_SK // SWA_KV_W)),
            pl.BlockSpec((seq, 2 * SWA_KV_W), lambda b, i: (b, REST_SV // (2 * SWA_KV_W))),
            smem, smem,
        ],
        out_specs=pl.BlockSpec((tq, SWA_Q_W), lambda b, i: (b * nq + i, 0)),
        compiler_params=_cparams(("parallel", "arbitrary")),
        name="swa",
    )(rest, rest, rest, sinks, slopes)


def _mem_kv_kernel(mem_ref, g_ref, wk_ref, wv_ref, kg_ref, mk_ref, mv_ref):
    mn = _rms(mem_ref[...], g_ref[...]).astype(BF16)
    k = jnp.dot(mn, wk_ref[...].astype(BF16), preferred_element_type=F32)
    v = jnp.dot(mn, wv_ref[...].astype(BF16), preferred_element_type=F32)
    ones = jnp.ones((mem_ref.shape[0], HEAD_DIM), BF16)
    for h in range(MEM_HEADS):
        hs = slice(h * HEAD_DIM, (h + 1) * HEAD_DIM)
        mk_ref[:, hs] = _rms(k[:, hs], kg_ref[...]).astype(BF16)
        mv_ref[:, 2 * h * HEAD_DIM:(2 * h + 1) * HEAD_DIM] = v[:, hs].astype(BF16)
        mv_ref[:, (2 * h + 1) * HEAD_DIM:(2 * h + 2) * HEAD_DIM] = ones


def _mem_kv(mem2d, gain, wk, wv, kgain):
    rows, d = mem2d.shape
    full = lambda shape: pl.BlockSpec(shape, lambda i: (0, 0))
    return pl.pallas_call(
        _mem_kv_kernel,
        out_shape=(jax.ShapeDtypeStruct((rows, MEM_W), BF16),
                   jax.ShapeDtypeStruct((rows, 2 * MEM_W), BF16)),
        grid=(1,),
        in_specs=[full((rows, d)), full((1, d)), full((d, MEM_W)), full((d, MEM_W)), full((1, HEAD_DIM))],
        out_specs=(full((rows, MEM_W)), full((rows, 2 * MEM_W))),
        compiler_params=_cparams(("arbitrary",)),
        name="mem_kv",
    )(mem2d, gain, wk, wv, kgain)


def _mem_attn_kernel(q_ref, mk_ref, mv_ref, o_ref):
    for h in range(MEM_HEADS):
        hs = slice(h * HEAD_DIM, (h + 1) * HEAD_DIM)
        s = lax.dot_general(q_ref[:, hs], mk_ref[:, hs], (((1,), (1,)), ((), ())),
                            preferred_element_type=F32)
        m = jnp.max(s, axis=-1, keepdims=True)
        p = jnp.exp2(s - m)
        o1 = jnp.dot(p.astype(BF16), mv_ref[:, 2 * h * HEAD_DIM:(2 * h + 2) * HEAD_DIM],
                     preferred_element_type=F32)
        o_ref[:, hs] = (o1[:, :HEAD_DIM] / o1[:, HEAD_DIM:]).astype(o_ref.dtype)


def _mem_attn(rest, mk, mv, *, batch, seq, mem_len, tq=2048):
    t = rest.shape[0]
    nq = seq // tq
    return pl.pallas_call(
        _mem_attn_kernel,
        out_shape=jax.ShapeDtypeStruct((t, MEM_W), BF16),
        grid=(batch, nq),
        in_specs=[
            pl.BlockSpec((tq, MEM_W), lambda b, i: (b * nq + i, REST_MQ // MEM_W)),
            pl.BlockSpec((mem_len, MEM_W), lambda b, i: (b, 0)),
            pl.BlockSpec((mem_len, 2 * MEM_W), lambda b, i: (b, 0)),
        ],
        out_specs=pl.BlockSpec((tq, MEM_W), lambda b, i: (b * nq + i, 0)),
        compiler_params=_cparams(("parallel", "arbitrary")),
        name="mem_attn",
    )(rest, mk, mv)


def _out_proj_kernel(x_ref, a_ref, b_ref, c_ref, w_ref, o_ref):
    acc = jnp.dot(a_ref[...], w_ref[0:FOX_W, :], preferred_element_type=F32)
    acc += jnp.dot(b_ref[...], w_ref[FOX_W:FOX_W + SWA_Q_W, :], preferred_element_type=F32)
    acc += jnp.dot(c_ref[...], w_ref[FOX_W + SWA_Q_W:, :], preferred_element_type=F32)
    o_ref[...] = x_ref[...] + acc


def _out_proj(x, a, b, c, w, *, tm=1024):
    t, d = x.shape
    return pl.pallas_call(
        _out_proj_kernel,
        out_shape=jax.ShapeDtypeStruct((t, d), F32),
        grid=(t // tm,),
        in_specs=[
            pl.BlockSpec((tm, d), lambda i: (i, 0)),
            pl.BlockSpec((tm, FOX_W), lambda i: (i, 0)),
            pl.BlockSpec((tm, SWA_Q_W), lambda i: (i, 0)),
            pl.BlockSpec((tm, MEM_W), lambda i: (i, 0)),
            pl.BlockSpec(w.shape, lambda i: (0, 0), pipeline_mode=pl.Buffered(1)),
        ],
        out_specs=pl.BlockSpec((tm, d), lambda i: (i, 0)),
        compiler_params=_cparams(("parallel",)),
        name="out_proj",
    )(x, a, b, c, w)


def _alibi_slopes(n):
    return jnp.asarray([2.0 ** (-8.0 * i / n) for i in range(1, n + 1)], dtype=F32)


def _pack_w_in_kernel(w_ref, *rest, n_main):
    n_ride = (len(rest) - 2) // 2
    ride_in, (main_ref, f_ref), ride_out = rest[:n_ride], rest[n_ride:n_ride + 2], rest[n_ride + 2:]
    i = pl.program_id(0)
    n_a = w_ref.shape[2]

    @pl.when(i < n_main)
    def _():
        for a in range(n_a):
            main_ref[:, a * LANES:(a + 1) * LANES] = w_ref[0, :, a, :].astype(BF16)
        for src, dst in zip(ride_in, ride_out):
            dst[...] = src[...].astype(BF16)

    @pl.when(i == n_main)
    def _():
        row = lax.broadcasted_iota(jnp.int32, (f_ref.shape[0], LANES), 0)
        for a in range(n_a):
            f_ref[:, a * LANES:(a + 1) * LANES] = jnp.where(
                row < FOX_HEADS, w_ref[0, 0:f_ref.shape[0], a, :], 0.0).astype(BF16)


def _pack_w_in(w_in, l, *, ride=(), rows=256):
    d, n = w_in.shape[1:]
    fl0 = 3 * FOX_W
    assert n == PROJ_W + FOX_HEADS and fl0 % rows == 0 and PROJ_W % rows == 0 and d % LANES == 0
    wt = jnp.swapaxes(w_in, 1, 2).reshape(w_in.shape[0], n, d // LANES, LANES)
    n_main = PROJ_W // rows

    def start(i):
        main = jnp.where(i * rows < fl0, i * rows, i * rows + FOX_HEADS)
        return jnp.where(i < n_main, main, fl0)

    last_main = lambda i: (jnp.minimum(i, n_main - 1), 0)
    ride_specs = []
    for w in ride:
        assert w.shape[0] % (n_main * 2 * SUBLANES) == 0
        ride_specs.append(pl.BlockSpec((w.shape[0] // n_main, w.shape[1]), last_main))
    out = pl.pallas_call(
        functools.partial(_pack_w_in_kernel, n_main=n_main),
        out_shape=(jax.ShapeDtypeStruct((PROJ_W, d), BF16), jax.ShapeDtypeStruct((LANES, d), BF16),
                   *[jax.ShapeDtypeStruct(w.shape, BF16) for w in ride]),
        grid=(n_main + 1,),
        in_specs=[pl.BlockSpec((pl.Element(1), pl.Element(rows), pl.Element(d // LANES), pl.Element(LANES)),
                               lambda i: (l, start(i), 0, 0)),
                  *ride_specs],
        out_specs=(pl.BlockSpec((rows, d), last_main), pl.BlockSpec((LANES, d), lambda i: (0, 0)),
                   *ride_specs),
        compiler_params=_cparams(("arbitrary",)),
        name="pack_w_in",
    )(wt, *ride)
    return out[0], out[1], out[2:]


def kernel(x, mem, ffn1_norm, ffn1_gate, ffn1_up, ffn1_down, mix_norm, mem_norm, w_in, forget_bias, w_mem_k, w_mem_v, fox_q_gain, fox_k_gain, swa_q_gain, swa_k_gain, swa_sinks, mem_q_gain, mem_k_gain, w_out, ffn2_norm, ffn2_gate, ffn2_up, ffn2_down):
    batch, seq, d = x.shape
    mem_len = mem.shape[1]
    depth = w_in.shape[0]
    scale = HEAD_DIM ** -0.5
    slopes = _alibi_slopes(SWA_HEADS)
    x2 = x.reshape(batch * seq, d)
    mem2 = mem.reshape(batch * mem_len, d)
    zeros = jnp.zeros((HEAD_DIM,), F32)

    for l in range(depth):
        w_main, w_f, ffn1_w = _pack_w_in(w_in, l, ride=(ffn1_gate[l], ffn1_up[l], ffn1_down[l]))
        x2, ffn2_w = _ffn(x2, ffn1_norm[l][None], *ffn1_w,
                          next_weights=(ffn2_gate[l], ffn2_up[l], ffn2_down[l]))

        qs = scale * LOG2E
        head_gain = jnp.stack([fox_q_gain[l] * qs, fox_k_gain[l], swa_q_gain[l] * qs,
                               swa_k_gain[l], mem_q_gain[l] * qs, zeros, zeros, zeros])
        fbias = jnp.pad(forget_bias[l], (0, LANES - FOX_HEADS))[None]
        qt, ka, vt, rest, anc, w_out_bf = _in_proj(x2, mix_norm[l][None], w_main, w_f, head_gain, fbias,
                                                   w_out[l], seq=seq)

        n_sub = (x2.shape[0] // FOX_BLK) // anc.shape[0]
        anchors = anc[:, :n_sub, :SUBLANES].reshape(-1)
        out_a = _fox(anchors, qt, ka, vt, batch=batch, seq=seq)
        out_b = _swa(rest, swa_sinks[l], slopes, batch=batch, seq=seq)
        mk, mv = _mem_kv(mem2, mem_norm[l][None], w_mem_k[l], w_mem_v[l], mem_k_gain[l][None])
        out_c = _mem_attn(rest, mk, mv, batch=batch, seq=seq, mem_len=mem_len)

        x2 = _out_proj(x2, out_a, out_b, out_c, w_out_bf)
        x2, _ = _ffn(x2, ffn2_norm[l][None], *ffn2_w)
    return x2.reshape(batch, seq, d)
```

```python
import functools

import jax
import jax.numpy as jnp
from jax import lax
from jax.experimental import pallas as pl
from jax.experimental.pallas import tpu as pltpu

F32 = jnp.float32
BF16 = jnp.bfloat16

HEAD_DIM = 128
FOX_HEADS = 6
SWA_HEADS = 6
SWA_KV_HEADS = 2
SWA_GROUP = SWA_HEADS // SWA_KV_HEADS
MEM_HEADS = 4
WINDOW = 128
EPS = 1e-6
NEG_INF = -1e30
LOG2E = 1.4426950408889634
FOX_BLK = 512
BIAS_LANES = 4
NORM_ROWS = 256

FOX_W = FOX_HEADS * HEAD_DIM
SWA_Q_W = SWA_HEADS * HEAD_DIM
SWA_KV_W = SWA_KV_HEADS * HEAD_DIM
MEM_W = MEM_HEADS * HEAD_DIM

PROJ_W = 3 * FOX_W + SWA_Q_W + 2 * SWA_KV_W + MEM_W
REST_W = SWA_Q_W + 3 * SWA_KV_W + MEM_W
REST_SQ, REST_SK, REST_MQ, REST_SV = 0, SWA_Q_W, SWA_Q_W + SWA_KV_W, SWA_Q_W + SWA_KV_W + MEM_W
GAIN_FQ, GAIN_FK, GAIN_SQ, GAIN_SK, GAIN_MQ = range(5)

LANES = 128
SUBLANES = 8
MXU_N = 256

VMEM_LIMIT = 58 * 1024 * 1024


def _cparams(sem):
    return pltpu.CompilerParams(dimension_semantics=sem, vmem_limit_bytes=VMEM_LIMIT)


def _rms(x, gain):
    ms = jnp.mean(x * x, axis=-1, keepdims=True)
    return x * lax.rsqrt(ms + EPS) * gain


def _ffn_kernel(x_ref, g_ref, wg_ref, wu_ref, wd_ref, *rest, n_cast):
    cast_in, o_ref, cast_out, xn_ref = rest[:n_cast], rest[n_cast], rest[n_cast + 1:-1], rest[-1]
    j = pl.program_id(1)

    def ride_along():
        for src, dst in zip(cast_in, cast_out):
            dst[...] = src[...].astype(BF16)

    def half_swiglu(xn):
        acc = None
        for c0 in range(0, wg_ref.shape[1], MXU_N):
            cs = slice(c0, c0 + MXU_N)
            gate = jnp.dot(xn, wg_ref[:, cs], preferred_element_type=F32)
            up = jnp.dot(xn, wu_ref[:, cs], preferred_element_type=F32)
            h = (gate * jax.nn.sigmoid(gate) * (0.5 * up)).astype(BF16)
            part = jnp.dot(h, wd_ref[cs, :], preferred_element_type=F32)
            acc = part if acc is None else acc + part
        return acc

    @pl.when(j == 0)
    def _():
        for r0 in range(0, x_ref.shape[0], NORM_ROWS):
            rows = slice(r0, r0 + NORM_ROWS)
            x = x_ref[rows, :]
            xn = _rms(x, g_ref[...]).astype(BF16)
            xn_ref[rows, :] = xn
            o_ref[rows, :] = x + half_swiglu(xn)
        ride_along()

    @pl.when(j > 0)
    def _():
        o_ref[...] += half_swiglu(xn_ref[...])
        ride_along()


def _ffn(x, gain, wg, wu, wd, *, next_weights=(), tm=1024, tf=512):
    t, d = x.shape
    dff = wg.shape[1]
    gi, gj = t // tm, dff // tf
    cast_specs, cast_shapes = [], []
    for w in next_weights:
        ri, cj = (gi, gj) if w.shape[1] == dff else (gj, gi)
        blk = (w.shape[0] // ri, w.shape[1] // cj)
        assert blk[0] % SUBLANES == 0 and blk[1] % LANES == 0
        index = (lambda i, j: (i, j)) if w.shape[1] == dff else (lambda i, j: (j, i))
        cast_specs.append(pl.BlockSpec(blk, index))
        cast_shapes.append(jax.ShapeDtypeStruct(w.shape, BF16))
    out = pl.pallas_call(
        functools.partial(_ffn_kernel, n_cast=len(next_weights)),
        out_shape=(jax.ShapeDtypeStruct((t, d), F32), *cast_shapes),
        grid=(gi, gj),
        in_specs=[
            pl.BlockSpec((tm, d), lambda i, j: (i, 0)),
            pl.BlockSpec((1, d), lambda i, j: (0, 0)),
            pl.BlockSpec((d, tf), lambda i, j: (0, j)),
            pl.BlockSpec((d, tf), lambda i, j: (0, j)),
            pl.BlockSpec((tf, d), lambda i, j: (j, 0)),
            *cast_specs,
        ],
        out_specs=(pl.BlockSpec((tm, d), lambda i, j: (i, 0)), *cast_specs),
        scratch_shapes=[pltpu.VMEM((tm, d), BF16)],
        compiler_params=_cparams(("parallel", "arbitrary")),
        name="ffn",
    )(x, gain, wg, wu, wd, *next_weights)
    return out[0], out[1:]


def _in_proj_kernel(x_ref, g_ref, w_ref, wf_ref, hg_ref, fb_ref, wo_ref,
                    qt_ref, ka_ref, vt_ref, rest_ref, anc_ref, wo_bf_ref, carry_ref, *, blocks_per_seq):
    i = pl.program_id(0)
    tm = x_ref.shape[0]

    @pl.when(i == 0)
    def _():
        carry_ref[...] = jnp.zeros_like(carry_ref)

    wo_bf_ref[...] = wo_ref[...].astype(BF16)
    hn = _rms(x_ref[...], g_ref[...]).astype(BF16)

    nt = (((1,), (1,)), ((), ()))
    logit = lax.dot_general(hn, wf_ref[...], nt, preferred_element_type=F32)
    c = jax.nn.log_sigmoid(logit + fb_ref[...]) * LOG2E
    row = lax.broadcasted_iota(jnp.int32, c.shape, 0) % FOX_BLK
    sh = 1
    while sh < FOX_BLK:
        c = c + jnp.where(row >= sh, pltpu.roll(c, sh, 0), 0.0)
        sh *= 2

    anchor = jnp.where(i % blocks_per_seq == 0, 0.0, carry_ref[0:1, :])
    anc_ref[...] = jnp.zeros_like(anc_ref)
    for r in range(tm // FOX_BLK):
        anc_ref[0, r:r + 1, :] = anchor
        anchor = anchor + c[(r + 1) * FOX_BLK - 1:(r + 1) * FOX_BLK, :]
    carry_ref[0:1, :] = anchor

    n_sub = tm // FOX_BLK
    lane = lax.broadcasted_iota(jnp.int32, c.shape, 1)
    sub = lax.broadcasted_iota(jnp.int32, (HEAD_DIM, FOX_BLK), 0)
    ct = c.T
    ak = jnp.zeros(c.shape, F32)
    for h in range(FOX_HEADS):
        ch = c[:, h:h + 1]
        hi = ch.astype(BF16).astype(F32)
        lo = ch - hi
        own = lane - BIAS_LANES * h
        ak = jnp.where((own == 0) | (own == 1), 1.0, jnp.where(own == 2, -hi, jnp.where(own == 3, -lo, ak)))
        for r in range(n_sub):
            cr = ct[h:h + 1, r * FOX_BLK:(r + 1) * FOX_BLK]
            hi_r = cr.astype(BF16).astype(F32)
            lo_r = cr - hi_r
            own_r = sub - BIAS_LANES * h
            aq = jnp.where(own_r == 0, hi_r, jnp.where(own_r == 1, lo_r,
                                                       jnp.where((own_r == 2) | (own_r == 3), 1.0, 0.0)))
            qt_ref[r, (2 * h + 1) * HEAD_DIM:(2 * h + 2) * HEAD_DIM, :] = aq.astype(BF16)
    ka_ref[:, FOX_W:] = ak.astype(BF16)

    def normed(ph, row_id):
        return _rms(ph, hg_ref[row_id:row_id + 1, :]).astype(BF16)

    heads_per_chunk = MXU_N // HEAD_DIM
    sv_chunk = (3 * FOX_W + SWA_Q_W + SWA_KV_W) // MXU_N
    chunks = [ck for ck in range(PROJ_W // MXU_N) if ck != sv_chunk] + [sv_chunk]
    for ck in chunks:
        p = lax.dot_general(hn, w_ref[ck * MXU_N:(ck + 1) * MXU_N, :], nt, preferred_element_type=F32)
        for e in range(heads_per_chunk):
            col = ck * MXU_N + e * HEAD_DIM
            ph = p[:, e * HEAD_DIM:(e + 1) * HEAD_DIM]
            if col < FOX_W:
                h = col // HEAD_DIM
                qn = _rms(ph, hg_ref[GAIN_FQ:GAIN_FQ + 1, :])
                for r in range(n_sub):
                    qt_ref[r, 2 * h * HEAD_DIM:(2 * h + 1) * HEAD_DIM, :] = (
                        qn[r * FOX_BLK:(r + 1) * FOX_BLK, :].T.astype(BF16))
            elif col < 2 * FOX_W:
                h = (col - FOX_W) // HEAD_DIM
                ka_ref[:, h * HEAD_DIM:(h + 1) * HEAD_DIM] = normed(ph, GAIN_FK)
            elif col < 3 * FOX_W:
                h = (col - 2 * FOX_W) // HEAD_DIM
                for r in range(n_sub):
                    vt_ref[r, h * HEAD_DIM:(h + 1) * HEAD_DIM, :] = (
                        ph[r * FOX_BLK:(r + 1) * FOX_BLK, :].T.astype(BF16))
            else:
                wc = col - 3 * FOX_W
                if wc < SWA_Q_W:
                    val, rc = normed(ph, GAIN_SQ), REST_SQ + wc
                elif wc < SWA_Q_W + SWA_KV_W:
                    val, rc = normed(ph, GAIN_SK), REST_SK + wc - SWA_Q_W
                elif wc < SWA_Q_W + 2 * SWA_KV_W:
                    val, rc = ph.astype(BF16), REST_SV + 2 * (wc - SWA_Q_W - SWA_KV_W)
                    rest_ref[:, rc + HEAD_DIM:rc + 2 * HEAD_DIM] = jnp.ones((tm, HEAD_DIM), BF16)
                else:
                    val, rc = normed(ph, GAIN_MQ), REST_MQ + wc - SWA_Q_W - 2 * SWA_KV_W
                rest_ref[:, rc:rc + HEAD_DIM] = val


def _in_proj(x, gain, w, wf, head_gain, fbias, w_out, *, seq, tm=512):
    t, d = x.shape
    wo_rows = w_out.shape[0] // (t // tm)
    assert wo_rows % (2 * SUBLANES) == 0
    assert seq % tm == 0 and tm % FOX_BLK == 0 and tm // FOX_BLK <= SUBLANES
    const = lambda shape: pl.BlockSpec(shape, lambda i: (0, 0), pipeline_mode=pl.Buffered(1))
    return pl.pallas_call(
        functools.partial(_in_proj_kernel, blocks_per_seq=seq // tm),
        out_shape=(
            jax.ShapeDtypeStruct((t // FOX_BLK, 2 * FOX_W, FOX_BLK), BF16),
            jax.ShapeDtypeStruct((t, FOX_W + HEAD_DIM), BF16),
            jax.ShapeDtypeStruct((t // FOX_BLK, FOX_W, FOX_BLK), BF16),
            jax.ShapeDtypeStruct((t, REST_W), BF16),
            jax.ShapeDtypeStruct((t // tm, SUBLANES, LANES), F32),
            jax.ShapeDtypeStruct(w_out.shape, BF16),
        ),
        grid=(t // tm,),
        in_specs=[
            pl.BlockSpec((tm, d), lambda i: (i, 0)),
            const((1, d)),
            const((PROJ_W, d)),
            const((LANES, d)),
            const((SUBLANES, HEAD_DIM)),
            const((1, LANES)),
            pl.BlockSpec((wo_rows, w_out.shape[1]), lambda i: (i, 0)),
        ],
        out_specs=(
            pl.BlockSpec((tm // FOX_BLK, 2 * FOX_W, FOX_BLK), lambda i: (i, 0, 0)),
            pl.BlockSpec((tm, FOX_W + HEAD_DIM), lambda i: (i, 0)),
            pl.BlockSpec((tm // FOX_BLK, FOX_W, FOX_BLK), lambda i: (i, 0, 0)),
            pl.BlockSpec((tm, REST_W), lambda i: (i, 0)),
            pl.BlockSpec((1, SUBLANES, LANES), lambda i: (i, 0, 0)),
            pl.BlockSpec((wo_rows, w_out.shape[1]), lambda i: (i, 0)),
        ),
        scratch_shapes=[pltpu.VMEM((SUBLANES, LANES), F32)],
        compiler_params=_cparams(("arbitrary",)),
        name="in_proj",
    )(x, gain, w, wf, head_gain, fbias, w_out)


def _fox_kernel(anc_ref, q_ref, k_ref, vt_ref, o_ref, m_ref, l_ref, acc_ref, st_ref, *, nblk):
    lax.fori_loop(0, q_ref.shape[0], functools.partial(
        _fox_query_block, anc_ref, q_ref, k_ref, vt_ref, o_ref, m_ref, l_ref, acc_ref, st_ref, nblk), 0)


def _fox_query_block(anc_ref, q_ref, k_ref, vt_ref, o_ref, m_ref, l_ref, acc_ref, st_ref, nblk, sub, carry):
    q_per_step, _, blk = q_ref.shape
    aw = 2 * HEAD_DIM
    b = pl.program_id(0)
    i = pl.program_id(1) * q_per_step + sub
    m_ref[...] = jnp.full(m_ref.shape, NEG_INF, F32)
    l_ref[...] = jnp.zeros(l_ref.shape, F32)
    acc_ref[...] = jnp.zeros(acc_ref.shape, F32)

    def scores(j, slot):
        ks = pl.ds(pl.multiple_of(j * blk, blk), blk)
        k_bias = k_ref[ks, FOX_W:]
        for h in range(FOX_HEADS):
            k_aug = jnp.concatenate([k_ref[ks, h * HEAD_DIM:(h + 1) * HEAD_DIM], k_bias], axis=1)
            st_ref[slot, h] = jnp.dot(k_aug, q_ref[sub, h * aw:(h + 1) * aw, :],
                                      preferred_element_type=F32)

    def softmax_pv(j, slot, masked):
        if masked:
            causal = (lax.broadcasted_iota(jnp.int32, (blk, blk), 0)
                      <= lax.broadcasted_iota(jnp.int32, (blk, blk), 1))
        for h in range(FOX_HEADS):
            st = st_ref[slot, h]
            d = anc_ref[(b * nblk + i) * SUBLANES + h] - anc_ref[(b * nblk + j) * SUBLANES + h]
            if masked:
                st = jnp.where(causal, st, NEG_INF)
            m_prev = m_ref[h]
            m_new = jnp.maximum(m_prev, jnp.max(st, axis=0, keepdims=True) + d)
            alpha = jnp.exp2(m_prev - m_new)
            p = jnp.exp2(st - (m_new - d))
            l_ref[h] = alpha * l_ref[h] + jnp.sum(p, axis=0, keepdims=True)
            pv = jnp.dot(vt_ref[j, h * HEAD_DIM:(h + 1) * HEAD_DIM, :], p.astype(BF16),
                         preferred_element_type=F32)
            acc_ref[h] = alpha * acc_ref[h] + pv
            m_ref[h] = m_new

    def body(t, c):
        j = 2 * t
        scores(j + 1, 1)
        softmax_pv(j, 0, False)
        scores(j + 2, 0)
        softmax_pv(j + 1, 1, False)
        return c

    scores(0, 0)
    lax.fori_loop(0, i // 2, body, 0)

    @pl.when(i % 2 == 0)
    def _():
        softmax_pv(i, 0, True)

    @pl.when(i % 2 == 1)
    def _():
        scores(i, 1)
        softmax_pv(i - 1, 0, False)
        softmax_pv(i, 1, True)

    rows = pl.ds(pl.multiple_of(sub * blk, blk), blk)
    for h in range(FOX_HEADS):
        o_ref[rows, h * HEAD_DIM:(h + 1) * HEAD_DIM] = (acc_ref[h] / l_ref[h]).T.astype(o_ref.dtype)
    return carry


def _fox(anchors, qt, ka, vt, *, batch, seq, q_per_step=2):
    t = ka.shape[0]
    blk = FOX_BLK
    nblk = seq // blk
    steps = nblk // q_per_step
    return pl.pallas_call(
        functools.partial(_fox_kernel, nblk=nblk),
        out_shape=jax.ShapeDtypeStruct((t, FOX_W), BF16),
        grid=(batch, steps),
        in_specs=[
            pl.BlockSpec(memory_space=pltpu.SMEM),
            pl.BlockSpec((q_per_step, 2 * FOX_W, blk), lambda b, i: (b * steps + i, 0, 0)),
            pl.BlockSpec((seq, FOX_W + HEAD_DIM), lambda b, i: (b, 0)),
            pl.BlockSpec((nblk, FOX_W, blk), lambda b, i: (b, 0, 0)),
        ],
        out_specs=pl.BlockSpec((q_per_step * blk, FOX_W), lambda b, i: (b * steps + i, 0)),
        scratch_shapes=[
            pltpu.VMEM((FOX_HEADS, 1, blk), F32),
            pltpu.VMEM((FOX_HEADS, 1, blk), F32),
            pltpu.VMEM((FOX_HEADS, HEAD_DIM, blk), F32),
            pltpu.VMEM((2, FOX_HEADS, blk, blk), F32),
        ],
        compiler_params=_cparams(("parallel", "arbitrary")),
        name="fox",
    )(anchors, qt, ka, vt)


def _swa_kernel(q_ref, k_ref, v_ref, sink_ref, slope_ref, o_ref):
    qi = pl.program_id(1)
    tq = q_ref.shape[0]
    w = WINDOW
    n_sub = tq // w
    row = lax.broadcasted_iota(jnp.int32, (w, 2 * w), 0)
    col = lax.broadcasted_iota(jnp.int32, (w, 2 * w), 1)

    def bias_tables(offset):
        dist = offset + row - col
        valid = (dist >= 0) & (dist < w)
        dist_f = dist.astype(F32)
        return [jnp.where(valid, -(slope_ref[h] * LOG2E) * dist_f, NEG_INF) for h in range(SWA_HEADS)]

    later_bias = bias_tables(w)
    for r in range(n_sub):
        n = qi * n_sub + r
        kstart = pl.multiple_of(jnp.maximum(n - 1, 0) * w, w)
        bias = bias_tables(n * w - kstart) if r == 0 else later_bias
        for h in range(SWA_HEADS):
            g = h // SWA_GROUP
            q = q_ref[r * w:(r + 1) * w, h * HEAD_DIM:(h + 1) * HEAD_DIM]
            k = k_ref[pl.ds(kstart, 2 * w), g * HEAD_DIM:(g + 1) * HEAD_DIM]
            v1 = v_ref[pl.ds(kstart, 2 * w), 2 * g * HEAD_DIM:2 * (g + 1) * HEAD_DIM]
            s = lax.dot_general(q, k, (((1,), (1,)), ((), ())), preferred_element_type=F32) + bias[h]
            sink = sink_ref[h] * LOG2E
            m = jnp.maximum(jnp.max(s, axis=-1, keepdims=True), sink)
            p = jnp.exp2(s - m)
            o1 = jnp.dot(p.astype(BF16), v1, preferred_element_type=F32)
            denom = o1[:, HEAD_DIM:] + jnp.exp2(sink - m)
            o_ref[r * w:(r + 1) * w, h * HEAD_DIM:(h + 1) * HEAD_DIM] = (
                o1[:, :HEAD_DIM] / denom).astype(o_ref.dtype)


def _swa(rest, sinks, slopes, *, batch, seq, tq=2048):
    t = rest.shape[0]
    nq = seq // tq
    smem = pl.BlockSpec(memory_space=pltpu.SMEM)
    return pl.pallas_call(
        _swa_kernel,
        out_shape=jax.ShapeDtypeStruct((t, SWA_Q_W), BF16),
        grid=(batch, nq),
        in_specs=[
            pl.BlockSpec((tq, SWA_Q_W), lambda b, i: (b * nq + i, REST_SQ // SWA_Q_W)),
            pl.BlockSpec((seq, SWA_KV_W), lambda b, i: (b, REST_SK // SWA_KV_W)),
            pl.BlockSpec((seq, 2 * SWA_KV_W), lambda b, i: (b, REST_SV // (2 * SWA_KV_W))),
            smem, smem,
        ],
        out_specs=pl.BlockSpec((tq, SWA_Q_W), lambda b, i: (b * nq + i, 0)),
        compiler_params=_cparams(("parallel", "arbitrary")),
        name="swa",
    )(rest, rest, rest, sinks, slopes)


def _mem_kv_kernel(mem_ref, g_ref, wk_ref, wv_ref, kg_ref, mk_ref, mv_ref):
    mn = _rms(mem_ref[...], g_ref[...]).astype(BF16)
    k = jnp.dot(mn, wk_ref[...].astype(BF16), preferred_element_type=F32)
    v = jnp.dot(mn, wv_ref[...].astype(BF16), preferred_element_type=F32)
    ones = jnp.ones((mem_ref.shape[0], HEAD_DIM), BF16)
    for h in range(MEM_HEADS):
        hs = slice(h * HEAD_DIM, (h + 1) * HEAD_DIM)
        mk_ref[:, hs] = _rms(k[:, hs], kg_ref[...]).astype(BF16)
        mv_ref[:, 2 * h * HEAD_DIM:(2 * h + 1) * HEAD_DIM] = v[:, hs].astype(BF16)
        mv_ref[:, (2 * h + 1) * HEAD_DIM:(2 * h + 2) * HEAD_DIM] = ones


def _mem_kv(mem2d, gain, wk, wv, kgain):
    rows, d = mem2d.shape
    full = lambda shape: pl.BlockSpec(shape, lambda i: (0, 0))
    return pl.pallas_call(
        _mem_kv_kernel,
        out_shape=(jax.ShapeDtypeStruct((rows, MEM_W), BF16),
                   jax.ShapeDtypeStruct((rows, 2 * MEM_W), BF16)),
        grid=(1,),
        in_specs=[full((rows, d)), full((1, d)), full((d, MEM_W)), full((d, MEM_W)), full((1, HEAD_DIM))],
        out_specs=(full((rows, MEM_W)), full((rows, 2 * MEM_W))),
        compiler_params=_cparams(("arbitrary",)),
        name="mem_kv",
    )(mem2d, gain, wk, wv, kgain)


def _mem_attn_kernel(q_ref, mk_ref, mv_ref, o_ref):
    for h in range(MEM_HEADS):
        hs = slice(h * HEAD_DIM, (h + 1) * HEAD_DIM)
        s = lax.dot_general(q_ref[:, hs], mk_ref[:, hs], (((1,), (1,)), ((), ())),
                            preferred_element_type=F32)
        m = jnp.max(s, axis=-1, keepdims=True)
        p = jnp.exp2(s - m)
        o1 = jnp.dot(p.astype(BF16), mv_ref[:, 2 * h * HEAD_DIM:(2 * h + 2) * HEAD_DIM],
                     preferred_element_type=F32)
        o_ref[:, hs] = (o1[:, :HEAD_DIM] / o1[:, HEAD_DIM:]).astype(o_ref.dtype)


def _mem_attn(rest, mk, mv, *, batch, seq, mem_len, tq=2048):
    t = rest.shape[0]
    nq = seq // tq
    return pl.pallas_call(
        _mem_attn_kernel,
        out_shape=jax.ShapeDtypeStruct((t, MEM_W), BF16),
        grid=(batch, nq),
        in_specs=[
            pl.BlockSpec((tq, MEM_W), lambda b, i: (b * nq + i, REST_MQ // MEM_W)),
            pl.BlockSpec((mem_len, MEM_W), lambda b, i: (b, 0)),
            pl.BlockSpec((mem_len, 2 * MEM_W), lambda b, i: (b, 0)),
        ],
        out_specs=pl.BlockSpec((tq, MEM_W), lambda b, i: (b * nq + i, 0)),
        compiler_params=_cparams(("parallel", "arbitrary")),
        name="mem_attn",
    )(rest, mk, mv)


def _out_proj_kernel(x_ref, a_ref, b_ref, c_ref, w_ref, o_ref):
    acc = jnp.dot(a_ref[...], w_ref[0:FOX_W, :], preferred_element_type=F32)
    acc += jnp.dot(b_ref[...], w_ref[FOX_W:FOX_W + SWA_Q_W, :], preferred_element_type=F32)
    acc += jnp.dot(c_ref[...], w_ref[FOX_W + SWA_Q_W:, :], preferred_element_type=F32)
    o_ref[...] = x_ref[...] + acc


def _out_proj(x, a, b, c, w, *, tm=1024):
    t, d = x.shape
    return pl.pallas_call(
        _out_proj_kernel,
        out_shape=jax.ShapeDtypeStruct((t, d), F32),
        grid=(t // tm,),
        in_specs=[
            pl.BlockSpec((tm, d), lambda i: (i, 0)),
            pl.BlockSpec((tm, FOX_W), lambda i: (i, 0)),
            pl.BlockSpec((tm, SWA_Q_W), lambda i: (i, 0)),
            pl.BlockSpec((tm, MEM_W), lambda i: (i, 0)),
            pl.BlockSpec(w.shape, lambda i: (0, 0), pipeline_mode=pl.Buffered(1)),
        ],
        out_specs=pl.BlockSpec((tm, d), lambda i: (i, 0)),
        compiler_params=_cparams(("parallel",)),
        name="out_proj",
    )(x, a, b, c, w)


def _alibi_slopes(n):
    return jnp.asarray([2.0 ** (-8.0 * i / n) for i in range(1, n + 1)], dtype=F32)


def _pack_w_in_kernel(w_ref, *rest, n_main):
    n_ride = (len(rest) - 2) // 2
    ride_in, (main_ref, f_ref), ride_out = rest[:n_ride], rest[n_ride:n_ride + 2], rest[n_ride + 2:]
    i = pl.program_id(0)
    n_a = w_ref.shape[2]

    @pl.when(i < n_main)
    def _():
        for a in range(n_a):
            main_ref[:, a * LANES:(a + 1) * LANES] = w_ref[0, :, a, :].astype(BF16)
        for src, dst in zip(ride_in, ride_out):
            dst[...] = src[...].astype(BF16)

    @pl.when(i == n_main)
    def _():
        row = lax.broadcasted_iota(jnp.int32, (f_ref.shape[0], LANES), 0)
        for a in range(n_a):
            f_ref[:, a * LANES:(a + 1) * LANES] = jnp.where(
                row < FOX_HEADS, w_ref[0, 0:f_ref.shape[0], a, :], 0.0).astype(BF16)


def _pack_w_in(w_in, l, *, ride=(), rows=256):
    d, n = w_in.shape[1:]
    fl0 = 3 * FOX_W
    assert n == PROJ_W + FOX_HEADS and fl0 % rows == 0 and PROJ_W % rows == 0 and d % LANES == 0
    wt = jnp.swapaxes(w_in, 1, 2).reshape(w_in.shape[0], n, d // LANES, LANES)
    n_main = PROJ_W // rows

    def start(i):
        main = jnp.where(i * rows < fl0, i * rows, i * rows + FOX_HEADS)
        return jnp.where(i < n_main, main, fl0)

    last_main = lambda i: (jnp.minimum(i, n_main - 1), 0)
    ride_specs = []
    for w in ride:
        assert w.shape[0] % (n_main * 2 * SUBLANES) == 0
        ride_specs.append(pl.BlockSpec((w.shape[0] // n_main, w.shape[1]), last_main))
    out = pl.pallas_call(
        functools.partial(_pack_w_in_kernel, n_main=n_main),
        out_shape=(jax.ShapeDtypeStruct((PROJ_W, d), BF16), jax.ShapeDtypeStruct((LANES, d), BF16),
                   *[jax.ShapeDtypeStruct(w.shape, BF16) for w in ride]),
        grid=(n_main + 1,),
        in_specs=[pl.BlockSpec((pl.Element(1), pl.Element(rows), pl.Element(d // LANES), pl.Element(LANES)),
                               lambda i: (l, start(i), 0, 0)),
                  *ride_specs],
        out_specs=(pl.BlockSpec((rows, d), last_main), pl.BlockSpec((LANES, d), lambda i: (0, 0)),
                   *ride_specs),
        compiler_params=_cparams(("arbitrary",)),
        name="pack_w_in",
    )(wt, *ride)
    return out[0], out[1], out[2:]


def kernel(x, mem, ffn1_norm, ffn1_gate, ffn1_up, ffn1_down, mix_norm, mem_norm, w_in, forget_bias, w_mem_k, w_mem_v, fox_q_gain, fox_k_gain, swa_q_gain, swa_k_gain, swa_sinks, mem_q_gain, mem_k_gain, w_out, ffn2_norm, ffn2_gate, ffn2_up, ffn2_down):
    batch, seq, d = x.shape
    mem_len = mem.shape[1]
    depth = w_in.shape[0]
    scale = HEAD_DIM ** -0.5
    slopes = _alibi_slopes(SWA_HEADS)
    x2 = x.reshape(batch * seq, d)
    mem2 = mem.reshape(batch * mem_len, d)
    zeros = jnp.zeros((HEAD_DIM,), F32)

    for l in range(depth):
        w_main, w_f, ffn1_w = _pack_w_in(w_in, l, ride=(ffn1_gate[l], ffn1_up[l], ffn1_down[l]))
        x2, ffn2_w = _ffn(x2, ffn1_norm[l][None], *ffn1_w,
                          next_weights=(ffn2_gate[l], ffn2_up[l], ffn2_down[l]))

        qs = scale * LOG2E
        head_gain = jnp.stack([fox_q_gain[l] * qs, fox_k_gain[l], swa_q_gain[l] * qs,
                               swa_k_gain[l], mem_q_gain[l] * qs, zeros, zeros, zeros])
        fbias = jnp.pad(forget_bias[l], (0, LANES - FOX_HEADS))[None]
        qt, ka, vt, rest, anc, w_out_bf = _in_proj(x2, mix_norm[l][None], w_main, w_f, head_gain, fbias,
                                                   w_out[l], seq=seq)

        n_sub = (x2.shape[0] // FOX_BLK) // anc.shape[0]
        anchors = anc[:, :n_sub, :SUBLANES].reshape(-1)
        out_a = _fox(anchors, qt, ka, vt, batch=batch, seq=seq)
        out_b = _swa(rest, swa_sinks[l], slopes, batch=batch, seq=seq)
        mk, mv = _mem_kv(mem2, mem_norm[l][None], w_mem_k[l], w_mem_v[l], mem_k_gain[l][None])
        out_c = _mem_attn(rest, mk, mv, batch=batch, seq=seq, mem_len=mem_len)

        x2 = _out_proj(x2, out_a, out_b, out_c, w_out_bf)
        x2, _ = _ffn(x2, ffn2_norm[l][None], *ffn2_w)
    return x2.reshape(batch, seq, d)
```

```python
import functools

import jax
import jax.numpy as jnp
from jax import lax
from jax.experimental import pallas as pl
from jax.experimental.pallas import tpu as pltpu

F32 = jnp.float32
BF16 = jnp.bfloat16

HEAD_DIM = 128
FOX_HEADS = 6
SWA_HEADS = 6
SWA_KV_HEADS = 2
SWA_GROUP = SWA_HEADS // SWA_KV_HEADS
MEM_HEADS = 4
WINDOW = 128
EPS = 1e-6
NEG_INF = -1e30
LOG2E = 1.4426950408889634
FOX_BLK = 512
BIAS_LANES = 4
NORM_ROWS = 512

FOX_W = FOX_HEADS * HEAD_DIM
SWA_Q_W = SWA_HEADS * HEAD_DIM
SWA_KV_W = SWA_KV_HEADS * HEAD_DIM
MEM_W = MEM_HEADS * HEAD_DIM

PROJ_W = 3 * FOX_W + SWA_Q_W + 2 * SWA_KV_W + MEM_W
REST_W = SWA_Q_W + 3 * SWA_KV_W + MEM_W
REST_SQ, REST_SK, REST_MQ, REST_SV = 0, SWA_Q_W, SWA_Q_W + SWA_KV_W, SWA_Q_W + SWA_KV_W + MEM_W
GAIN_FQ, GAIN_FK, GAIN_SQ, GAIN_SK, GAIN_MQ = range(5)

LANES = 128
SUBLANES = 8
MXU_N = 256

VMEM_LIMIT = 58 * 1024 * 1024


def _cparams(sem):
    return pltpu.CompilerParams(dimension_semantics=sem, vmem_limit_bytes=VMEM_LIMIT)


def _rms(x, gain):
    ms = jnp.mean(x * x, axis=-1, keepdims=True)
    return x * lax.rsqrt(ms + EPS) * gain


def _ffn_kernel(x_ref, g_ref, wg_ref, wu_ref, wd_ref, *rest, n_cast):
    cast_in, o_ref, cast_out, xn_ref = rest[:n_cast], rest[n_cast], rest[n_cast + 1:-1], rest[-1]
    j = pl.program_id(1)

    def ride_along():
        for src, dst in zip(cast_in, cast_out):
            dst[...] = src[...].astype(BF16)

    def half_swiglu(xn):
        acc = None
        for c0 in range(0, wg_ref.shape[1], MXU_N):
            cs = slice(c0, c0 + MXU_N)
            gate = jnp.dot(xn, wg_ref[:, cs], preferred_element_type=F32)
            up = jnp.dot(xn, wu_ref[:, cs], preferred_element_type=F32)
            h = (gate * jax.nn.sigmoid(gate) * (0.5 * up)).astype(BF16)
            part = jnp.dot(h, wd_ref[cs, :], preferred_element_type=F32)
            acc = part if acc is None else acc + part
        return acc

    @pl.when(j == 0)
    def _():
        for r0 in range(0, x_ref.shape[0], NORM_ROWS):
            rows = slice(r0, r0 + NORM_ROWS)
            x = x_ref[rows, :]
            xn = _rms(x, g_ref[...]).astype(BF16)
            xn_ref[rows, :] = xn
            o_ref[rows, :] = x + half_swiglu(xn)
        ride_along()

    @pl.when(j > 0)
    def _():
        o_ref[...] += half_swiglu(xn_ref[...])
        ride_along()


def _ffn(x, gain, wg, wu, wd, *, next_weights=(), tm=1024, tf=512):
    t, d = x.shape
    dff = wg.shape[1]
    gi, gj = t // tm, dff // tf
    cast_specs, cast_shapes = [], []
    for w in next_weights:
        ri, cj = (gi, gj) if w.shape[1] == dff else (gj, gi)
        blk = (w.shape[0] // ri, w.shape[1] // cj)
        assert blk[0] % SUBLANES == 0 and blk[1] % LANES == 0
        index = (lambda i, j: (i, j)) if w.shape[1] == dff else (lambda i, j: (j, i))
        cast_specs.append(pl.BlockSpec(blk, index))
        cast_shapes.append(jax.ShapeDtypeStruct(w.shape, BF16))
    out = pl.pallas_call(
        functools.partial(_ffn_kernel, n_cast=len(next_weights)),
        out_shape=(jax.ShapeDtypeStruct((t, d), F32), *cast_shapes),
        grid=(gi, gj),
        in_specs=[
            pl.BlockSpec((tm, d), lambda i, j: (i, 0)),
            pl.BlockSpec((1, d), lambda i, j: (0, 0)),
            pl.BlockSpec((d, tf), lambda i, j: (0, j)),
            pl.BlockSpec((d, tf), lambda i, j: (0, j)),
            pl.BlockSpec((tf, d), lambda i, j: (j, 0)),
            *cast_specs,
        ],
        out_specs=(pl.BlockSpec((tm, d), lambda i, j: (i, 0)), *cast_specs),
        scratch_shapes=[pltpu.VMEM((tm, d), BF16)],
        compiler_params=_cparams(("parallel", "arbitrary")),
        name="ffn",
    )(x, gain, wg, wu, wd, *next_weights)
    return out[0], out[1:]


def _in_proj_kernel(x_ref, g_ref, w_ref, wf_ref, hg_ref, fb_ref, wo_ref,
                    qt_ref, ka_ref, vt_ref, rest_ref, anc_ref, wo_bf_ref, carry_ref, *, blocks_per_seq):
    i = pl.program_id(0)
    tm = x_ref.shape[0]

    @pl.when(i == 0)
    def _():
        carry_ref[...] = jnp.zeros_like(carry_ref)

    wo_bf_ref[...] = wo_ref[...].astype(BF16)
    hn = _rms(x_ref[...], g_ref[...]).astype(BF16)

    nt = (((1,), (1,)), ((), ()))
    logit = lax.dot_general(hn, wf_ref[...], nt, preferred_element_type=F32)
    c = jax.nn.log_sigmoid(logit + fb_ref[...]) * LOG2E
    row = lax.broadcasted_iota(jnp.int32, c.shape, 0) % FOX_BLK
    sh = 1
    while sh < FOX_BLK:
        c = c + jnp.where(row >= sh, pltpu.roll(c, sh, 0), 0.0)
        sh *= 2

    anchor = jnp.where(i % blocks_per_seq == 0, 0.0, carry_ref[0:1, :])
    anc_ref[...] = jnp.zeros_like(anc_ref)
    for r in range(tm // FOX_BLK):
        anc_ref[0, r:r + 1, :] = anchor
        anchor = anchor + c[(r + 1) * FOX_BLK - 1:(r + 1) * FOX_BLK, :]
    carry_ref[0:1, :] = anchor

    n_sub = tm // FOX_BLK
    lane = lax.broadcasted_iota(jnp.int32, c.shape, 1)
    sub = lax.broadcasted_iota(jnp.int32, (HEAD_DIM, FOX_BLK), 0)
    ct = c.T
    ak = jnp.zeros(c.shape, F32)
    for h in range(FOX_HEADS):
        ch = c[:, h:h + 1]
        hi = ch.astype(BF16).astype(F32)
        lo = ch - hi
        own = lane - BIAS_LANES * h
        ak = jnp.where((own == 0) | (own == 1), 1.0, jnp.where(own == 2, -hi, jnp.where(own == 3, -lo, ak)))
        for r in range(n_sub):
            cr = ct[h:h + 1, r * FOX_BLK:(r + 1) * FOX_BLK]
            hi_r = cr.astype(BF16).astype(F32)
            lo_r = cr - hi_r
            own_r = sub - BIAS_LANES * h
            aq = jnp.where(own_r == 0, hi_r, jnp.where(own_r == 1, lo_r,
                                                       jnp.where((own_r == 2) | (own_r == 3), 1.0, 0.0)))
            qt_ref[r, (2 * h + 1) * HEAD_DIM:(2 * h + 2) * HEAD_DIM, :] = aq.astype(BF16)
    ka_ref[:, FOX_W:] = ak.astype(BF16)

    def normed(ph, row_id):
        return _rms(ph, hg_ref[row_id:row_id + 1, :]).astype(BF16)

    heads_per_chunk = MXU_N // HEAD_DIM
    sv_chunk = (3 * FOX_W + SWA_Q_W + SWA_KV_W) // MXU_N
    chunks = [ck for ck in range(PROJ_W // MXU_N) if ck != sv_chunk] + [sv_chunk]
    for ck in chunks:
        p = lax.dot_general(hn, w_ref[ck * MXU_N:(ck + 1) * MXU_N, :], nt, preferred_element_type=F32)
        for e in range(heads_per_chunk):
            col = ck * MXU_N + e * HEAD_DIM
            ph = p[:, e * HEAD_DIM:(e + 1) * HEAD_DIM]
            if col < FOX_W:
                h = col // HEAD_DIM
                qn = _rms(ph, hg_ref[GAIN_FQ:GAIN_FQ + 1, :])
                for r in range(n_sub):
                    qt_ref[r, 2 * h * HEAD_DIM:(2 * h + 1) * HEAD_DIM, :] = (
                        qn[r * FOX_BLK:(r + 1) * FOX_BLK, :].T.astype(BF16))
            elif col < 2 * FOX_W:
                h = (col - FOX_W) // HEAD_DIM
                ka_ref[:, h * HEAD_DIM:(h + 1) * HEAD_DIM] = normed(ph, GAIN_FK)
            elif col < 3 * FOX_W:
                h = (col - 2 * FOX_W) // HEAD_DIM
                for r in range(n_sub):
                    vt_ref[r, h * HEAD_DIM:(h + 1) * HEAD_DIM, :] = (
                        ph[r * FOX_BLK:(r + 1) * FOX_BLK, :].T.astype(BF16))
            else:
                wc = col - 3 * FOX_W
                if wc < SWA_Q_W:
                    val, rc = normed(ph, GAIN_SQ), REST_SQ + wc
                elif wc < SWA_Q_W + SWA_KV_W:
                    val, rc = normed(ph, GAIN_SK), REST_SK + wc - SWA_Q_W
                elif wc < SWA_Q_W + 2 * SWA_KV_W:
                    val, rc = ph.astype(BF16), REST_SV + 2 * (wc - SWA_Q_W - SWA_KV_W)
                    rest_ref[:, rc + HEAD_DIM:rc + 2 * HEAD_DIM] = jnp.ones((tm, HEAD_DIM), BF16)
                else:
                    val, rc = normed(ph, GAIN_MQ), REST_MQ + wc - SWA_Q_W - 2 * SWA_KV_W
                rest_ref[:, rc:rc + HEAD_DIM] = val


def _in_proj(x, gain, w, wf, head_gain, fbias, w_out, *, seq, tm=512):
    t, d = x.shape
    wo_rows = w_out.shape[0] // (t // tm)
    assert wo_rows % (2 * SUBLANES) == 0
    assert seq % tm == 0 and tm % FOX_BLK == 0 and tm // FOX_BLK <= SUBLANES
    const = lambda shape: pl.BlockSpec(shape, lambda i: (0, 0), pipeline_mode=pl.Buffered(1))
    return pl.pallas_call(
        functools.partial(_in_proj_kernel, blocks_per_seq=seq // tm),
        out_shape=(
            jax.ShapeDtypeStruct((t // FOX_BLK, 2 * FOX_W, FOX_BLK), BF16),
            jax.ShapeDtypeStruct((t, FOX_W + HEAD_DIM), BF16),
            jax.ShapeDtypeStruct((t // FOX_BLK, FOX_W, FOX_BLK), BF16),
            jax.ShapeDtypeStruct((t, REST_W), BF16),
            jax.ShapeDtypeStruct((t // tm, SUBLANES, LANES), F32),
            jax.ShapeDtypeStruct(w_out.shape, BF16),
        ),
        grid=(t // tm,),
        in_specs=[
            pl.BlockSpec((tm, d), lambda i: (i, 0)),
            const((1, d)),
            const((PROJ_W, d)),
            const((LANES, d)),
            const((SUBLANES, HEAD_DIM)),
            const((1, LANES)),
            pl.BlockSpec((wo_rows, w_out.shape[1]), lambda i: (i, 0)),
        ],
        out_specs=(
            pl.BlockSpec((tm // FOX_BLK, 2 * FOX_W, FOX_BLK), lambda i: (i, 0, 0)),
            pl.BlockSpec((tm, FOX_W + HEAD_DIM), lambda i: (i, 0)),
            pl.BlockSpec((tm // FOX_BLK, FOX_W, FOX_BLK), lambda i: (i, 0, 0)),
            pl.BlockSpec((tm, REST_W), lambda i: (i, 0)),
            pl.BlockSpec((1, SUBLANES, LANES), lambda i: (i, 0, 0)),
            pl.BlockSpec((wo_rows, w_out.shape[1]), lambda i: (i, 0)),
        ),
        scratch_shapes=[pltpu.VMEM((SUBLANES, LANES), F32)],
        compiler_params=_cparams(("arbitrary",)),
        name="in_proj",
    )(x, gain, w, wf, head_gain, fbias, w_out)


def _fox_kernel(anc_ref, q_ref, k_ref, vt_ref, o_ref, m_ref, l_ref, acc_ref, st_ref, *, nblk):
    lax.fori_loop(0, q_ref.shape[0], functools.partial(
        _fox_query_block, anc_ref, q_ref, k_ref, vt_ref, o_ref, m_ref, l_ref, acc_ref, st_ref, nblk), 0)


def _fox_query_block(anc_ref, q_ref, k_ref, vt_ref, o_ref, m_ref, l_ref, acc_ref, st_ref, nblk, sub, carry):
    q_per_step, _, blk = q_ref.shape
    aw = 2 * HEAD_DIM
    b = pl.program_id(0)
    i = pl.program_id(1) * q_per_step + sub
    m_ref[...] = jnp.full(m_ref.shape, NEG_INF, F32)
    l_ref[...] = jnp.zeros(l_ref.shape, F32)
    acc_ref[...] = jnp.zeros(acc_ref.shape, F32)

    def scores(j, slot):
        ks = pl.ds(pl.multiple_of(j * blk, blk), blk)
        k_bias = k_ref[ks, FOX_W:]
        for h in range(FOX_HEADS):
            k_aug = jnp.concatenate([k_ref[ks, h * HEAD_DIM:(h + 1) * HEAD_DIM], k_bias], axis=1)
            st_ref[slot, h] = jnp.dot(k_aug, q_ref[sub, h * aw:(h + 1) * aw, :],
                                      preferred_element_type=F32)

    def softmax_pv(j, slot, masked):
        if masked:
            causal = (lax.broadcasted_iota(jnp.int32, (blk, blk), 0)
                      <= lax.broadcasted_iota(jnp.int32, (blk, blk), 1))
        for h in range(FOX_HEADS):
            st = st_ref[slot, h]
            d = anc_ref[(b * nblk + i) * SUBLANES + h] - anc_ref[(b * nblk + j) * SUBLANES + h]
            if masked:
                st = jnp.where(causal, st, NEG_INF)
            m_prev = m_ref[h]
            m_new = jnp.maximum(m_prev, jnp.max(st, axis=0, keepdims=True) + d)
            alpha = jnp.exp2(m_prev - m_new)
            p = jnp.exp2(st - (m_new - d))
            l_ref[h] = alpha * l_ref[h] + jnp.sum(p, axis=0, keepdims=True)
            pv = jnp.dot(vt_ref[j, h * HEAD_DIM:(h + 1) * HEAD_DIM, :], p.astype(BF16),
                         preferred_element_type=F32)
            acc_ref[h] = alpha * acc_ref[h] + pv
            m_ref[h] = m_new

    def body(t, c):
        j = 2 * t
        scores(j + 1, 1)
        softmax_pv(j, 0, False)
        scores(j + 2, 0)
        softmax_pv(j + 1, 1, False)
        return c

    scores(0, 0)
    lax.fori_loop(0, i // 2, body, 0)

    @pl.when(i % 2 == 0)
    def _():
        softmax_pv(i, 0, True)

    @pl.when(i % 2 == 1)
    def _():
        scores(i, 1)
        softmax_pv(i - 1, 0, False)
        softmax_pv(i, 1, True)

    rows = pl.ds(pl.multiple_of(sub * blk, blk), blk)
    for h in range(FOX_HEADS):
        o_ref[rows, h * HEAD_DIM:(h + 1) * HEAD_DIM] = (acc_ref[h] / l_ref[h]).T.astype(o_ref.dtype)
    return carry


def _fox(anchors, qt, ka, vt, *, batch, seq, q_per_step=2):
    t = ka.shape[0]
    blk = FOX_BLK
    nblk = seq // blk
    steps = nblk // q_per_step
    return pl.pallas_call(
        functools.partial(_fox_kernel, nblk=nblk),
        out_shape=jax.ShapeDtypeStruct((t, FOX_W), BF16),
        grid=(batch, steps),
        in_specs=[
            pl.BlockSpec(memory_space=pltpu.SMEM),
            pl.BlockSpec((q_per_step, 2 * FOX_W, blk), lambda b, i: (b * steps + i, 0, 0)),
            pl.BlockSpec((seq, FOX_W + HEAD_DIM), lambda b, i: (b, 0)),
            pl.BlockSpec((nblk, FOX_W, blk), lambda b, i: (b, 0, 0)),
        ],
        out_specs=pl.BlockSpec((q_per_step * blk, FOX_W), lambda b, i: (b * steps + i, 0)),
        scratch_shapes=[
            pltpu.VMEM((FOX_HEADS, 1, blk), F32),
            pltpu.VMEM((FOX_HEADS, 1, blk), F32),
            pltpu.VMEM((FOX_HEADS, HEAD_DIM, blk), F32),
            pltpu.VMEM((2, FOX_HEADS, blk, blk), F32),
        ],
        compiler_params=_cparams(("parallel", "arbitrary")),
        name="fox",
    )(anchors, qt, ka, vt)


def _swa_kernel(q_ref, k_ref, v_ref, sink_ref, slope_ref, o_ref):
    qi = pl.program_id(1)
    tq = q_ref.shape[0]
    w = WINDOW
    n_sub = tq // w
    row = lax.broadcasted_iota(jnp.int32, (w, 2 * w), 0)
    col = lax.broadcasted_iota(jnp.int32, (w, 2 * w), 1)

    def bias_tables(offset):
        dist = offset + row - col
        valid = (dist >= 0) & (dist < w)
        dist_f = dist.astype(F32)
        return [jnp.where(valid, -(slope_ref[h] * LOG2E) * dist_f, NEG_INF) for h in range(SWA_HEADS)]

    later_bias = bias_tables(w)
    for r in range(n_sub):
        n = qi * n_sub + r
        kstart = pl.multiple_of(jnp.maximum(n - 1, 0) * w, w)
        bias = bias_tables(n * w - kstart) if r == 0 else later_bias
        for h in range(SWA_HEADS):
            g = h // SWA_GROUP
            q = q_ref[r * w:(r + 1) * w, h * HEAD_DIM:(h + 1) * HEAD_DIM]
            k = k_ref[pl.ds(kstart, 2 * w), g * HEAD_DIM:(g + 1) * HEAD_DIM]
            v1 = v_ref[pl.ds(kstart, 2 * w), 2 * g * HEAD_DIM:2 * (g + 1) * HEAD_DIM]
            s = lax.dot_general(q, k, (((1,), (1,)), ((), ())), preferred_element_type=F32) + bias[h]
            sink = sink_ref[h] * LOG2E
            m = jnp.maximum(jnp.max(s, axis=-1, keepdims=True), sink)
            p = jnp.exp2(s - m)
            o1 = jnp.dot(p.astype(BF16), v1, preferred_element_type=F32)
            denom = o1[:, HEAD_DIM:] + jnp.exp2(sink - m)
            o_ref[r * w:(r + 1) * w, h * HEAD_DIM:(h + 1) * HEAD_DIM] = (
                o1[:, :HEAD_DIM] / denom).astype(o_ref.dtype)


def _swa(rest, sinks, slopes, *, batch, seq, tq=2048):
    t = rest.shape[0]
    nq = seq // tq
    smem = pl.BlockSpec(memory_space=pltpu.SMEM)
    return pl.pallas_call(
        _swa_kernel,
        out_shape=jax.ShapeDtypeStruct((t, SWA_Q_W), BF16),
        grid=(batch, nq),
        in_specs=[
            pl.BlockSpec((tq, SWA_Q_W), lambda b, i: (b * nq + i, REST_SQ // SWA_Q_W)),
            pl.BlockSpec((seq, SWA_KV_W), lambda b, i: (b, REST_SK // SWA_KV_W)),
            pl.BlockSpec((seq, 2 * SWA_KV_W), lambda b, i: (b, REST_SV // (2 * SWA_KV_W))),
            smem, smem,
        ],
        out_specs=pl.BlockSpec((tq, SWA_Q_W), lambda b, i: (b * nq + i, 0)),
        compiler_params=_cparams(("parallel", "arbitrary")),
        name="swa",
    )(rest, rest, rest, sinks, slopes)


def _mem_kv_kernel(mem_ref, g_ref, wk_ref, wv_ref, kg_ref, mk_ref, mv_ref):
    mn = _rms(mem_ref[...], g_ref[...]).astype(BF16)
    k = jnp.dot(mn, wk_ref[...].astype(BF16), preferred_element_type=F32)
    v = jnp.dot(mn, wv_ref[...].astype(BF16), preferred_element_type=F32)
    ones = jnp.ones((mem_ref.shape[0], HEAD_DIM), BF16)
    for h in range(MEM_HEADS):
        hs = slice(h * HEAD_DIM, (h + 1) * HEAD_DIM)
        mk_ref[:, hs] = _rms(k[:, hs], kg_ref[...]).astype(BF16)
        mv_ref[:, 2 * h * HEAD_DIM:(2 * h + 1) * HEAD_DIM] = v[:, hs].astype(BF16)
        mv_ref[:, (2 * h + 1) * HEAD_DIM:(2 * h + 2) * HEAD_DIM] = ones


def _mem_kv(mem2d, gain, wk, wv, kgain):
    rows, d = mem2d.shape
    full = lambda shape: pl.BlockSpec(shape, lambda i: (0, 0))
    return pl.pallas_call(
        _mem_kv_kernel,
        out_shape=(jax.ShapeDtypeStruct((rows, MEM_W), BF16),
                   jax.ShapeDtypeStruct((rows, 2 * MEM_W), BF16)),
        grid=(1,),
        in_specs=[full((rows, d)), full((1, d)), full((d, MEM_W)), full((d, MEM_W)), full((1, HEAD_DIM))],
        out_specs=(full((rows, MEM_W)), full((rows, 2 * MEM_W))),
        compiler_params=_cparams(("arbitrary",)),
        name="mem_kv",
    )(mem2d, gain, wk, wv, kgain)


def _mem_attn_kernel(q_ref, mk_ref, mv_ref, o_ref):
    for h in range(MEM_HEADS):
        hs = slice(h * HEAD_DIM, (h + 1) * HEAD_DIM)
        s = lax.dot_general(q_ref[:, hs], mk_ref[:, hs], (((1,), (1,)), ((), ())),
                            preferred_element_type=F32)
        m = jnp.max(s, axis=-1, keepdims=True)
        p = jnp.exp2(s - m)
        o1 = jnp.dot(p.astype(BF16), mv_ref[:, 2 * h * HEAD_DIM:(2 * h + 2) * HEAD_DIM],
                     preferred_element_type=F32)
        o_ref[:, hs] = (o1[:, :HEAD_DIM] / o1[:, HEAD_DIM:]).astype(o_ref.dtype)


def _mem_attn(rest, mk, mv, *, batch, seq, mem_len, tq=2048):
    t = rest.shape[0]
    nq = seq // tq
    return pl.pallas_call(
        _mem_attn_kernel,
        out_shape=jax.ShapeDtypeStruct((t, MEM_W), BF16),
        grid=(batch, nq),
        in_specs=[
            pl.BlockSpec((tq, MEM_W), lambda b, i: (b * nq + i, REST_MQ // MEM_W)),
            pl.BlockSpec((mem_len, MEM_W), lambda b, i: (b, 0)),
            pl.BlockSpec((mem_len, 2 * MEM_W), lambda b, i: (b, 0)),
        ],
        out_specs=pl.BlockSpec((tq, MEM_W), lambda b, i: (b * nq + i, 0)),
        compiler_params=_cparams(("parallel", "arbitrary")),
        name="mem_attn",
    )(rest, mk, mv)


def _out_proj_kernel(x_ref, a_ref, b_ref, c_ref, w_ref, o_ref):
    acc = jnp.dot(a_ref[...], w_ref[0:FOX_W, :], preferred_element_type=F32)
    acc += jnp.dot(b_ref[...], w_ref[FOX_W:FOX_W + SWA_Q_W, :], preferred_element_type=F32)
    acc += jnp.dot(c_ref[...], w_ref[FOX_W + SWA_Q_W:, :], preferred_element_type=F32)
    o_ref[...] = x_ref[...] + acc


def _out_proj(x, a, b, c, w, *, tm=1024):
    t, d = x.shape
    return pl.pallas_call(
        _out_proj_kernel,
        out_shape=jax.ShapeDtypeStruct((t, d), F32),
        grid=(t // tm,),
        in_specs=[
            pl.BlockSpec((tm, d), lambda i: (i, 0)),
            pl.BlockSpec((tm, FOX_W), lambda i: (i, 0)),
            pl.BlockSpec((tm, SWA_Q_W), lambda i: (i, 0)),
            pl.BlockSpec((tm, MEM_W), lambda i: (i, 0)),
            pl.BlockSpec(w.shape, lambda i: (0, 0), pipeline_mode=pl.Buffered(1)),
        ],
        out_specs=pl.BlockSpec((tm, d), lambda i: (i, 0)),
        compiler_params=_cparams(("parallel",)),
        name="out_proj",
    )(x, a, b, c, w)


def _alibi_slopes(n):
    return jnp.asarray([2.0 ** (-8.0 * i / n) for i in range(1, n + 1)], dtype=F32)


def _pack_w_in_kernel(w_ref, *rest, n_main):
    n_ride = (len(rest) - 2) // 2
    ride_in, (main_ref, f_ref), ride_out = rest[:n_ride], rest[n_ride:n_ride + 2], rest[n_ride + 2:]
    i = pl.program_id(0)
    n_a = w_ref.shape[2]

    @pl.when(i < n_main)
    def _():
        for a in range(n_a):
            main_ref[:, a * LANES:(a + 1) * LANES] = w_ref[0, :, a, :].astype(BF16)
        for src, dst in zip(ride_in, ride_out):
            dst[...] = src[...].astype(BF16)

    @pl.when(i == n_main)
    def _():
        row = lax.broadcasted_iota(jnp.int32, (f_ref.shape[0], LANES), 0)
        for a in range(n_a):
            f_ref[:, a * LANES:(a + 1) * LANES] = jnp.where(
                row < FOX_HEADS, w_ref[0, 0:f_ref.shape[0], a, :], 0.0).astype(BF16)


def _pack_w_in(w_in, l, *, ride=(), rows=256):
    d, n = w_in.shape[1:]
    fl0 = 3 * FOX_W
    assert n == PROJ_W + FOX_HEADS and fl0 % rows == 0 and PROJ_W % rows == 0 and d % LANES == 0
    wt = jnp.swapaxes(w_in, 1, 2).reshape(w_in.shape[0], n, d // LANES, LANES)
    n_main = PROJ_W // rows

    def start(i):
        main = jnp.where(i * rows < fl0, i * rows, i * rows + FOX_HEADS)
        return jnp.where(i < n_main, main, fl0)

    last_main = lambda i: (jnp.minimum(i, n_main - 1), 0)
    ride_specs = []
    for w in ride:
        assert w.shape[0] % (n_main * 2 * SUBLANES) == 0
        ride_specs.append(pl.BlockSpec((w.shape[0] // n_main, w.shape[1]), last_main))
    out = pl.pallas_call(
        functools.partial(_pack_w_in_kernel, n_main=n_main),
        out_shape=(jax.ShapeDtypeStruct((PROJ_W, d), BF16), jax.ShapeDtypeStruct((LANES, d), BF16),
                   *[jax.ShapeDtypeStruct(w.shape, BF16) for w in ride]),
        grid=(n_main + 1,),
        in_specs=[pl.BlockSpec((pl.Element(1), pl.Element(rows), pl.Element(d // LANES), pl.Element(LANES)),
                               lambda i: (l, start(i), 0, 0)),
                  *ride_specs],
        out_specs=(pl.BlockSpec((rows, d), last_main), pl.BlockSpec((LANES, d), lambda i: (0, 0)),
                   *ride_specs),
        compiler_params=_cparams(("arbitrary",)),
        name="pack_w_in",
    )(wt, *ride)
    return out[0], out[1], out[2:]


def kernel(x, mem, ffn1_norm, ffn1_gate, ffn1_up, ffn1_down, mix_norm, mem_norm, w_in, forget_bias, w_mem_k, w_mem_v, fox_q_gain, fox_k_gain, swa_q_gain, swa_k_gain, swa_sinks, mem_q_gain, mem_k_gain, w_out, ffn2_norm, ffn2_gate, ffn2_up, ffn2_down):
    batch, seq, d = x.shape
    mem_len = mem.shape[1]
    depth = w_in.shape[0]
    scale = HEAD_DIM ** -0.5
    slopes = _alibi_slopes(SWA_HEADS)
    x2 = x.reshape(batch * seq, d)
    mem2 = mem.reshape(batch * mem_len, d)
    zeros = jnp.zeros((HEAD_DIM,), F32)

    for l in range(depth):
        w_main, w_f, ffn1_w = _pack_w_in(w_in, l, ride=(ffn1_gate[l], ffn1_up[l], ffn1_down[l]))
        x2, ffn2_w = _ffn(x2, ffn1_norm[l][None], *ffn1_w,
                          next_weights=(ffn2_gate[l], ffn2_up[l], ffn2_down[l]))

        qs = scale * LOG2E
        head_gain = jnp.stack([fox_q_gain[l] * qs, fox_k_gain[l], swa_q_gain[l] * qs,
                               swa_k_gain[l], mem_q_gain[l] * qs, zeros, zeros, zeros])
        fbias = jnp.pad(forget_bias[l], (0, LANES - FOX_HEADS))[None]
        qt, ka, vt, rest, anc, w_out_bf = _in_proj(x2, mix_norm[l][None], w_main, w_f, head_gain, fbias,
                                                   w_out[l], seq=seq)

        n_sub = (x2.shape[0] // FOX_BLK) // anc.shape[0]
        anchors = anc[:, :n_sub, :SUBLANES].reshape(-1)
        out_a = _fox(anchors, qt, ka, vt, batch=batch, seq=seq)
        out_b = _swa(rest, swa_sinks[l], slopes, batch=batch, seq=seq)
        mk, mv = _mem_kv(mem2, mem_norm[l][None], w_mem_k[l], w_mem_v[l], mem_k_gain[l][None])
        out_c = _mem_attn(rest, mk, mv, batch=batch, seq=seq, mem_len=mem_len)

        x2 = _out_proj(x2, out_a, out_b, out_c, w_out_bf)
        x2, _ = _ffn(x2, ffn2_norm[l][None], *ffn2_w)
    return x2.reshape(batch, seq, d)
```

```python
import functools

import jax
import jax.numpy as jnp
from jax import lax
from jax.experimental import pallas as pl
from jax.experimental.pallas import tpu as pltpu

F32 = jnp.float32
BF16 = jnp.bfloat16

HEAD_DIM = 128
FOX_HEADS = 6
SWA_HEADS = 6
SWA_KV_HEADS = 2
SWA_GROUP = SWA_HEADS // SWA_KV_HEADS
MEM_HEADS = 4
WINDOW = 128
EPS = 1e-6
NEG_INF = -1e30
LOG2E = 1.4426950408889634
FOX_BLK = 512
BIAS_LANES = 4
NORM_ROWS = 512

FOX_W = FOX_HEADS * HEAD_DIM
SWA_Q_W = SWA_HEADS * HEAD_DIM
SWA_KV_W = SWA_KV_HEADS * HEAD_DIM
MEM_W = MEM_HEADS * HEAD_DIM

PROJ_W = 3 * FOX_W + SWA_Q_W + 2 * SWA_KV_W + MEM_W
REST_W = SWA_Q_W + 3 * SWA_KV_W + MEM_W
REST_SQ, REST_SK, REST_MQ, REST_SV = 0, SWA_Q_W, SWA_Q_W + SWA_KV_W, SWA_Q_W + SWA_KV_W + MEM_W
GAIN_FQ, GAIN_FK, GAIN_SQ, GAIN_SK, GAIN_MQ = range(5)

LANES = 128
SUBLANES = 8
MXU_N = 256

VMEM_LIMIT = 58 * 1024 * 1024


def _cparams(sem):
    return pltpu.CompilerParams(dimension_semantics=sem, vmem_limit_bytes=VMEM_LIMIT)


def _rms(x, gain):
    ms = jnp.mean(x * x, axis=-1, keepdims=True)
    return x * lax.rsqrt(ms + EPS) * gain


def _ffn_kernel(x_ref, g_ref, wg_ref, wu_ref, wd_ref, *rest, n_cast):
    cast_in, o_ref, cast_out, xn_ref = rest[:n_cast], rest[n_cast], rest[n_cast + 1:-1], rest[-1]
    j = pl.program_id(1)

    def ride_along():
        for src, dst in zip(cast_in, cast_out):
            dst[...] = src[...].astype(BF16)

    def half_swiglu(xn):
        acc = None
        for c0 in range(0, wg_ref.shape[1], MXU_N):
            cs = slice(c0, c0 + MXU_N)
            gate = jnp.dot(xn, wg_ref[:, cs], preferred_element_type=F32)
            up = jnp.dot(xn, wu_ref[:, cs], preferred_element_type=F32)
            h = (gate * jax.nn.sigmoid(gate) * (0.5 * up)).astype(BF16)
            part = jnp.dot(h, wd_ref[cs, :], preferred_element_type=F32)
            acc = part if acc is None else acc + part
        return acc

    @pl.when(j == 0)
    def _():
        for r0 in range(0, x_ref.shape[0], NORM_ROWS):
            rows = slice(r0, r0 + NORM_ROWS)
            x = x_ref[rows, :]
            xn = _rms(x, g_ref[...]).astype(BF16)
            xn_ref[rows, :] = xn
            o_ref[rows, :] = x + half_swiglu(xn)
        ride_along()

    @pl.when(j > 0)
    def _():
        o_ref[...] += half_swiglu(xn_ref[...])
        ride_along()


def _ffn(x, gain, wg, wu, wd, *, next_weights=(), tm=1024, tf=512):
    t, d = x.shape
    dff = wg.shape[1]
    gi, gj = t // tm, dff // tf
    cast_specs, cast_shapes = [], []
    for w in next_weights:
        ri, cj = (gi, gj) if w.shape[1] == dff else (gj, gi)
        blk = (w.shape[0] // ri, w.shape[1] // cj)
        assert blk[0] % SUBLANES == 0 and blk[1] % LANES == 0
        index = (lambda i, j: (i, j)) if w.shape[1] == dff else (lambda i, j: (j, i))
        cast_specs.append(pl.BlockSpec(blk, index))
        cast_shapes.append(jax.ShapeDtypeStruct(w.shape, BF16))
    out = pl.pallas_call(
        functools.partial(_ffn_kernel, n_cast=len(next_weights)),
        out_shape=(jax.ShapeDtypeStruct((t, d), F32), *cast_shapes),
        grid=(gi, gj),
        in_specs=[
            pl.BlockSpec((tm, d), lambda i, j: (i, 0)),
            pl.BlockSpec((1, d), lambda i, j: (0, 0)),
            pl.BlockSpec((d, tf), lambda i, j: (0, j)),
            pl.BlockSpec((d, tf), lambda i, j: (0, j)),
            pl.BlockSpec((tf, d), lambda i, j: (j, 0)),
            *cast_specs,
        ],
        out_specs=(pl.BlockSpec((tm, d), lambda i, j: (i, 0)), *cast_specs),
        scratch_shapes=[pltpu.VMEM((tm, d), BF16)],
        compiler_params=_cparams(("parallel", "arbitrary")),
        name="ffn",
    )(x, gain, wg, wu, wd, *next_weights)
    return out[0], out[1:]


def _in_proj_kernel(x_ref, g_ref, w_ref, wf_ref, hg_ref, fb_ref, wo_ref,
                    qt_ref, ka_ref, vt_ref, rest_ref, anc_ref, wo_bf_ref, carry_ref, *, blocks_per_seq):
    i = pl.program_id(0)
    tm = x_ref.shape[0]

    @pl.when(i == 0)
    def _():
        carry_ref[...] = jnp.zeros_like(carry_ref)

    wo_bf_ref[...] = wo_ref[...].astype(BF16)
    hn = _rms(x_ref[...], g_ref[...]).astype(BF16)

    nt = (((1,), (1,)), ((), ()))
    logit = lax.dot_general(hn, wf_ref[...], nt, preferred_element_type=F32)
    c = jax.nn.log_sigmoid(logit + fb_ref[...]) * LOG2E
    row = lax.broadcasted_iota(jnp.int32, c.shape, 0) % FOX_BLK
    sh = 1
    while sh < FOX_BLK:
        c = c + jnp.where(row >= sh, pltpu.roll(c, sh, 0), 0.0)
        sh *= 2

    anchor = jnp.where(i % blocks_per_seq == 0, 0.0, carry_ref[0:1, :])
    anc_ref[...] = jnp.zeros_like(anc_ref)
    for r in range(tm // FOX_BLK):
        anc_ref[0, r:r + 1, :] = anchor
        anchor = anchor + c[(r + 1) * FOX_BLK - 1:(r + 1) * FOX_BLK, :]
    carry_ref[0:1, :] = anchor

    n_sub = tm // FOX_BLK
    lane = lax.broadcasted_iota(jnp.int32, c.shape, 1)
    sub = lax.broadcasted_iota(jnp.int32, (HEAD_DIM, FOX_BLK), 0)
    ct = c.T
    ak = jnp.zeros(c.shape, F32)
    for h in range(FOX_HEADS):
        ch = c[:, h:h + 1]
        hi = ch.astype(BF16).astype(F32)
        lo = ch - hi
        own = lane - BIAS_LANES * h
        ak = jnp.where((own == 0) | (own == 1), 1.0, jnp.where(own == 2, -hi, jnp.where(own == 3, -lo, ak)))
        for r in range(n_sub):
            cr = ct[h:h + 1, r * FOX_BLK:(r + 1) * FOX_BLK]
            hi_r = cr.astype(BF16).astype(F32)
            lo_r = cr - hi_r
            own_r = sub - BIAS_LANES * h
            aq = jnp.where(own_r == 0, hi_r, jnp.where(own_r == 1, lo_r,
                                                       jnp.where((own_r == 2) | (own_r == 3), 1.0, 0.0)))
            qt_ref[r, (2 * h + 1) * HEAD_DIM:(2 * h + 2) * HEAD_DIM, :] = aq.astype(BF16)
    ka_ref[:, FOX_W:] = ak.astype(BF16)

    def normed(ph, row_id):
        return _rms(ph, hg_ref[row_id:row_id + 1, :]).astype(BF16)

    heads_per_chunk = MXU_N // HEAD_DIM
    sv_chunk = (3 * FOX_W + SWA_Q_W + SWA_KV_W) // MXU_N
    chunks = [ck for ck in range(PROJ_W // MXU_N) if ck != sv_chunk] + [sv_chunk]
    for ck in chunks:
        p = lax.dot_general(hn, w_ref[ck * MXU_N:(ck + 1) * MXU_N, :], nt, preferred_element_type=F32)
        for e in range(heads_per_chunk):
            col = ck * MXU_N + e * HEAD_DIM
            ph = p[:, e * HEAD_DIM:(e + 1) * HEAD_DIM]
            if col < FOX_W:
                h = col // HEAD_DIM
                qn = _rms(ph, hg_ref[GAIN_FQ:GAIN_FQ + 1, :])
                for r in range(n_sub):
                    qt_ref[r, 2 * h * HEAD_DIM:(2 * h + 1) * HEAD_DIM, :] = (
                        qn[r * FOX_BLK:(r + 1) * FOX_BLK, :].T.astype(BF16))
            elif col < 2 * FOX_W:
                h = (col - FOX_W) // HEAD_DIM
                ka_ref[:, h * HEAD_DIM:(h + 1) * HEAD_DIM] = normed(ph, GAIN_FK)
            elif col < 3 * FOX_W:
                h = (col - 2 * FOX_W) // HEAD_DIM
                for r in range(n_sub):
                    vt_ref[r, h * HEAD_DIM:(h + 1) * HEAD_DIM, :] = (
                        ph[r * FOX_BLK:(r + 1) * FOX_BLK, :].T.astype(BF16))
            else:
                wc = col - 3 * FOX_W
                if wc < SWA_Q_W:
                    val, rc = normed(ph, GAIN_SQ), REST_SQ + wc
                elif wc < SWA_Q_W + SWA_KV_W:
                    val, rc = normed(ph, GAIN_SK), REST_SK + wc - SWA_Q_W
                elif wc < SWA_Q_W + 2 * SWA_KV_W:
                    val, rc = ph.astype(BF16), REST_SV + 2 * (wc - SWA_Q_W - SWA_KV_W)
                    rest_ref[:, rc + HEAD_DIM:rc + 2 * HEAD_DIM] = jnp.ones((tm, HEAD_DIM), BF16)
                else:
                    val, rc = normed(ph, GAIN_MQ), REST_MQ + wc - SWA_Q_W - 2 * SWA_KV_W
                rest_ref[:, rc:rc + HEAD_DIM] = val


def _in_proj(x, gain, w, wf, head_gain, fbias, w_out, *, seq, tm=512):
    t, d = x.shape
    wo_rows = w_out.shape[0] // (t // tm)
    assert wo_rows % (2 * SUBLANES) == 0
    assert seq % tm == 0 and tm % FOX_BLK == 0 and tm // FOX_BLK <= SUBLANES
    const = lambda shape: pl.BlockSpec(shape, lambda i: (0, 0), pipeline_mode=pl.Buffered(1))
    return pl.pallas_call(
        functools.partial(_in_proj_kernel, blocks_per_seq=seq // tm),
        out_shape=(
            jax.ShapeDtypeStruct((t // FOX_BLK, 2 * FOX_W, FOX_BLK), BF16),
            jax.ShapeDtypeStruct((t, FOX_W + HEAD_DIM), BF16),
            jax.ShapeDtypeStruct((t // FOX_BLK, FOX_W, FOX_BLK), BF16),
            jax.ShapeDtypeStruct((t, REST_W), BF16),
            jax.ShapeDtypeStruct((t // tm, SUBLANES, LANES), F32),
            jax.ShapeDtypeStruct(w_out.shape, BF16),
        ),
        grid=(t // tm,),
        in_specs=[
            pl.BlockSpec((tm, d), lambda i: (i, 0)),
            const((1, d)),
            const((PROJ_W, d)),
            const((LANES, d)),
            const((SUBLANES, HEAD_DIM)),
            const((1, LANES)),
            pl.BlockSpec((wo_rows, w_out.shape[1]), lambda i: (i, 0)),
        ],
        out_specs=(
            pl.BlockSpec((tm // FOX_BLK, 2 * FOX_W, FOX_BLK), lambda i: (i, 0, 0)),
            pl.BlockSpec((tm, FOX_W + HEAD_DIM), lambda i: (i, 0)),
            pl.BlockSpec((tm // FOX_BLK, FOX_W, FOX_BLK), lambda i: (i, 0, 0)),
            pl.BlockSpec((tm, REST_W), lambda i: (i, 0)),
            pl.BlockSpec((1, SUBLANES, LANES), lambda i: (i, 0, 0)),
            pl.BlockSpec((wo_rows, w_out.shape[1]), lambda i: (i, 0)),
        ),
        scratch_shapes=[pltpu.VMEM((SUBLANES, LANES), F32)],
        compiler_params=_cparams(("arbitrary",)),
        name="in_proj",
    )(x, gain, w, wf, head_gain, fbias, w_out)


def _fox_kernel(anc_ref, q_ref, k_ref, vt_ref, o_ref, m_ref, l_ref, acc_ref, st_ref, *, nblk):
    lax.fori_loop(0, q_ref.shape[0], functools.partial(
        _fox_query_block, anc_ref, q_ref, k_ref, vt_ref, o_ref, m_ref, l_ref, acc_ref, st_ref, nblk), 0)


def _fox_query_block(anc_ref, q_ref, k_ref, vt_ref, o_ref, m_ref, l_ref, acc_ref, st_ref, nblk, sub, carry):
    q_per_step, _, blk = q_ref.shape
    aw = 2 * HEAD_DIM
    b = pl.program_id(0)
    i = pl.program_id(1) * q_per_step + sub
    m_ref[...] = jnp.full(m_ref.shape, NEG_INF, F32)
    l_ref[...] = jnp.zeros(l_ref.shape, F32)
    acc_ref[...] = jnp.zeros(acc_ref.shape, F32)

    def scores(j, slot):
        ks = pl.ds(pl.multiple_of(j * blk, blk), blk)
        k_bias = k_ref[ks, FOX_W:]
        for h in range(FOX_HEADS):
            k_aug = jnp.concatenate([k_ref[ks, h * HEAD_DIM:(h + 1) * HEAD_DIM], k_bias], axis=1)
            st_ref[slot, h] = jnp.dot(k_aug, q_ref[sub, h * aw:(h + 1) * aw, :],
                                      preferred_element_type=F32)

    def softmax_pv(j, slot, masked):
        if masked:
            causal = (lax.broadcasted_iota(jnp.int32, (blk, blk), 0)
                      <= lax.broadcasted_iota(jnp.int32, (blk, blk), 1))
        for h in range(FOX_HEADS):
            st = st_ref[slot, h]
            d = anc_ref[(b * nblk + i) * SUBLANES + h] - anc_ref[(b * nblk + j) * SUBLANES + h]
            if masked:
                st = jnp.where(causal, st, NEG_INF)
            m_prev = m_ref[h]
            m_new = jnp.maximum(m_prev, jnp.max(st, axis=0, keepdims=True) + d)
            alpha = jnp.exp2(m_prev - m_new)
            p = jnp.exp2(st - (m_new - d))
            l_ref[h] = alpha * l_ref[h] + jnp.sum(p, axis=0, keepdims=True)
            pv = jnp.dot(vt_ref[j, h * HEAD_DIM:(h + 1) * HEAD_DIM, :], p.astype(BF16),
                         preferred_element_type=F32)
            acc_ref[h] = alpha * acc_ref[h] + pv
            m_ref[h] = m_new

    def body(t, c):
        j = 2 * t
        scores(j + 1, 1)
        softmax_pv(j, 0, False)
        scores(j + 2, 0)
        softmax_pv(j + 1, 1, False)
        return c

    scores(0, 0)
    lax.fori_loop(0, i // 2, body, 0)

    @pl.when(i % 2 == 0)
    def _():
        softmax_pv(i, 0, True)

    @pl.when(i % 2 == 1)
    def _():
        scores(i, 1)
        softmax_pv(i - 1, 0, False)
        softmax_pv(i, 1, True)

    rows = pl.ds(pl.multiple_of(sub * blk, blk), blk)
    for h in range(FOX_HEADS):
        o_ref[rows, h * HEAD_DIM:(h + 1) * HEAD_DIM] = (acc_ref[h] / l_ref[h]).T.astype(o_ref.dtype)
    return carry


def _fox(anchors, qt, ka, vt, *, batch, seq, q_per_step=2):
    t = ka.shape[0]
    blk = FOX_BLK
    nblk = seq // blk
    steps = nblk // q_per_step
    return pl.pallas_call(
        functools.partial(_fox_kernel, nblk=nblk),
        out_shape=jax.ShapeDtypeStruct((t, FOX_W), BF16),
        grid=(batch, steps),
        in_specs=[
            pl.BlockSpec(memory_space=pltpu.SMEM),
            pl.BlockSpec((q_per_step, 2 * FOX_W, blk), lambda b, i: (b * steps + i, 0, 0)),
            pl.BlockSpec((seq, FOX_W + HEAD_DIM), lambda b, i: (b, 0)),
            pl.BlockSpec((nblk, FOX_W, blk), lambda b, i: (b, 0, 0)),
        ],
        out_specs=pl.BlockSpec((q_per_step * blk, FOX_W), lambda b, i: (b * steps + i, 0)),
        scratch_shapes=[
            pltpu.VMEM((FOX_HEADS, 1, blk), F32),
            pltpu.VMEM((FOX_HEADS, 1, blk), F32),
            pltpu.VMEM((FOX_HEADS, HEAD_DIM, blk), F32),
            pltpu.VMEM((2, FOX_HEADS, blk, blk), F32),
        ],
        compiler_params=_cparams(("parallel", "arbitrary")),
        name="fox",
    )(anchors, qt, ka, vt)


def _swa_kernel(q_ref, k_ref, v_ref, sink_ref, slope_ref, o_ref):
    qi = pl.program_id(1)
    tq = q_ref.shape[0]
    w = WINDOW
    n_sub = tq // w
    row = lax.broadcasted_iota(jnp.int32, (w, 2 * w), 0)
    col = lax.broadcasted_iota(jnp.int32, (w, 2 * w), 1)

    def bias_tables(offset):
        dist = offset + row - col
        valid = (dist >= 0) & (dist < w)
        dist_f = dist.astype(F32)
        return [jnp.where(valid, -(slope_ref[h] * LOG2E) * dist_f, NEG_INF) for h in range(SWA_HEADS)]

    later_bias = bias_tables(w)
    for r in range(n_sub):
        n = qi * n_sub + r
        kstart = pl.multiple_of(jnp.maximum(n - 1, 0) * w, w)
        bias = bias_tables(n * w - kstart) if r == 0 else later_bias
        for h in range(SWA_HEADS):
            g = h // SWA_GROUP
            q = q_ref[r * w:(r + 1) * w, h * HEAD_DIM:(h + 1) * HEAD_DIM]
            k = k_ref[pl.ds(kstart, 2 * w), g * HEAD_DIM:(g + 1) * HEAD_DIM]
            v1 = v_ref[pl.ds(kstart, 2 * w), 2 * g * HEAD_DIM:2 * (g + 1) * HEAD_DIM]
            s = lax.dot_general(q, k, (((1,), (1,)), ((), ())), preferred_element_type=F32) + bias[h]
            sink = sink_ref[h] * LOG2E
            m = jnp.maximum(jnp.max(s, axis=-1, keepdims=True), sink)
            p = jnp.exp2(s - m)
            o1 = jnp.dot(p.astype(BF16), v1, preferred_element_type=F32)
            denom = o1[:, HEAD_DIM:] + jnp.exp2(sink - m)
            o_ref[r * w:(r + 1) * w, h * HEAD_DIM:(h + 1) * HEAD_DIM] = (
                o1[:, :HEAD_DIM] / denom).astype(o_ref.dtype)


def _swa(rest, sinks, slopes, *, batch, seq, tq=4096):
    t = rest.shape[0]
    nq = seq // tq
    smem = pl.BlockSpec(memory_space=pltpu.SMEM)
    return pl.pallas_call(
        _swa_kernel,
        out_shape=jax.ShapeDtypeStruct((t, SWA_Q_W), BF16),
        grid=(batch, nq),
        in_specs=[
            pl.BlockSpec((tq, SWA_Q_W), lambda b, i: (b * nq + i, REST_SQ // SWA_Q_W)),
            pl.BlockSpec((seq, SWA_KV_W), lambda b, i: (b, REST_SK // SWA_KV_W)),
            pl.BlockSpec((seq, 2 * SWA_KV_W), lambda b, i: (b, REST_SV // (2 * SWA_KV_W))),
            smem, smem,
        ],
        out_specs=pl.BlockSpec((tq, SWA_Q_W), lambda b, i: (b * nq + i, 0)),
        compiler_params=_cparams(("parallel", "arbitrary")),
        name="swa",
    )(rest, rest, rest, sinks, slopes)


def _mem_kv_kernel(mem_ref, g_ref, wk_ref, wv_ref, kg_ref, mk_ref, mv_ref):
    mn = _rms(mem_ref[...], g_ref[...]).astype(BF16)
    k = jnp.dot(mn, wk_ref[...].astype(BF16), preferred_element_type=F32)
    v = jnp.dot(mn, wv_ref[...].astype(BF16), preferred_element_type=F32)
    ones = jnp.ones((mem_ref.shape[0], HEAD_DIM), BF16)
    for h in range(MEM_HEADS):
        hs = slice(h * HEAD_DIM, (h + 1) * HEAD_DIM)
        mk_ref[:, hs] = _rms(k[:, hs], kg_ref[...]).astype(BF16)
        mv_ref[:, 2 * h * HEAD_DIM:(2 * h + 1) * HEAD_DIM] = v[:, hs].astype(BF16)
        mv_ref[:, (2 * h + 1) * HEAD_DIM:(2 * h + 2) * HEAD_DIM] = ones


def _mem_kv(mem2d, gain, wk, wv, kgain):
    rows, d = mem2d.shape
    full = lambda shape: pl.BlockSpec(shape, lambda i: (0, 0))
    return pl.pallas_call(
        _mem_kv_kernel,
        out_shape=(jax.ShapeDtypeStruct((rows, MEM_W), BF16),
                   jax.ShapeDtypeStruct((rows, 2 * MEM_W), BF16)),
        grid=(1,),
        in_specs=[full((rows, d)), full((1, d)), full((d, MEM_W)), full((d, MEM_W)), full((1, HEAD_DIM))],
        out_specs=(full((rows, MEM_W)), full((rows, 2 * MEM_W))),
        compiler_params=_cparams(("arbitrary",)),
        name="mem_kv",
    )(mem2d, gain, wk, wv, kgain)


def _mem_attn_kernel(q_ref, mk_ref, mv_ref, o_ref):
    for h in range(MEM_HEADS):
        hs = slice(h * HEAD_DIM, (h + 1) * HEAD_DIM)
        s = lax.dot_general(q_ref[:, hs], mk_ref[:, hs], (((1,), (1,)), ((), ())),
                            preferred_element_type=F32)
        m = jnp.max(s, axis=-1, keepdims=True)
        p = jnp.exp2(s - m)
        o1 = jnp.dot(p.astype(BF16), mv_ref[:, 2 * h * HEAD_DIM:(2 * h + 2) * HEAD_DIM],
                     preferred_element_type=F32)
        o_ref[:, hs] = (o1[:, :HEAD_DIM] / o1[:, HEAD_DIM:]).astype(o_ref.dtype)


def _mem_attn(rest, mk, mv, *, batch, seq, mem_len, tq=4096):
    t = rest.shape[0]
    nq = seq // tq
    return pl.pallas_call(
        _mem_attn_kernel,
        out_shape=jax.ShapeDtypeStruct((t, MEM_W), BF16),
        grid=(batch, nq),
        in_specs=[
            pl.BlockSpec((tq, MEM_W), lambda b, i: (b * nq + i, REST_MQ // MEM_W)),
            pl.BlockSpec((mem_len, MEM_W), lambda b, i: (b, 0)),
            pl.BlockSpec((mem_len, 2 * MEM_W), lambda b, i: (b, 0)),
        ],
        out_specs=pl.BlockSpec((tq, MEM_W), lambda b, i: (b * nq + i, 0)),
        compiler_params=_cparams(("parallel", "arbitrary")),
        name="mem_attn",
    )(rest, mk, mv)


def _out_proj_kernel(x_ref, a_ref, b_ref, c_ref, w_ref, o_ref):
    acc = jnp.dot(a_ref[...], w_ref[0:FOX_W, :], preferred_element_type=F32)
    acc += jnp.dot(b_ref[...], w_ref[FOX_W:FOX_W + SWA_Q_W, :], preferred_element_type=F32)
    acc += jnp.dot(c_ref[...], w_ref[FOX_W + SWA_Q_W:, :], preferred_element_type=F32)
    o_ref[...] = x_ref[...] + acc


def _out_proj(x, a, b, c, w, *, tm=1024):
    t, d = x.shape
    return pl.pallas_call(
        _out_proj_kernel,
        out_shape=jax.ShapeDtypeStruct((t, d), F32),
        grid=(t // tm,),
        in_specs=[
            pl.BlockSpec((tm, d), lambda i: (i, 0)),
            pl.BlockSpec((tm, FOX_W), lambda i: (i, 0)),
            pl.BlockSpec((tm, SWA_Q_W), lambda i: (i, 0)),
            pl.BlockSpec((tm, MEM_W), lambda i: (i, 0)),
            pl.BlockSpec(w.shape, lambda i: (0, 0), pipeline_mode=pl.Buffered(1)),
        ],
        out_specs=pl.BlockSpec((tm, d), lambda i: (i, 0)),
        compiler_params=_cparams(("parallel",)),
        name="out_proj",
    )(x, a, b, c, w)


def _alibi_slopes(n):
    return jnp.asarray([2.0 ** (-8.0 * i / n) for i in range(1, n + 1)], dtype=F32)


def _pack_w_in_kernel(w_ref, *rest, n_main):
    n_ride = (len(rest) - 2) // 2
    ride_in, (main_ref, f_ref), ride_out = rest[:n_ride], rest[n_ride:n_ride + 2], rest[n_ride + 2:]
    i = pl.program_id(0)
    n_a = w_ref.shape[2]

    @pl.when(i < n_main)
    def _():
        for a in range(n_a):
            main_ref[:, a * LANES:(a + 1) * LANES] = w_ref[0, :, a, :].astype(BF16)
        for src, dst in zip(ride_in, ride_out):
            dst[...] = src[...].astype(BF16)

    @pl.when(i == n_main)
    def _():
        row = lax.broadcasted_iota(jnp.int32, (f_ref.shape[0], LANES), 0)
        for a in range(n_a):
            f_ref[:, a * LANES:(a + 1) * LANES] = jnp.where(
                row < FOX_HEADS, w_ref[0, 0:f_ref.shape[0], a, :], 0.0).astype(BF16)


def _pack_w_in(w_in, l, *, ride=(), rows=256):
    d, n = w_in.shape[1:]
    fl0 = 3 * FOX_W
    assert n == PROJ_W + FOX_HEADS and fl0 % rows == 0 and PROJ_W % rows == 0 and d % LANES == 0
    wt = jnp.swapaxes(w_in, 1, 2).reshape(w_in.shape[0], n, d // LANES, LANES)
    n_main = PROJ_W // rows

    def start(i):
        main = jnp.where(i * rows < fl0, i * rows, i * rows + FOX_HEADS)
        return jnp.where(i < n_main, main, fl0)

    last_main = lambda i: (jnp.minimum(i, n_main - 1), 0)
    ride_specs = []
    for w in ride:
        assert w.shape[0] % (n_main * 2 * SUBLANES) == 0
        ride_specs.append(pl.BlockSpec((w.shape[0] // n_main, w.shape[1]), last_main))
    out = pl.pallas_call(
        functools.partial(_pack_w_in_kernel, n_main=n_main),
        out_shape=(jax.ShapeDtypeStruct((PROJ_W, d), BF16), jax.ShapeDtypeStruct((LANES, d), BF16),
                   *[jax.ShapeDtypeStruct(w.shape, BF16) for w in ride]),
        grid=(n_main + 1,),
        in_specs=[pl.BlockSpec((pl.Element(1), pl.Element(rows), pl.Element(d // LANES), pl.Element(LANES)),
                               lambda i: (l, start(i), 0, 0)),
                  *ride_specs],
        out_specs=(pl.BlockSpec((rows, d), last_main), pl.BlockSpec((LANES, d), lambda i: (0, 0)),
                   *ride_specs),
        compiler_params=_cparams(("arbitrary",)),
        name="pack_w_in",
    )(wt, *ride)
    return out[0], out[1], out[2:]


def kernel(x, mem, ffn1_norm, ffn1_gate, ffn1_up, ffn1_down, mix_norm, mem_norm, w_in, forget_bias, w_mem_k, w_mem_v, fox_q_gain, fox_k_gain, swa_q_gain, swa_k_gain, swa_sinks, mem_q_gain, mem_k_gain, w_out, ffn2_norm, ffn2_gate, ffn2_up, ffn2_down):
    batch, seq, d = x.shape
    mem_len = mem.shape[1]
    depth = w_in.shape[0]
    scale = HEAD_DIM ** -0.5
    slopes = _alibi_slopes(SWA_HEADS)
    x2 = x.reshape(batch * seq, d)
    mem2 = mem.reshape(batch * mem_len, d)
    zeros = jnp.zeros((HEAD_DIM,), F32)

    for l in range(depth):
        w_main, w_f, ffn1_w = _pack_w_in(w_in, l, ride=(ffn1_gate[l], ffn1_up[l], ffn1_down[l]))
        x2, ffn2_w = _ffn(x2, ffn1_norm[l][None], *ffn1_w,
                          next_weights=(ffn2_gate[l], ffn2_up[l], ffn2_down[l]))

        qs = scale * LOG2E
        head_gain = jnp.stack([fox_q_gain[l] * qs, fox_k_gain[l], swa_q_gain[l] * qs,
                               swa_k_gain[l], mem_q_gain[l] * qs, zeros, zeros, zeros])
        fbias = jnp.pad(forget_bias[l], (0, LANES - FOX_HEADS))[None]
        qt, ka, vt, rest, anc, w_out_bf = _in_proj(x2, mix_norm[l][None], w_main, w_f, head_gain, fbias,
                                                   w_out[l], seq=seq)

        n_sub = (x2.shape[0] // FOX_BLK) // anc.shape[0]
        anchors = anc[:, :n_sub, :SUBLANES].reshape(-1)
        out_a = _fox(anchors, qt, ka, vt, batch=batch, seq=seq)
        out_b = _swa(rest, swa_sinks[l], slopes, batch=batch, seq=seq)
        mk, mv = _mem_kv(mem2, mem_norm[l][None], w_mem_k[l], w_mem_v[l], mem_k_gain[l][None])
        out_c = _mem_attn(rest, mk, mv, batch=batch, seq=seq, mem_len=mem_len)

        x2 = _out_proj(x2, out_a, out_b, out_c, w_out_bf)
        x2, _ = _ffn(x2, ffn2_norm[l][None], *ffn2_w)
    return x2.reshape(batch, seq, d)
```

```python
import functools

import jax
import jax.numpy as jnp
from jax import lax
from jax.experimental import pallas as pl
from jax.experimental.pallas import tpu as pltpu

F32 = jnp.float32
BF16 = jnp.bfloat16

HEAD_DIM = 128
FOX_HEADS = 6
SWA_HEADS = 6
SWA_KV_HEADS = 2
SWA_GROUP = SWA_HEADS // SWA_KV_HEADS
MEM_HEADS = 4
WINDOW = 128
EPS = 1e-6
NEG_INF = -1e30
LOG2E = 1.4426950408889634
FOX_BLK = 512
BIAS_LANES = 4
NORM_ROWS = 512

FOX_W = FOX_HEADS * HEAD_DIM
SWA_Q_W = SWA_HEADS * HEAD_DIM
SWA_KV_W = SWA_KV_HEADS * HEAD_DIM
MEM_W = MEM_HEADS * HEAD_DIM

PROJ_W = 3 * FOX_W + SWA_Q_W + 2 * SWA_KV_W + MEM_W
REST_W = SWA_Q_W + 3 * SWA_KV_W + MEM_W
REST_SQ, REST_SK, REST_MQ, REST_SV = 0, SWA_Q_W, SWA_Q_W + SWA_KV_W, SWA_Q_W + SWA_KV_W + MEM_W
GAIN_FQ, GAIN_FK, GAIN_SQ, GAIN_SK, GAIN_MQ = range(5)

LANES = 128
SUBLANES = 8
MXU_N = 256

VMEM_LIMIT = 58 * 1024 * 1024


def _cparams(sem):
    return pltpu.CompilerParams(dimension_semantics=sem, vmem_limit_bytes=VMEM_LIMIT)


def _rms(x, gain):
    ms = jnp.mean(x * x, axis=-1, keepdims=True)
    return x * lax.rsqrt(ms + EPS) * gain


def _ffn_kernel(x_ref, g_ref, wg_ref, wu_ref, wd_ref, *rest, n_cast):
    cast_in, o_ref, cast_out, xn_ref = rest[:n_cast], rest[n_cast], rest[n_cast + 1:-1], rest[-1]
    j = pl.program_id(1)

    def ride_along():
        for src, dst in zip(cast_in, cast_out):
            dst[...] = src[...].astype(BF16)

    def half_swiglu(xn):
        acc = None
        for c0 in range(0, wg_ref.shape[1], MXU_N):
            cs = slice(c0, c0 + MXU_N)
            gate = jnp.dot(xn, wg_ref[:, cs], preferred_element_type=F32)
            up = jnp.dot(xn, wu_ref[:, cs], preferred_element_type=F32)
            h = (gate * jax.nn.sigmoid(gate) * (0.5 * up)).astype(BF16)
            part = jnp.dot(h, wd_ref[cs, :], preferred_element_type=F32)
            acc = part if acc is None else acc + part
        return acc

    @pl.when(j == 0)
    def _():
        for r0 in range(0, x_ref.shape[0], NORM_ROWS):
            rows = slice(r0, r0 + NORM_ROWS)
            x = x_ref[rows, :]
            xn = _rms(x, g_ref[...]).astype(BF16)
            xn_ref[rows, :] = xn
            o_ref[rows, :] = x + half_swiglu(xn)
        ride_along()

    @pl.when(j > 0)
    def _():
        o_ref[...] += half_swiglu(xn_ref[...])
        ride_along()


def _ffn(x, gain, wg, wu, wd, *, next_weights=(), tm=1024, tf=512):
    t, d = x.shape
    dff = wg.shape[1]
    gi, gj = t // tm, dff // tf
    cast_specs, cast_shapes = [], []
    for w in next_weights:
        ri, cj = (gi, gj) if w.shape[1] == dff else (gj, gi)
        blk = (w.shape[0] // ri, w.shape[1] // cj)
        assert blk[0] % SUBLANES == 0 and blk[1] % LANES == 0
        index = (lambda i, j: (i, j)) if w.shape[1] == dff else (lambda i, j: (j, i))
        cast_specs.append(pl.BlockSpec(blk, index))
        cast_shapes.append(jax.ShapeDtypeStruct(w.shape, BF16))
    out = pl.pallas_call(
        functools.partial(_ffn_kernel, n_cast=len(next_weights)),
        out_shape=(jax.ShapeDtypeStruct((t, d), F32), *cast_shapes),
        grid=(gi, gj),
        in_specs=[
            pl.BlockSpec((tm, d), lambda i, j: (i, 0)),
            pl.BlockSpec((1, d), lambda i, j: (0, 0)),
            pl.BlockSpec((d, tf), lambda i, j: (0, j)),
            pl.BlockSpec((d, tf), lambda i, j: (0, j)),
            pl.BlockSpec((tf, d), lambda i, j: (j, 0)),
            *cast_specs,
        ],
        out_specs=(pl.BlockSpec((tm, d), lambda i, j: (i, 0)), *cast_specs),
        scratch_shapes=[pltpu.VMEM((tm, d), BF16)],
        compiler_params=_cparams(("parallel", "arbitrary")),
        name="ffn",
    )(x, gain, wg, wu, wd, *next_weights)
    return out[0], out[1:]


def _in_proj_kernel(x_ref, g_ref, w_ref, wf_ref, hg_ref, fb_ref, wo_ref,
                    qt_ref, ka_ref, vt_ref, rest_ref, anc_ref, wo_bf_ref, carry_ref, *, blocks_per_seq):
    i = pl.program_id(0)
    tm = x_ref.shape[0]

    @pl.when(i == 0)
    def _():
        carry_ref[...] = jnp.zeros_like(carry_ref)

    wo_bf_ref[...] = wo_ref[...].astype(BF16)
    hn = _rms(x_ref[...], g_ref[...]).astype(BF16)

    nt = (((1,), (1,)), ((), ()))
    logit = lax.dot_general(hn, wf_ref[...], nt, preferred_element_type=F32)
    c = jax.nn.log_sigmoid(logit + fb_ref[...]) * LOG2E
    row = lax.broadcasted_iota(jnp.int32, c.shape, 0) % FOX_BLK
    sh = 1
    while sh < FOX_BLK:
        c = c + jnp.where(row >= sh, pltpu.roll(c, sh, 0), 0.0)
        sh *= 2

    anchor = jnp.where(i % blocks_per_seq == 0, 0.0, carry_ref[0:1, :])
    anc_ref[...] = jnp.zeros_like(anc_ref)
    for r in range(tm // FOX_BLK):
        anc_ref[0, r:r + 1, :] = anchor
        anchor = anchor + c[(r + 1) * FOX_BLK - 1:(r + 1) * FOX_BLK, :]
    carry_ref[0:1, :] = anchor

    n_sub = tm // FOX_BLK
    lane = lax.broadcasted_iota(jnp.int32, c.shape, 1)
    sub = lax.broadcasted_iota(jnp.int32, (HEAD_DIM, FOX_BLK), 0)
    ct = c.T
    ak = jnp.zeros(c.shape, F32)
    for h in range(FOX_HEADS):
        ch = c[:, h:h + 1]
        hi = ch.astype(BF16).astype(F32)
        lo = ch - hi
        own = lane - BIAS_LANES * h
        ak = jnp.where((own == 0) | (own == 1), 1.0, jnp.where(own == 2, -hi, jnp.where(own == 3, -lo, ak)))
        for r in range(n_sub):
            cr = ct[h:h + 1, r * FOX_BLK:(r + 1) * FOX_BLK]
            hi_r = cr.astype(BF16).astype(F32)
            lo_r = cr - hi_r
            own_r = sub - BIAS_LANES * h
            aq = jnp.where(own_r == 0, hi_r, jnp.where(own_r == 1, lo_r,
                                                       jnp.where((own_r == 2) | (own_r == 3), 1.0, 0.0)))
            qt_ref[r, (2 * h + 1) * HEAD_DIM:(2 * h + 2) * HEAD_DIM, :] = aq.astype(BF16)
    ka_ref[:, FOX_W:] = ak.astype(BF16)

    def normed(ph, row_id):
        return _rms(ph, hg_ref[row_id:row_id + 1, :]).astype(BF16)

    heads_per_chunk = MXU_N // HEAD_DIM
    sv_chunk = (3 * FOX_W + SWA_Q_W + SWA_KV_W) // MXU_N
    chunks = [ck for ck in range(PROJ_W // MXU_N) if ck != sv_chunk] + [sv_chunk]
    for ck in chunks:
        p = lax.dot_general(hn, w_ref[ck * MXU_N:(ck + 1) * MXU_N, :], nt, preferred_element_type=F32)
        for e in range(heads_per_chunk):
            col = ck * MXU_N + e * HEAD_DIM
            ph = p[:, e * HEAD_DIM:(e + 1) * HEAD_DIM]
            if col < FOX_W:
                h = col // HEAD_DIM
                qn = _rms(ph, hg_ref[GAIN_FQ:GAIN_FQ + 1, :])
                for r in range(n_sub):
                    qt_ref[r, 2 * h * HEAD_DIM:(2 * h + 1) * HEAD_DIM, :] = (
                        qn[r * FOX_BLK:(r + 1) * FOX_BLK, :].T.astype(BF16))
            elif col < 2 * FOX_W:
                h = (col - FOX_W) // HEAD_DIM
                ka_ref[:, h * HEAD_DIM:(h + 1) * HEAD_DIM] = normed(ph, GAIN_FK)
            elif col < 3 * FOX_W:
                h = (col - 2 * FOX_W) // HEAD_DIM
                for r in range(n_sub):
                    vt_ref[r, h * HEAD_DIM:(h + 1) * HEAD_DIM, :] = (
                        ph[r * FOX_BLK:(r + 1) * FOX_BLK, :].T.astype(BF16))
            else:
                wc = col - 3 * FOX_W
                if wc < SWA_Q_W:
                    val, rc = normed(ph, GAIN_SQ), REST_SQ + wc
                elif wc < SWA_Q_W + SWA_KV_W:
                    val, rc = normed(ph, GAIN_SK), REST_SK + wc - SWA_Q_W
                elif wc < SWA_Q_W + 2 * SWA_KV_W:
                    val, rc = ph.astype(BF16), REST_SV + 2 * (wc - SWA_Q_W - SWA_KV_W)
                    rest_ref[:, rc + HEAD_DIM:rc + 2 * HEAD_DIM] = jnp.ones((tm, HEAD_DIM), BF16)
                else:
                    val, rc = normed(ph, GAIN_MQ), REST_MQ + wc - SWA_Q_W - 2 * SWA_KV_W
                rest_ref[:, rc:rc + HEAD_DIM] = val


def _in_proj(x, gain, w, wf, head_gain, fbias, w_out, *, seq, tm=512):
    t, d = x.shape
    wo_rows = w_out.shape[0] // (t // tm)
    assert wo_rows % (2 * SUBLANES) == 0
    assert seq % tm == 0 and tm % FOX_BLK == 0 and tm // FOX_BLK <= SUBLANES
    const = lambda shape: pl.BlockSpec(shape, lambda i: (0, 0), pipeline_mode=pl.Buffered(1))
    return pl.pallas_call(
        functools.partial(_in_proj_kernel, blocks_per_seq=seq // tm),
        out_shape=(
            jax.ShapeDtypeStruct((t // FOX_BLK, 2 * FOX_W, FOX_BLK), BF16),
            jax.ShapeDtypeStruct((t, FOX_W + HEAD_DIM), BF16),
            jax.ShapeDtypeStruct((t // FOX_BLK, FOX_W, FOX_BLK), BF16),
            jax.ShapeDtypeStruct((t, REST_W), BF16),
            jax.ShapeDtypeStruct((t // tm, SUBLANES, LANES), F32),
            jax.ShapeDtypeStruct(w_out.shape, BF16),
        ),
        grid=(t // tm,),
        in_specs=[
            pl.BlockSpec((tm, d), lambda i: (i, 0)),
            const((1, d)),
            const((PROJ_W, d)),
            const((LANES, d)),
            const((SUBLANES, HEAD_DIM)),
            const((1, LANES)),
            pl.BlockSpec((wo_rows, w_out.shape[1]), lambda i: (i, 0)),
        ],
        out_specs=(
            pl.BlockSpec((tm // FOX_BLK, 2 * FOX_W, FOX_BLK), lambda i: (i, 0, 0)),
            pl.BlockSpec((tm, FOX_W + HEAD_DIM), lambda i: (i, 0)),
            pl.BlockSpec((tm // FOX_BLK, FOX_W, FOX_BLK), lambda i: (i, 0, 0)),
            pl.BlockSpec((tm, REST_W), lambda i: (i, 0)),
            pl.BlockSpec((1, SUBLANES, LANES), lambda i: (i, 0, 0)),
            pl.BlockSpec((wo_rows, w_out.shape[1]), lambda i: (i, 0)),
        ),
        scratch_shapes=[pltpu.VMEM((SUBLANES, LANES), F32)],
        compiler_params=_cparams(("arbitrary",)),
        name="in_proj",
    )(x, gain, w, wf, head_gain, fbias, w_out)


def _fox_kernel(anc_ref, q_ref, k_ref, vt_ref, o_ref, m_ref, l_ref, acc_ref, st_ref, *, nblk):
    lax.fori_loop(0, q_ref.shape[0], functools.partial(
        _fox_query_block, anc_ref, q_ref, k_ref, vt_ref, o_ref, m_ref, l_ref, acc_ref, st_ref, nblk), 0)


def _fox_query_block(anc_ref, q_ref, k_ref, vt_ref, o_ref, m_ref, l_ref, acc_ref, st_ref, nblk, sub, carry):
    q_per_step, _, blk = q_ref.shape
    aw = 2 * HEAD_DIM
    b = pl.program_id(0)
    i = pl.program_id(1) * q_per_step + sub
    m_ref[...] = jnp.full(m_ref.shape, NEG_INF, F32)
    l_ref[...] = jnp.zeros(l_ref.shape, F32)
    acc_ref[...] = jnp.zeros(acc_ref.shape, F32)

    def scores(j, slot):
        ks = pl.ds(pl.multiple_of(j * blk, blk), blk)
        k_bias = k_ref[ks, FOX_W:]
        for h in range(FOX_HEADS):
            k_aug = jnp.concatenate([k_ref[ks, h * HEAD_DIM:(h + 1) * HEAD_DIM], k_bias], axis=1)
            st_ref[slot, h] = jnp.dot(k_aug, q_ref[sub, h * aw:(h + 1) * aw, :],
                                      preferred_element_type=F32)

    def softmax_pv(j, slot, masked):
        if masked:
            causal = (lax.broadcasted_iota(jnp.int32, (blk, blk), 0)
                      <= lax.broadcasted_iota(jnp.int32, (blk, blk), 1))
        for h in range(FOX_HEADS):
            st = st_ref[slot, h]
            d = anc_ref[(b * nblk + i) * SUBLANES + h] - anc_ref[(b * nblk + j) * SUBLANES + h]
            if masked:
                st = jnp.where(causal, st, NEG_INF)
            m_prev = m_ref[h]
            m_new = jnp.maximum(m_prev, jnp.max(st, axis=0, keepdims=True) + d)
            alpha = jnp.exp2(m_prev - m_new)
            p = jnp.exp2(st - (m_new - d))
            l_ref[h] = alpha * l_ref[h] + jnp.sum(p, axis=0, keepdims=True)
            pv = jnp.dot(vt_ref[j, h * HEAD_DIM:(h + 1) * HEAD_DIM, :], p.astype(BF16),
                         preferred_element_type=F32)
            acc_ref[h] = alpha * acc_ref[h] + pv
            m_ref[h] = m_new

    def body(t, c):
        j = 2 * t
        scores(j + 1, 1)
        softmax_pv(j, 0, False)
        scores(j + 2, 0)
        softmax_pv(j + 1, 1, False)
        return c

    scores(0, 0)
    lax.fori_loop(0, i // 2, body, 0)

    @pl.when(i % 2 == 0)
    def _():
        softmax_pv(i, 0, True)

    @pl.when(i % 2 == 1)
    def _():
        scores(i, 1)
        softmax_pv(i - 1, 0, False)
        softmax_pv(i, 1, True)

    rows = pl.ds(pl.multiple_of(sub * blk, blk), blk)
    for h in range(FOX_HEADS):
        o_ref[rows, h * HEAD_DIM:(h + 1) * HEAD_DIM] = (acc_ref[h] / l_ref[h]).T.astype(o_ref.dtype)
    return carry


def _fox(anchors, qt, ka, vt, *, batch, seq, q_per_step=2):
    t = ka.shape[0]
    blk = FOX_BLK
    nblk = seq // blk
    steps = nblk // q_per_step
    return pl.pallas_call(
        functools.partial(_fox_kernel, nblk=nblk),
        out_shape=jax.ShapeDtypeStruct((t, FOX_W), BF16),
        grid=(batch, steps),
        in_specs=[
            pl.BlockSpec(memory_space=pltpu.SMEM),
            pl.BlockSpec((q_per_step, 2 * FOX_W, blk), lambda b, i: (b * steps + i, 0, 0)),
            pl.BlockSpec((seq, FOX_W + HEAD_DIM), lambda b, i: (b, 0)),
            pl.BlockSpec((nblk, FOX_W, blk), lambda b, i: (b, 0, 0)),
        ],
        out_specs=pl.BlockSpec((q_per_step * blk, FOX_W), lambda b, i: (b * steps + i, 0)),
        scratch_shapes=[
            pltpu.VMEM((FOX_HEADS, 1, blk), F32),
            pltpu.VMEM((FOX_HEADS, 1, blk), F32),
            pltpu.VMEM((FOX_HEADS, HEAD_DIM, blk), F32),
            pltpu.VMEM((2, FOX_HEADS, blk, blk), F32),
        ],
        compiler_params=_cparams(("parallel", "arbitrary")),
        name="fox",
    )(anchors, qt, ka, vt)


def _swa_kernel(q_ref, k_ref, v_ref, sink_ref, slope_ref, o_ref):
    qi = pl.program_id(1)
    tq = q_ref.shape[0]
    w = WINDOW
    n_sub = tq // w
    row = lax.broadcasted_iota(jnp.int32, (w, 2 * w), 0)
    col = lax.broadcasted_iota(jnp.int32, (w, 2 * w), 1)

    def bias_tables(offset):
        dist = offset + row - col
        valid = (dist >= 0) & (dist < w)
        dist_f = dist.astype(F32)
        return [jnp.where(valid, -(slope_ref[h] * LOG2E) * dist_f, NEG_INF) for h in range(SWA_HEADS)]

    later_bias = bias_tables(w)
    for r in range(n_sub):
        n = qi * n_sub + r
        kstart = pl.multiple_of(jnp.maximum(n - 1, 0) * w, w)
        bias = bias_tables(n * w - kstart) if r == 0 else later_bias
        for h in range(SWA_HEADS):
            g = h // SWA_GROUP
            q = q_ref[r * w:(r + 1) * w, h * HEAD_DIM:(h + 1) * HEAD_DIM]
            k = k_ref[pl.ds(kstart, 2 * w), g * HEAD_DIM:(g + 1) * HEAD_DIM]
            v1 = v_ref[pl.ds(kstart, 2 * w), 2 * g * HEAD_DIM:2 * (g + 1) * HEAD_DIM]
            s = lax.dot_general(q, k, (((1,), (1,)), ((), ())), preferred_element_type=F32) + bias[h]
            sink = sink_ref[h] * LOG2E
            m = jnp.maximum(jnp.max(s, axis=-1, keepdims=True), sink)
            p = jnp.exp2(s - m)
            o1 = jnp.dot(p.astype(BF16), v1, preferred_element_type=F32)
            denom = o1[:, HEAD_DIM:] + jnp.exp2(sink - m)
            o_ref[r * w:(r + 1) * w, h * HEAD_DIM:(h + 1) * HEAD_DIM] = (
                o1[:, :HEAD_DIM] / denom).astype(o_ref.dtype)


def _swa(rest, sinks, slopes, *, batch, seq, tq=4096):
    t = rest.shape[0]
    nq = seq // tq
    smem = pl.BlockSpec(memory_space=pltpu.SMEM)
    return pl.pallas_call(
        _swa_kernel,
        out_shape=jax.ShapeDtypeStruct((t, SWA_Q_W), BF16),
        grid=(batch, nq),
        in_specs=[
            pl.BlockSpec((tq, SWA_Q_W), lambda b, i: (b * nq + i, REST_SQ // SWA_Q_W)),
            pl.BlockSpec((seq, SWA_KV_W), lambda b, i: (b, REST_SK // SWA_KV_W)),
            pl.BlockSpec((seq, 2 * SWA_KV_W), lambda b, i: (b, REST_SV // (2 * SWA_KV_W))),
            smem, smem,
        ],
        out_specs=pl.BlockSpec((tq, SWA_Q_W), lambda b, i: (b * nq + i, 0)),
        compiler_params=_cparams(("parallel", "arbitrary")),
        name="swa",
    )(rest, rest, rest, sinks, slopes)


def _mem_attn_kernel(q_ref, mem_ref, g_ref, wk_ref, wv_ref, kg_ref, o_ref, mk_ref, mv_ref):
    @pl.when(pl.program_id(1) == 0)
    def _():
        mn = _rms(mem_ref[...], g_ref[...]).astype(BF16)
        k = jnp.dot(mn, wk_ref[...].astype(BF16), preferred_element_type=F32)
        v = jnp.dot(mn, wv_ref[...].astype(BF16), preferred_element_type=F32)
        ones = jnp.ones((mem_ref.shape[0], HEAD_DIM), BF16)
        for h in range(MEM_HEADS):
            hs = slice(h * HEAD_DIM, (h + 1) * HEAD_DIM)
            mk_ref[:, hs] = _rms(k[:, hs], kg_ref[...]).astype(BF16)
            mv_ref[:, 2 * h * HEAD_DIM:(2 * h + 1) * HEAD_DIM] = v[:, hs].astype(BF16)
            mv_ref[:, (2 * h + 1) * HEAD_DIM:(2 * h + 2) * HEAD_DIM] = ones

    for h in range(MEM_HEADS):
        hs = slice(h * HEAD_DIM, (h + 1) * HEAD_DIM)
        s = lax.dot_general(q_ref[:, hs], mk_ref[:, hs], (((1,), (1,)), ((), ())),
                            preferred_element_type=F32)
        m = jnp.max(s, axis=-1, keepdims=True)
        p = jnp.exp2(s - m)
        o1 = jnp.dot(p.astype(BF16), mv_ref[:, 2 * h * HEAD_DIM:(2 * h + 2) * HEAD_DIM],
                     preferred_element_type=F32)
        o_ref[:, hs] = (o1[:, :HEAD_DIM] / o1[:, HEAD_DIM:]).astype(o_ref.dtype)


def _mem_attn(rest, mem2d, gain, wk, wv, kgain, *, batch, seq, mem_len, tq=4096):
    t = rest.shape[0]
    d = mem2d.shape[1]
    nq = seq // tq
    const = lambda shape: pl.BlockSpec(shape, lambda b, i: (0, 0), pipeline_mode=pl.Buffered(1))
    return pl.pallas_call(
        _mem_attn_kernel,
        out_shape=jax.ShapeDtypeStruct((t, MEM_W), BF16),
        grid=(batch, nq),
        in_specs=[
            pl.BlockSpec((tq, MEM_W), lambda b, i: (b * nq + i, REST_MQ // MEM_W)),
            pl.BlockSpec((mem_len, d), lambda b, i: (b, 0)),
            const((1, d)), const((d, MEM_W)), const((d, MEM_W)), const((1, HEAD_DIM)),
        ],
        out_specs=pl.BlockSpec((tq, MEM_W), lambda b, i: (b * nq + i, 0)),
        scratch_shapes=[pltpu.VMEM((mem_len, MEM_W), BF16), pltpu.VMEM((mem_len, 2 * MEM_W), BF16)],
        compiler_params=_cparams(("parallel", "arbitrary")),
        name="mem_attn",
    )(rest, mem2d, gain, wk, wv, kgain)


def _out_proj_kernel(x_ref, a_ref, b_ref, c_ref, w_ref, o_ref):
    acc = jnp.dot(a_ref[...], w_ref[0:FOX_W, :], preferred_element_type=F32)
    acc += jnp.dot(b_ref[...], w_ref[FOX_W:FOX_W + SWA_Q_W, :], preferred_element_type=F32)
    acc += jnp.dot(c_ref[...], w_ref[FOX_W + SWA_Q_W:, :], preferred_element_type=F32)
    o_ref[...] = x_ref[...] + acc


def _out_proj(x, a, b, c, w, *, tm=1024):
    t, d = x.shape
    return pl.pallas_call(
        _out_proj_kernel,
        out_shape=jax.ShapeDtypeStruct((t, d), F32),
        grid=(t // tm,),
        in_specs=[
            pl.BlockSpec((tm, d), lambda i: (i, 0)),
            pl.BlockSpec((tm, FOX_W), lambda i: (i, 0)),
            pl.BlockSpec((tm, SWA_Q_W), lambda i: (i, 0)),
            pl.BlockSpec((tm, MEM_W), lambda i: (i, 0)),
            pl.BlockSpec(w.shape, lambda i: (0, 0), pipeline_mode=pl.Buffered(1)),
        ],
        out_specs=pl.BlockSpec((tm, d), lambda i: (i, 0)),
        compiler_params=_cparams(("parallel",)),
        name="out_proj",
    )(x, a, b, c, w)


def _alibi_slopes(n):
    return jnp.asarray([2.0 ** (-8.0 * i / n) for i in range(1, n + 1)], dtype=F32)


def _pack_w_in_kernel(w_ref, *rest, n_main):
    n_ride = (len(rest) - 2) // 2
    ride_in, (main_ref, f_ref), ride_out = rest[:n_ride], rest[n_ride:n_ride + 2], rest[n_ride + 2:]
    i = pl.program_id(0)
    n_a = w_ref.shape[2]

    @pl.when(i < n_main)
    def _():
        for a in range(n_a):
            main_ref[:, a * LANES:(a + 1) * LANES] = w_ref[0, :, a, :].astype(BF16)
        for src, dst in zip(ride_in, ride_out):
            dst[...] = src[...].astype(BF16)

    @pl.when(i == n_main)
    def _():
        row = lax.broadcasted_iota(jnp.int32, (f_ref.shape[0], LANES), 0)
        for a in range(n_a):
            f_ref[:, a * LANES:(a + 1) * LANES] = jnp.where(
                row < FOX_HEADS, w_ref[0, 0:f_ref.shape[0], a, :], 0.0).astype(BF16)


def _pack_w_in(w_in, l, *, ride=(), rows=256):
    d, n = w_in.shape[1:]
    fl0 = 3 * FOX_W
    assert n == PROJ_W + FOX_HEADS and fl0 % rows == 0 and PROJ_W % rows == 0 and d % LANES == 0
    wt = jnp.swapaxes(w_in, 1, 2).reshape(w_in.shape[0], n, d // LANES, LANES)
    n_main = PROJ_W // rows

    def start(i):
        main = jnp.where(i * rows < fl0, i * rows, i * rows + FOX_HEADS)
        return jnp.where(i < n_main, main, fl0)

    last_main = lambda i: (jnp.minimum(i, n_main - 1), 0)
    ride_specs = []
    for w in ride:
        assert w.shape[0] % (n_main * 2 * SUBLANES) == 0
        ride_specs.append(pl.BlockSpec((w.shape[0] // n_main, w.shape[1]), last_main))
    out = pl.pallas_call(
        functools.partial(_pack_w_in_kernel, n_main=n_main),
        out_shape=(jax.ShapeDtypeStruct((PROJ_W, d), BF16), jax.ShapeDtypeStruct((LANES, d), BF16),
                   *[jax.ShapeDtypeStruct(w.shape, BF16) for w in ride]),
        grid=(n_main + 1,),
        in_specs=[pl.BlockSpec((pl.Element(1), pl.Element(rows), pl.Element(d // LANES), pl.Element(LANES)),
                               lambda i: (l, start(i), 0, 0)),
                  *ride_specs],
        out_specs=(pl.BlockSpec((rows, d), last_main), pl.BlockSpec((LANES, d), lambda i: (0, 0)),
                   *ride_specs),
        compiler_params=_cparams(("arbitrary",)),
        name="pack_w_in",
    )(wt, *ride)
    return out[0], out[1], out[2:]


def kernel(x, mem, ffn1_norm, ffn1_gate, ffn1_up, ffn1_down, mix_norm, mem_norm, w_in, forget_bias, w_mem_k, w_mem_v, fox_q_gain, fox_k_gain, swa_q_gain, swa_k_gain, swa_sinks, mem_q_gain, mem_k_gain, w_out, ffn2_norm, ffn2_gate, ffn2_up, ffn2_down):
    batch, seq, d = x.shape
    mem_len = mem.shape[1]
    depth = w_in.shape[0]
    scale = HEAD_DIM ** -0.5
    slopes = _alibi_slopes(SWA_HEADS)
    x2 = x.reshape(batch * seq, d)
    mem2 = mem.reshape(batch * mem_len, d)
    zeros = jnp.zeros((HEAD_DIM,), F32)

    for l in range(depth):
        w_main, w_f, ffn1_w = _pack_w_in(w_in, l, ride=(ffn1_gate[l], ffn1_up[l], ffn1_down[l]))
        x2, ffn2_w = _ffn(x2, ffn1_norm[l][None], *ffn1_w,
                          next_weights=(ffn2_gate[l], ffn2_up[l], ffn2_down[l]))

        qs = scale * LOG2E
        head_gain = jnp.stack([fox_q_gain[l] * qs, fox_k_gain[l], swa_q_gain[l] * qs,
                               swa_k_gain[l], mem_q_gain[l] * qs, zeros, zeros, zeros])
        fbias = jnp.pad(forget_bias[l], (0, LANES - FOX_HEADS))[None]
        qt, ka, vt, rest, anc, w_out_bf = _in_proj(x2, mix_norm[l][None], w_main, w_f, head_gain, fbias,
                                                   w_out[l], seq=seq)

        n_sub = (x2.shape[0] // FOX_BLK) // anc.shape[0]
        anchors = anc[:, :n_sub, :SUBLANES].reshape(-1)
        out_a = _fox(anchors, qt, ka, vt, batch=batch, seq=seq)
        out_b = _swa(rest, swa_sinks[l], slopes, batch=batch, seq=seq)
        out_c = _mem_attn(rest, mem2, mem_norm[l][None], w_mem_k[l], w_mem_v[l], mem_k_gain[l][None],
                          batch=batch, seq=seq, mem_len=mem_len)

        x2 = _out_proj(x2, out_a, out_b, out_c, w_out_bf)
        x2, _ = _ffn(x2, ffn2_norm[l][None], *ffn2_w)
    return x2.reshape(batch, seq, d)
```
